```python
import math
import jax, jax.numpy as jnp
from jax import lax
import numpy as np


D_MODEL = 1024
BATCH = 8
SEQ = 8192
DEPTH = 4

CTX_LEN = 256
GRID_W = 64
D_FF = 2816
MLA_HEADS = 8
MLA_NOPE = 64
MLA_ROPE = 32
MLA_V = 64
MLA_Q_RANK = 384
MLA_KV_RANK = 256
SSM_WIDTH = 512
SSM_GROUP = 16
SSM_GROUPS = SSM_WIDTH // SSM_GROUP
SSM_STATE = 64
DT_MIN = 1e-3
DT_MAX = 1e-1
GQA_HEADS = 8
GQA_KV_HEADS = 2
GQA_HEAD_DIM = 64
WINDOW = 128
BLOCK = 128
N_BRANCH = 3
N_MOD = 9
ROPE_BASE = 10000.0
EPS = 1e-6
NEG_INF = -1e30
IN_SPLITS = (MLA_Q_RANK, MLA_KV_RANK, MLA_ROPE, SSM_WIDTH, GQA_HEADS * GQA_HEAD_DIM, GQA_KV_HEADS * GQA_HEAD_DIM, GQA_KV_HEADS * GQA_HEAD_DIM, N_BRANCH * D_MODEL)
IN_DIM = sum(IN_SPLITS)

kernel_name = 'hybrid_mla_s5_swa_dit_block'


def _offsets():
    return np.cumsum(IN_SPLITS)[:-1].tolist()


def bcast(t):
    return t[..., None, :]


def rmsnorm(x, g):
    x32 = x.astype(jnp.float32)
    y = x32 * lax.rsqrt(jnp.mean(x32 * x32, axis=-1, keepdims=True) + EPS)
    return (y * g.astype(jnp.float32)).astype(x.dtype)


def modulate(x, shift, scale):
    return x * (1 + bcast(scale)) + bcast(shift)


def swiglu(x, w13, w2):
    a, b = jnp.split(x @ w13, 2, axis=-1)
    return (jax.nn.silu(a) * b) @ w2


def rope_1d(x, pos):
    n = x.shape[-1]
    inv = ROPE_BASE ** (-jnp.arange(0, n, 2, dtype=jnp.float32) / n)
    ang = pos.astype(jnp.float32)[:, None, None] * inv
    cos, sin = jnp.cos(ang), jnp.sin(ang)
    x32 = x.astype(jnp.float32)
    x1, x2 = x32[..., : n // 2], x32[..., n // 2:]
    return jnp.concatenate([x1 * cos - x2 * sin, x1 * sin + x2 * cos], axis=-1).astype(x.dtype)


def axial_rope(x, row, col):
    half = x.shape[-1] // 2
    return jnp.concatenate([rope_1d(x[..., :half], row), rope_1d(x[..., half:], col)], axis=-1)


def sink_softmax(score_list, sink_logit):
    m = sink_logit
    for s in score_list:
        m = jnp.maximum(m, s.max(axis=-1, keepdims=True))
    e = [jnp.exp(s - m) for s in score_list]
    denom = jnp.exp(sink_logit - m)
    for t in e:
        denom = denom + t.sum(axis=-1, keepdims=True)
    return [t / denom for t in e]


def dense_attention_blocks(q, k, v):
    B, T, H, dk = q.shape
    dv = v.shape[-1]
    nb = T // BLOCK
    scale = dk ** -0.5
    qb = jnp.moveaxis(q.reshape(B, nb, BLOCK, H, dk), 1, 0)

    def one_block(qi):
        s = jnp.einsum('bqhd,bkhd->bhqk', qi, k, preferred_element_type=jnp.float32) * scale
        p = jax.nn.softmax(s, axis=-1).astype(v.dtype)
        return jnp.einsum('bhqk,bkhd->bqhd', p, v)

    o = lax.map(one_block, qb)
    return jnp.moveaxis(o, 0, 1).reshape(B, T, H * dv)


def mla_queries(cq, q_norm, w_uq, row, col):
    B, T, _ = cq.shape
    q = (rmsnorm(cq, q_norm) @ w_uq).reshape(B, T, MLA_HEADS, MLA_NOPE + MLA_ROPE)
    q_nope, q_rope = q[..., :MLA_NOPE], q[..., MLA_NOPE:]
    if row is not None:
        q_rope = axial_rope(q_rope, row, col)
    return jnp.concatenate([q_nope, q_rope], axis=-1)


def mla_keys_values(ckv, kr, kv_norm, w_ukv, row, col):
    B, T, _ = ckv.shape
    kv = (rmsnorm(ckv, kv_norm) @ w_ukv).reshape(B, T, MLA_HEADS, MLA_NOPE + MLA_V)
    k_nope, v = kv[..., :MLA_NOPE], kv[..., MLA_NOPE:]
    kr = kr[:, :, None, :]
    if row is not None:
        kr = axial_rope(kr, row, col)
    k = jnp.concatenate([k_nope, jnp.broadcast_to(kr, (B, T, MLA_HEADS, MLA_ROPE))], axis=-1)
    return k, v


def window_gqa(q, k, v, kc, vc, sink):
    B, T, H, d = q.shape
    G = H // GQA_KV_HEADS
    nb = T // BLOCK
    scale = d ** -0.5
    qb = q.reshape(B, nb, BLOCK, GQA_KV_HEADS, G, d)

    def band(t):
        tb = t.reshape(B, nb, BLOCK, GQA_KV_HEADS, d)
        tp = jnp.pad(tb, ((0, 0), (1, 1), (0, 0), (0, 0), (0, 0)))
        return jnp.concatenate([tp[:, :-2], tp[:, 1:-1], tp[:, 2:]], axis=2)

    kb, vb = band(k), band(v)
    s_band = jnp.einsum('bnqhgd,bnkhd->bhgnqk', qb, kb, preferred_element_type=jnp.float32) * scale
    blk = jnp.arange(nb)[:, None, None]
    qpos = blk * BLOCK + jnp.arange(BLOCK)[None, :, None]
    kpos = (blk - 1) * BLOCK + jnp.arange(3 * BLOCK)[None, None, :]
    valid = (jnp.abs(qpos - kpos) <= WINDOW) & (kpos >= 0) & (kpos < T)
    s_band = jnp.where(valid, s_band, NEG_INF)
    s_ctx = jnp.einsum('bnqhgd,bchd->bhgnqc', qb, kc, preferred_element_type=jnp.float32) * scale
    sk = sink.astype(jnp.float32).reshape(GQA_KV_HEADS, G)[None, :, :, None, None, None]
    p_band, p_ctx = sink_softmax([s_band, s_ctx], sk)
    o = (jnp.einsum('bhgnqk,bnkhd->bnqhgd', p_band.astype(v.dtype), vb)
         + jnp.einsum('bhgnqc,bchd->bnqhgd', p_ctx.astype(vc.dtype), vc))
    return o.reshape(B, T, H * d)


def context_gqa(qc, kc, vc, sink):
    B, C, H, d = qc.shape
    G = H // GQA_KV_HEADS
    qg = qc.reshape(B, C, GQA_KV_HEADS, G, d)
    s = jnp.einsum('bqhgd,bkhd->bhgqk', qg, kc, preferred_element_type=jnp.float32) * d ** -0.5
    sk = sink.astype(jnp.float32).reshape(GQA_KV_HEADS, G)[None, :, :, None, None]
    (p,) = sink_softmax([s], sk)
    o = jnp.einsum('bhgqk,bkhd->bqhgd', p.astype(vc.dtype), vc)
    return o.reshape(B, C, H * d)


def cmul(ar, ai, br, bi):
    return ar * br - ai * bi, ar * bi + ai * br


def ssm_discretize(lam_re, lam_im, log_dt, b_re, b_im):
    lr, li = lam_re.astype(jnp.float32), lam_im.astype(jnp.float32)
    dt = jnp.exp(log_dt.astype(jnp.float32))[:, None]
    mag = jnp.exp(lr * dt)
    a_re, a_im = mag * jnp.cos(li * dt), mag * jnp.sin(li * dt)
    den = lr * lr + li * li
    w_re = ((a_re - 1) * lr + a_im * li) / den
    w_im = (a_im * lr - (a_re - 1) * li) / den
    bb_re, bb_im = cmul(w_re[..., None], w_im[..., None], b_re.astype(jnp.float32), b_im.astype(jnp.float32))
    return a_re, a_im, bb_re, bb_im


def diag_scan(a_re, a_im, b_re, b_im, h0, reverse):
    if reverse:
        b_re, b_im = jnp.flip(b_re, axis=1), jnp.flip(b_im, axis=1)
    if h0 is not None:
        i_re, i_im = cmul(a_re, a_im, h0[0], h0[1])
        b_re = b_re.at[:, 0].add(i_re)
        b_im = b_im.at[:, 0].add(i_im)
    T = b_re.shape[1]
    ar = jnp.broadcast_to(a_re, (1, T) + a_re.shape)
    ai = jnp.broadcast_to(a_im, (1, T) + a_im.shape)

    def combine(e1, e2):
        a1r, a1i, b1r, b1i = e1
        a2r, a2i, b2r, b2i = e2
        nar, nai = cmul(a2r, a2i, a1r, a1i)
        nbr, nbi = cmul(a2r, a2i, b1r, b1i)
        return nar, nai, nbr + b2r, nbi + b2i

    _, _, s_re, s_im = lax.associative_scan(combine, (ar, ai, b_re, b_im), axis=1)
    if reverse:
        s_re, s_im = jnp.flip(s_re, axis=1), jnp.flip(s_im, axis=1)
    return s_re, s_im


def ssm_branch(u_lat, u_ctx, lam_re, lam_im, log_dt, b_re, b_im, c_re, c_im, d_skip, w_glu, ctx_out):
    dtype = u_lat.dtype

    def to_groups(u):
        return u.astype(jnp.float32).reshape(u.shape[0], u.shape[1], SSM_GROUPS, SSM_GROUP)

    ul, uc = to_groups(u_lat), to_groups(u_ctx)
    d_g = d_skip.astype(jnp.float32).reshape(SSM_GROUPS, SSM_GROUP)
    y_lat = ul * d_g
    y_ctx = uc * d_g if ctx_out else None
    for direction in range(2):
        reverse = direction == 1
        a_re, a_im, bb_re, bb_im = ssm_discretize(lam_re[direction], lam_im[direction], log_dt[direction], b_re[direction], b_im[direction])
        c_r, c_i = c_re[direction].astype(jnp.float32), c_im[direction].astype(jnp.float32)

        def drive(u):
            return jnp.einsum('btgm,gpm->btgp', u, bb_re), jnp.einsum('btgm,gpm->btgp', u, bb_im)

        def readout(sr, si):
            return jnp.einsum('btgp,gmp->btgm', sr, c_r) - jnp.einsum('btgp,gmp->btgm', si, c_i)

        sc_re, sc_im = diag_scan(a_re, a_im, *drive(uc), None, reverse)
        end = 0 if reverse else -1
        sl_re, sl_im = diag_scan(a_re, a_im, *drive(ul), (sc_re[:, end], sc_im[:, end]), reverse)
        y_lat = y_lat + readout(sl_re, sl_im)
        if ctx_out:
            y_ctx = y_ctx + readout(sc_re, sc_im)

    def glu(y):
        y = jax.nn.gelu(y).reshape(y.shape[0], y.shape[1], SSM_WIDTH).astype(dtype)
        a, g = jnp.split(y @ w_glu, 2, axis=-1)
        return a * jax.nn.sigmoid(g)

    return glu(y_lat), (glu(y_ctx) if ctx_out else None)


def mixing_sublayer(xl, xc, row, col, ctx_out, w_in, mla_q_norm, mla_kv_norm, mla_w_uq, mla_w_ukv, mla_w_o,
                    lam_re, lam_im, log_dt, b_re, b_im, c_re, c_im, d_skip, w_glu, sink, gqa_w_o, w_out):
    B, T, _ = xl.shape
    C = xc.shape[1]
    offs = _offsets()
    cq, ckv, kr, u, gq, gk, gv, gates = jnp.split(xl @ w_in, offs, axis=-1)
    w_parts = jnp.split(w_in, offs, axis=1)
    ckv_c, kr_c, u_c, gk_c, gv_c = (xc @ w_parts[i] for i in (1, 2, 3, 5, 6))
    k_c, v_c = mla_keys_values(ckv_c, kr_c, mla_kv_norm, mla_w_ukv, None, None)
    gk_c = gk_c.reshape(B, C, GQA_KV_HEADS, GQA_HEAD_DIM)
    gv_c = gv_c.reshape(B, C, GQA_KV_HEADS, GQA_HEAD_DIM)
    q = mla_queries(cq, mla_q_norm, mla_w_uq, row, col)
    k, v = mla_keys_values(ckv, kr, mla_kv_norm, mla_w_ukv, row, col)
    mla_lat = dense_attention_blocks(q, jnp.concatenate([k_c, k], axis=1), jnp.concatenate([v_c, v], axis=1)) @ mla_w_o
    ssm_lat, ssm_ctx = ssm_branch(u, u_c, lam_re, lam_im, log_dt, b_re, b_im, c_re, c_im, d_skip, w_glu, ctx_out)
    gq = axial_rope(gq.reshape(B, T, GQA_HEADS, GQA_HEAD_DIM), row, col)
    gk = axial_rope(gk.reshape(B, T, GQA_KV_HEADS, GQA_HEAD_DIM), row, col)
    gv = gv.reshape(B, T, GQA_KV_HEADS, GQA_HEAD_DIM)
    gqa_lat = window_gqa(gq, gk, gv, gk_c, gv_c, sink) @ gqa_w_o

    def merge(gate_logits, b0, b1, b2):
        g0, g1, g2 = jnp.split(jax.nn.sigmoid(gate_logits), N_BRANCH, axis=-1)
        return (g0 * b0 + g1 * b1 + g2 * b2) @ w_out

    out_lat = merge(gates, mla_lat, ssm_lat, gqa_lat)
    if not ctx_out:
        return out_lat, None
    cq_c, gq_c, gates_c = (xc @ w_parts[i] for i in (0, 4, 7))
    q_c = mla_queries(cq_c, mla_q_norm, mla_w_uq, None, None)
    mla_ctx = dense_attention_blocks(q_c, k_c, v_c) @ mla_w_o
    gqa_ctx = context_gqa(gq_c.reshape(B, C, GQA_HEADS, GQA_HEAD_DIM), gk_c, gv_c, sink) @ gqa_w_o
    out_ctx = merge(gates_c, mla_ctx, ssm_ctx, gqa_ctx)
    return out_lat, out_ctx


def _fwd_setup_inputs(seed: int = 0) -> dict:
    key = jax.random.key(seed)
    ks = list(jax.random.split(key, 32))
    f32 = jnp.float32

    def nrm(shape, scale):
        return jax.random.normal(ks.pop(), shape, f32) * scale

    D, F, Lr = D_MODEL, D_FF, DEPTH
    G, P, M = SSM_GROUPS, SSM_STATE, SSM_GROUP
    inp = {}
    inp['x'] = nrm((BATCH, SEQ, D), 1.0)
    inp['c'] = nrm((BATCH, D), 1.0)
    inp['ctx'] = nrm((BATCH, CTX_LEN, D), 1.0)
    inp['c_ctx'] = nrm((D,), 1.0)
    inp['ada_w'] = nrm((Lr, D, N_MOD * D), 0.5 * D ** -0.5)
    inp['ada_b'] = nrm((Lr, N_MOD * D), 0.01)
    inp['norm_ffn1'] = 1.0 + nrm((Lr, D), 0.01)
    inp['norm_mix'] = 1.0 + nrm((Lr, D), 0.01)
    inp['norm_ffn2'] = 1.0 + nrm((Lr, D), 0.01)
    inp['ffn1_w13'] = nrm((Lr, D, 2 * F), D ** -0.5)
    inp['ffn1_w2'] = nrm((Lr, F, D), F ** -0.5)
    inp['ffn2_w13'] = nrm((Lr, D, 2 * F), D ** -0.5)
    inp['ffn2_w2'] = nrm((Lr, F, D), F ** -0.5)
    inp['w_in'] = nrm((Lr, D, IN_DIM), D ** -0.5)
    inp['mla_q_norm'] = 1.0 + nrm((Lr, MLA_Q_RANK), 0.01)
    inp['mla_kv_norm'] = 1.0 + nrm((Lr, MLA_KV_RANK), 0.01)
    inp['mla_w_uq'] = nrm((Lr, MLA_Q_RANK, MLA_HEADS * (MLA_NOPE + MLA_ROPE)), MLA_Q_RANK ** -0.5)
    inp['mla_w_ukv'] = nrm((Lr, MLA_KV_RANK, MLA_HEADS * (MLA_NOPE + MLA_V)), MLA_KV_RANK ** -0.5)
    inp['mla_w_o'] = nrm((Lr, MLA_HEADS * MLA_V, D), (MLA_HEADS * MLA_V) ** -0.5)
    inp['ssm_lambda_re'] = -0.5 + nrm((Lr, 2, G, P), 0.01)
    inp['ssm_lambda_im'] = jnp.pi * jnp.arange(P, dtype=f32) + nrm((Lr, 2, G, P), 0.01)
    inp['ssm_log_dt'] = jax.random.uniform(ks.pop(), (Lr, 2, G), f32, minval=math.log(DT_MIN), maxval=math.log(DT_MAX))
    inp['ssm_b_re'] = nrm((Lr, 2, G, P, M), (2 * M) ** -0.5)
    inp['ssm_b_im'] = nrm((Lr, 2, G, P, M), (2 * M) ** -0.5)
    inp['ssm_c_re'] = nrm((Lr, 2, G, M, P), 0.5)
    inp['ssm_c_im'] = nrm((Lr, 2, G, M, P), 0.5)
    inp['ssm_d'] = nrm((Lr, SSM_WIDTH), 1.0)
    inp['ssm_w_glu'] = nrm((Lr, SSM_WIDTH, 2 * D), SSM_WIDTH ** -0.5)
    inp['gqa_sink'] = nrm((Lr, GQA_HEADS), 0.5)
    inp['gqa_w_o'] = nrm((Lr, GQA_HEADS * GQA_HEAD_DIM, D), (GQA_HEADS * GQA_HEAD_DIM) ** -0.5)
    inp['w_out'] = nrm((Lr, D, D), D ** -0.5)
    inp['final_norm'] = 1.0 + nrm((D,), 0.01)
    return inp


def _fwd_reference(x, c, ctx, c_ctx, ada_w, ada_b, norm_ffn1, norm_mix, norm_ffn2, ffn1_w13, ffn1_w2, ffn2_w13, ffn2_w2,
              w_in, mla_q_norm, mla_kv_norm, mla_w_uq, mla_w_ukv, mla_w_o, ssm_lambda_re, ssm_lambda_im, ssm_log_dt,
              ssm_b_re, ssm_b_im, ssm_c_re, ssm_c_im, ssm_d, ssm_w_glu, gqa_sink, gqa_w_o, w_out, final_norm):
    L = x.shape[1]
    ROWS = L // GRID_W
    row = jnp.repeat(jnp.arange(ROWS, dtype=jnp.int32), GRID_W)
    col = jnp.tile(jnp.arange(GRID_W, dtype=jnp.int32), ROWS)
    h, hc = x, ctx
    for layer in range(DEPTH):
        ctx_out = layer < DEPTH - 1
        mod = jnp.split(jax.nn.silu(c) @ ada_w[layer] + ada_b[layer], N_MOD, axis=-1)
        mod_c = jnp.split(jax.nn.silu(c_ctx) @ ada_w[layer] + ada_b[layer], N_MOD, axis=-1)
        h = h + 0.5 * bcast(mod[2]) * swiglu(modulate(rmsnorm(h, norm_ffn1[layer]), mod[0], mod[1]), ffn1_w13[layer], ffn1_w2[layer])
        hc = hc + 0.5 * bcast(mod_c[2]) * swiglu(modulate(rmsnorm(hc, norm_ffn1[layer]), mod_c[0], mod_c[1]), ffn1_w13[layer], ffn1_w2[layer])
        mix_lat, mix_ctx = mixing_sublayer(
            modulate(rmsnorm(h, norm_mix[layer]), mod[3], mod[4]),
            modulate(rmsnorm(hc, norm_mix[layer]), mod_c[3], mod_c[4]),
            row, col, ctx_out, w_in[layer], mla_q_norm[layer], mla_kv_norm[layer], mla_w_uq[layer], mla_w_ukv[layer],
            mla_w_o[layer], ssm_lambda_re[layer], ssm_lambda_im[layer], ssm_log_dt[layer], ssm_b_re[layer], ssm_b_im[layer],
            ssm_c_re[layer], ssm_c_im[layer], ssm_d[layer], ssm_w_glu[layer], gqa_sink[layer], gqa_w_o[layer], w_out[layer])
        h = h + bcast(mod[5]) * mix_lat
        h = h + 0.5 * bcast(mod[8]) * swiglu(modulate(rmsnorm(h, norm_ffn2[layer]), mod[6], mod[7]), ffn2_w13[layer], ffn2_w2[layer])
        if ctx_out:
            hc = hc + bcast(mod_c[5]) * mix_ctx
            hc = hc + 0.5 * bcast(mod_c[8]) * swiglu(modulate(rmsnorm(hc, norm_ffn2[layer]), mod_c[6], mod_c[7]), ffn2_w13[layer], ffn2_w2[layer])
    return rmsnorm(h, final_norm)


import jax as _jax
import jax.numpy as _jnp

TWIN_FORMAT = 'train_step'
FWD_PARAMS = ['x', 'c', 'ctx', 'c_ctx', 'ada_w', 'ada_b', 'norm_ffn1', 'norm_mix', 'norm_ffn2', 'ffn1_w13', 'ffn1_w2', 'ffn2_w13', 'ffn2_w2', 'w_in', 'mla_q_norm', 'mla_kv_norm', 'mla_w_uq', 'mla_w_ukv', 'mla_w_o', 'ssm_lambda_re', 'ssm_lambda_im', 'ssm_log_dt', 'ssm_b_re', 'ssm_b_im', 'ssm_c_re', 'ssm_c_im', 'ssm_d', 'ssm_w_glu', 'gqa_sink', 'gqa_w_o', 'w_out', 'final_norm']
TWIN_WEIGHTS = ['c_ctx', 'ada_w', 'ada_b', 'norm_ffn1', 'norm_mix', 'norm_ffn2', 'ffn1_w13', 'ffn1_w2', 'ffn2_w13', 'ffn2_w2', 'w_in', 'mla_q_norm', 'mla_kv_norm', 'mla_w_uq', 'mla_w_ukv', 'mla_w_o', 'ssm_lambda_re', 'ssm_lambda_im', 'ssm_log_dt', 'ssm_b_re', 'ssm_b_im', 'ssm_c_re', 'ssm_c_im', 'ssm_d', 'ssm_w_glu', 'gqa_sink', 'gqa_w_o', 'w_out', 'final_norm']
TWIN_DIFF_INPUT = 'x'
TWIN_INPUTS = ['x', 'c', 'ctx', 'c_ctx', 'ada_w', 'ada_b', 'norm_ffn1', 'norm_mix', 'norm_ffn2', 'ffn1_w13', 'ffn1_w2', 'ffn2_w13', 'ffn2_w2', 'w_in', 'mla_q_norm', 'mla_kv_norm', 'mla_w_uq', 'mla_w_ukv', 'mla_w_o', 'ssm_lambda_re', 'ssm_lambda_im', 'ssm_log_dt', 'ssm_b_re', 'ssm_b_im', 'ssm_c_re', 'ssm_c_im', 'ssm_d', 'ssm_w_glu', 'gqa_sink', 'gqa_w_o', 'w_out', 'final_norm', 'loss_target', 'm_c_ctx', 'm_ada_w', 'm_ada_b', 'm_norm_ffn1', 'm_norm_mix', 'm_norm_ffn2', 'm_ffn1_w13', 'm_ffn1_w2', 'm_ffn2_w13', 'm_ffn2_w2', 'm_w_in', 'm_mla_q_norm', 'm_mla_kv_norm', 'm_mla_w_uq', 'm_mla_w_ukv', 'm_mla_w_o', 'm_ssm_lambda_re', 'm_ssm_lambda_im', 'm_ssm_log_dt', 'm_ssm_b_re', 'm_ssm_b_im', 'm_ssm_c_re', 'm_ssm_c_im', 'm_ssm_d', 'm_ssm_w_glu', 'm_gqa_sink', 'm_gqa_w_o', 'm_w_out', 'm_final_norm', 'v_c_ctx', 'v_ada_w', 'v_ada_b', 'v_norm_ffn1', 'v_norm_mix', 'v_norm_ffn2', 'v_ffn1_w13', 'v_ffn1_w2', 'v_ffn2_w13', 'v_ffn2_w2', 'v_w_in', 'v_mla_q_norm', 'v_mla_kv_norm', 'v_mla_w_uq', 'v_mla_w_ukv', 'v_mla_w_o', 'v_ssm_lambda_re', 'v_ssm_lambda_im', 'v_ssm_log_dt', 'v_ssm_b_re', 'v_ssm_b_im', 'v_ssm_c_re', 'v_ssm_c_im', 'v_ssm_d', 'v_ssm_w_glu', 'v_gqa_sink', 'v_gqa_w_o', 'v_w_out', 'v_final_norm']
TWIN_OUTPUTS = ['loss', 'grad_x', 'grad_c_ctx', 'grad_ada_w', 'grad_ada_b', 'grad_norm_ffn1', 'grad_norm_mix', 'grad_norm_ffn2', 'grad_ffn1_w13', 'grad_ffn1_w2', 'grad_ffn2_w13', 'grad_ffn2_w2', 'grad_w_in', 'grad_mla_q_norm', 'grad_mla_kv_norm', 'grad_mla_w_uq', 'grad_mla_w_ukv', 'grad_mla_w_o', 'grad_ssm_lambda_re', 'grad_ssm_lambda_im', 'grad_ssm_log_dt', 'grad_ssm_b_re', 'grad_ssm_b_im', 'grad_ssm_c_re', 'grad_ssm_c_im', 'grad_ssm_d', 'grad_ssm_w_glu', 'grad_gqa_sink', 'grad_gqa_w_o', 'grad_w_out', 'grad_final_norm', 'delta_c_ctx', 'delta_ada_w', 'delta_ada_b', 'delta_norm_ffn1', 'delta_norm_mix', 'delta_norm_ffn2', 'delta_ffn1_w13', 'delta_ffn1_w2', 'delta_ffn2_w13', 'delta_ffn2_w2', 'delta_w_in', 'delta_mla_q_norm', 'delta_mla_kv_norm', 'delta_mla_w_uq', 'delta_mla_w_ukv', 'delta_mla_w_o', 'delta_ssm_lambda_re', 'delta_ssm_lambda_im', 'delta_ssm_log_dt', 'delta_ssm_b_re', 'delta_ssm_b_im', 'delta_ssm_c_re', 'delta_ssm_c_im', 'delta_ssm_d', 'delta_ssm_w_glu', 'delta_gqa_sink', 'delta_gqa_w_o', 'delta_w_out', 'delta_final_norm', 'new_m_c_ctx', 'new_m_ada_w', 'new_m_ada_b', 'new_m_norm_ffn1', 'new_m_norm_mix', 'new_m_norm_ffn2', 'new_m_ffn1_w13', 'new_m_ffn1_w2', 'new_m_ffn2_w13', 'new_m_ffn2_w2', 'new_m_w_in', 'new_m_mla_q_norm', 'new_m_mla_kv_norm', 'new_m_mla_w_uq', 'new_m_mla_w_ukv', 'new_m_mla_w_o', 'new_m_ssm_lambda_re', 'new_m_ssm_lambda_im', 'new_m_ssm_log_dt', 'new_m_ssm_b_re', 'new_m_ssm_b_im', 'new_m_ssm_c_re', 'new_m_ssm_c_im', 'new_m_ssm_d', 'new_m_ssm_w_glu', 'new_m_gqa_sink', 'new_m_gqa_w_o', 'new_m_w_out', 'new_m_final_norm', 'new_v_c_ctx', 'new_v_ada_w', 'new_v_ada_b', 'new_v_norm_ffn1', 'new_v_norm_mix', 'new_v_norm_ffn2', 'new_v_ffn1_w13', 'new_v_ffn1_w2', 'new_v_ffn2_w13', 'new_v_ffn2_w2', 'new_v_w_in', 'new_v_mla_q_norm', 'new_v_mla_kv_norm', 'new_v_mla_w_uq', 'new_v_mla_w_ukv', 'new_v_mla_w_o', 'new_v_ssm_lambda_re', 'new_v_ssm_lambda_im', 'new_v_ssm_log_dt', 'new_v_ssm_b_re', 'new_v_ssm_b_im', 'new_v_ssm_c_re', 'new_v_ssm_c_im', 'new_v_ssm_d', 'new_v_ssm_w_glu', 'new_v_gqa_sink', 'new_v_gqa_w_o', 'new_v_w_out', 'new_v_final_norm']
TWIN_LEAF_KINDS = {'loss': 'loss', 'grad_x': 'grad_x', 'grad_c_ctx': 'grad_w', 'grad_ada_w': 'grad_w', 'grad_ada_b': 'grad_w', 'grad_norm_ffn1': 'grad_w', 'grad_norm_mix': 'grad_w', 'grad_norm_ffn2': 'grad_w', 'grad_ffn1_w13': 'grad_w', 'grad_ffn1_w2': 'grad_w', 'grad_ffn2_w13': 'grad_w', 'grad_ffn2_w2': 'grad_w', 'grad_w_in': 'grad_w', 'grad_mla_q_norm': 'grad_w', 'grad_mla_kv_norm': 'grad_w', 'grad_mla_w_uq': 'grad_w', 'grad_mla_w_ukv': 'grad_w', 'grad_mla_w_o': 'grad_w', 'grad_ssm_lambda_re': 'grad_w', 'grad_ssm_lambda_im': 'grad_w', 'grad_ssm_log_dt': 'grad_w', 'grad_ssm_b_re': 'grad_w', 'grad_ssm_b_im': 'grad_w', 'grad_ssm_c_re': 'grad_w', 'grad_ssm_c_im': 'grad_w', 'grad_ssm_d': 'grad_w', 'grad_ssm_w_glu': 'grad_w', 'grad_gqa_sink': 'grad_w', 'grad_gqa_w_o': 'grad_w', 'grad_w_out': 'grad_w', 'grad_final_norm': 'grad_w', 'delta_c_ctx': 'delta_w', 'delta_ada_w': 'delta_w', 'delta_ada_b': 'delta_w', 'delta_norm_ffn1': 'delta_w', 'delta_norm_mix': 'delta_w', 'delta_norm_ffn2': 'delta_w', 'delta_ffn1_w13': 'delta_w', 'delta_ffn1_w2': 'delta_w', 'delta_ffn2_w13': 'delta_w', 'delta_ffn2_w2': 'delta_w', 'delta_w_in': 'delta_w', 'delta_mla_q_norm': 'delta_w', 'delta_mla_kv_norm': 'delta_w', 'delta_mla_w_uq': 'delta_w', 'delta_mla_w_ukv': 'delta_w', 'delta_mla_w_o': 'delta_w', 'delta_ssm_lambda_re': 'delta_w', 'delta_ssm_lambda_im': 'delta_w', 'delta_ssm_log_dt': 'delta_w', 'delta_ssm_b_re': 'delta_w', 'delta_ssm_b_im': 'delta_w', 'delta_ssm_c_re': 'delta_w', 'delta_ssm_c_im': 'delta_w', 'delta_ssm_d': 'delta_w', 'delta_ssm_w_glu': 'delta_w', 'delta_gqa_sink': 'delta_w', 'delta_gqa_w_o': 'delta_w', 'delta_w_out': 'delta_w', 'delta_final_norm': 'delta_w', 'new_m_c_ctx': 'new_m', 'new_m_ada_w': 'new_m', 'new_m_ada_b': 'new_m', 'new_m_norm_ffn1': 'new_m', 'new_m_norm_mix': 'new_m', 'new_m_norm_ffn2': 'new_m', 'new_m_ffn1_w13': 'new_m', 'new_m_ffn1_w2': 'new_m', 'new_m_ffn2_w13': 'new_m', 'new_m_ffn2_w2': 'new_m', 'new_m_w_in': 'new_m', 'new_m_mla_q_norm': 'new_m', 'new_m_mla_kv_norm': 'new_m', 'new_m_mla_w_uq': 'new_m', 'new_m_mla_w_ukv': 'new_m', 'new_m_mla_w_o': 'new_m', 'new_m_ssm_lambda_re': 'new_m', 'new_m_ssm_lambda_im': 'new_m', 'new_m_ssm_log_dt': 'new_m', 'new_m_ssm_b_re': 'new_m', 'new_m_ssm_b_im': 'new_m', 'new_m_ssm_c_re': 'new_m', 'new_m_ssm_c_im': 'new_m', 'new_m_ssm_d': 'new_m', 'new_m_ssm_w_glu': 'new_m', 'new_m_gqa_sink': 'new_m', 'new_m_gqa_w_o': 'new_m', 'new_m_w_out': 'new_m', 'new_m_final_norm': 'new_m', 'new_v_c_ctx': 'new_v', 'new_v_ada_w': 'new_v', 'new_v_ada_b': 'new_v', 'new_v_norm_ffn1': 'new_v', 'new_v_norm_mix': 'new_v', 'new_v_norm_ffn2': 'new_v', 'new_v_ffn1_w13': 'new_v', 'new_v_ffn1_w2': 'new_v', 'new_v_ffn2_w13': 'new_v', 'new_v_ffn2_w2': 'new_v', 'new_v_w_in': 'new_v', 'new_v_mla_q_norm': 'new_v', 'new_v_mla_kv_norm': 'new_v', 'new_v_mla_w_uq': 'new_v', 'new_v_mla_w_ukv': 'new_v', 'new_v_mla_w_o': 'new_v', 'new_v_ssm_lambda_re': 'new_v', 'new_v_ssm_lambda_im': 'new_v', 'new_v_ssm_log_dt': 'new_v', 'new_v_ssm_b_re': 'new_v', 'new_v_ssm_b_im': 'new_v', 'new_v_ssm_c_re': 'new_v', 'new_v_ssm_c_im': 'new_v', 'new_v_ssm_d': 'new_v', 'new_v_ssm_w_glu': 'new_v', 'new_v_gqa_sink': 'new_v', 'new_v_gqa_w_o': 'new_v', 'new_v_w_out': 'new_v', 'new_v_final_norm': 'new_v'}


def _forward(args):
    return _fwd_reference(*[args[k] for k in FWD_PARAMS])


def _output_shape():
    def fwd():
        inp = _fwd_setup_inputs(0)
        return _fwd_reference(*[inp[k] for k in FWD_PARAMS])
    out = _jax.eval_shape(fwd)
    return out.shape, out.dtype

N_MICROBATCH = 1
ADAM_LR = 0.001
ADAM_B1 = 0.9
ADAM_B2 = 0.999
ADAM_EPS = 1e-08
ADAM_WD = 0.01
ADAM_STEP = 10
PER_EXAMPLE_BATCH_AXIS = {'x': 0, 'c': 0, 'ctx': 0, 'loss_target': 0}
SHARED_INPUTS = []
_WEIGHT_DTYPES = {'c_ctx': _jnp.float32, 'ada_w': _jnp.float32, 'ada_b': _jnp.float32, 'norm_ffn1': _jnp.float32, 'norm_mix': _jnp.float32, 'norm_ffn2': _jnp.float32, 'ffn1_w13': _jnp.float32, 'ffn1_w2': _jnp.float32, 'ffn2_w13': _jnp.float32, 'ffn2_w2': _jnp.float32, 'w_in': _jnp.float32, 'mla_q_norm': _jnp.float32, 'mla_kv_norm': _jnp.float32, 'mla_w_uq': _jnp.float32, 'mla_w_ukv': _jnp.float32, 'mla_w_o': _jnp.float32, 'ssm_lambda_re': _jnp.float32, 'ssm_lambda_im': _jnp.float32, 'ssm_log_dt': _jnp.float32, 'ssm_b_re': _jnp.float32, 'ssm_b_im': _jnp.float32, 'ssm_c_re': _jnp.float32, 'ssm_c_im': _jnp.float32, 'ssm_d': _jnp.float32, 'ssm_w_glu': _jnp.float32, 'gqa_sink': _jnp.float32, 'gqa_w_o': _jnp.float32, 'w_out': _jnp.float32, 'final_norm': _jnp.float32}
MOMENT_SCALE = {'c_ctx': 2.062616e-02, 'ada_w': 3.576045e-02, 'ada_b': 6.098491e-02, 'norm_ffn1': 3.747880e-02, 'norm_mix': 2.409519e-02, 'norm_ffn2': 3.718482e-02, 'ffn1_w13': 1.643304e-02, 'ffn1_w2': 2.676135e-02, 'ffn2_w13': 1.622913e-02, 'ffn2_w2': 2.651641e-02, 'w_in': 1.296924e-02, 'mla_q_norm': 6.507348e-03, 'mla_kv_norm': 2.203624e-02, 'mla_w_uq': 4.570427e-03, 'mla_w_ukv': 1.178204e-02, 'mla_w_o': 1.161573e-02, 'ssm_lambda_re': 8.959830e-03, 'ssm_lambda_im': 9.912148e-03, 'ssm_log_dt': 4.374552e+00, 'ssm_b_re': 5.686029e-03, 'ssm_b_im': 5.776009e-03, 'ssm_c_re': 1.932289e-03, 'ssm_c_im': 1.964823e-03, 'ssm_d': 2.735465e-02, 'ssm_w_glu': 1.454870e-02, 'gqa_sink': 2.257081e-04, 'gqa_w_o': 1.065325e-02, 'w_out': 2.516006e-02, 'final_norm': 6.390095e+01}


def _to_microbatches(a, axis):
    t = _jnp.moveaxis(a, axis, 0)
    t = t.reshape((N_MICROBATCH, t.shape[0] // N_MICROBATCH) + t.shape[1:])
    return _jnp.moveaxis(t, 1, axis + 1)


def setup_inputs(seed: int = 0) -> dict:
    inp = _fwd_setup_inputs(seed)
    key = _jax.random.fold_in(_jax.random.key(seed), 7919)
    shape, _ = _output_shape()
    out = dict(inp)
    out["loss_target"] = _jax.random.normal(_jax.random.fold_in(key, 0), shape, _jnp.float32)
    for i, name in enumerate(TWIN_WEIGHTS):
        w = inp[name].astype(_jnp.float32)
        if MOMENT_SCALE is None:
            s = _jnp.sqrt(_jnp.mean(_jnp.square(w)) + 1e-30)
        else:
            s = MOMENT_SCALE[name]
        km, kv = _jax.random.split(_jax.random.fold_in(key, i + 1))
        out[name] = w
        out["m_" + name] = s * _jax.random.normal(km, w.shape, _jnp.float32)
        out["v_" + name] = (s * s) * _jax.random.uniform(kv, w.shape, _jnp.float32, 0.5, 1.5)
    if N_MICROBATCH > 1:
        for name, axis in PER_EXAMPLE_BATCH_AXIS.items():
            out[name] = _to_microbatches(out[name], axis)
    return {'x': out['x'], 'c': out['c'], 'ctx': out['ctx'], 'c_ctx': out['c_ctx'], 'ada_w': out['ada_w'], 'ada_b': out['ada_b'], 'norm_ffn1': out['norm_ffn1'], 'norm_mix': out['norm_mix'], 'norm_ffn2': out['norm_ffn2'], 'ffn1_w13': out['ffn1_w13'], 'ffn1_w2': out['ffn1_w2'], 'ffn2_w13': out['ffn2_w13'], 'ffn2_w2': out['ffn2_w2'], 'w_in': out['w_in'], 'mla_q_norm': out['mla_q_norm'], 'mla_kv_norm': out['mla_kv_norm'], 'mla_w_uq': out['mla_w_uq'], 'mla_w_ukv': out['mla_w_ukv'], 'mla_w_o': out['mla_w_o'], 'ssm_lambda_re': out['ssm_lambda_re'], 'ssm_lambda_im': out['ssm_lambda_im'], 'ssm_log_dt': out['ssm_log_dt'], 'ssm_b_re': out['ssm_b_re'], 'ssm_b_im': out['ssm_b_im'], 'ssm_c_re': out['ssm_c_re'], 'ssm_c_im': out['ssm_c_im'], 'ssm_d': out['ssm_d'], 'ssm_w_glu': out['ssm_w_glu'], 'gqa_sink': out['gqa_sink'], 'gqa_w_o': out['gqa_w_o'], 'w_out': out['w_out'], 'final_norm': out['final_norm'], 'loss_target': out['loss_target'], 'm_c_ctx': out['m_c_ctx'], 'm_ada_w': out['m_ada_w'], 'm_ada_b': out['m_ada_b'], 'm_norm_ffn1': out['m_norm_ffn1'], 'm_norm_mix': out['m_norm_mix'], 'm_norm_ffn2': out['m_norm_ffn2'], 'm_ffn1_w13': out['m_ffn1_w13'], 'm_ffn1_w2': out['m_ffn1_w2'], 'm_ffn2_w13': out['m_ffn2_w13'], 'm_ffn2_w2': out['m_ffn2_w2'], 'm_w_in': out['m_w_in'], 'm_mla_q_norm': out['m_mla_q_norm'], 'm_mla_kv_norm': out['m_mla_kv_norm'], 'm_mla_w_uq': out['m_mla_w_uq'], 'm_mla_w_ukv': out['m_mla_w_ukv'], 'm_mla_w_o': out['m_mla_w_o'], 'm_ssm_lambda_re': out['m_ssm_lambda_re'], 'm_ssm_lambda_im': out['m_ssm_lambda_im'], 'm_ssm_log_dt': out['m_ssm_log_dt'], 'm_ssm_b_re': out['m_ssm_b_re'], 'm_ssm_b_im': out['m_ssm_b_im'], 'm_ssm_c_re': out['m_ssm_c_re'], 'm_ssm_c_im': out['m_ssm_c_im'], 'm_ssm_d': out['m_ssm_d'], 'm_ssm_w_glu': out['m_ssm_w_glu'], 'm_gqa_sink': out['m_gqa_sink'], 'm_gqa_w_o': out['m_gqa_w_o'], 'm_w_out': out['m_w_out'], 'm_final_norm': out['m_final_norm'], 'v_c_ctx': out['v_c_ctx'], 'v_ada_w': out['v_ada_w'], 'v_ada_b': out['v_ada_b'], 'v_norm_ffn1': out['v_norm_ffn1'], 'v_norm_mix': out['v_norm_mix'], 'v_norm_ffn2': out['v_norm_ffn2'], 'v_ffn1_w13': out['v_ffn1_w13'], 'v_ffn1_w2': out['v_ffn1_w2'], 'v_ffn2_w13': out['v_ffn2_w13'], 'v_ffn2_w2': out['v_ffn2_w2'], 'v_w_in': out['v_w_in'], 'v_mla_q_norm': out['v_mla_q_norm'], 'v_mla_kv_norm': out['v_mla_kv_norm'], 'v_mla_w_uq': out['v_mla_w_uq'], 'v_mla_w_ukv': out['v_mla_w_ukv'], 'v_mla_w_o': out['v_mla_w_o'], 'v_ssm_lambda_re': out['v_ssm_lambda_re'], 'v_ssm_lambda_im': out['v_ssm_lambda_im'], 'v_ssm_log_dt': out['v_ssm_log_dt'], 'v_ssm_b_re': out['v_ssm_b_re'], 'v_ssm_b_im': out['v_ssm_b_im'], 'v_ssm_c_re': out['v_ssm_c_re'], 'v_ssm_c_im': out['v_ssm_c_im'], 'v_ssm_d': out['v_ssm_d'], 'v_ssm_w_glu': out['v_ssm_w_glu'], 'v_gqa_sink': out['v_gqa_sink'], 'v_gqa_w_o': out['v_gqa_w_o'], 'v_w_out': out['v_w_out'], 'v_final_norm': out['v_final_norm']}


def _loss(weights, diff, rest, loss_target):
    with _jax.named_scope("forward"):
        args = {**rest, TWIN_DIFF_INPUT: diff, **{k: w.astype(_WEIGHT_DTYPES[k]) for k, w in weights.items()}}
        y = _forward(args)
    with _jax.named_scope("loss_head"):
        err = _jnp.square(y.astype(_jnp.float32) - loss_target)
        return 0.5 * _jnp.sum(_jnp.mean(err, axis=-1)) if err.ndim else 0.5 * err


def _adamw(w, g, m, v):
    m = ADAM_B1 * m + (1.0 - ADAM_B1) * g
    v = ADAM_B2 * v + (1.0 - ADAM_B2) * _jnp.square(g)
    m_hat = m / (1.0 - ADAM_B1 ** ADAM_STEP)
    v_hat = v / (1.0 - ADAM_B2 ** ADAM_STEP)
    delta = -ADAM_LR * (m_hat / (_jnp.sqrt(v_hat) + ADAM_EPS) + ADAM_WD * w)
    return delta, m, v


def reference(x, c, ctx, c_ctx, ada_w, ada_b, norm_ffn1, norm_mix, norm_ffn2, ffn1_w13, ffn1_w2, ffn2_w13, ffn2_w2, w_in, mla_q_norm, mla_kv_norm, mla_w_uq, mla_w_ukv, mla_w_o, ssm_lambda_re, ssm_lambda_im, ssm_log_dt, ssm_b_re, ssm_b_im, ssm_c_re, ssm_c_im, ssm_d, ssm_w_glu, gqa_sink, gqa_w_o, w_out, final_norm, loss_target, m_c_ctx, m_ada_w, m_ada_b, m_norm_ffn1, m_norm_mix, m_norm_ffn2, m_ffn1_w13, m_ffn1_w2, m_ffn2_w13, m_ffn2_w2, m_w_in, m_mla_q_norm, m_mla_kv_norm, m_mla_w_uq, m_mla_w_ukv, m_mla_w_o, m_ssm_lambda_re, m_ssm_lambda_im, m_ssm_log_dt, m_ssm_b_re, m_ssm_b_im, m_ssm_c_re, m_ssm_c_im, m_ssm_d, m_ssm_w_glu, m_gqa_sink, m_gqa_w_o, m_w_out, m_final_norm, v_c_ctx, v_ada_w, v_ada_b, v_norm_ffn1, v_norm_mix, v_norm_ffn2, v_ffn1_w13, v_ffn1_w2, v_ffn2_w13, v_ffn2_w2, v_w_in, v_mla_q_norm, v_mla_kv_norm, v_mla_w_uq, v_mla_w_ukv, v_mla_w_o, v_ssm_lambda_re, v_ssm_lambda_im, v_ssm_log_dt, v_ssm_b_re, v_ssm_b_im, v_ssm_c_re, v_ssm_c_im, v_ssm_d, v_ssm_w_glu, v_gqa_sink, v_gqa_w_o, v_w_out, v_final_norm):
    given = dict(x=x, c=c, ctx=ctx, c_ctx=c_ctx, ada_w=ada_w, ada_b=ada_b, norm_ffn1=norm_ffn1, norm_mix=norm_mix, norm_ffn2=norm_ffn2, ffn1_w13=ffn1_w13, ffn1_w2=ffn1_w2, ffn2_w13=ffn2_w13, ffn2_w2=ffn2_w2, w_in=w_in, mla_q_norm=mla_q_norm, mla_kv_norm=mla_kv_norm, mla_w_uq=mla_w_uq, mla_w_ukv=mla_w_ukv, mla_w_o=mla_w_o, ssm_lambda_re=ssm_lambda_re, ssm_lambda_im=ssm_lambda_im, ssm_log_dt=ssm_log_dt, ssm_b_re=ssm_b_re, ssm_b_im=ssm_b_im, ssm_c_re=ssm_c_re, ssm_c_im=ssm_c_im, ssm_d=ssm_d, ssm_w_glu=ssm_w_glu, gqa_sink=gqa_sink, gqa_w_o=gqa_w_o, w_out=w_out, final_norm=final_norm, loss_target=loss_target, m_c_ctx=m_c_ctx, m_ada_w=m_ada_w, m_ada_b=m_ada_b, m_norm_ffn1=m_norm_ffn1, m_norm_mix=m_norm_mix, m_norm_ffn2=m_norm_ffn2, m_ffn1_w13=m_ffn1_w13, m_ffn1_w2=m_ffn1_w2, m_ffn2_w13=m_ffn2_w13, m_ffn2_w2=m_ffn2_w2, m_w_in=m_w_in, m_mla_q_norm=m_mla_q_norm, m_mla_kv_norm=m_mla_kv_norm, m_mla_w_uq=m_mla_w_uq, m_mla_w_ukv=m_mla_w_ukv, m_mla_w_o=m_mla_w_o, m_ssm_lambda_re=m_ssm_lambda_re, m_ssm_lambda_im=m_ssm_lambda_im, m_ssm_log_dt=m_ssm_log_dt, m_ssm_b_re=m_ssm_b_re, m_ssm_b_im=m_ssm_b_im, m_ssm_c_re=m_ssm_c_re, m_ssm_c_im=m_ssm_c_im, m_ssm_d=m_ssm_d, m_ssm_w_glu=m_ssm_w_glu, m_gqa_sink=m_gqa_sink, m_gqa_w_o=m_gqa_w_o, m_w_out=m_w_out, m_final_norm=m_final_norm, v_c_ctx=v_c_ctx, v_ada_w=v_ada_w, v_ada_b=v_ada_b, v_norm_ffn1=v_norm_ffn1, v_norm_mix=v_norm_mix, v_norm_ffn2=v_norm_ffn2, v_ffn1_w13=v_ffn1_w13, v_ffn1_w2=v_ffn1_w2, v_ffn2_w13=v_ffn2_w13, v_ffn2_w2=v_ffn2_w2, v_w_in=v_w_in, v_mla_q_norm=v_mla_q_norm, v_mla_kv_norm=v_mla_kv_norm, v_mla_w_uq=v_mla_w_uq, v_mla_w_ukv=v_mla_w_ukv, v_mla_w_o=v_mla_w_o, v_ssm_lambda_re=v_ssm_lambda_re, v_ssm_lambda_im=v_ssm_lambda_im, v_ssm_log_dt=v_ssm_log_dt, v_ssm_b_re=v_ssm_b_re, v_ssm_b_im=v_ssm_b_im, v_ssm_c_re=v_ssm_c_re, v_ssm_c_im=v_ssm_c_im, v_ssm_d=v_ssm_d, v_ssm_w_glu=v_ssm_w_glu, v_gqa_sink=v_gqa_sink, v_gqa_w_o=v_gqa_w_o, v_w_out=v_w_out, v_final_norm=v_final_norm)
    weights = {n: given[n] for n in TWIN_WEIGHTS}
    shared = {n: given[n] for n in SHARED_INPUTS}
    per_example = {n: given[n] for n in ['x', 'c', 'ctx']}
    grad_fn = _jax.value_and_grad(_loss, argnums=(0, 1))

    def one_microbatch(ex, loss_target):
        ex = dict(ex)
        diff = ex.pop(TWIN_DIFF_INPUT)
        return grad_fn(weights, diff, {**shared, **ex}, loss_target)

    if N_MICROBATCH == 1:
        loss, (grad_w, grad_x) = one_microbatch(per_example, given["loss_target"])
    else:
        def body(carry, xs):
            loss_sum, grad_sum = carry
            l_k, (gw_k, gx_k) = one_microbatch(xs[0], xs[1])
            with _jax.named_scope("update"):
                return (loss_sum + l_k, _jax.tree.map(_jnp.add, grad_sum, gw_k)), gx_k

        init = (_jnp.zeros((), _jnp.float32), _jax.tree.map(_jnp.zeros_like, weights))
        (loss, grad_w), grad_x = _jax.lax.scan(body, init, (per_example, given["loss_target"]))
    with _jax.named_scope("update"):
        delta_w, new_m, new_v = {}, {}, {}
        for n in TWIN_WEIGHTS:
            delta_w[n], new_m[n], new_v[n] = _adamw(weights[n], grad_w[n], given["m_" + n], given["v_" + n])
    return (loss, grad_x, *[grad_w[n] for n in TWIN_WEIGHTS], *[delta_w[n] for n in TWIN_WEIGHTS],
            *[new_m[n] for n in TWIN_WEIGHTS], *[new_v[n] for n in TWIN_WEIGHTS])
```

```python
import functools
import math

import jax
import jax.numpy as jnp
from jax import lax
from jax.experimental import pallas as pl
from jax.experimental.pallas import tpu as pltpu

F32 = jnp.float32
BF16 = jnp.bfloat16

MLA_HEADS, MLA_NOPE, MLA_ROPE, MLA_V = 8, 64, 32, 64
MLA_Q_RANK, MLA_KV_RANK = 384, 256
SSM_WIDTH, SSM_GROUP, SSM_STATE = 512, 16, 64
SSM_GROUPS = SSM_WIDTH // SSM_GROUP
GQA_HEADS, GQA_KV, GQA_D = 8, 2, 64
GQA_G = GQA_HEADS // GQA_KV
WINDOW, BLOCK, GRID_W = 128, 128, 64
N_MOD = 9
ROPE_BASE = 10000.0
EPS = 1e-6
NEG_INF = -1e30
ADAM_LR, ADAM_B1, ADAM_B2, ADAM_EPS, ADAM_WD, ADAM_STEP = 0.001, 0.9, 0.999, 1e-08, 0.01, 10

N_DEV = 8
LANES = 128
SUBLANES = 8
VMEM_LIMIT = 56 * 1024 * 1024
PACK_ROWS = 1024

SHARDED = ('ada_w', 'ffn1_w13', 'ffn1_w2', 'ffn2_w13', 'ffn2_w2', 'w_in', 'mla_w_uq', 'mla_w_ukv',
           'mla_w_o', 'ssm_w_glu', 'gqa_w_o', 'w_out')
ROW_SHARDED = ('ffn1_w2', 'ffn2_w2', 'w_out')
REPLICATED = ('c_ctx', 'ada_b', 'norm_ffn1', 'norm_mix', 'norm_ffn2', 'mla_q_norm', 'mla_kv_norm',
              'ssm_lambda_re', 'ssm_lambda_im', 'ssm_log_dt', 'ssm_b_re', 'ssm_b_im', 'ssm_c_re', 'ssm_c_im',
              'ssm_d', 'gqa_sink', 'final_norm')
WEIGHTS = ('c_ctx', 'ada_w', 'ada_b', 'norm_ffn1', 'norm_mix', 'norm_ffn2', 'ffn1_w13', 'ffn1_w2', 'ffn2_w13',
           'ffn2_w2', 'w_in', 'mla_q_norm', 'mla_kv_norm', 'mla_w_uq', 'mla_w_ukv', 'mla_w_o', 'ssm_lambda_re',
           'ssm_lambda_im', 'ssm_log_dt', 'ssm_b_re', 'ssm_b_im', 'ssm_c_re', 'ssm_c_im', 'ssm_d', 'ssm_w_glu',
           'gqa_sink', 'gqa_w_o', 'w_out', 'final_norm')


def _tile(n, target, mult):
    t = (min(target, n) // mult) * mult
    while t >= mult:
        if n % t == 0:
            return t
        t -= mult
    return n


def _cparams(sem):
    return pltpu.CompilerParams(dimension_semantics=sem, vmem_limit_bytes=VMEM_LIMIT)


def _mm(a, b, mode):
    if mode == 'nn':
        (M, K), N = a.shape, b.shape[1]
    elif mode == 'nt':
        (M, K), N = a.shape, b.shape[0]
    else:
        (K, M), N = a.shape, b.shape[1]
    tm = _tile(M, 768, LANES if mode == 'tn' else SUBLANES)
    tn = _tile(N, 512, LANES)
    tk = _tile(K, 1024, LANES if mode != 'tn' else SUBLANES)
    nk = K // tk
    dims = {'nn': (((1,), (0,)), ((), ())), 'nt': (((1,), (1,)), ((), ())), 'tn': (((0,), (0,)), ((), ()))}[mode]

    def body(a_ref, b_ref, o_ref):
        part = lax.dot_general(a_ref[...].astype(BF16), b_ref[...].astype(BF16), dims, preferred_element_type=F32)
        if nk == 1:
            o_ref[...] = part
        else:
            @pl.when(pl.program_id(2) == 0)
            def _():
                o_ref[...] = part

            @pl.when(pl.program_id(2) > 0)
            def _():
                o_ref[...] += part

    a_spec = pl.BlockSpec((tk, tm), lambda i, j, k: (k, i)) if mode == 'tn' else pl.BlockSpec((tm, tk), lambda i, j, k: (i, k))
    b_spec = pl.BlockSpec((tn, tk), lambda i, j, k: (j, k)) if mode == 'nt' else pl.BlockSpec((tk, tn), lambda i, j, k: (k, j))
    return pl.pallas_call(
        body, name='mm_' + mode, grid=(M // tm, N // tn, nk),
        in_specs=[a_spec, b_spec], out_specs=pl.BlockSpec((tm, tn), lambda i, j, k: (i, j)),
        out_shape=jax.ShapeDtypeStruct((M, N), F32),
        compiler_params=_cparams(('parallel', 'parallel', 'arbitrary')),
    )(a, b)


@jax.custom_vjp
def matmul(x, w):
    return _mm(x, w, 'nn')


def _matmul_fwd(x, w):
    return _mm(x, w, 'nn'), (x, w)


def _matmul_bwd(res, g):
    x, w = res
    return _mm(g, w, 'nt'), _mm(x, g, 'tn')


matmul.defvjp(_matmul_fwd, _matmul_bwd)


def _norm_fwd_call(x, g, sh, sc, rb):
    n, d = x.shape
    has_mod = sh is not None

    def body(*refs):
        x_ref, g_ref = refs[0], refs[1]
        o_ref = refs[-1]
        xv = x_ref[...]
        r = lax.rsqrt(jnp.mean(xv * xv, axis=-1, keepdims=True) + EPS)
        y = xv * r * g_ref[...]
        if has_mod:
            lat = pl.program_id(0) > 0
            shv = jnp.where(lat, refs[2][1:2, :], refs[2][0:1, :])
            scv = jnp.where(lat, refs[3][1:2, :], refs[3][0:1, :])
            y = y * (1.0 + scv) + shv
        o_ref[...] = y

    row = pl.BlockSpec((rb, d), lambda i: (i, 0))
    vec = pl.BlockSpec((1, d), lambda i: (0, 0))
    two = pl.BlockSpec((2, d), lambda i: (0, 0))
    ins = [x, g.reshape(1, d)] + ([sh, sc] if has_mod else [])
    return pl.pallas_call(
        body, name='norm_fwd', grid=(n // rb,), in_specs=[row, vec] + ([two, two] if has_mod else []),
        out_specs=row, out_shape=jax.ShapeDtypeStruct((n, d), F32), compiler_params=_cparams(('parallel',)),
    )(*ins)


def _norm_bwd_call(x, g, sh, sc, dy, rb):
    n, d = x.shape
    has_mod = sh is not None

    def body(*refs):
        x_ref, g_ref, dy_ref = refs[0], refs[1], refs[-3]
        dx_ref, acc_ref = refs[-2], refs[-1]
        i = pl.program_id(0)
        xv, dyv, gv = x_ref[...], dy_ref[...], g_ref[...]
        r = lax.rsqrt(jnp.mean(xv * xv, axis=-1, keepdims=True) + EPS)
        xh = xv * r
        if has_mod:
            lat = i > 0
            scv = jnp.where(lat, refs[3][1:2, :], refs[3][0:1, :])
            dyg = dyv * (1.0 + scv)
        else:
            dyg = dyv
        dxh = dyg * gv
        dx_ref[...] = r * (dxh - xh * jnp.mean(dxh * xh, axis=-1, keepdims=True))

        @pl.when(i == 0)
        def _():
            acc_ref[...] = jnp.zeros_like(acc_ref)

        acc_ref[0:1, :] += jnp.sum(dyg * xh, axis=0, keepdims=True)
        if has_mod:
            dsh = jnp.sum(dyv, axis=0, keepdims=True)
            dsc = jnp.sum(dyv * xh * gv, axis=0, keepdims=True)

            @pl.when(i == 0)
            def _():
                acc_ref[1:2, :] += dsh
                acc_ref[3:4, :] += dsc

            @pl.when(i > 0)
            def _():
                acc_ref[2:3, :] += dsh
                acc_ref[4:5, :] += dsc

    row = pl.BlockSpec((rb, d), lambda i: (i, 0))
    vec = pl.BlockSpec((1, d), lambda i: (0, 0))
    two = pl.BlockSpec((2, d), lambda i: (0, 0))
    ins = [x, g.reshape(1, d)] + ([sh, sc] if has_mod else []) + [dy]
    return pl.pallas_call(
        body, name='norm_bwd', grid=(n // rb,), in_specs=[row, vec] + ([two, two] if has_mod else []) + [row],
        out_specs=[row, pl.BlockSpec((SUBLANES, d), lambda i: (0, 0))],
        out_shape=[jax.ShapeDtypeStruct((n, d), F32), jax.ShapeDtypeStruct((SUBLANES, d), F32)],
        compiler_params=_cparams(('arbitrary',)),
    )(*ins)


def _make_norm(rb, has_mod):
    if has_mod:
        @jax.custom_vjp
        def f(x, g, sh, sc):
            return _norm_fwd_call(x, g, sh, sc, rb)

        def fwd(x, g, sh, sc):
            return _norm_fwd_call(x, g, sh, sc, rb), (x, g, sh, sc)

        def bwd(res, dy):
            x, g, sh, sc = res
            dx, acc = _norm_bwd_call(x, g, sh, sc, dy, rb)
            return dx, acc[0], acc[1:3], acc[3:5]
    else:
        @jax.custom_vjp
        def f(x, g):
            return _norm_fwd_call(x, g, None, None, rb)

        def fwd(x, g):
            return _norm_fwd_call(x, g, None, None, rb), (x, g)

        def bwd(res, dy):
            x, g = res
            dx, acc = _norm_bwd_call(x, g, None, None, dy, rb)
            return dx, acc[0]
    f.defvjp(fwd, bwd)
    return f


def _make_gated_res(rb, coef):
    def fwd_call(x, f, gate):
        n, d = x.shape

        def body(x_ref, f_ref, g_ref, o_ref):
            gv = jnp.where(pl.program_id(0) > 0, g_ref[1:2, :], g_ref[0:1, :])
            o_ref[...] = x_ref[...] + coef * gv * f_ref[...]

        row = pl.BlockSpec((rb, d), lambda i: (i, 0))
        return pl.pallas_call(
            body, name='gated_res_fwd', grid=(n // rb,), in_specs=[row, row, pl.BlockSpec((2, d), lambda i: (0, 0))],
            out_specs=row, out_shape=jax.ShapeDtypeStruct((n, d), F32), compiler_params=_cparams(('parallel',)),
        )(x, f, gate)

    def bwd_call(dy, f, gate):
        n, d = dy.shape

        def body(dy_ref, f_ref, g_ref, df_ref, acc_ref):
            i = pl.program_id(0)
            gv = jnp.where(i > 0, g_ref[1:2, :], g_ref[0:1, :])
            dyv = dy_ref[...]
            df_ref[...] = coef * gv * dyv
            part = coef * jnp.sum(dyv * f_ref[...], axis=0, keepdims=True)

            @pl.when(i == 0)
            def _():
                acc_ref[...] = jnp.zeros_like(acc_ref)
                acc_ref[0:1, :] += part

            @pl.when(i > 0)
            def _():
                acc_ref[1:2, :] += part

        row = pl.BlockSpec((rb, d), lambda i: (i, 0))
        return pl.pallas_call(
            body, name='gated_res_bwd', grid=(n // rb,), in_specs=[row, row, pl.BlockSpec((2, d), lambda i: (0, 0))],
            out_specs=[row, pl.BlockSpec((SUBLANES, d), lambda i: (0, 0))],
            out_shape=[jax.ShapeDtypeStruct((n, d), F32), jax.ShapeDtypeStruct((SUBLANES, d), F32)],
            compiler_params=_cparams(('arbitrary',)),
        )(dy, f, gate)

    @jax.custom_vjp
    def f(x, fv, gate):
        return fwd_call(x, fv, gate)

    def fwd(x, fv, gate):
        return fwd_call(x, fv, gate), (fv, gate)

    def bwd(res, dy):
        fv, gate = res
        df, acc = bwd_call(dy, fv, gate)
        return dy, df, acc[0:2]

    f.defvjp(fwd, bwd)
    return f


def _attn_fwd_call(q, k, v):
    h, tq_all, dk = q.shape
    tk_all, dv = k.shape[1], v.shape[2]
    tq = _tile(tq_all, 128, SUBLANES)
    scale = dk ** -0.5

    def body(q_ref, k_ref, v_ref, o_ref, lse_ref, kb, vb):
        @pl.when(pl.program_id(1) == 0)
        def _():
            kb[...] = k_ref[0].astype(BF16)
            vb[...] = v_ref[0].astype(BF16)

        s = lax.dot_general(q_ref[0].astype(BF16), kb[...], (((1,), (1,)), ((), ())), preferred_element_type=F32) * scale
        m = jnp.max(s, axis=-1, keepdims=True)
        p = jnp.exp(s - m)
        l = jnp.sum(p, axis=-1, keepdims=True)
        o_ref[0] = jnp.dot(p.astype(BF16), vb[...], preferred_element_type=F32) / l
        lse_ref[0] = m + jnp.log(l)

    return pl.pallas_call(
        body, name='mla_attn_fwd', grid=(h, tq_all // tq),
        in_specs=[pl.BlockSpec((1, tq, dk), lambda a, i: (a, i, 0)),
                  pl.BlockSpec((1, tk_all, dk), lambda a, i: (a, 0, 0)),
                  pl.BlockSpec((1, tk_all, dv), lambda a, i: (a, 0, 0))],
        out_specs=[pl.BlockSpec((1, tq, dv), lambda a, i: (a, i, 0)), pl.BlockSpec((1, tq, 1), lambda a, i: (a, i, 0))],
        out_shape=[jax.ShapeDtypeStruct((h, tq_all, dv), F32), jax.ShapeDtypeStruct((h, tq_all, 1), F32)],
        scratch_shapes=[pltpu.VMEM((tk_all, dk), BF16), pltpu.VMEM((tk_all, dv), BF16)],
        compiler_params=_cparams(('arbitrary', 'arbitrary')),
    )(q, k, v)


def _attn_dq_call(q, k, v, o, lse, do):
    h, tq_all, dk = q.shape
    tk_all, dv = k.shape[1], v.shape[2]
    tq = _tile(tq_all, 128, SUBLANES)
    scale = dk ** -0.5

    def body(q_ref, k_ref, v_ref, o_ref, lse_ref, do_ref, dq_ref, delta_ref, kb, vb):
        @pl.when(pl.program_id(1) == 0)
        def _():
            kb[...] = k_ref[0].astype(BF16)
            vb[...] = v_ref[0].astype(BF16)

        dov = do_ref[0]
        delta = jnp.sum(dov * o_ref[0], axis=-1, keepdims=True)
        s = lax.dot_general(q_ref[0].astype(BF16), kb[...], (((1,), (1,)), ((), ())), preferred_element_type=F32) * scale
        p = jnp.exp(s - lse_ref[0])
        dp = lax.dot_general(dov.astype(BF16), vb[...], (((1,), (1,)), ((), ())), preferred_element_type=F32)
        ds = p * (dp - delta) * scale
        dq_ref[0] = jnp.dot(ds.astype(BF16), kb[...], preferred_element_type=F32)
        delta_ref[0] = delta

    qspec = lambda d: pl.BlockSpec((1, tq, d), lambda a, i: (a, i, 0))
    return pl.pallas_call(
        body, name='mla_attn_dq', grid=(h, tq_all // tq),
        in_specs=[qspec(dk), pl.BlockSpec((1, tk_all, dk), lambda a, i: (a, 0, 0)),
                  pl.BlockSpec((1, tk_all, dv), lambda a, i: (a, 0, 0)), qspec(dv), qspec(1), qspec(dv)],
        out_specs=[qspec(dk), qspec(1)],
        out_shape=[jax.ShapeDtypeStruct((h, tq_all, dk), F32), jax.ShapeDtypeStruct((h, tq_all, 1), F32)],
        scratch_shapes=[pltpu.VMEM((tk_all, dk), BF16), pltpu.VMEM((tk_all, dv), BF16)],
        compiler_params=_cparams(('arbitrary', 'arbitrary')),
    )(q, k, v, o, lse, do)


def _attn_dkv_call(q, k, v, lse_row, delta_row, do):
    h, tq_all, dk = q.shape
    tk_all, dv = k.shape[1], v.shape[2]
    tk = _tile(tk_all, 128, SUBLANES)
    scale = dk ** -0.5

    def body(q_ref, k_ref, v_ref, lse_ref, delta_ref, do_ref, dk_ref, dv_ref, qb, dob):
        @pl.when(pl.program_id(1) == 0)
        def _():
            qb[...] = q_ref[0].astype(BF16)
            dob[...] = do_ref[0].astype(BF16)

        st = lax.dot_general(k_ref[0].astype(BF16), qb[...], (((1,), (1,)), ((), ())), preferred_element_type=F32) * scale
        pt = jnp.exp(st - lse_ref[0])
        dv_ref[0] = jnp.dot(pt.astype(BF16), dob[...], preferred_element_type=F32)
        dpt = lax.dot_general(v_ref[0].astype(BF16), dob[...], (((1,), (1,)), ((), ())), preferred_element_type=F32)
        dst = pt * (dpt - delta_ref[0]) * scale
        dk_ref[0] = jnp.dot(dst.astype(BF16), qb[...], preferred_element_type=F32)

    kspec = lambda d: pl.BlockSpec((1, tk, d), lambda a, j: (a, j, 0))
    full = lambda r, d: pl.BlockSpec((1, r, d), lambda a, j: (a, 0, 0))
    return pl.pallas_call(
        body, name='mla_attn_dkv', grid=(h, tk_all // tk),
        in_specs=[full(tq_all, dk), kspec(dk), kspec(dv), full(1, tq_all), full(1, tq_all), full(tq_all, dv)],
        out_specs=[kspec(dk), kspec(dv)],
        out_shape=[jax.ShapeDtypeStruct((h, tk_all, dk), F32), jax.ShapeDtypeStruct((h, tk_all, dv), F32)],
        scratch_shapes=[pltpu.VMEM((tq_all, dk), BF16), pltpu.VMEM((tq_all, dv), BF16)],
        compiler_params=_cparams(('arbitrary', 'arbitrary')),
    )(q, k, v, lse_row, delta_row, do)


@jax.custom_vjp
def dense_attn(q, k, v):
    return _attn_fwd_call(q, k, v)[0]


def _dense_attn_fwd(q, k, v):
    o, lse = _attn_fwd_call(q, k, v)
    return o, (q, k, v, o, lse)


def _dense_attn_bwd(res, do):
    q, k, v, o, lse = res
    h, tq_all, _ = q.shape
    dq, delta = _attn_dq_call(q, k, v, o, lse, do)
    dk, dv = _attn_dkv_call(q, k, v, lse.reshape(h, 1, tq_all), delta.reshape(h, 1, tq_all), do)
    return dq, dk, dv


dense_attn.defvjp(_dense_attn_fwd, _dense_attn_bwd)


def _gqa_specs(band, c_len, nb, rows):
    cb = c_len // BLOCK
    q_spec = pl.BlockSpec((1, GQA_G, rows, GQA_D), lambda a, b: (a, 0, b, 0))
    ctx_spec = pl.BlockSpec((1, c_len, GQA_D), lambda a, b: (a, 0, 0))
    kv_specs = [ctx_spec]
    if band:
        kv_specs += [pl.BlockSpec((1, BLOCK, GQA_D), lambda a, b: (a, jnp.maximum(b - 1, 0) + cb, 0)),
                     pl.BlockSpec((1, BLOCK, GQA_D), lambda a, b: (a, b + cb, 0)),
                     pl.BlockSpec((1, BLOCK, GQA_D), lambda a, b: (a, jnp.minimum(b + 1, nb - 1) + cb, 0))]
    sink_spec = pl.BlockSpec((1, GQA_G * rows, 1), lambda a, b: (a, 0, 0))
    return q_spec, kv_specs, sink_spec


def _gqa_scores(q, kcat, sink, band, c_len, t_len, rows):
    scale = GQA_D ** -0.5
    s = lax.dot_general(q, kcat, (((1,), (1,)), ((), ())), preferred_element_type=F32) * scale
    if band:
        b = pl.program_id(1)
        shape = s.shape
        col = lax.broadcasted_iota(jnp.int32, shape, 1)
        qpos = b * BLOCK + (lax.broadcasted_iota(jnp.int32, shape, 0) & (BLOCK - 1))
        kpos = (b - 1) * BLOCK + (col - c_len)
        valid = (col < c_len) | ((jnp.abs(qpos - kpos) <= WINDOW) & (kpos >= 0) & (kpos < t_len))
        s = jnp.where(valid, s, NEG_INF)
    m = jnp.maximum(jnp.max(s, axis=-1, keepdims=True), sink)
    e = jnp.exp(s - m)
    es = jnp.exp(sink - m)
    den = es + jnp.sum(e, axis=-1, keepdims=True)
    return e / den, es / den


def _gqa_fwd_call(q4, k2, v2, sink_rows, band, c_len):
    kv, g, tq_all, d = q4.shape
    rows = BLOCK if band else tq_all
    nb = tq_all // rows
    t_len = k2.shape[1] - c_len
    nkv = 4 if band else 1
    q_spec, kv_specs, sink_spec = _gqa_specs(band, c_len, nb, rows)

    def body(*refs):
        q_ref, sink_ref, o_ref = refs[0], refs[1 + 2 * nkv], refs[-1]
        kcat = jnp.concatenate([r[0] for r in refs[1:1 + nkv]], axis=0).astype(BF16)
        vcat = jnp.concatenate([r[0] for r in refs[1 + nkv:1 + 2 * nkv]], axis=0).astype(BF16)
        q = q_ref[0].reshape(g * rows, d).astype(BF16)
        p, _ = _gqa_scores(q, kcat, sink_ref[0], band, c_len, t_len, rows)
        o_ref[0] = jnp.dot(p.astype(BF16), vcat, preferred_element_type=F32).reshape(g, rows, d)

    return pl.pallas_call(
        body, name='gqa_fwd_band' if band else 'gqa_fwd_ctx', grid=(kv, nb),
        in_specs=[q_spec] + kv_specs + kv_specs + [sink_spec], out_specs=q_spec,
        out_shape=jax.ShapeDtypeStruct(q4.shape, F32), compiler_params=_cparams(('parallel', 'parallel')),
    )(q4, *([k2] * nkv), *([v2] * nkv), sink_rows)


def _gqa_bwd_call(q4, k2, v2, sink_rows, do4, band, c_len):
    kv, g, tq_all, d = q4.shape
    rows = BLOCK if band else tq_all
    nb = tq_all // rows
    t_len = k2.shape[1] - c_len
    nkv = 4 if band else 1
    scale = GQA_D ** -0.5
    q_spec, kv_specs, sink_spec = _gqa_specs(band, c_len, nb, rows)

    def body(*refs):
        q_ref, sink_ref, do_ref = refs[0], refs[1 + 2 * nkv], refs[2 + 2 * nkv]
        outs = refs[3 + 2 * nkv:]
        dq_ref, dkc_ref, dvc_ref = outs[0], outs[1], outs[2]
        dsink_ref = outs[-1]
        b = pl.program_id(1)
        kcat = jnp.concatenate([r[0] for r in refs[1:1 + nkv]], axis=0).astype(BF16)
        vcat = jnp.concatenate([r[0] for r in refs[1 + nkv:1 + 2 * nkv]], axis=0).astype(BF16)
        q = q_ref[0].reshape(g * rows, d).astype(BF16)
        do = do_ref[0].reshape(g * rows, d).astype(BF16)
        p, p_sink = _gqa_scores(q, kcat, sink_ref[0], band, c_len, t_len, rows)
        dp = lax.dot_general(do, vcat, (((1,), (1,)), ((), ())), preferred_element_type=F32)
        rd = jnp.sum(p * dp, axis=-1, keepdims=True)
        ds = (p * (dp - rd) * scale).astype(BF16)
        dq_ref[0] = jnp.dot(ds, kcat, preferred_element_type=F32).reshape(g, rows, d)
        dkcat = lax.dot_general(ds, q, (((0,), (0,)), ((), ())), preferred_element_type=F32)
        dvcat = lax.dot_general(p.astype(BF16), do, (((0,), (0,)), ((), ())), preferred_element_type=F32)

        @pl.when(b == 0)
        def _():
            dkc_ref[...] = jnp.zeros_like(dkc_ref)
            dvc_ref[...] = jnp.zeros_like(dvc_ref)
            dsink_ref[...] = jnp.zeros_like(dsink_ref)

        dkc_ref[0] += dkcat[:c_len]
        dvc_ref[0] += dvcat[:c_len]
        dsink_ref[0] += -p_sink * rd
        if band:
            outs[3][0, 0] = dkcat[c_len:]
            outs[4][0, 0] = dvcat[c_len:]

    ctx_out = pl.BlockSpec((1, c_len, d), lambda a, b: (a, 0, 0))
    band_out = pl.BlockSpec((1, 1, 3 * BLOCK, d), lambda a, b: (a, b, 0, 0))
    out_specs = [q_spec, ctx_out, ctx_out] + ([band_out, band_out] if band else []) + [sink_spec]
    ctx_shape = jax.ShapeDtypeStruct((kv, c_len, d), F32)
    band_shape = jax.ShapeDtypeStruct((kv, nb, 3 * BLOCK, d), F32)
    out_shape = ([jax.ShapeDtypeStruct(q4.shape, F32), ctx_shape, ctx_shape] + ([band_shape, band_shape] if band else [])
                 + [jax.ShapeDtypeStruct(sink_rows.shape, F32)])
    return pl.pallas_call(
        body, name='gqa_bwd_band' if band else 'gqa_bwd_ctx', grid=(kv, nb),
        in_specs=[q_spec] + kv_specs + kv_specs + [sink_spec, q_spec], out_specs=out_specs, out_shape=out_shape,
        compiler_params=_cparams(('arbitrary', 'arbitrary')),
    )(q4, *([k2] * nkv), *([v2] * nkv), sink_rows, do4)


def _make_gqa(band, c_len):
    @jax.custom_vjp
    def f(q4, k2, v2, sink_rows):
        return _gqa_fwd_call(q4, k2, v2, sink_rows, band, c_len)

    def fwd(q4, k2, v2, sink_rows):
        return _gqa_fwd_call(q4, k2, v2, sink_rows, band, c_len), (q4, k2, v2, sink_rows)

    def bwd(res, do4):
        q4, k2, v2, sink_rows = res
        outs = _gqa_bwd_call(q4, k2, v2, sink_rows, do4, band, c_len)
        kv, n, d = k2.shape
        if not band:
            dq4, dkc, dvc, dsink = outs
            return dq4, dkc, dvc, dsink
        dq4, dkc, dvc, dkb, dvb, dsink = outs

        def fold(ctx_part, bands):
            cur = bands[:, :, BLOCK:2 * BLOCK]
            prv = jnp.pad(bands[:, 1:, :BLOCK], ((0, 0), (0, 1), (0, 0), (0, 0)))
            nxt = jnp.pad(bands[:, :-1, 2 * BLOCK:], ((0, 0), (1, 0), (0, 0), (0, 0)))
            lat = (cur + prv + nxt).reshape(kv, n - c_len, d)
            return jnp.concatenate([ctx_part, lat], axis=1)

        return dq4, fold(dkc, dkb), fold(dvc, dvb), dsink

    f.defvjp(fwd, bwd)
    return f


def _cmul(ar, ai, br, bi):
    return ar * br - ai * bi, ar * bi + ai * br


def _scan_tables(ar, ai, desc):
    a1 = (ar, ai)
    a2 = _cmul(*a1, *a1)
    a4 = _cmul(*a2, *a2)
    pw = [a1]
    for _ in range(SUBLANES - 1):
        pw.append(_cmul(*pw[-1], *a1))
    row = jnp.arange(SUBLANES)[:, None]
    tabs = []
    for dist, (pr, pi) in ((1, a1), (2, a2), (4, a4)):
        keep = (row <= SUBLANES - 1 - dist) if desc else (row >= dist)
        tabs += [jnp.where(keep, pr[None, :], 0.0), jnp.where(keep, pi[None, :], 0.0)]
    order = pw[::-1] if desc else pw
    tabs += [jnp.stack([p[0] for p in order]), jnp.stack([p[1] for p in order])]
    return jnp.stack(tabs).astype(F32)


def _scan_call(b_re, b_im, tabs, order, chunk, prev=None):
    n, s_dim = b_re.shape
    nch = n // chunk
    ng = chunk // SUBLANES
    desc = order in ('Fb', 'R')
    with_da = prev is not None

    def chunk_of(i):
        if order == 'F':
            return i
        if order == 'Fb':
            return nch - 1 - i
        if order == 'R':
            return jnp.where(i == 0, 0, nch - i)
        return jnp.where(i == nch - 1, 0, i + 1)

    def body(*refs):
        br_ref, bi_ref, tab_ref = refs[0], refs[1], refs[2]
        if with_da:
            pr_ref, pi_ref, sr_ref, si_ref, dar_ref, dai_ref, cr_ref, ci_ref = refs[3:]
        else:
            sr_ref, si_ref, pr_ref, pi_ref, cr_ref, ci_ref = refs[3:]

        @pl.when(pl.program_id(0) == 0)
        def _():
            cr_ref[...] = jnp.zeros_like(cr_ref)
            ci_ref[...] = jnp.zeros_like(ci_ref)
            if with_da:
                dar_ref[...] = jnp.zeros_like(dar_ref)
                dai_ref[...] = jnp.zeros_like(dai_ref)

        sub = lax.broadcasted_iota(jnp.int32, (SUBLANES, s_dim), 0)
        edge = SUBLANES - 1 if desc else 0
        last = 0 if desc else SUBLANES - 1

        def step(t, carry):
            gi = (ng - 1 - t) if desc else t
            rows = pl.ds(pl.multiple_of(gi * SUBLANES, SUBLANES), SUBLANES)
            xr, xi = br_ref[rows, :], bi_ref[rows, :]
            for j, dist in enumerate((1, 2, 4)):
                shift = SUBLANES - dist if desc else dist
                rr, ri = pltpu.roll(xr, shift, 0), pltpu.roll(xi, shift, 0)
                mr, mi = tab_ref[2 * j], tab_ref[2 * j + 1]
                xr, xi = xr + mr * rr - mi * ri, xi + mr * ri + mi * rr
            cr, ci = cr_ref[...], ci_ref[...]
            pwr, pwi = tab_ref[6], tab_ref[7]
            sr = xr + pwr * cr - pwi * ci
            si = xi + pwr * ci + pwi * cr
            sr_ref[rows, :] = sr
            si_ref[rows, :] = si
            if with_da:
                pr, pi = pr_ref[rows, :], pi_ref[rows, :]
                dar_ref[...] += sr * pr + si * pi
                dai_ref[...] += si * pr - sr * pi
            else:
                shift1 = SUBLANES - 1 if desc else 1
                pr_ref[rows, :] = jnp.where(sub == edge, cr, pltpu.roll(sr, shift1, 0))
                pi_ref[rows, :] = jnp.where(sub == edge, ci, pltpu.roll(si, shift1, 0))
            cr_ref[...] = jnp.broadcast_to(sr[last:last + 1, :], (SUBLANES, s_dim))
            ci_ref[...] = jnp.broadcast_to(si[last:last + 1, :], (SUBLANES, s_dim))
            return carry

        lax.fori_loop(0, ng, step, 0)

    blk = pl.BlockSpec((chunk, s_dim), lambda i: (chunk_of(i), 0))
    tab_spec = pl.BlockSpec((8, SUBLANES, s_dim), lambda i: (0, 0, 0))
    acc = pl.BlockSpec((SUBLANES, s_dim), lambda i: (0, 0))
    big = jax.ShapeDtypeStruct((n, s_dim), F32)
    small = jax.ShapeDtypeStruct((SUBLANES, s_dim), F32)
    if with_da:
        in_specs, ins = [blk, blk, tab_spec, blk, blk], [b_re, b_im, tabs, prev[0], prev[1]]
        out_specs, out_shape = [blk, blk, acc, acc], [big, big, small, small]
    else:
        in_specs, ins = [blk, blk, tab_spec], [b_re, b_im, tabs]
        out_specs, out_shape = [blk, blk, blk, blk], [big, big, big, big]
    return pl.pallas_call(
        body, name='s5_scan_' + order, grid=(nch,), in_specs=in_specs, out_specs=out_specs, out_shape=out_shape,
        scratch_shapes=[pltpu.VMEM((SUBLANES, s_dim), F32), pltpu.VMEM((SUBLANES, s_dim), F32)],
        compiler_params=_cparams(('arbitrary',)),
    )(*ins)


def _make_scan(rev, chunk):
    def run(b_re, b_im, ar, ai):
        tabs = _scan_tables(ar, ai, desc=rev)
        return _scan_call(b_re, b_im, tabs, 'R' if rev else 'F', chunk)

    @jax.custom_vjp
    def f(b_re, b_im, ar, ai):
        return tuple(run(b_re, b_im, ar, ai)[:2])

    def fwd(b_re, b_im, ar, ai):
        s_re, s_im, p_re, p_im = run(b_re, b_im, ar, ai)
        return (s_re, s_im), (p_re, p_im, ar, ai)

    def bwd(res, g):
        p_re, p_im, ar, ai = res
        tabs = _scan_tables(ar, -ai, desc=not rev)
        db_re, db_im, dar, dai = _scan_call(g[0], g[1], tabs, 'Rb' if rev else 'Fb', chunk, prev=(p_re, p_im))
        return db_re, db_im, jnp.sum(dar, axis=0), jnp.sum(dai, axis=0)

    f.defvjp(fwd, bwd)
    return f


def _sqerr_call(y, t):
    n, d = y.shape
    rb = _tile(n, 512, SUBLANES)

    def body(y_ref, t_ref, o_ref):
        @pl.when(pl.program_id(0) == 0)
        def _():
            o_ref[...] = jnp.zeros_like(o_ref)

        e = y_ref[...] - t_ref[...]
        o_ref[...] += jnp.sum(e * e, axis=0, keepdims=True)

    row = pl.BlockSpec((rb, d), lambda i: (i, 0))
    return pl.pallas_call(
        body, name='sq_err', grid=(n // rb,), in_specs=[row, row], out_specs=pl.BlockSpec((1, d), lambda i: (0, 0)),
        out_shape=jax.ShapeDtypeStruct((1, d), F32), compiler_params=_cparams(('arbitrary',)),
    )(y, t)


@jax.custom_vjp
def loss_head(y, t):
    return 0.5 * jnp.sum(_sqerr_call(y, t)) / y.shape[1]


def _loss_head_fwd(y, t):
    return loss_head(y, t), (y, t)


def _loss_head_bwd(res, g):
    y, t = res
    return g * (y - t) / y.shape[1], None


loss_head.defvjp(_loss_head_fwd, _loss_head_bwd)


def _adamw_call(w, g, m, v):
    r, c = w.shape
    rb = _tile(r, max(SUBLANES, (256 * 1024) // max(c, LANES) // SUBLANES * SUBLANES), SUBLANES)

    def body(w_ref, g_ref, m_ref, v_ref, d_ref, nm_ref, nv_ref):
        gv = g_ref[...]
        nm = ADAM_B1 * m_ref[...] + (1.0 - ADAM_B1) * gv
        nv = ADAM_B2 * v_ref[...] + (1.0 - ADAM_B2) * (gv * gv)
        m_hat = nm / (1.0 - ADAM_B1 ** ADAM_STEP)
        v_hat = nv / (1.0 - ADAM_B2 ** ADAM_STEP)
        d_ref[...] = -ADAM_LR * (m_hat / (jnp.sqrt(v_hat) + ADAM_EPS) + ADAM_WD * w_ref[...])
        nm_ref[...] = nm
        nv_ref[...] = nv

    blk = pl.BlockSpec((rb, c), lambda i: (i, 0))
    shape = jax.ShapeDtypeStruct((r, c), F32)
    return pl.pallas_call(
        body, name='adamw', grid=(r // rb,), in_specs=[blk] * 4, out_specs=[blk] * 3, out_shape=[shape] * 3,
        compiler_params=_cparams(('parallel',)),
    )(w, g, m, v)


MESH = pl.DeviceIdType.MESH
HBM_SPEC = pl.BlockSpec(memory_space=pltpu.HBM)


def _all_gather(x):
    def body(x_ref, out_ref, send_sems, recv_sems, local_sem):
        x, y, c = lax.axis_index('x'), lax.axis_index('y'), lax.axis_index('c')
        me, sibling = (x, y, c), (x, y, 1 - c)
        chips = [(1 - x, y), (x, 1 - y), (1 - x, 1 - y)]

        def slot(px, py, pc):
            return out_ref.at[4 * px + 2 * py + pc]

        def copy(k, block, to, src=None):
            return pltpu.make_async_remote_copy(
                src_ref=slot(*block) if src is None else src, dst_ref=slot(*block),
                send_sem=send_sems.at[k], recv_sem=recv_sems.at[k], device_id=to, device_id_type=MESH)

        mine = pltpu.make_async_copy(x_ref, slot(*me), local_sem)
        mine.start()
        first = [copy(0, me, sibling, src=x_ref)]
        first += [copy(1 + j, me, (*chip, c), src=x_ref) for j, chip in enumerate(chips)]
        for cp in first:
            cp.start()
        passed = [copy(4 + j, (*chip, c), sibling) for j, chip in enumerate(chips)]
        for j, chip in enumerate(chips):
            copy(1 + j, (*chip, c), me).wait_recv()
            passed[j].start()
        copy(0, sibling, me).wait_recv()
        for j, chip in enumerate(chips):
            copy(4 + j, (*chip, 1 - c), me).wait_recv()
        for cp in first + passed:
            cp.wait_send()
        mine.wait()

    return pl.pallas_call(
        body, name='all_gather', out_shape=jax.ShapeDtypeStruct((N_DEV,) + x.shape, x.dtype),
        in_specs=[HBM_SPEC], out_specs=HBM_SPEC,
        scratch_shapes=[pltpu.SemaphoreType.DMA((7,)), pltpu.SemaphoreType.DMA((7,)), pltpu.SemaphoreType.DMA],
    )(x)


def _exchange_sibling(g_all):
    def body(g_ref, out_ref, send_sem, recv_sem):
        x, y, c = lax.axis_index('x'), lax.axis_index('y'), lax.axis_index('c')
        cp = pltpu.make_async_remote_copy(src_ref=g_ref.at[1 - c], dst_ref=out_ref, send_sem=send_sem, recv_sem=recv_sem,
                                          device_id=(x, y, 1 - c), device_id_type=MESH)
        cp.start()
        cp.wait()

    return pl.pallas_call(
        body, name='rs_sibling', out_shape=jax.ShapeDtypeStruct(g_all.shape[1:], g_all.dtype),
        in_specs=[HBM_SPEC], out_specs=HBM_SPEC,
        scratch_shapes=[pltpu.SemaphoreType.DMA, pltpu.SemaphoreType.DMA],
    )(g_all)


def _exchange_chips(p):
    def body(p_ref, out_ref, send_sems, recv_sems):
        x, y, c = lax.axis_index('x'), lax.axis_index('y'), lax.axis_index('c')
        chips = [(1 - x, y), (x, 1 - y), (1 - x, 1 - y)]
        copies = [pltpu.make_async_remote_copy(src_ref=p_ref.at[2 * px + py], dst_ref=out_ref.at[j],
                                               send_sem=send_sems.at[j], recv_sem=recv_sems.at[j],
                                               device_id=(px, py, c), device_id_type=MESH)
                  for j, (px, py) in enumerate(chips)]
        for cp in copies:
            cp.start()
        for cp in copies:
            cp.wait_recv()
        for cp in copies:
            cp.wait_send()

    return pl.pallas_call(
        body, name='rs_chips', out_shape=jax.ShapeDtypeStruct((3,) + p.shape[1:], p.dtype),
        in_specs=[HBM_SPEC], out_specs=HBM_SPEC,
        scratch_shapes=[pltpu.SemaphoreType.DMA((3,)), pltpu.SemaphoreType.DMA((3,))],
    )(p)


def _add_sibling(g_all, recv, c_idx):
    _, nchip, r, _ = g_all.shape
    rb = _tile(r, PACK_ROWS, SUBLANES)

    def body(c_ref, g_ref, r_ref, o_ref):
        o_ref[...] = g_ref[0] + r_ref[...]

    return pl.pallas_call(
        body, name='rs_add_sibling',
        grid_spec=pltpu.PrefetchScalarGridSpec(
            num_scalar_prefetch=1, grid=(nchip, r // rb),
            in_specs=[pl.BlockSpec((1, 1, rb, LANES), lambda k, i, c: (c[0], k, i, 0)),
                      pl.BlockSpec((1, rb, LANES), lambda k, i, c: (k, i, 0))],
            out_specs=pl.BlockSpec((1, rb, LANES), lambda k, i, c: (k, i, 0))),
        out_shape=jax.ShapeDtypeStruct(recv.shape, F32), compiler_params=_cparams(('parallel', 'parallel')),
    )(c_idx, g_all, recv)


def _add_chips(p, recv, chip_idx):
    _, r, _ = p.shape
    rb = _tile(r, PACK_ROWS, SUBLANES)

    def body(k_ref, p_ref, r0, r1, r2, o_ref):
        o_ref[...] = ((p_ref[0] + r0[0]) + r1[0]) + r2[0]

    rspec = lambda j: pl.BlockSpec((1, rb, LANES), lambda i, k: (j, i, 0))
    return pl.pallas_call(
        body, name='rs_add_chips',
        grid_spec=pltpu.PrefetchScalarGridSpec(
            num_scalar_prefetch=1, grid=(r // rb,),
            in_specs=[pl.BlockSpec((1, rb, LANES), lambda i, k: (k[0], i, 0)), rspec(0), rspec(1), rspec(2)],
            out_specs=pl.BlockSpec((rb, LANES), lambda i, k: (i, 0))),
        out_shape=jax.ShapeDtypeStruct((r, LANES), F32), compiler_params=_cparams(('parallel',)),
    )(chip_idx, p, recv, recv, recv)


def _sum_devices(g):
    _, r, _ = g.shape
    rb = _tile(r, 512, SUBLANES)

    def body(g_ref, o_ref):
        acc = g_ref[0]
        for j in range(1, N_DEV):
            acc = acc + g_ref[j]
        o_ref[...] = acc

    return pl.pallas_call(
        body, name='sum_devices', grid=(r // rb,), in_specs=[pl.BlockSpec((N_DEV, rb, LANES), lambda i: (0, i, 0))],
        out_specs=pl.BlockSpec((rb, LANES), lambda i: (i, 0)), out_shape=jax.ShapeDtypeStruct((r, LANES), F32),
        compiler_params=_cparams(('parallel',)),
    )(g)


def _reduce_scatter(g_all, c_idx, chip_idx):
    part = _add_sibling(g_all, _exchange_sibling(g_all), c_idx)
    return _add_chips(part, _exchange_chips(part), chip_idx)


def _pad_to(n, mult):
    return (n + mult - 1) // mult * mult


def _pack(pieces, lead, dtype):
    flat = []
    total = 0
    for p in pieces:
        f = p.reshape(lead + (-1,)).astype(dtype)
        n = _pad_to(f.shape[-1], 16 * LANES)
        flat.append(jnp.pad(f, [(0, 0)] * len(lead) + [(0, n - f.shape[-1])]))
        total += n
    full = _pad_to(total, PACK_ROWS * LANES)
    if full > total:
        flat.append(jnp.zeros(lead + (full - total,), dtype))
    return jnp.concatenate(flat, axis=-1).reshape(lead + (full // LANES, LANES))


def _unpack(buf, lead, shapes):
    flat = buf.reshape(lead + (-1,))
    out, off = [], 0
    for s in shapes:
        n = math.prod(s)
        out.append(flat[..., off:off + n].reshape(lead + tuple(s)))
        off += _pad_to(n, 16 * LANES)
    return out


def _rope_tables(c_len, t_len, n):
    quarter = n // 4
    inv = ROPE_BASE ** (-jnp.arange(0, 2 * quarter, 2, dtype=F32) / (2 * quarter))
    t = jnp.arange(t_len, dtype=jnp.int32)
    pos = jnp.stack([(t // GRID_W).astype(F32), (t % GRID_W).astype(F32)], axis=1)
    ang = pos[:, :, None] * inv[None, None, :]
    ang = jnp.concatenate([jnp.zeros((c_len, 2, quarter), F32), ang], axis=0)
    return jnp.cos(ang), jnp.sin(ang)


def _axial_rope(x, cos, sin):
    n_rows, h, n = x.shape
    xs = x.reshape(n_rows, h, 2, 2, n // 4)
    x1, x2 = xs[:, :, :, 0], xs[:, :, :, 1]
    c, s = cos[:, None], sin[:, None]
    return jnp.stack([x1 * c - x2 * s, x1 * s + x2 * c], axis=3).reshape(n_rows, h, n)


def _ssm_discretize(lam_re, lam_im, log_dt, b_re, b_im):
    dt = jnp.exp(log_dt)[:, None]
    mag = jnp.exp(lam_re * dt)
    a_re, a_im = mag * jnp.cos(lam_im * dt), mag * jnp.sin(lam_im * dt)
    den = lam_re * lam_re + lam_im * lam_im
    w_re = ((a_re - 1) * lam_re + a_im * lam_im) / den
    w_im = (a_im * lam_re - (a_re - 1) * lam_im) / den
    bb_re, bb_im = _cmul(w_re[..., None], w_im[..., None], b_re, b_im)
    return a_re, a_im, bb_re, bb_im


def _block_diag_in(b):
    g = b.shape[0]
    return jnp.einsum('gpm,gh->gmhp', b, jnp.eye(g, dtype=F32)).reshape(g * b.shape[2], g * b.shape[1])


def _block_diag_out(c):
    g = c.shape[0]
    return jnp.einsum('gmp,gh->gphm', c, jnp.eye(g, dtype=F32)).reshape(g * c.shape[2], g * c.shape[1])


def _w_in_layout(d_model):
    sizes = (MLA_Q_RANK, MLA_KV_RANK, MLA_ROPE, SSM_WIDTH, GQA_HEADS * GQA_D, GQA_KV * GQA_D, GQA_KV * GQA_D, 3 * d_model)
    starts = [0]
    for s in sizes[:-1]:
        starts.append(starts[-1] + s)
    names = ('cq', 'ckv', 'kr', 'u', 'gq', 'gk', 'gv', 'gates')
    orig = dict(zip(names, zip(starts, sizes)))
    order = ('cq', 'ckv', 'u', 'gq', 'gk', 'gv', 'gates', 'kr')
    return orig, order


def _permute_w_in(w, d_model):
    orig, order = _w_in_layout(d_model)
    cols = [w[:, orig[k][0]:orig[k][0] + orig[k][1]] for k in order]
    width = sum(orig[k][1] for k in order)
    return jnp.pad(jnp.concatenate(cols, axis=1), ((0, 0), (0, _pad_to(width, LANES) - width)))


def _unpermute_w_in(gp, d_model):
    orig, order = _w_in_layout(d_model)
    off, parts = 0, {}
    for k in order:
        parts[k] = gp[:, off:off + orig[k][1]]
        off += orig[k][1]
    return jnp.concatenate([parts[k] for k in ('cq', 'ckv', 'kr', 'u', 'gq', 'gk', 'gv', 'gates')], axis=1)


def _forward_loss(x_all, wl, rp, cc_in, target, c_len):
    n, d = x_all.shape
    t_len = n - c_len
    depth = len(wl)
    norm_mod = _make_norm(c_len, True)
    norm_tok = _make_norm(_tile(n, 512, SUBLANES), False)
    norm_out = _make_norm(_tile(t_len, 512, SUBLANES), False)
    half_res = _make_gated_res(c_len, 0.5)
    full_res = _make_gated_res(c_len, 1.0)
    gqa_band = _make_gqa(True, c_len)
    gqa_ctx = _make_gqa(False, c_len)
    scans = (_make_scan(False, c_len), _make_scan(True, c_len))
    cos_m, sin_m = _rope_tables(c_len, t_len, MLA_ROPE)
    cos_g, sin_g = _rope_tables(c_len, t_len, GQA_D)
    orig, order = _w_in_layout(d)
    offs, off = {}, 0
    for k in order:
        offs[k] = (off, orig[k][1])
        off += orig[k][1]

    cc = jnp.zeros((SUBLANES, d), F32).at[0].set(jax.nn.silu(rp['c_ctx'])).at[1].set(jax.nn.silu(cc_in))

    def swiglu(h, w13, w2):
        a13 = matmul(h, w13)
        f = a13.shape[1] // 2
        return matmul(jax.nn.silu(a13[:, :f]) * a13[:, f:], w2)

    for l in range(depth):
        w = wl[l]
        ctx_out = l < depth - 1
        mod = matmul(cc, w['ada_w']) + rp['ada_b'][l][None, :]
        md = [mod[0:2, i * d:(i + 1) * d] for i in range(N_MOD)]
        x_all = half_res(x_all, swiglu(norm_mod(x_all, rp['norm_ffn1'][l], md[0], md[1]), w['ffn1_w13'], w['ffn1_w2']), md[2])

        z = matmul(norm_mod(x_all, rp['norm_mix'][l], md[3], md[4]), w['w_in'])
        part = {k: z[:, o:o + s] for k, (o, s) in offs.items()}

        q = matmul(norm_tok(part['cq'], rp['mla_q_norm'][l]), w['mla_w_uq']).reshape(n, MLA_HEADS, MLA_NOPE + MLA_ROPE)
        q = jnp.concatenate([q[..., :MLA_NOPE], _axial_rope(q[..., MLA_NOPE:], cos_m, sin_m)], axis=-1)
        kvp = matmul(norm_tok(part['ckv'], rp['mla_kv_norm'][l]), w['mla_w_ukv']).reshape(n, MLA_HEADS, MLA_NOPE + MLA_V)
        kr = _axial_rope(part['kr'].reshape(n, 1, MLA_ROPE), cos_m, sin_m)
        k = jnp.concatenate([kvp[..., :MLA_NOPE], jnp.broadcast_to(kr, (n, MLA_HEADS, MLA_ROPE))], axis=-1)
        qh, kh, vh = (jnp.transpose(a, (1, 0, 2)) for a in (q, k, kvp[..., MLA_NOPE:]))
        o_lat = dense_attn(qh[:, c_len:], kh, vh)
        if ctx_out:
            o_ctx = dense_attn(qh[:, :c_len], kh[:, :c_len], vh[:, :c_len])
        else:
            o_ctx = jnp.zeros((MLA_HEADS, c_len, MLA_V), F32)
        o = jnp.transpose(jnp.concatenate([o_ctx, o_lat], axis=1), (1, 0, 2)).reshape(n, MLA_HEADS * MLA_V)
        mla = matmul(o, w['mla_w_o'])

        u = part['u']
        y = u * rp['ssm_d'][l][None, :]
        for direction in range(2):
            a_re, a_im, bb_re, bb_im = _ssm_discretize(
                rp['ssm_lambda_re'][l, direction], rp['ssm_lambda_im'][l, direction], rp['ssm_log_dt'][l, direction],
                rp['ssm_b_re'][l, direction], rp['ssm_b_im'][l, direction])
            s_re, s_im = scans[direction](matmul(u, _block_diag_in(bb_re)), matmul(u, _block_diag_in(bb_im)),
                                          a_re.reshape(-1), a_im.reshape(-1))
            y = y + (matmul(s_re, _block_diag_out(rp['ssm_c_re'][l, direction]))
                     - matmul(s_im, _block_diag_out(rp['ssm_c_im'][l, direction])))
        yg = matmul(jax.nn.gelu(y), w['ssm_w_glu'])
        ssm = yg[:, :d] * jax.nn.sigmoid(yg[:, d:])

        gq = _axial_rope(part['gq'].reshape(n, GQA_HEADS, GQA_D), cos_g, sin_g)
        gk = _axial_rope(part['gk'].reshape(n, GQA_KV, GQA_D), cos_g, sin_g)
        q4 = jnp.transpose(gq.reshape(n, GQA_KV, GQA_G, GQA_D), (1, 2, 0, 3))
        k2 = jnp.transpose(gk, (1, 0, 2))
        v2 = jnp.transpose(part['gv'].reshape(n, GQA_KV, GQA_D), (1, 0, 2))
        sink = rp['gqa_sink'][l].reshape(GQA_KV, GQA_G, 1, 1)
        sink_rows = lambda rows: jnp.broadcast_to(sink, (GQA_KV, GQA_G, rows, 1)).reshape(GQA_KV, GQA_G * rows, 1)
        g_lat = gqa_band(q4[:, :, c_len:], k2, v2, sink_rows(BLOCK))
        if ctx_out:
            g_ctx = gqa_ctx(q4[:, :, :c_len], k2[:, :c_len], v2[:, :c_len], sink_rows(c_len))
        else:
            g_ctx = jnp.zeros((GQA_KV, GQA_G, c_len, GQA_D), F32)
        go = jnp.transpose(jnp.concatenate([g_ctx, g_lat], axis=2), (2, 0, 1, 3)).reshape(n, GQA_HEADS * GQA_D)
        gqa = matmul(go, w['gqa_w_o'])

        gates = jax.nn.sigmoid(part['gates'])
        mixed = gates[:, :d] * mla + gates[:, d:2 * d] * ssm + gates[:, 2 * d:] * gqa
        x_all = full_res(x_all, matmul(mixed, w['w_out']), md[5])
        x_all = half_res(x_all, swiglu(norm_mod(x_all, rp['norm_ffn2'][l], md[6], md[7]), w['ffn2_w13'], w['ffn2_w2']), md[8])

    return loss_head(norm_out(x_all[c_len:], rp['final_norm']), target)


def kernel(x, c, ctx, c_ctx, ada_w, ada_b, norm_ffn1, norm_mix, norm_ffn2, ffn1_w13, ffn1_w2, ffn2_w13, ffn2_w2, w_in, mla_q_norm, mla_kv_norm, mla_w_uq, mla_w_ukv, mla_w_o, ssm_lambda_re, ssm_lambda_im, ssm_log_dt, ssm_b_re, ssm_b_im, ssm_c_re, ssm_c_im, ssm_d, ssm_w_glu, gqa_sink, gqa_w_o, w_out, final_norm, loss_target, m_c_ctx, m_ada_w, m_ada_b, m_norm_ffn1, m_norm_mix, m_norm_ffn2, m_ffn1_w13, m_ffn1_w2, m_ffn2_w13, m_ffn2_w2, m_w_in, m_mla_q_norm, m_mla_kv_norm, m_mla_w_uq, m_mla_w_ukv, m_mla_w_o, m_ssm_lambda_re, m_ssm_lambda_im, m_ssm_log_dt, m_ssm_b_re, m_ssm_b_im, m_ssm_c_re, m_ssm_c_im, m_ssm_d, m_ssm_w_glu, m_gqa_sink, m_gqa_w_o, m_w_out, m_final_norm, v_c_ctx, v_ada_w, v_ada_b, v_norm_ffn1, v_norm_mix, v_norm_ffn2, v_ffn1_w13, v_ffn1_w2, v_ffn2_w13, v_ffn2_w2, v_w_in, v_mla_q_norm, v_mla_kv_norm, v_mla_w_uq, v_mla_w_ukv, v_mla_w_o, v_ssm_lambda_re, v_ssm_lambda_im, v_ssm_log_dt, v_ssm_b_re, v_ssm_b_im, v_ssm_c_re, v_ssm_c_im, v_ssm_d, v_ssm_w_glu, v_gqa_sink, v_gqa_w_o, v_w_out, v_final_norm):
    args = dict(locals())
    weights = {k: args[k] for k in WEIGHTS}
    moments_m = {k: args['m_' + k] for k in WEIGHTS}
    moments_v = {k: args['v_' + k] for k in WEIGHTS}
    depth = ada_w.shape[0]
    d = x.shape[-1]
    c_len = ctx.shape[1]
    my_c = lax.axis_index('c')
    my_chip = 2 * lax.axis_index('x') + lax.axis_index('y')

    shard_shapes = [weights[k].shape[1:] for k in SHARDED]
    layers = []
    for l in range(depth):
        gathered = _all_gather(_pack([weights[k][l] for k in SHARDED], (), BF16))
        full = {}
        for k, blocks in zip(SHARDED, _unpack(gathered, (N_DEV,), shard_shapes)):
            r, cdim = blocks.shape[1:]
            if k in ROW_SHARDED:
                full[k] = blocks.reshape(N_DEV * r, cdim).astype(F32)
            else:
                full[k] = jnp.moveaxis(blocks, 0, 1).reshape(r, N_DEV * cdim).astype(F32)
        full['w_in'] = _permute_w_in(full['w_in'], d)
        layers.append(full)

    rp = {k: weights[k] for k in REPLICATED}
    x_all = jnp.concatenate([ctx[0], x[0]], axis=0)
    loss_fn = functools.partial(_forward_loss, cc_in=c[0], target=loss_target[0], c_len=c_len)
    loss, vjp = jax.vjp(loss_fn, x_all, layers, rp)
    g_x, g_layers, g_rp = vjp(jnp.ones((), F32))

    grads = {k: [] for k in SHARDED}
    for l in range(depth):
        gl = dict(g_layers[l])
        gl['w_in'] = _unpermute_w_in(gl['w_in'], d)
        pieces = []
        for k in SHARDED:
            r, cdim = weights[k].shape[1:]
            g = gl[k]
            if k in ROW_SHARDED:
                pieces.append(g.reshape(N_DEV, r, cdim))
            else:
                pieces.append(jnp.moveaxis(g.reshape(r, N_DEV, cdim), 1, 0))
        packed = _pack(pieces, (N_DEV,), F32)
        g_all = jnp.swapaxes(packed.reshape((4, 2) + packed.shape[1:]), 0, 1)
        mine = _reduce_scatter(g_all, my_c.reshape(1).astype(jnp.int32), my_chip.reshape(1).astype(jnp.int32))
        for k, g in zip(SHARDED, _unpack(mine, (), shard_shapes)):
            grads[k].append(g)
    grads = {k: jnp.stack(v) for k, v in grads.items()}

    rep_shapes = [weights[k].shape for k in REPLICATED] + [(1,)]
    small = _pack([g_rp[k] for k in REPLICATED] + [loss.reshape(1)], (), F32)
    summed = _unpack(_sum_devices(_all_gather(small)), (), rep_shapes)
    for k, g in zip(REPLICATED, summed[:-1]):
        grads[k] = g
    loss_total = summed[-1].reshape(())

    delta, new_m, new_v = {}, {}, {}
    for k in SHARDED:
        shape = weights[k].shape
        as2d = lambda a: a.reshape(-1, shape[-1])
        outs = _adamw_call(as2d(weights[k]), as2d(grads[k]), as2d(moments_m[k]), as2d(moments_v[k]))
        delta[k], new_m[k], new_v[k] = (o.reshape(shape) for o in outs)
    rep_all = [weights[k].shape for k in REPLICATED]
    packs = [_pack([src[k] for k in REPLICATED], (), F32) for src in (weights, grads, moments_m, moments_v)]
    outs = [_unpack(o, (), rep_all) for o in _adamw_call(*packs)]
    for i, k in enumerate(REPLICATED):
        delta[k], new_m[k], new_v[k] = outs[0][i], outs[1][i], outs[2][i]

    return (loss_total, g_x[c_len:][None], *[grads[k] for k in WEIGHTS], *[delta[k] for k in WEIGHTS],
            *[new_m[k] for k in WEIGHTS], *[new_v[k] for k in WEIGHTS])
```

```python
import functools
import math

import jax
import jax.numpy as jnp
from jax import lax
from jax.experimental import pallas as pl
from jax.experimental.pallas import tpu as pltpu

F32 = jnp.float32
BF16 = jnp.bfloat16

MLA_HEADS, MLA_NOPE, MLA_ROPE, MLA_V = 8, 64, 32, 64
MLA_Q_RANK, MLA_KV_RANK = 384, 256
SSM_WIDTH, SSM_GROUP, SSM_STATE = 512, 16, 64
SSM_GROUPS = SSM_WIDTH // SSM_GROUP
GQA_HEADS, GQA_KV, GQA_D = 8, 2, 64
GQA_G = GQA_HEADS // GQA_KV
WINDOW, BLOCK, GRID_W = 128, 128, 64
N_MOD = 9
ROPE_BASE = 10000.0
EPS = 1e-6
NEG_INF = -1e30
ADAM_LR, ADAM_B1, ADAM_B2, ADAM_EPS, ADAM_WD, ADAM_STEP = 0.001, 0.9, 0.999, 1e-08, 0.01, 10

N_DEV = 8
LANES = 128
SUBLANES = 8
VMEM_LIMIT = 56 * 1024 * 1024
PACK_ROWS = 1024

SHARDED = ('ada_w', 'ffn1_w13', 'ffn1_w2', 'ffn2_w13', 'ffn2_w2', 'w_in', 'mla_w_uq', 'mla_w_ukv',
           'mla_w_o', 'ssm_w_glu', 'gqa_w_o', 'w_out')
ROW_SHARDED = ('ffn1_w2', 'ffn2_w2', 'w_out')
REPLICATED = ('c_ctx', 'ada_b', 'norm_ffn1', 'norm_mix', 'norm_ffn2', 'mla_q_norm', 'mla_kv_norm',
              'ssm_lambda_re', 'ssm_lambda_im', 'ssm_log_dt', 'ssm_b_re', 'ssm_b_im', 'ssm_c_re', 'ssm_c_im',
              'ssm_d', 'gqa_sink', 'final_norm')
WEIGHTS = ('c_ctx', 'ada_w', 'ada_b', 'norm_ffn1', 'norm_mix', 'norm_ffn2', 'ffn1_w13', 'ffn1_w2', 'ffn2_w13',
           'ffn2_w2', 'w_in', 'mla_q_norm', 'mla_kv_norm', 'mla_w_uq', 'mla_w_ukv', 'mla_w_o', 'ssm_lambda_re',
           'ssm_lambda_im', 'ssm_log_dt', 'ssm_b_re', 'ssm_b_im', 'ssm_c_re', 'ssm_c_im', 'ssm_d', 'ssm_w_glu',
           'gqa_sink', 'gqa_w_o', 'w_out', 'final_norm')


def _tile(n, target, mult):
    t = (min(target, n) // mult) * mult
    while t >= mult:
        if n % t == 0:
            return t
        t -= mult
    return n


def _cparams(sem):
    return pltpu.CompilerParams(dimension_semantics=sem, vmem_limit_bytes=VMEM_LIMIT)


def _mm(a, b, mode):
    if mode == 'nn':
        (M, K), N = a.shape, b.shape[1]
        tm, tn, tk = _tile(M, 1408, SUBLANES), _tile(N, 512, LANES), _tile(K, 1408, LANES)
    elif mode == 'nt':
        (M, K), N = a.shape, b.shape[0]
        tm, tn, tk = _tile(M, 1408, SUBLANES), _tile(N, 1024, LANES), _tile(K, 1408, LANES)
    else:
        (K, M), N = a.shape, b.shape[1]
        tm, tn, tk = _tile(M, 1024, LANES), _tile(N, 1408, LANES), _tile(K, 768, 2 * SUBLANES)
    nk = K // tk
    dims = {'nn': (((1,), (0,)), ((), ())), 'nt': (((1,), (1,)), ((), ())), 'tn': (((0,), (0,)), ((), ()))}[mode]
    keep_a = nk == 1 and mode != 'tn' and N // tn > 1

    def body(a_ref, b_ref, o_ref, *scratch):
        if keep_a:
            @pl.when(pl.program_id(1) == 0)
            def _():
                scratch[0][...] = a_ref[...].astype(BF16)

            av = scratch[0][...]
        else:
            av = a_ref[...].astype(BF16)
        part = lax.dot_general(av, b_ref[...].astype(BF16), dims, preferred_element_type=F32)
        if nk == 1:
            o_ref[...] = part
        else:
            @pl.when(pl.program_id(2) == 0)
            def _():
                o_ref[...] = part

            @pl.when(pl.program_id(2) > 0)
            def _():
                o_ref[...] += part

    a_spec = pl.BlockSpec((tk, tm), lambda i, j, k: (k, i)) if mode == 'tn' else pl.BlockSpec((tm, tk), lambda i, j, k: (i, k))
    b_spec = pl.BlockSpec((tn, tk), lambda i, j, k: (j, k)) if mode == 'nt' else pl.BlockSpec((tk, tn), lambda i, j, k: (k, j))
    return pl.pallas_call(
        body, name='mm_' + mode, grid=(M // tm, N // tn, nk),
        in_specs=[a_spec, b_spec], out_specs=pl.BlockSpec((tm, tn), lambda i, j, k: (i, j)),
        out_shape=jax.ShapeDtypeStruct((M, N), F32),
        scratch_shapes=[pltpu.VMEM((tm, tk), BF16)] if keep_a else [],
        compiler_params=_cparams(('parallel', 'arbitrary', 'arbitrary')),
    )(a, b)


@jax.custom_vjp
def matmul_d(x, w):
    return _mm(x, w, 'nn')


def _matmul_d_fwd(x, w):
    return _mm(x, w, 'nn'), (x, w)


def _matmul_d_bwd(res, g):
    x, w = res
    return _mm(g, w, 'nt'), _mm(x, g, 'tn')


matmul_d.defvjp(_matmul_d_fwd, _matmul_d_bwd)


@jax.custom_vjp
def matmul(x, w, sink):
    return _mm(x, w, 'nn')


def _matmul_fwd(x, w, sink):
    return _mm(x, w, 'nn'), (x, w)


def _matmul_bwd(res, g):
    x, w = res
    return _mm(g, w, 'nt'), jnp.zeros_like(w), _mm(x, g, 'tn')


matmul.defvjp(_matmul_fwd, _matmul_bwd)


def _norm_fwd_call(x, g, sh, sc, rb):
    n, d = x.shape
    has_mod = sh is not None

    def body(*refs):
        x_ref, g_ref = refs[0], refs[1]
        o_ref = refs[-1]
        xv = x_ref[...]
        r = lax.rsqrt(jnp.mean(xv * xv, axis=-1, keepdims=True) + EPS)
        y = xv * r * g_ref[...]
        if has_mod:
            lat = pl.program_id(0) > 0
            shv = jnp.where(lat, refs[2][1:2, :], refs[2][0:1, :])
            scv = jnp.where(lat, refs[3][1:2, :], refs[3][0:1, :])
            y = y * (1.0 + scv) + shv
        o_ref[...] = y

    row = pl.BlockSpec((rb, d), lambda i: (i, 0))
    vec = pl.BlockSpec((1, d), lambda i: (0, 0))
    two = pl.BlockSpec((2, d), lambda i: (0, 0))
    ins = [x, g.reshape(1, d)] + ([sh, sc] if has_mod else [])
    return pl.pallas_call(
        body, name='norm_fwd', grid=(n // rb,), in_specs=[row, vec] + ([two, two] if has_mod else []),
        out_specs=row, out_shape=jax.ShapeDtypeStruct((n, d), F32), compiler_params=_cparams(('parallel',)),
    )(*ins)


def _norm_bwd_call(x, g, sh, sc, dy, rb):
    n, d = x.shape
    has_mod = sh is not None

    def body(*refs):
        x_ref, g_ref, dy_ref = refs[0], refs[1], refs[-3]
        dx_ref, acc_ref = refs[-2], refs[-1]
        i = pl.program_id(0)
        xv, dyv, gv = x_ref[...], dy_ref[...], g_ref[...]
        r = lax.rsqrt(jnp.mean(xv * xv, axis=-1, keepdims=True) + EPS)
        xh = xv * r
        if has_mod:
            lat = i > 0
            scv = jnp.where(lat, refs[3][1:2, :], refs[3][0:1, :])
            dyg = dyv * (1.0 + scv)
        else:
            dyg = dyv
        dxh = dyg * gv
        dx_ref[...] = r * (dxh - xh * jnp.mean(dxh * xh, axis=-1, keepdims=True))

        @pl.when(i == 0)
        def _():
            acc_ref[...] = jnp.zeros_like(acc_ref)

        acc_ref[0:1, :] += jnp.sum(dyg * xh, axis=0, keepdims=True)
        if has_mod:
            dsh = jnp.sum(dyv, axis=0, keepdims=True)
            dsc = jnp.sum(dyv * xh * gv, axis=0, keepdims=True)

            @pl.when(i == 0)
            def _():
                acc_ref[1:2, :] += dsh
                acc_ref[3:4, :] += dsc

            @pl.when(i > 0)
            def _():
                acc_ref[2:3, :] += dsh
                acc_ref[4:5, :] += dsc

    row = pl.BlockSpec((rb, d), lambda i: (i, 0))
    vec = pl.BlockSpec((1, d), lambda i: (0, 0))
    two = pl.BlockSpec((2, d), lambda i: (0, 0))
    ins = [x, g.reshape(1, d)] + ([sh, sc] if has_mod else []) + [dy]
    return pl.pallas_call(
        body, name='norm_bwd', grid=(n // rb,), in_specs=[row, vec] + ([two, two] if has_mod else []) + [row],
        out_specs=[row, pl.BlockSpec((SUBLANES, d), lambda i: (0, 0))],
        out_shape=[jax.ShapeDtypeStruct((n, d), F32), jax.ShapeDtypeStruct((SUBLANES, d), F32)],
        compiler_params=_cparams(('arbitrary',)),
    )(*ins)


def _make_norm(rb, has_mod):
    if has_mod:
        @jax.custom_vjp
        def f(x, g, sh, sc):
            return _norm_fwd_call(x, g, sh, sc, rb)

        def fwd(x, g, sh, sc):
            return _norm_fwd_call(x, g, sh, sc, rb), (x, g, sh, sc)

        def bwd(res, dy):
            x, g, sh, sc = res
            dx, acc = _norm_bwd_call(x, g, sh, sc, dy, rb)
            return dx, acc[0], acc[1:3], acc[3:5]
    else:
        @jax.custom_vjp
        def f(x, g):
            return _norm_fwd_call(x, g, None, None, rb)

        def fwd(x, g):
            return _norm_fwd_call(x, g, None, None, rb), (x, g)

        def bwd(res, dy):
            x, g = res
            dx, acc = _norm_bwd_call(x, g, None, None, dy, rb)
            return dx, acc[0]
    f.defvjp(fwd, bwd)
    return f


def _make_gated_res(rb, coef):
    def fwd_call(x, f, gate):
        n, d = x.shape

        def body(x_ref, f_ref, g_ref, o_ref):
            gv = jnp.where(pl.program_id(0) > 0, g_ref[1:2, :], g_ref[0:1, :])
            o_ref[...] = x_ref[...] + coef * gv * f_ref[...]

        row = pl.BlockSpec((rb, d), lambda i: (i, 0))
        return pl.pallas_call(
            body, name='gated_res_fwd', grid=(n // rb,), in_specs=[row, row, pl.BlockSpec((2, d), lambda i: (0, 0))],
            out_specs=row, out_shape=jax.ShapeDtypeStruct((n, d), F32), compiler_params=_cparams(('parallel',)),
        )(x, f, gate)

    def bwd_call(dy, f, gate):
        n, d = dy.shape

        def body(dy_ref, f_ref, g_ref, df_ref, acc_ref):
            i = pl.program_id(0)
            gv = jnp.where(i > 0, g_ref[1:2, :], g_ref[0:1, :])
            dyv = dy_ref[...]
            df_ref[...] = coef * gv * dyv
            part = coef * jnp.sum(dyv * f_ref[...], axis=0, keepdims=True)

            @pl.when(i == 0)
            def _():
                acc_ref[...] = jnp.zeros_like(acc_ref)
                acc_ref[0:1, :] += part

            @pl.when(i > 0)
            def _():
                acc_ref[1:2, :] += part

        row = pl.BlockSpec((rb, d), lambda i: (i, 0))
        return pl.pallas_call(
            body, name='gated_res_bwd', grid=(n // rb,), in_specs=[row, row, pl.BlockSpec((2, d), lambda i: (0, 0))],
            out_specs=[row, pl.BlockSpec((SUBLANES, d), lambda i: (0, 0))],
            out_shape=[jax.ShapeDtypeStruct((n, d), F32), jax.ShapeDtypeStruct((SUBLANES, d), F32)],
            compiler_params=_cparams(('arbitrary',)),
        )(dy, f, gate)

    @jax.custom_vjp
    def f(x, fv, gate):
        return fwd_call(x, fv, gate)

    def fwd(x, fv, gate):
        return fwd_call(x, fv, gate), (fv, gate)

    def bwd(res, dy):
        fv, gate = res
        df, acc = bwd_call(dy, fv, gate)
        return dy, df, acc[0:2]

    f.defvjp(fwd, bwd)
    return f


MLA_SCALE = (MLA_NOPE + MLA_ROPE) ** -0.5
NT_DIMS = (((1,), (1,)), ((), ()))


def _mla_keys(kv, kr):
    lane = lax.broadcasted_iota(jnp.int32, kv.shape, 1)
    return jnp.where(lane < MLA_NOPE, kv, kr).astype(BF16), kv.astype(BF16)


def _mla_fwd_call(q, kv, kr, c_len):
    n = q.shape[0]
    h = q.shape[1] // LANES
    tq = BLOCK
    cb = c_len // tq

    def body(q_ref, kv_ref, kr_ref, o_ref, lse_ref, kb, vb):
        i = pl.program_id(1)

        @pl.when(i == 0)
        def _():
            kb[...], vb[...] = _mla_keys(kv_ref[...], kr_ref[...])

        qv = q_ref[...].astype(BF16)

        def attend(nk):
            s = lax.dot_general(qv, kb[:nk, :], NT_DIMS, preferred_element_type=F32) * MLA_SCALE
            m = jnp.max(s, axis=-1, keepdims=True)
            p = jnp.exp(s - m)
            l = jnp.sum(p, axis=-1, keepdims=True)
            o_ref[...] = jnp.dot(p.astype(BF16), vb[:nk, :], preferred_element_type=F32) / l
            lse_ref[0] = m + jnp.log(l)

        pl.when(i < cb)(lambda: attend(c_len))
        pl.when(i >= cb)(lambda: attend(n))

    qspec = pl.BlockSpec((tq, LANES), lambda a, i: (i, a))
    return pl.pallas_call(
        body, name='mla_attn_fwd', grid=(h, n // tq),
        in_specs=[qspec, pl.BlockSpec((n, LANES), lambda a, i: (0, a)), pl.BlockSpec((n, LANES), lambda a, i: (0, 0))],
        out_specs=[qspec, pl.BlockSpec((1, tq, 1), lambda a, i: (a, i, 0))],
        out_shape=[jax.ShapeDtypeStruct((n, h * LANES), F32), jax.ShapeDtypeStruct((h, n, 1), F32)],
        scratch_shapes=[pltpu.VMEM((n, LANES), BF16), pltpu.VMEM((n, LANES), BF16)],
        compiler_params=_cparams(('arbitrary', 'arbitrary')),
    )(q, kv, kr)


def _mla_dq_call(q, kv, kr, o, lse, do, c_len):
    n = q.shape[0]
    h = q.shape[1] // LANES
    tq = BLOCK
    cb = c_len // tq

    def body(q_ref, kv_ref, kr_ref, o_ref, lse_ref, do_ref, dq_ref, delta_ref, kb, vb):
        i = pl.program_id(1)

        @pl.when(i == 0)
        def _():
            kb[...], vb[...] = _mla_keys(kv_ref[...], kr_ref[...])

        dov = do_ref[...]
        delta = jnp.sum(dov * o_ref[...], axis=-1, keepdims=True)
        delta_ref[0] = delta
        qv, dob = q_ref[...].astype(BF16), dov.astype(BF16)

        def grad(nk):
            s = lax.dot_general(qv, kb[:nk, :], NT_DIMS, preferred_element_type=F32) * MLA_SCALE
            p = jnp.exp(s - lse_ref[0])
            dp = lax.dot_general(dob, vb[:nk, :], NT_DIMS, preferred_element_type=F32)
            ds = p * (dp - delta) * MLA_SCALE
            dq_ref[...] = jnp.dot(ds.astype(BF16), kb[:nk, :], preferred_element_type=F32)

        pl.when(i < cb)(lambda: grad(c_len))
        pl.when(i >= cb)(lambda: grad(n))

    qspec = pl.BlockSpec((tq, LANES), lambda a, i: (i, a))
    col = pl.BlockSpec((1, tq, 1), lambda a, i: (a, i, 0))
    return pl.pallas_call(
        body, name='mla_attn_dq', grid=(h, n // tq),
        in_specs=[qspec, pl.BlockSpec((n, LANES), lambda a, i: (0, a)), pl.BlockSpec((n, LANES), lambda a, i: (0, 0)),
                  qspec, col, qspec],
        out_specs=[qspec, col],
        out_shape=[jax.ShapeDtypeStruct((n, h * LANES), F32), jax.ShapeDtypeStruct((h, n, 1), F32)],
        scratch_shapes=[pltpu.VMEM((n, LANES), BF16), pltpu.VMEM((n, LANES), BF16)],
        compiler_params=_cparams(('arbitrary', 'arbitrary')),
    )(q, kv, kr, o, lse, do)


def _mla_dkv_call(q, kv, kr, lse_row, delta_row, do, c_len):
    n = q.shape[0]
    h = q.shape[1] // LANES
    tk = BLOCK
    cb = c_len // tk

    def body(q_ref, kv_ref, kr_ref, lse_ref, delta_ref, do_ref, dkv_ref, dkf_ref, qb, dob):
        j = pl.program_id(1)

        @pl.when(j == 0)
        def _():
            qb[...] = q_ref[...].astype(BF16)
            dob[...] = do_ref[...].astype(BF16)

        kblk, vblk = _mla_keys(kv_ref[...], kr_ref[...])
        lane = lax.broadcasted_iota(jnp.int32, (tk, LANES), 1)

        def grad(q0):
            qs, dos = qb[q0:, :], dob[q0:, :]
            st = lax.dot_general(kblk, qs, NT_DIMS, preferred_element_type=F32) * MLA_SCALE
            pt = jnp.exp(st - lse_ref[0][:, q0:])
            dv = jnp.dot(pt.astype(BF16), dos, preferred_element_type=F32)
            dpt = lax.dot_general(vblk, dos, NT_DIMS, preferred_element_type=F32)
            dst = pt * (dpt - delta_ref[0][:, q0:]) * MLA_SCALE
            dk = jnp.dot(dst.astype(BF16), qs, preferred_element_type=F32)
            dkv_ref[...] = jnp.where(lane < MLA_NOPE, dk, dv)
            dkf_ref[0] = dk

        pl.when(j < cb)(lambda: grad(0))
        pl.when(j >= cb)(lambda: grad(c_len))

    full = pl.BlockSpec((n, LANES), lambda a, j: (0, a))
    row = pl.BlockSpec((1, 1, n), lambda a, j: (a, 0, 0))
    kspec = pl.BlockSpec((tk, LANES), lambda a, j: (j, a))
    return pl.pallas_call(
        body, name='mla_attn_dkv', grid=(h, n // tk),
        in_specs=[full, kspec, pl.BlockSpec((tk, LANES), lambda a, j: (j, 0)), row, row, full],
        out_specs=[kspec, pl.BlockSpec((1, tk, LANES), lambda a, j: (a, j, 0))],
        out_shape=[jax.ShapeDtypeStruct((n, h * LANES), F32), jax.ShapeDtypeStruct((h, n, LANES), F32)],
        scratch_shapes=[pltpu.VMEM((n, LANES), BF16), pltpu.VMEM((n, LANES), BF16)],
        compiler_params=_cparams(('arbitrary', 'arbitrary')),
    )(q, kv, kr, lse_row, delta_row, do)


def _sum_leading(g):
    nl, r, _ = g.shape
    rb = _tile(r, 512, SUBLANES)

    def body(g_ref, o_ref):
        acc = g_ref[0]
        for j in range(1, nl):
            acc = acc + g_ref[j]
        o_ref[...] = acc

    return pl.pallas_call(
        body, name='sum_leading', grid=(r // rb,), in_specs=[pl.BlockSpec((nl, rb, LANES), lambda i: (0, i, 0))],
        out_specs=pl.BlockSpec((rb, LANES), lambda i: (i, 0)), out_shape=jax.ShapeDtypeStruct((r, LANES), F32),
        compiler_params=_cparams(('parallel',)),
    )(g)


def _make_mla(c_len):
    @jax.custom_vjp
    def f(q, kv, kr):
        return _mla_fwd_call(q, kv, kr, c_len)[0]

    def fwd(q, kv, kr):
        o, lse = _mla_fwd_call(q, kv, kr, c_len)
        return o, (q, kv, kr, o, lse)

    def bwd(res, do):
        q, kv, kr, o, lse = res
        n = q.shape[0]
        h = q.shape[1] // LANES
        dq, delta = _mla_dq_call(q, kv, kr, o, lse, do, c_len)
        dkv, dk_full = _mla_dkv_call(q, kv, kr, lse.reshape(h, 1, n), delta.reshape(h, 1, n), do, c_len)
        return dq, dkv, _sum_leading(dk_full)

    f.defvjp(fwd, bwd)
    return f


def _gqa_specs(band, c_len, nb, rows):
    cb = c_len // BLOCK
    q_spec = pl.BlockSpec((1, GQA_G, rows, GQA_D), lambda a, b: (a, 0, b, 0))
    ctx_spec = pl.BlockSpec((1, c_len, GQA_D), lambda a, b: (a, 0, 0))
    kv_specs = [ctx_spec]
    if band:
        kv_specs += [pl.BlockSpec((1, BLOCK, GQA_D), lambda a, b: (a, jnp.maximum(b - 1, 0) + cb, 0)),
                     pl.BlockSpec((1, BLOCK, GQA_D), lambda a, b: (a, b + cb, 0)),
                     pl.BlockSpec((1, BLOCK, GQA_D), lambda a, b: (a, jnp.minimum(b + 1, nb - 1) + cb, 0))]
    sink_spec = pl.BlockSpec((1, GQA_G * rows, 1), lambda a, b: (a, 0, 0))
    return q_spec, kv_specs, sink_spec


def _gqa_scores(q, kcat, sink, band, c_len, t_len, rows):
    scale = GQA_D ** -0.5
    s = lax.dot_general(q, kcat, (((1,), (1,)), ((), ())), preferred_element_type=F32) * scale
    if band:
        b = pl.program_id(1)
        shape = s.shape
        col = lax.broadcasted_iota(jnp.int32, shape, 1)
        qpos = b * BLOCK + (lax.broadcasted_iota(jnp.int32, shape, 0) & (BLOCK - 1))
        kpos = (b - 1) * BLOCK + (col - c_len)
        valid = (col < c_len) | ((jnp.abs(qpos - kpos) <= WINDOW) & (kpos >= 0) & (kpos < t_len))
        s = jnp.where(valid, s, NEG_INF)
    m = jnp.maximum(jnp.max(s, axis=-1, keepdims=True), sink)
    e = jnp.exp(s - m)
    es = jnp.exp(sink - m)
    den = es + jnp.sum(e, axis=-1, keepdims=True)
    return e / den, es / den


def _gqa_fwd_call(q4, k2, v2, sink_rows, band, c_len):
    kv, g, tq_all, d = q4.shape
    rows = BLOCK if band else tq_all
    nb = tq_all // rows
    t_len = k2.shape[1] - c_len
    nkv = 4 if band else 1
    q_spec, kv_specs, sink_spec = _gqa_specs(band, c_len, nb, rows)

    def body(*refs):
        q_ref, sink_ref, o_ref = refs[0], refs[1 + 2 * nkv], refs[-1]
        kcat = jnp.concatenate([r[0] for r in refs[1:1 + nkv]], axis=0).astype(BF16)
        vcat = jnp.concatenate([r[0] for r in refs[1 + nkv:1 + 2 * nkv]], axis=0).astype(BF16)
        q = q_ref[0].reshape(g * rows, d).astype(BF16)
        p, _ = _gqa_scores(q, kcat, sink_ref[0], band, c_len, t_len, rows)
        o_ref[0] = jnp.dot(p.astype(BF16), vcat, preferred_element_type=F32).reshape(g, rows, d)

    return pl.pallas_call(
        body, name='gqa_fwd_band' if band else 'gqa_fwd_ctx', grid=(kv, nb),
        in_specs=[q_spec] + kv_specs + kv_specs + [sink_spec], out_specs=q_spec,
        out_shape=jax.ShapeDtypeStruct(q4.shape, F32), compiler_params=_cparams(('parallel', 'parallel')),
    )(q4, *([k2] * nkv), *([v2] * nkv), sink_rows)


def _gqa_bwd_call(q4, k2, v2, sink_rows, do4, band, c_len):
    kv, g, tq_all, d = q4.shape
    rows = BLOCK if band else tq_all
    nb = tq_all // rows
    t_len = k2.shape[1] - c_len
    nkv = 4 if band else 1
    scale = GQA_D ** -0.5
    q_spec, kv_specs, sink_spec = _gqa_specs(band, c_len, nb, rows)

    def body(*refs):
        q_ref, sink_ref, do_ref = refs[0], refs[1 + 2 * nkv], refs[2 + 2 * nkv]
        outs = refs[3 + 2 * nkv:]
        dq_ref, dkc_ref, dvc_ref = outs[0], outs[1], outs[2]
        dsink_ref = outs[-1]
        b = pl.program_id(1)
        kcat = jnp.concatenate([r[0] for r in refs[1:1 + nkv]], axis=0).astype(BF16)
        vcat = jnp.concatenate([r[0] for r in refs[1 + nkv:1 + 2 * nkv]], axis=0).astype(BF16)
        q = q_ref[0].reshape(g * rows, d).astype(BF16)
        do = do_ref[0].reshape(g * rows, d).astype(BF16)
        p, p_sink = _gqa_scores(q, kcat, sink_ref[0], band, c_len, t_len, rows)
        dp = lax.dot_general(do, vcat, (((1,), (1,)), ((), ())), preferred_element_type=F32)
        rd = jnp.sum(p * dp, axis=-1, keepdims=True)
        ds = (p * (dp - rd) * scale).astype(BF16)
        dq_ref[0] = jnp.dot(ds, kcat, preferred_element_type=F32).reshape(g, rows, d)
        dkcat = lax.dot_general(ds, q, (((0,), (0,)), ((), ())), preferred_element_type=F32)
        dvcat = lax.dot_general(p.astype(BF16), do, (((0,), (0,)), ((), ())), preferred_element_type=F32)

        @pl.when(b == 0)
        def _():
            dkc_ref[...] = jnp.zeros_like(dkc_ref)
            dvc_ref[...] = jnp.zeros_like(dvc_ref)
            dsink_ref[...] = jnp.zeros_like(dsink_ref)

        dkc_ref[0] += dkcat[:c_len]
        dvc_ref[0] += dvcat[:c_len]
        dsink_ref[0] += -p_sink * rd
        if band:
            outs[3][0, 0] = dkcat[c_len:]
            outs[4][0, 0] = dvcat[c_len:]

    ctx_out = pl.BlockSpec((1, c_len, d), lambda a, b: (a, 0, 0))
    band_out = pl.BlockSpec((1, 1, 3 * BLOCK, d), lambda a, b: (a, b, 0, 0))
    out_specs = [q_spec, ctx_out, ctx_out] + ([band_out, band_out] if band else []) + [sink_spec]
    ctx_shape = jax.ShapeDtypeStruct((kv, c_len, d), F32)
    band_shape = jax.ShapeDtypeStruct((kv, nb, 3 * BLOCK, d), F32)
    out_shape = ([jax.ShapeDtypeStruct(q4.shape, F32), ctx_shape, ctx_shape] + ([band_shape, band_shape] if band else [])
                 + [jax.ShapeDtypeStruct(sink_rows.shape, F32)])
    return pl.pallas_call(
        body, name='gqa_bwd_band' if band else 'gqa_bwd_ctx', grid=(kv, nb),
        in_specs=[q_spec] + kv_specs + kv_specs + [sink_spec, q_spec], out_specs=out_specs, out_shape=out_shape,
        compiler_params=_cparams(('arbitrary', 'arbitrary')),
    )(q4, *([k2] * nkv), *([v2] * nkv), sink_rows, do4)


def _make_gqa(band, c_len):
    @jax.custom_vjp
    def f(q4, k2, v2, sink_rows):
        return _gqa_fwd_call(q4, k2, v2, sink_rows, band, c_len)

    def fwd(q4, k2, v2, sink_rows):
        return _gqa_fwd_call(q4, k2, v2, sink_rows, band, c_len), (q4, k2, v2, sink_rows)

    def bwd(res, do4):
        q4, k2, v2, sink_rows = res
        outs = _gqa_bwd_call(q4, k2, v2, sink_rows, do4, band, c_len)
        kv, n, d = k2.shape
        if not band:
            dq4, dkc, dvc, dsink = outs
            return dq4, dkc, dvc, dsink
        dq4, dkc, dvc, dkb, dvb, dsink = outs

        def fold(ctx_part, bands):
            cur = bands[:, :, BLOCK:2 * BLOCK]
            prv = jnp.pad(bands[:, 1:, :BLOCK], ((0, 0), (0, 1), (0, 0), (0, 0)))
            nxt = jnp.pad(bands[:, :-1, 2 * BLOCK:], ((0, 0), (1, 0), (0, 0), (0, 0)))
            lat = (cur + prv + nxt).reshape(kv, n - c_len, d)
            return jnp.concatenate([ctx_part, lat], axis=1)

        return dq4, fold(dkc, dkb), fold(dvc, dvb), dsink

    f.defvjp(fwd, bwd)
    return f


def _cmul(ar, ai, br, bi):
    return ar * br - ai * bi, ar * bi + ai * br


def _scan_tables(ar, ai, desc):
    a1 = (ar, ai)
    a2 = _cmul(*a1, *a1)
    a4 = _cmul(*a2, *a2)
    pw = [a1]
    for _ in range(SUBLANES - 1):
        pw.append(_cmul(*pw[-1], *a1))
    row = jnp.arange(SUBLANES)[:, None]
    tabs = []
    for dist, (pr, pi) in ((1, a1), (2, a2), (4, a4)):
        keep = (row <= SUBLANES - 1 - dist) if desc else (row >= dist)
        tabs += [jnp.where(keep, pr[None, :], 0.0), jnp.where(keep, pi[None, :], 0.0)]
    order = pw[::-1] if desc else pw
    tabs += [jnp.stack([p[0] for p in order]), jnp.stack([p[1] for p in order])]
    return jnp.stack(tabs).astype(F32)


def _scan_call(b_re, b_im, tabs, order, chunk, prev=None):
    n, s_dim = b_re.shape
    nch = n // chunk
    ng = chunk // SUBLANES
    desc = order in ('Fb', 'R')
    with_da = prev is not None

    def chunk_of(i):
        if order == 'F':
            return i
        if order == 'Fb':
            return nch - 1 - i
        if order == 'R':
            return jnp.where(i == 0, 0, nch - i)
        return jnp.where(i == nch - 1, 0, i + 1)

    def body(*refs):
        br_ref, bi_ref, tab_ref = refs[0], refs[1], refs[2]
        if with_da:
            pr_ref, pi_ref, sr_ref, si_ref, dar_ref, dai_ref, cr_ref, ci_ref = refs[3:]
        else:
            sr_ref, si_ref, pr_ref, pi_ref, cr_ref, ci_ref = refs[3:]

        @pl.when(pl.program_id(0) == 0)
        def _():
            cr_ref[...] = jnp.zeros_like(cr_ref)
            ci_ref[...] = jnp.zeros_like(ci_ref)
            if with_da:
                dar_ref[...] = jnp.zeros_like(dar_ref)
                dai_ref[...] = jnp.zeros_like(dai_ref)

        sub = lax.broadcasted_iota(jnp.int32, (SUBLANES, s_dim), 0)
        edge = SUBLANES - 1 if desc else 0
        last = 0 if desc else SUBLANES - 1

        def step(t, carry):
            gi = (ng - 1 - t) if desc else t
            rows = pl.ds(pl.multiple_of(gi * SUBLANES, SUBLANES), SUBLANES)
            xr, xi = br_ref[rows, :], bi_ref[rows, :]
            for j, dist in enumerate((1, 2, 4)):
                shift = SUBLANES - dist if desc else dist
                rr, ri = pltpu.roll(xr, shift, 0), pltpu.roll(xi, shift, 0)
                mr, mi = tab_ref[2 * j], tab_ref[2 * j + 1]
                xr, xi = xr + mr * rr - mi * ri, xi + mr * ri + mi * rr
            cr, ci = cr_ref[...], ci_ref[...]
            pwr, pwi = tab_ref[6], tab_ref[7]
            sr = xr + pwr * cr - pwi * ci
            si = xi + pwr * ci + pwi * cr
            sr_ref[rows, :] = sr
            si_ref[rows, :] = si
            if with_da:
                pr, pi = pr_ref[rows, :], pi_ref[rows, :]
                dar_ref[...] += sr * pr + si * pi
                dai_ref[...] += si * pr - sr * pi
            else:
                shift1 = SUBLANES - 1 if desc else 1
                pr_ref[rows, :] = jnp.where(sub == edge, cr, pltpu.roll(sr, shift1, 0))
                pi_ref[rows, :] = jnp.where(sub == edge, ci, pltpu.roll(si, shift1, 0))
            cr_ref[...] = jnp.broadcast_to(sr[last:last + 1, :], (SUBLANES, s_dim))
            ci_ref[...] = jnp.broadcast_to(si[last:last + 1, :], (SUBLANES, s_dim))
            return carry

        lax.fori_loop(0, ng, step, 0)

    blk = pl.BlockSpec((chunk, s_dim), lambda i: (chunk_of(i), 0))
    tab_spec = pl.BlockSpec((8, SUBLANES, s_dim), lambda i: (0, 0, 0))
    acc = pl.BlockSpec((SUBLANES, s_dim), lambda i: (0, 0))
    big = jax.ShapeDtypeStruct((n, s_dim), F32)
    small = jax.ShapeDtypeStruct((SUBLANES, s_dim), F32)
    if with_da:
        in_specs, ins = [blk, blk, tab_spec, blk, blk], [b_re, b_im, tabs, prev[0], prev[1]]
        out_specs, out_shape = [blk, blk, acc, acc], [big, big, small, small]
    else:
        in_specs, ins = [blk, blk, tab_spec], [b_re, b_im, tabs]
        out_specs, out_shape = [blk, blk, blk, blk], [big, big, big, big]
    return pl.pallas_call(
        body, name='s5_scan_' + order, grid=(nch,), in_specs=in_specs, out_specs=out_specs, out_shape=out_shape,
        scratch_shapes=[pltpu.VMEM((SUBLANES, s_dim), F32), pltpu.VMEM((SUBLANES, s_dim), F32)],
        compiler_params=_cparams(('arbitrary',)),
    )(*ins)


def _make_scan(rev, chunk):
    def run(b_re, b_im, ar, ai):
        tabs = _scan_tables(ar, ai, desc=rev)
        return _scan_call(b_re, b_im, tabs, 'R' if rev else 'F', chunk)

    @jax.custom_vjp
    def f(b_re, b_im, ar, ai):
        return tuple(run(b_re, b_im, ar, ai)[:2])

    def fwd(b_re, b_im, ar, ai):
        s_re, s_im, p_re, p_im = run(b_re, b_im, ar, ai)
        return (s_re, s_im), (p_re, p_im, ar, ai)

    def bwd(res, g):
        p_re, p_im, ar, ai = res
        tabs = _scan_tables(ar, -ai, desc=not rev)
        db_re, db_im, dar, dai = _scan_call(g[0], g[1], tabs, 'Rb' if rev else 'Fb', chunk, prev=(p_re, p_im))
        return db_re, db_im, jnp.sum(dar, axis=0), jnp.sum(dai, axis=0)

    f.defvjp(fwd, bwd)
    return f


def _sqerr_call(y, t):
    n, d = y.shape
    rb = _tile(n, 512, SUBLANES)

    def body(y_ref, t_ref, o_ref):
        @pl.when(pl.program_id(0) == 0)
        def _():
            o_ref[...] = jnp.zeros_like(o_ref)

        e = y_ref[...] - t_ref[...]
        o_ref[...] += jnp.sum(e * e, axis=0, keepdims=True)

    row = pl.BlockSpec((rb, d), lambda i: (i, 0))
    return pl.pallas_call(
        body, name='sq_err', grid=(n // rb,), in_specs=[row, row], out_specs=pl.BlockSpec((1, d), lambda i: (0, 0)),
        out_shape=jax.ShapeDtypeStruct((1, d), F32), compiler_params=_cparams(('arbitrary',)),
    )(y, t)


@jax.custom_vjp
def loss_head(y, t):
    return 0.5 * jnp.sum(_sqerr_call(y, t)) / y.shape[1]


def _loss_head_fwd(y, t):
    return loss_head(y, t), (y, t)


def _loss_head_bwd(res, g):
    y, t = res
    return g * (y - t) / y.shape[1], None


loss_head.defvjp(_loss_head_fwd, _loss_head_bwd)


def _adamw_call(w, g, m, v):
    r, c = w.shape
    rb = _tile(r, max(SUBLANES, (256 * 1024) // max(c, LANES) // SUBLANES * SUBLANES), SUBLANES)

    def body(w_ref, g_ref, m_ref, v_ref, d_ref, nm_ref, nv_ref):
        gv = g_ref[...]
        nm = ADAM_B1 * m_ref[...] + (1.0 - ADAM_B1) * gv
        nv = ADAM_B2 * v_ref[...] + (1.0 - ADAM_B2) * (gv * gv)
        m_hat = nm / (1.0 - ADAM_B1 ** ADAM_STEP)
        v_hat = nv / (1.0 - ADAM_B2 ** ADAM_STEP)
        d_ref[...] = -ADAM_LR * (m_hat / (jnp.sqrt(v_hat) + ADAM_EPS) + ADAM_WD * w_ref[...])
        nm_ref[...] = nm
        nv_ref[...] = nv

    blk = pl.BlockSpec((rb, c), lambda i: (i, 0))
    shape = jax.ShapeDtypeStruct((r, c), F32)
    return pl.pallas_call(
        body, name='adamw', grid=(r // rb,), in_specs=[blk] * 4, out_specs=[blk] * 3, out_shape=[shape] * 3,
        compiler_params=_cparams(('parallel',)),
    )(w, g, m, v)


MESH = pl.DeviceIdType.MESH
HBM_SPEC = pl.BlockSpec(memory_space=pltpu.HBM)


def _all_gather(x):
    def body(x_ref, out_ref, send_sems, recv_sems, local_sem):
        x, y, c = lax.axis_index('x'), lax.axis_index('y'), lax.axis_index('c')
        me, sibling = (x, y, c), (x, y, 1 - c)
        chips = [(1 - x, y), (x, 1 - y), (1 - x, 1 - y)]

        def slot(px, py, pc):
            return out_ref.at[4 * px + 2 * py + pc]

        def copy(k, block, to, src=None):
            return pltpu.make_async_remote_copy(
                src_ref=slot(*block) if src is None else src, dst_ref=slot(*block),
                send_sem=send_sems.at[k], recv_sem=recv_sems.at[k], device_id=to, device_id_type=MESH)

        mine = pltpu.make_async_copy(x_ref, slot(*me), local_sem)
        mine.start()
        first = [copy(0, me, sibling, src=x_ref)]
        first += [copy(1 + j, me, (*chip, c), src=x_ref) for j, chip in enumerate(chips)]
        for cp in first:
            cp.start()
        passed = [copy(4 + j, (*chip, c), sibling) for j, chip in enumerate(chips)]
        for j, chip in enumerate(chips):
            copy(1 + j, (*chip, c), me).wait_recv()
            passed[j].start()
        copy(0, sibling, me).wait_recv()
        for j, chip in enumerate(chips):
            copy(4 + j, (*chip, 1 - c), me).wait_recv()
        for cp in first + passed:
            cp.wait_send()
        mine.wait()

    return pl.pallas_call(
        body, name='all_gather', out_shape=jax.ShapeDtypeStruct((N_DEV,) + x.shape, x.dtype),
        in_specs=[HBM_SPEC], out_specs=HBM_SPEC,
        scratch_shapes=[pltpu.SemaphoreType.DMA((7,)), pltpu.SemaphoreType.DMA((7,)), pltpu.SemaphoreType.DMA],
    )(x)


def _exchange_sibling(g_all):
    def body(g_ref, out_ref, send_sem, recv_sem):
        x, y, c = lax.axis_index('x'), lax.axis_index('y'), lax.axis_index('c')
        cp = pltpu.make_async_remote_copy(src_ref=g_ref.at[1 - c], dst_ref=out_ref, send_sem=send_sem, recv_sem=recv_sem,
                                          device_id=(x, y, 1 - c), device_id_type=MESH)
        cp.start()
        cp.wait()

    return pl.pallas_call(
        body, name='rs_sibling', out_shape=jax.ShapeDtypeStruct(g_all.shape[1:], g_all.dtype),
        in_specs=[HBM_SPEC], out_specs=HBM_SPEC,
        scratch_shapes=[pltpu.SemaphoreType.DMA, pltpu.SemaphoreType.DMA],
    )(g_all)


def _exchange_chips(p):
    def body(p_ref, out_ref, send_sems, recv_sems):
        x, y, c = lax.axis_index('x'), lax.axis_index('y'), lax.axis_index('c')
        chips = [(1 - x, y), (x, 1 - y), (1 - x, 1 - y)]
        copies = [pltpu.make_async_remote_copy(src_ref=p_ref.at[2 * px + py], dst_ref=out_ref.at[j],
                                               send_sem=send_sems.at[j], recv_sem=recv_sems.at[j],
                                               device_id=(px, py, c), device_id_type=MESH)
                  for j, (px, py) in enumerate(chips)]
        for cp in copies:
            cp.start()
        for cp in copies:
            cp.wait_recv()
        for cp in copies:
            cp.wait_send()

    return pl.pallas_call(
        body, name='rs_chips', out_shape=jax.ShapeDtypeStruct((3,) + p.shape[1:], p.dtype),
        in_specs=[HBM_SPEC], out_specs=HBM_SPEC,
        scratch_shapes=[pltpu.SemaphoreType.DMA((3,)), pltpu.SemaphoreType.DMA((3,))],
    )(p)


def _add_sibling(g_all, recv, c_idx):
    _, nchip, r, _ = g_all.shape
    rb = _tile(r, PACK_ROWS, SUBLANES)

    def body(c_ref, g_ref, r_ref, o_ref):
        o_ref[...] = g_ref[0] + r_ref[...]

    return pl.pallas_call(
        body, name='rs_add_sibling',
        grid_spec=pltpu.PrefetchScalarGridSpec(
            num_scalar_prefetch=1, grid=(nchip, r // rb),
            in_specs=[pl.BlockSpec((1, 1, rb, LANES), lambda k, i, c: (c[0], k, i, 0)),
                      pl.BlockSpec((1, rb, LANES), lambda k, i, c: (k, i, 0))],
            out_specs=pl.BlockSpec((1, rb, LANES), lambda k, i, c: (k, i, 0))),
        out_shape=jax.ShapeDtypeStruct(recv.shape, F32), compiler_params=_cparams(('parallel', 'parallel')),
    )(c_idx, g_all, recv)


def _add_chips(p, recv, chip_idx):
    _, r, _ = p.shape
    rb = _tile(r, PACK_ROWS, SUBLANES)

    def body(k_ref, p_ref, r0, r1, r2, o_ref):
        o_ref[...] = ((p_ref[0] + r0[0]) + r1[0]) + r2[0]

    rspec = lambda j: pl.BlockSpec((1, rb, LANES), lambda i, k: (j, i, 0))
    return pl.pallas_call(
        body, name='rs_add_chips',
        grid_spec=pltpu.PrefetchScalarGridSpec(
            num_scalar_prefetch=1, grid=(r // rb,),
            in_specs=[pl.BlockSpec((1, rb, LANES), lambda i, k: (k[0], i, 0)), rspec(0), rspec(1), rspec(2)],
            out_specs=pl.BlockSpec((rb, LANES), lambda i, k: (i, 0))),
        out_shape=jax.ShapeDtypeStruct((r, LANES), F32), compiler_params=_cparams(('parallel',)),
    )(chip_idx, p, recv, recv, recv)


def _reduce_scatter(g_all, c_idx, chip_idx):
    part = _add_sibling(g_all, _exchange_sibling(g_all), c_idx)
    return _add_chips(part, _exchange_chips(part), chip_idx)


def _pad_to(n, mult):
    return (n + mult - 1) // mult * mult


def _pack(pieces, lead, dtype):
    flat = []
    total = 0
    for p in pieces:
        f = p.reshape(lead + (-1,)).astype(dtype)
        n = _pad_to(f.shape[-1], 16 * LANES)
        flat.append(jnp.pad(f, [(0, 0)] * len(lead) + [(0, n - f.shape[-1])]))
        total += n
    full = _pad_to(total, PACK_ROWS * LANES)
    if full > total:
        flat.append(jnp.zeros(lead + (full - total,), dtype))
    return jnp.concatenate(flat, axis=-1).reshape(lead + (full // LANES, LANES))


def _unpack(buf, lead, shapes):
    flat = buf.reshape(lead + (-1,))
    out, off = [], 0
    for s in shapes:
        n = math.prod(s)
        out.append(flat[..., off:off + n].reshape(lead + tuple(s)))
        off += _pad_to(n, 16 * LANES)
    return out


def _rope_tables(c_len, t_len, n):
    quarter = n // 4
    inv = ROPE_BASE ** (-jnp.arange(0, 2 * quarter, 2, dtype=F32) / (2 * quarter))
    t = jnp.arange(t_len, dtype=jnp.int32)
    pos = jnp.stack([(t // GRID_W).astype(F32), (t % GRID_W).astype(F32)], axis=1)
    ang = pos[:, :, None] * inv[None, None, :]
    ang = jnp.concatenate([jnp.zeros((c_len, 2, quarter), F32), ang], axis=0)
    return jnp.cos(ang), jnp.sin(ang)


def _axial_rope(x, cos, sin):
    n_rows, h, n = x.shape
    xs = x.reshape(n_rows, h, 2, 2, n // 4)
    x1, x2 = xs[:, :, :, 0], xs[:, :, :, 1]
    c, s = cos[:, None], sin[:, None]
    return jnp.stack([x1 * c - x2 * s, x1 * s + x2 * c], axis=3).reshape(n_rows, h, n)


def _ssm_discretize(lam_re, lam_im, log_dt, b_re, b_im):
    dt = jnp.exp(log_dt)[:, None]
    mag = jnp.exp(lam_re * dt)
    a_re, a_im = mag * jnp.cos(lam_im * dt), mag * jnp.sin(lam_im * dt)
    den = lam_re * lam_re + lam_im * lam_im
    w_re = ((a_re - 1) * lam_re + a_im * lam_im) / den
    w_im = (a_im * lam_re - (a_re - 1) * lam_im) / den
    bb_re, bb_im = _cmul(w_re[..., None], w_im[..., None], b_re, b_im)
    return a_re, a_im, bb_re, bb_im


def _block_diag_in(b):
    g = b.shape[0]
    return jnp.einsum('gpm,gh->gmhp', b, jnp.eye(g, dtype=F32)).reshape(g * b.shape[2], g * b.shape[1])


def _block_diag_out(c):
    g = c.shape[0]
    return jnp.einsum('gmp,gh->gphm', c, jnp.eye(g, dtype=F32)).reshape(g * c.shape[2], g * c.shape[1])


def _w_in_layout(d_model):
    sizes = (MLA_Q_RANK, MLA_KV_RANK, MLA_ROPE, SSM_WIDTH, GQA_HEADS * GQA_D, GQA_KV * GQA_D, GQA_KV * GQA_D, 3 * d_model)
    starts = [0]
    for s in sizes[:-1]:
        starts.append(starts[-1] + s)
    names = ('cq', 'ckv', 'kr', 'u', 'gq', 'gk', 'gv', 'gates')
    orig = dict(zip(names, zip(starts, sizes)))
    order = ('cq', 'ckv', 'u', 'gq', 'gk', 'gv', 'gates', 'kr')
    return orig, order


def _permute_w_in(w, d_model):
    orig, order = _w_in_layout(d_model)
    cols = [w[:, orig[k][0]:orig[k][0] + orig[k][1]] for k in order]
    width = sum(orig[k][1] for k in order)
    return jnp.pad(jnp.concatenate(cols, axis=1), ((0, 0), (0, _pad_to(width, LANES) - width)))


def _unpermute_w_in(gp, d_model):
    orig, order = _w_in_layout(d_model)
    off, parts = 0, {}
    for k in order:
        parts[k] = gp[:, off:off + orig[k][1]]
        off += orig[k][1]
    return jnp.concatenate([parts[k] for k in ('cq', 'ckv', 'kr', 'u', 'gq', 'gk', 'gv', 'gates')], axis=1)


def _forward_loss(x_all, sinks, rp, wl, cc_in, target, c_len):
    n, d = x_all.shape
    t_len = n - c_len
    depth = len(wl)
    mla_attn = _make_mla(c_len)
    norm_mod = _make_norm(c_len, True)
    norm_tok = _make_norm(_tile(n, 512, SUBLANES), False)
    norm_out = _make_norm(_tile(t_len, 512, SUBLANES), False)
    half_res = _make_gated_res(c_len, 0.5)
    full_res = _make_gated_res(c_len, 1.0)
    gqa_band = _make_gqa(True, c_len)
    gqa_ctx = _make_gqa(False, c_len)
    scans = (_make_scan(False, c_len), _make_scan(True, c_len))
    cos_m, sin_m = _rope_tables(c_len, t_len, MLA_ROPE)
    cos_g, sin_g = _rope_tables(c_len, t_len, GQA_D)
    orig, order = _w_in_layout(d)
    offs, off = {}, 0
    for k in order:
        offs[k] = (off, orig[k][1])
        off += orig[k][1]

    cc = jnp.zeros((SUBLANES, d), F32).at[0].set(jax.nn.silu(rp['c_ctx'])).at[1].set(jax.nn.silu(cc_in))

    for l in range(depth):
        w, sk = wl[l], sinks[l]
        ctx_out = l < depth - 1

        def mm(h, name):
            return matmul(h, w[name], sk[name])

        def swiglu(h, name13, name2):
            a13 = mm(h, name13)
            f = a13.shape[1] // 2
            return mm(jax.nn.silu(a13[:, :f]) * a13[:, f:], name2)

        mod = mm(cc, 'ada_w') + rp['ada_b'][l][None, :]
        md = [mod[0:2, i * d:(i + 1) * d] for i in range(N_MOD)]
        x_all = half_res(x_all, swiglu(norm_mod(x_all, rp['norm_ffn1'][l], md[0], md[1]), 'ffn1_w13', 'ffn1_w2'), md[2])

        z = mm(norm_mod(x_all, rp['norm_mix'][l], md[3], md[4]), 'w_in')
        part = {k: z[:, o:o + s] for k, (o, s) in offs.items()}

        q3 = mm(norm_tok(part['cq'], rp['mla_q_norm'][l]), 'mla_w_uq').reshape(n, MLA_HEADS, LANES)
        q = jnp.concatenate([q3[..., :MLA_NOPE], _axial_rope(q3[..., MLA_NOPE:MLA_NOPE + MLA_ROPE], cos_m, sin_m),
                             q3[..., MLA_NOPE + MLA_ROPE:]], axis=-1).reshape(n, MLA_HEADS * LANES)
        kvp = mm(norm_tok(part['ckv'], rp['mla_kv_norm'][l]), 'mla_w_ukv')
        kr = _axial_rope(part['kr'].reshape(n, 1, MLA_ROPE), cos_m, sin_m).reshape(n, MLA_ROPE)
        kr = jnp.pad(kr, ((0, 0), (MLA_NOPE, LANES - MLA_NOPE - MLA_ROPE)))
        mla = mm(mla_attn(q, kvp, kr), 'mla_w_o')

        u = part['u']
        y = u * rp['ssm_d'][l][None, :]
        for direction in range(2):
            a_re, a_im, bb_re, bb_im = _ssm_discretize(
                rp['ssm_lambda_re'][l, direction], rp['ssm_lambda_im'][l, direction], rp['ssm_log_dt'][l, direction],
                rp['ssm_b_re'][l, direction], rp['ssm_b_im'][l, direction])
            s_re, s_im = scans[direction](matmul_d(u, _block_diag_in(bb_re)), matmul_d(u, _block_diag_in(bb_im)),
                                          a_re.reshape(-1), a_im.reshape(-1))
            y = y + (matmul_d(s_re, _block_diag_out(rp['ssm_c_re'][l, direction]))
                     - matmul_d(s_im, _block_diag_out(rp['ssm_c_im'][l, direction])))
        yg = mm(jax.nn.gelu(y), 'ssm_w_glu')
        ssm = yg[:, :d] * jax.nn.sigmoid(yg[:, d:])

        gq = _axial_rope(part['gq'].reshape(n, GQA_HEADS, GQA_D), cos_g, sin_g)
        gk = _axial_rope(part['gk'].reshape(n, GQA_KV, GQA_D), cos_g, sin_g)
        q4 = jnp.transpose(gq.reshape(n, GQA_KV, GQA_G, GQA_D), (1, 2, 0, 3))
        k2 = jnp.transpose(gk, (1, 0, 2))
        v2 = jnp.transpose(part['gv'].reshape(n, GQA_KV, GQA_D), (1, 0, 2))
        sink = rp['gqa_sink'][l].reshape(GQA_KV, GQA_G, 1, 1)
        sink_rows = lambda rows: jnp.broadcast_to(sink, (GQA_KV, GQA_G, rows, 1)).reshape(GQA_KV, GQA_G * rows, 1)
        g_lat = gqa_band(q4[:, :, c_len:], k2, v2, sink_rows(BLOCK))
        if ctx_out:
            g_ctx = gqa_ctx(q4[:, :, :c_len], k2[:, :c_len], v2[:, :c_len], sink_rows(c_len))
        else:
            g_ctx = jnp.zeros((GQA_KV, GQA_G, c_len, GQA_D), F32)
        go = jnp.transpose(jnp.concatenate([g_ctx, g_lat], axis=2), (2, 0, 1, 3)).reshape(n, GQA_HEADS * GQA_D)
        gqa = mm(go, 'gqa_w_o')

        gates = jax.nn.sigmoid(part['gates'])
        mixed = gates[:, :d] * mla + gates[:, d:2 * d] * ssm + gates[:, 2 * d:] * gqa
        x_all = full_res(x_all, mm(mixed, 'w_out'), md[5])
        x_all = half_res(x_all, swiglu(norm_mod(x_all, rp['norm_ffn2'][l], md[6], md[7]), 'ffn2_w13', 'ffn2_w2'), md[8])

    return loss_head(norm_out(x_all[c_len:], rp['final_norm']), target)


def kernel(x, c, ctx, c_ctx, ada_w, ada_b, norm_ffn1, norm_mix, norm_ffn2, ffn1_w13, ffn1_w2, ffn2_w13, ffn2_w2, w_in, mla_q_norm, mla_kv_norm, mla_w_uq, mla_w_ukv, mla_w_o, ssm_lambda_re, ssm_lambda_im, ssm_log_dt, ssm_b_re, ssm_b_im, ssm_c_re, ssm_c_im, ssm_d, ssm_w_glu, gqa_sink, gqa_w_o, w_out, final_norm, loss_target, m_c_ctx, m_ada_w, m_ada_b, m_norm_ffn1, m_norm_mix, m_norm_ffn2, m_ffn1_w13, m_ffn1_w2, m_ffn2_w13, m_ffn2_w2, m_w_in, m_mla_q_norm, m_mla_kv_norm, m_mla_w_uq, m_mla_w_ukv, m_mla_w_o, m_ssm_lambda_re, m_ssm_lambda_im, m_ssm_log_dt, m_ssm_b_re, m_ssm_b_im, m_ssm_c_re, m_ssm_c_im, m_ssm_d, m_ssm_w_glu, m_gqa_sink, m_gqa_w_o, m_w_out, m_final_norm, v_c_ctx, v_ada_w, v_ada_b, v_norm_ffn1, v_norm_mix, v_norm_ffn2, v_ffn1_w13, v_ffn1_w2, v_ffn2_w13, v_ffn2_w2, v_w_in, v_mla_q_norm, v_mla_kv_norm, v_mla_w_uq, v_mla_w_ukv, v_mla_w_o, v_ssm_lambda_re, v_ssm_lambda_im, v_ssm_log_dt, v_ssm_b_re, v_ssm_b_im, v_ssm_c_re, v_ssm_c_im, v_ssm_d, v_ssm_w_glu, v_gqa_sink, v_gqa_w_o, v_w_out, v_final_norm):
    args = dict(locals())
    weights = {k: args[k] for k in WEIGHTS}
    moments_m = {k: args['m_' + k] for k in WEIGHTS}
    moments_v = {k: args['v_' + k] for k in WEIGHTS}
    depth = ada_w.shape[0]
    d = x.shape[-1]
    c_len = ctx.shape[1]
    my_c = lax.axis_index('c')
    my_chip = 2 * lax.axis_index('x') + lax.axis_index('y')
    qk_w = MLA_NOPE + MLA_ROPE

    shard_shapes = [weights[k].shape[1:] for k in SHARDED]
    layers = []
    for l in range(depth):
        gathered = _all_gather(_pack([weights[k][l] for k in SHARDED], (), BF16))
        full = {}
        for k, blocks in zip(SHARDED, _unpack(gathered, (N_DEV,), shard_shapes)):
            r, cdim = blocks.shape[1:]
            if k in ROW_SHARDED:
                full[k] = blocks.reshape(N_DEV * r, cdim)
            else:
                full[k] = jnp.moveaxis(blocks, 0, 1).reshape(r, N_DEV * cdim)
        full['w_in'] = _permute_w_in(full['w_in'], d)
        full['mla_w_uq'] = jnp.pad(full['mla_w_uq'].reshape(MLA_Q_RANK, MLA_HEADS, qk_w),
                                   ((0, 0), (0, 0), (0, LANES - qk_w))).reshape(MLA_Q_RANK, MLA_HEADS * LANES)
        full['mla_w_o'] = jnp.pad(full['mla_w_o'].reshape(MLA_HEADS, MLA_V, d),
                                  ((0, 0), (LANES - MLA_V, 0), (0, 0))).reshape(MLA_HEADS * LANES, d)
        layers.append(full)
    sinks = [{k: jnp.zeros(v.shape, F32) for k, v in full.items()} for full in layers]

    rp = {k: weights[k] for k in REPLICATED}
    x_all = jnp.concatenate([ctx[0], x[0]], axis=0)
    loss_fn = functools.partial(_forward_loss, wl=layers, cc_in=c[0], target=loss_target[0], c_len=c_len)
    loss, vjp = jax.vjp(loss_fn, x_all, sinks, rp)
    g_x, g_layers, g_rp = vjp(jnp.ones((), F32))

    grads = {k: [] for k in SHARDED}
    for l in range(depth):
        gl = dict(g_layers[l])
        gl['w_in'] = _unpermute_w_in(gl['w_in'], d)
        gl['mla_w_uq'] = gl['mla_w_uq'].reshape(MLA_Q_RANK, MLA_HEADS, LANES)[:, :, :qk_w].reshape(MLA_Q_RANK, MLA_HEADS * qk_w)
        gl['mla_w_o'] = gl['mla_w_o'].reshape(MLA_HEADS, LANES, d)[:, LANES - MLA_V:].reshape(MLA_HEADS * MLA_V, d)
        pieces = []
        for k in SHARDED:
            r, cdim = weights[k].shape[1:]
            g = gl[k]
            if k in ROW_SHARDED:
                pieces.append(g.reshape(N_DEV, r, cdim))
            else:
                pieces.append(jnp.moveaxis(g.reshape(r, N_DEV, cdim), 1, 0))
        packed = _pack(pieces, (N_DEV,), F32)
        g_all = jnp.swapaxes(packed.reshape((4, 2) + packed.shape[1:]), 0, 1)
        mine = _reduce_scatter(g_all, my_c.reshape(1).astype(jnp.int32), my_chip.reshape(1).astype(jnp.int32))
        for k, g in zip(SHARDED, _unpack(mine, (), shard_shapes)):
            grads[k].append(g)
    grads = {k: jnp.stack(v) for k, v in grads.items()}

    rep_shapes = [weights[k].shape for k in REPLICATED] + [(1,)]
    small = _pack([g_rp[k] for k in REPLICATED] + [loss.reshape(1)], (), F32)
    summed = _unpack(_sum_leading(_all_gather(small)), (), rep_shapes)
    for k, g in zip(REPLICATED, summed[:-1]):
        grads[k] = g
    loss_total = summed[-1].reshape(())

    delta, new_m, new_v = {}, {}, {}
    for k in SHARDED:
        shape = weights[k].shape
        as2d = lambda a: a.reshape(-1, shape[-1])
        outs = _adamw_call(as2d(weights[k]), as2d(grads[k]), as2d(moments_m[k]), as2d(moments_v[k]))
        delta[k], new_m[k], new_v[k] = (o.reshape(shape) for o in outs)
    rep_all = [weights[k].shape for k in REPLICATED]
    packs = [_pack([src[k] for k in REPLICATED], (), F32) for src in (weights, grads, moments_m, moments_v)]
    outs = [_unpack(o, (), rep_all) for o in _adamw_call(*packs)]
    for i, k in enumerate(REPLICATED):
        delta[k], new_m[k], new_v[k] = outs[0][i], outs[1][i], outs[2][i]

    return (loss_total, g_x[c_len:][None], *[grads[k] for k in WEIGHTS], *[delta[k] for k in WEIGHTS],
            *[new_m[k] for k in WEIGHTS], *[new_v[k] for k in WEIGHTS])
```

```python
import functools
import math

import jax
import jax.numpy as jnp
from jax import lax
from jax.experimental import pallas as pl
from jax.experimental.pallas import tpu as pltpu

F32 = jnp.float32
BF16 = jnp.bfloat16

MLA_HEADS, MLA_NOPE, MLA_ROPE, MLA_V = 8, 64, 32, 64
MLA_Q_RANK, MLA_KV_RANK = 384, 256
SSM_WIDTH, SSM_GROUP, SSM_STATE = 512, 16, 64
SSM_GROUPS = SSM_WIDTH // SSM_GROUP
GQA_HEADS, GQA_KV, GQA_D = 8, 2, 64
GQA_G = GQA_HEADS // GQA_KV
WINDOW, BLOCK, GRID_W = 128, 128, 64
N_MOD = 9
ROPE_BASE = 10000.0
EPS = 1e-6
NEG_INF = -1e30
ADAM_LR, ADAM_B1, ADAM_B2, ADAM_EPS, ADAM_WD, ADAM_STEP = 0.001, 0.9, 0.999, 1e-08, 0.01, 10

N_DEV = 8
LANES = 128
SUBLANES = 8
VMEM_LIMIT = 56 * 1024 * 1024
PACK_ROWS = 1024

SHARDED = ('ada_w', 'ffn1_w13', 'ffn1_w2', 'ffn2_w13', 'ffn2_w2', 'w_in', 'mla_w_uq', 'mla_w_ukv',
           'mla_w_o', 'ssm_w_glu', 'gqa_w_o', 'w_out')
ROW_SHARDED = ('ffn1_w2', 'ffn2_w2', 'w_out')
REPLICATED = ('c_ctx', 'ada_b', 'norm_ffn1', 'norm_mix', 'norm_ffn2', 'mla_q_norm', 'mla_kv_norm',
              'ssm_lambda_re', 'ssm_lambda_im', 'ssm_log_dt', 'ssm_b_re', 'ssm_b_im', 'ssm_c_re', 'ssm_c_im',
              'ssm_d', 'gqa_sink', 'final_norm')
WEIGHTS = ('c_ctx', 'ada_w', 'ada_b', 'norm_ffn1', 'norm_mix', 'norm_ffn2', 'ffn1_w13', 'ffn1_w2', 'ffn2_w13',
           'ffn2_w2', 'w_in', 'mla_q_norm', 'mla_kv_norm', 'mla_w_uq', 'mla_w_ukv', 'mla_w_o', 'ssm_lambda_re',
           'ssm_lambda_im', 'ssm_log_dt', 'ssm_b_re', 'ssm_b_im', 'ssm_c_re', 'ssm_c_im', 'ssm_d', 'ssm_w_glu',
           'gqa_sink', 'gqa_w_o', 'w_out', 'final_norm')


def _tile(n, target, mult):
    t = (min(target, n) // mult) * mult
    while t >= mult:
        if n % t == 0:
            return t
        t -= mult
    return n


def _cparams(sem):
    return pltpu.CompilerParams(dimension_semantics=sem, vmem_limit_bytes=VMEM_LIMIT)


def _mm(a, b, mode):
    if mode == 'nn':
        (M, K), N = a.shape, b.shape[1]
        tm, tn, tk = _tile(M, 1408, SUBLANES), _tile(N, 1024, LANES), _tile(K, 1408, LANES)
    elif mode == 'nt':
        (M, K), N = a.shape, b.shape[0]
        tm, tn, tk = _tile(M, 1408, SUBLANES), _tile(N, 1024, LANES), _tile(K, 1408, LANES)
    else:
        (K, M), N = a.shape, b.shape[1]
        tm, tn, tk = _tile(M, 1408, LANES), _tile(N, 1408, LANES), _tile(K, 768, 2 * SUBLANES)
    nk = K // tk
    dims = {'nn': (((1,), (0,)), ((), ())), 'nt': (((1,), (1,)), ((), ())), 'tn': (((0,), (0,)), ((), ()))}[mode]
    keep_a = nk == 1 and mode != 'tn' and N // tn > 1

    def body(a_ref, b_ref, o_ref, *scratch):
        if keep_a:
            @pl.when(pl.program_id(1) == 0)
            def _():
                scratch[0][...] = a_ref[...].astype(BF16)

            av = scratch[0][...]
        else:
            av = a_ref[...].astype(BF16)
        part = lax.dot_general(av, b_ref[...].astype(BF16), dims, preferred_element_type=F32)
        if nk == 1:
            o_ref[...] = part
        else:
            @pl.when(pl.program_id(2) == 0)
            def _():
                o_ref[...] = part

            @pl.when(pl.program_id(2) > 0)
            def _():
                o_ref[...] += part

    a_spec = pl.BlockSpec((tk, tm), lambda i, j, k: (k, i)) if mode == 'tn' else pl.BlockSpec((tm, tk), lambda i, j, k: (i, k))
    b_spec = pl.BlockSpec((tn, tk), lambda i, j, k: (j, k)) if mode == 'nt' else pl.BlockSpec((tk, tn), lambda i, j, k: (k, j))
    return pl.pallas_call(
        body, name='mm_' + mode, grid=(M // tm, N // tn, nk),
        in_specs=[a_spec, b_spec], out_specs=pl.BlockSpec((tm, tn), lambda i, j, k: (i, j)),
        out_shape=jax.ShapeDtypeStruct((M, N), F32),
        scratch_shapes=[pltpu.VMEM((tm, tk), BF16)] if keep_a else [],
        compiler_params=_cparams(('parallel', 'arbitrary', 'arbitrary')),
    )(a, b)


@jax.custom_vjp
def matmul_d(x, w):
    return _mm(x, w, 'nn')


def _matmul_d_fwd(x, w):
    return _mm(x, w, 'nn'), (x, w)


def _matmul_d_bwd(res, g):
    x, w = res
    return _mm(g, w, 'nt'), _mm(x, g, 'tn')


matmul_d.defvjp(_matmul_d_fwd, _matmul_d_bwd)


@jax.custom_vjp
def matmul(x, w, sink):
    return _mm(x, w, 'nn')


def _matmul_fwd(x, w, sink):
    return _mm(x, w, 'nn'), (x, w)


def _matmul_bwd(res, g):
    x, w = res
    return _mm(g, w, 'nt'), jnp.zeros_like(w), _mm(x, g, 'tn')


matmul.defvjp(_matmul_fwd, _matmul_bwd)


@jax.custom_vjp
def matmul_t(x, wt, sink):
    return _mm(x, wt, 'nt')


def _matmul_t_fwd(x, wt, sink):
    return _mm(x, wt, 'nt'), (x, wt)


def _matmul_t_bwd(res, g):
    x, wt = res
    return _mm(g, wt, 'nn'), jnp.zeros_like(wt), _mm(g, x, 'tn')


matmul_t.defvjp(_matmul_t_fwd, _matmul_t_bwd)


def _norm_fwd_call(x, g, sh, sc, rb):
    n, d = x.shape
    has_mod = sh is not None

    def body(*refs):
        x_ref, g_ref = refs[0], refs[1]
        o_ref = refs[-1]
        xv = x_ref[...]
        r = lax.rsqrt(jnp.mean(xv * xv, axis=-1, keepdims=True) + EPS)
        y = xv * r * g_ref[...]
        if has_mod:
            lat = pl.program_id(0) > 0
            shv = jnp.where(lat, refs[2][1:2, :], refs[2][0:1, :])
            scv = jnp.where(lat, refs[3][1:2, :], refs[3][0:1, :])
            y = y * (1.0 + scv) + shv
        o_ref[...] = y

    row = pl.BlockSpec((rb, d), lambda i: (i, 0))
    vec = pl.BlockSpec((1, d), lambda i: (0, 0))
    two = pl.BlockSpec((2, d), lambda i: (0, 0))
    ins = [x, g.reshape(1, d)] + ([sh, sc] if has_mod else [])
    return pl.pallas_call(
        body, name='norm_fwd', grid=(n // rb,), in_specs=[row, vec] + ([two, two] if has_mod else []),
        out_specs=row, out_shape=jax.ShapeDtypeStruct((n, d), F32), compiler_params=_cparams(('parallel',)),
    )(*ins)


def _norm_bwd_call(x, g, sh, sc, dy, rb):
    n, d = x.shape
    has_mod = sh is not None

    def body(*refs):
        x_ref, g_ref, dy_ref = refs[0], refs[1], refs[-3]
        dx_ref, acc_ref = refs[-2], refs[-1]
        i = pl.program_id(0)
        xv, dyv, gv = x_ref[...], dy_ref[...], g_ref[...]
        r = lax.rsqrt(jnp.mean(xv * xv, axis=-1, keepdims=True) + EPS)
        xh = xv * r
        if has_mod:
            lat = i > 0
            scv = jnp.where(lat, refs[3][1:2, :], refs[3][0:1, :])
            dyg = dyv * (1.0 + scv)
        else:
            dyg = dyv
        dxh = dyg * gv
        dx_ref[...] = r * (dxh - xh * jnp.mean(dxh * xh, axis=-1, keepdims=True))

        @pl.when(i == 0)
        def _():
            acc_ref[...] = jnp.zeros_like(acc_ref)

        acc_ref[0:1, :] += jnp.sum(dyg * xh, axis=0, keepdims=True)
        if has_mod:
            dsh = jnp.sum(dyv, axis=0, keepdims=True)
            dsc = jnp.sum(dyv * xh * gv, axis=0, keepdims=True)

            @pl.when(i == 0)
            def _():
                acc_ref[1:2, :] += dsh
                acc_ref[3:4, :] += dsc

            @pl.when(i > 0)
            def _():
                acc_ref[2:3, :] += dsh
                acc_ref[4:5, :] += dsc

    row = pl.BlockSpec((rb, d), lambda i: (i, 0))
    vec = pl.BlockSpec((1, d), lambda i: (0, 0))
    two = pl.BlockSpec((2, d), lambda i: (0, 0))
    ins = [x, g.reshape(1, d)] + ([sh, sc] if has_mod else []) + [dy]
    return pl.pallas_call(
        body, name='norm_bwd', grid=(n // rb,), in_specs=[row, vec] + ([two, two] if has_mod else []) + [row],
        out_specs=[row, pl.BlockSpec((SUBLANES, d), lambda i: (0, 0))],
        out_shape=[jax.ShapeDtypeStruct((n, d), F32), jax.ShapeDtypeStruct((SUBLANES, d), F32)],
        compiler_params=_cparams(('arbitrary',)),
    )(*ins)


def _make_norm(rb, has_mod):
    if has_mod:
        @jax.custom_vjp
        def f(x, g, sh, sc):
            return _norm_fwd_call(x, g, sh, sc, rb)

        def fwd(x, g, sh, sc):
            return _norm_fwd_call(x, g, sh, sc, rb), (x, g, sh, sc)

        def bwd(res, dy):
            x, g, sh, sc = res
            dx, acc = _norm_bwd_call(x, g, sh, sc, dy, rb)
            return dx, acc[0], acc[1:3], acc[3:5]
    else:
        @jax.custom_vjp
        def f(x, g):
            return _norm_fwd_call(x, g, None, None, rb)

        def fwd(x, g):
            return _norm_fwd_call(x, g, None, None, rb), (x, g)

        def bwd(res, dy):
            x, g = res
            dx, acc = _norm_bwd_call(x, g, None, None, dy, rb)
            return dx, acc[0]
    f.defvjp(fwd, bwd)
    return f


def _make_gated_res(rb, coef):
    def fwd_call(x, f, gate):
        n, d = x.shape

        def body(x_ref, f_ref, g_ref, o_ref):
            gv = jnp.where(pl.program_id(0) > 0, g_ref[1:2, :], g_ref[0:1, :])
            o_ref[...] = x_ref[...] + coef * gv * f_ref[...]

        row = pl.BlockSpec((rb, d), lambda i: (i, 0))
        return pl.pallas_call(
            body, name='gated_res_fwd', grid=(n // rb,), in_specs=[row, row, pl.BlockSpec((2, d), lambda i: (0, 0))],
            out_specs=row, out_shape=jax.ShapeDtypeStruct((n, d), F32), compiler_params=_cparams(('parallel',)),
        )(x, f, gate)

    def bwd_call(dy, f, gate):
        n, d = dy.shape

        def body(dy_ref, f_ref, g_ref, df_ref, acc_ref):
            i = pl.program_id(0)
            gv = jnp.where(i > 0, g_ref[1:2, :], g_ref[0:1, :])
            dyv = dy_ref[...]
            df_ref[...] = coef * gv * dyv
            part = coef * jnp.sum(dyv * f_ref[...], axis=0, keepdims=True)

            @pl.when(i == 0)
            def _():
                acc_ref[...] = jnp.zeros_like(acc_ref)
                acc_ref[0:1, :] += part

            @pl.when(i > 0)
            def _():
                acc_ref[1:2, :] += part

        row = pl.BlockSpec((rb, d), lambda i: (i, 0))
        return pl.pallas_call(
            body, name='gated_res_bwd', grid=(n // rb,), in_specs=[row, row, pl.BlockSpec((2, d), lambda i: (0, 0))],
            out_specs=[row, pl.BlockSpec((SUBLANES, d), lambda i: (0, 0))],
            out_shape=[jax.ShapeDtypeStruct((n, d), F32), jax.ShapeDtypeStruct((SUBLANES, d), F32)],
            compiler_params=_cparams(('arbitrary',)),
        )(dy, f, gate)

    @jax.custom_vjp
    def f(x, fv, gate):
        return fwd_call(x, fv, gate)

    def fwd(x, fv, gate):
        return fwd_call(x, fv, gate), (fv, gate)

    def bwd(res, dy):
        fv, gate = res
        df, acc = bwd_call(dy, fv, gate)
        return dy, df, acc[0:2]

    f.defvjp(fwd, bwd)
    return f


MLA_SCALE = (MLA_NOPE + MLA_ROPE) ** -0.5
NT_DIMS = (((1,), (1,)), ((), ()))


def _mla_keys(kv, kr):
    lane = lax.broadcasted_iota(jnp.int32, kv.shape, 1)
    return jnp.where(lane < MLA_NOPE, kv, kr).astype(BF16), kv.astype(BF16)


def _mla_fwd_call(q, kv, kr, c_len):
    n = q.shape[0]
    h = q.shape[1] // LANES
    tq = BLOCK
    cb = c_len // tq

    def body(q_ref, kv_ref, kr_ref, o_ref, lse_ref, kb, vb):
        i = pl.program_id(1)

        @pl.when(i == 0)
        def _():
            kb[...], vb[...] = _mla_keys(kv_ref[...], kr_ref[...])

        qv = q_ref[...].astype(BF16)

        def attend(nk):
            s = lax.dot_general(qv, kb[:nk, :], NT_DIMS, preferred_element_type=F32) * MLA_SCALE
            m = jnp.max(s, axis=-1, keepdims=True)
            p = jnp.exp(s - m)
            l = jnp.sum(p, axis=-1, keepdims=True)
            o_ref[...] = jnp.dot(p.astype(BF16), vb[:nk, :], preferred_element_type=F32) / l
            lse_ref[0] = m + jnp.log(l)

        pl.when(i < cb)(lambda: attend(c_len))
        pl.when(i >= cb)(lambda: attend(n))

    qspec = pl.BlockSpec((tq, LANES), lambda a, i: (i, a))
    return pl.pallas_call(
        body, name='mla_attn_fwd', grid=(h, n // tq),
        in_specs=[qspec, pl.BlockSpec((n, LANES), lambda a, i: (0, a)), pl.BlockSpec((n, LANES), lambda a, i: (0, 0))],
        out_specs=[qspec, pl.BlockSpec((1, tq, 1), lambda a, i: (a, i, 0))],
        out_shape=[jax.ShapeDtypeStruct((n, h * LANES), F32), jax.ShapeDtypeStruct((h, n, 1), F32)],
        scratch_shapes=[pltpu.VMEM((n, LANES), BF16), pltpu.VMEM((n, LANES), BF16)],
        compiler_params=_cparams(('arbitrary', 'arbitrary')),
    )(q, kv, kr)


def _mla_dq_call(q, kv, kr, o, lse, do, c_len):
    n = q.shape[0]
    h = q.shape[1] // LANES
    tq = BLOCK
    cb = c_len // tq

    def body(q_ref, kv_ref, kr_ref, o_ref, lse_ref, do_ref, dq_ref, delta_ref, kb, vb):
        i = pl.program_id(1)

        @pl.when(i == 0)
        def _():
            kb[...], vb[...] = _mla_keys(kv_ref[...], kr_ref[...])

        dov = do_ref[...]
        delta = jnp.sum(dov * o_ref[...], axis=-1, keepdims=True)
        delta_ref[0] = delta
        qv, dob = q_ref[...].astype(BF16), dov.astype(BF16)

        def grad(nk):
            s = lax.dot_general(qv, kb[:nk, :], NT_DIMS, preferred_element_type=F32) * MLA_SCALE
            p = jnp.exp(s - lse_ref[0])
            dp = lax.dot_general(dob, vb[:nk, :], NT_DIMS, preferred_element_type=F32)
            ds = p * (dp - delta) * MLA_SCALE
            dq_ref[...] = jnp.dot(ds.astype(BF16), kb[:nk, :], preferred_element_type=F32)

        pl.when(i < cb)(lambda: grad(c_len))
        pl.when(i >= cb)(lambda: grad(n))

    qspec = pl.BlockSpec((tq, LANES), lambda a, i: (i, a))
    col = pl.BlockSpec((1, tq, 1), lambda a, i: (a, i, 0))
    return pl.pallas_call(
        body, name='mla_attn_dq', grid=(h, n // tq),
        in_specs=[qspec, pl.BlockSpec((n, LANES), lambda a, i: (0, a)), pl.BlockSpec((n, LANES), lambda a, i: (0, 0)),
                  qspec, col, qspec],
        out_specs=[qspec, col],
        out_shape=[jax.ShapeDtypeStruct((n, h * LANES), F32), jax.ShapeDtypeStruct((h, n, 1), F32)],
        scratch_shapes=[pltpu.VMEM((n, LANES), BF16), pltpu.VMEM((n, LANES), BF16)],
        compiler_params=_cparams(('arbitrary', 'arbitrary')),
    )(q, kv, kr, o, lse, do)


def _mla_dkv_call(q, kv, kr, lse_row, delta_row, do, c_len):
    n = q.shape[0]
    h = q.shape[1] // LANES
    tk = BLOCK
    cb = c_len // tk

    def body(q_ref, kv_ref, kr_ref, lse_ref, delta_ref, do_ref, dkv_ref, dkf_ref, qb, dob):
        j = pl.program_id(1)

        @pl.when(j == 0)
        def _():
            qb[...] = q_ref[...].astype(BF16)
            dob[...] = do_ref[...].astype(BF16)

        kblk, vblk = _mla_keys(kv_ref[...], kr_ref[...])
        lane = lax.broadcasted_iota(jnp.int32, (tk, LANES), 1)

        def grad(q0):
            qs, dos = qb[q0:, :], dob[q0:, :]
            st = lax.dot_general(kblk, qs, NT_DIMS, preferred_element_type=F32) * MLA_SCALE
            pt = jnp.exp(st - lse_ref[0][:, q0:])
            dv = jnp.dot(pt.astype(BF16), dos, preferred_element_type=F32)
            dpt = lax.dot_general(vblk, dos, NT_DIMS, preferred_element_type=F32)
            dst = pt * (dpt - delta_ref[0][:, q0:]) * MLA_SCALE
            dk = jnp.dot(dst.astype(BF16), qs, preferred_element_type=F32)
            dkv_ref[...] = jnp.where(lane < MLA_NOPE, dk, dv)
            dkf_ref[0] = dk

        pl.when(j < cb)(lambda: grad(0))
        pl.when(j >= cb)(lambda: grad(c_len))

    full = pl.BlockSpec((n, LANES), lambda a, j: (0, a))
    row = pl.BlockSpec((1, 1, n), lambda a, j: (a, 0, 0))
    kspec = pl.BlockSpec((tk, LANES), lambda a, j: (j, a))
    return pl.pallas_call(
        body, name='mla_attn_dkv', grid=(h, n // tk),
        in_specs=[full, kspec, pl.BlockSpec((tk, LANES), lambda a, j: (j, 0)), row, row, full],
        out_specs=[kspec, pl.BlockSpec((1, tk, LANES), lambda a, j: (a, j, 0))],
        out_shape=[jax.ShapeDtypeStruct((n, h * LANES), F32), jax.ShapeDtypeStruct((h, n, LANES), F32)],
        scratch_shapes=[pltpu.VMEM((n, LANES), BF16), pltpu.VMEM((n, LANES), BF16)],
        compiler_params=_cparams(('arbitrary', 'arbitrary')),
    )(q, kv, kr, lse_row, delta_row, do)


def _sum_leading(g):
    nl, r, _ = g.shape
    rb = _tile(r, 512, SUBLANES)

    def body(g_ref, o_ref):
        acc = g_ref[0]
        for j in range(1, nl):
            acc = acc + g_ref[j]
        o_ref[...] = acc

    return pl.pallas_call(
        body, name='sum_leading', grid=(r // rb,), in_specs=[pl.BlockSpec((nl, rb, LANES), lambda i: (0, i, 0))],
        out_specs=pl.BlockSpec((rb, LANES), lambda i: (i, 0)), out_shape=jax.ShapeDtypeStruct((r, LANES), F32),
        compiler_params=_cparams(('parallel',)),
    )(g)


def _make_mla(c_len):
    @jax.custom_vjp
    def f(q, kv, kr):
        return _mla_fwd_call(q, kv, kr, c_len)[0]

    def fwd(q, kv, kr):
        o, lse = _mla_fwd_call(q, kv, kr, c_len)
        return o, (q, kv, kr, o, lse)

    def bwd(res, do):
        q, kv, kr, o, lse = res
        n = q.shape[0]
        h = q.shape[1] // LANES
        dq, delta = _mla_dq_call(q, kv, kr, o, lse, do, c_len)
        dkv, dk_full = _mla_dkv_call(q, kv, kr, lse.reshape(h, 1, n), delta.reshape(h, 1, n), do, c_len)
        return dq, dkv, _sum_leading(dk_full)

    f.defvjp(fwd, bwd)
    return f


def _gqa_specs(band, c_len, nb, rows):
    cb = c_len // BLOCK
    q_spec = pl.BlockSpec((1, GQA_G, rows, GQA_D), lambda a, b: (a, 0, b, 0))
    ctx_spec = pl.BlockSpec((1, c_len, GQA_D), lambda a, b: (a, 0, 0))
    kv_specs = [ctx_spec]
    if band:
        kv_specs += [pl.BlockSpec((1, BLOCK, GQA_D), lambda a, b: (a, jnp.maximum(b - 1, 0) + cb, 0)),
                     pl.BlockSpec((1, BLOCK, GQA_D), lambda a, b: (a, b + cb, 0)),
                     pl.BlockSpec((1, BLOCK, GQA_D), lambda a, b: (a, jnp.minimum(b + 1, nb - 1) + cb, 0))]
    sink_spec = pl.BlockSpec((1, GQA_G * rows, 1), lambda a, b: (a, 0, 0))
    return q_spec, kv_specs, sink_spec


def _gqa_scores(q, kcat, sink, band, c_len, t_len, rows):
    scale = GQA_D ** -0.5
    s = lax.dot_general(q, kcat, (((1,), (1,)), ((), ())), preferred_element_type=F32) * scale
    if band:
        b = pl.program_id(1)
        shape = s.shape
        col = lax.broadcasted_iota(jnp.int32, shape, 1)
        qpos = b * BLOCK + (lax.broadcasted_iota(jnp.int32, shape, 0) & (BLOCK - 1))
        kpos = (b - 1) * BLOCK + (col - c_len)
        valid = (col < c_len) | ((jnp.abs(qpos - kpos) <= WINDOW) & (kpos >= 0) & (kpos < t_len))
        s = jnp.where(valid, s, NEG_INF)
    m = jnp.maximum(jnp.max(s, axis=-1, keepdims=True), sink)
    e = jnp.exp(s - m)
    es = jnp.exp(sink - m)
    den = es + jnp.sum(e, axis=-1, keepdims=True)
    return e / den, es / den


def _gqa_fwd_call(q4, k2, v2, sink_rows, band, c_len):
    kv, g, tq_all, d = q4.shape
    rows = BLOCK if band else tq_all
    nb = tq_all // rows
    t_len = k2.shape[1] - c_len
    nkv = 4 if band else 1
    q_spec, kv_specs, sink_spec = _gqa_specs(band, c_len, nb, rows)

    def body(*refs):
        q_ref, sink_ref, o_ref = refs[0], refs[1 + 2 * nkv], refs[-1]
        kcat = jnp.concatenate([r[0] for r in refs[1:1 + nkv]], axis=0).astype(BF16)
        vcat = jnp.concatenate([r[0] for r in refs[1 + nkv:1 + 2 * nkv]], axis=0).astype(BF16)
        q = q_ref[0].reshape(g * rows, d).astype(BF16)
        p, _ = _gqa_scores(q, kcat, sink_ref[0], band, c_len, t_len, rows)
        o_ref[0] = jnp.dot(p.astype(BF16), vcat, preferred_element_type=F32).reshape(g, rows, d)

    return pl.pallas_call(
        body, name='gqa_fwd_band' if band else 'gqa_fwd_ctx', grid=(kv, nb),
        in_specs=[q_spec] + kv_specs + kv_specs + [sink_spec], out_specs=q_spec,
        out_shape=jax.ShapeDtypeStruct(q4.shape, F32), compiler_params=_cparams(('parallel', 'parallel')),
    )(q4, *([k2] * nkv), *([v2] * nkv), sink_rows)


def _gqa_bwd_call(q4, k2, v2, sink_rows, do4, band, c_len):
    kv, g, tq_all, d = q4.shape
    rows = BLOCK if band else tq_all
    nb = tq_all // rows
    t_len = k2.shape[1] - c_len
    nkv = 4 if band else 1
    scale = GQA_D ** -0.5
    q_spec, kv_specs, sink_spec = _gqa_specs(band, c_len, nb, rows)

    def body(*refs):
        q_ref, sink_ref, do_ref = refs[0], refs[1 + 2 * nkv], refs[2 + 2 * nkv]
        outs = refs[3 + 2 * nkv:]
        dq_ref, dkc_ref, dvc_ref = outs[0], outs[1], outs[2]
        dsink_ref = outs[-1]
        b = pl.program_id(1)
        kcat = jnp.concatenate([r[0] for r in refs[1:1 + nkv]], axis=0).astype(BF16)
        vcat = jnp.concatenate([r[0] for r in refs[1 + nkv:1 + 2 * nkv]], axis=0).astype(BF16)
        q = q_ref[0].reshape(g * rows, d).astype(BF16)
        do = do_ref[0].reshape(g * rows, d).astype(BF16)
        p, p_sink = _gqa_scores(q, kcat, sink_ref[0], band, c_len, t_len, rows)
        dp = lax.dot_general(do, vcat, (((1,), (1,)), ((), ())), preferred_element_type=F32)
        rd = jnp.sum(p * dp, axis=-1, keepdims=True)
        ds = (p * (dp - rd) * scale).astype(BF16)
        dq_ref[0] = jnp.dot(ds, kcat, preferred_element_type=F32).reshape(g, rows, d)
        dkcat = lax.dot_general(ds, q, (((0,), (0,)), ((), ())), preferred_element_type=F32)
        dvcat = lax.dot_general(p.astype(BF16), do, (((0,), (0,)), ((), ())), preferred_element_type=F32)

        @pl.when(b == 0)
        def _():
            dkc_ref[...] = jnp.zeros_like(dkc_ref)
            dvc_ref[...] = jnp.zeros_like(dvc_ref)
            dsink_ref[...] = jnp.zeros_like(dsink_ref)

        dkc_ref[0] += dkcat[:c_len]
        dvc_ref[0] += dvcat[:c_len]
        dsink_ref[0] += -p_sink * rd
        if band:
            outs[3][0, 0] = dkcat[c_len:]
            outs[4][0, 0] = dvcat[c_len:]

    ctx_out = pl.BlockSpec((1, c_len, d), lambda a, b: (a, 0, 0))
    band_out = pl.BlockSpec((1, 1, 3 * BLOCK, d), lambda a, b: (a, b, 0, 0))
    out_specs = [q_spec, ctx_out, ctx_out] + ([band_out, band_out] if band else []) + [sink_spec]
    ctx_shape = jax.ShapeDtypeStruct((kv, c_len, d), F32)
    band_shape = jax.ShapeDtypeStruct((kv, nb, 3 * BLOCK, d), F32)
    out_shape = ([jax.ShapeDtypeStruct(q4.shape, F32), ctx_shape, ctx_shape] + ([band_shape, band_shape] if band else [])
                 + [jax.ShapeDtypeStruct(sink_rows.shape, F32)])
    return pl.pallas_call(
        body, name='gqa_bwd_band' if band else 'gqa_bwd_ctx', grid=(kv, nb),
        in_specs=[q_spec] + kv_specs + kv_specs + [sink_spec, q_spec], out_specs=out_specs, out_shape=out_shape,
        compiler_params=_cparams(('arbitrary', 'arbitrary')),
    )(q4, *([k2] * nkv), *([v2] * nkv), sink_rows, do4)


def _make_gqa(band, c_len):
    @jax.custom_vjp
    def f(q4, k2, v2, sink_rows):
        return _gqa_fwd_call(q4, k2, v2, sink_rows, band, c_len)

    def fwd(q4, k2, v2, sink_rows):
        return _gqa_fwd_call(q4, k2, v2, sink_rows, band, c_len), (q4, k2, v2, sink_rows)

    def bwd(res, do4):
        q4, k2, v2, sink_rows = res
        outs = _gqa_bwd_call(q4, k2, v2, sink_rows, do4, band, c_len)
        kv, n, d = k2.shape
        if not band:
            dq4, dkc, dvc, dsink = outs
            return dq4, dkc, dvc, dsink
        dq4, dkc, dvc, dkb, dvb, dsink = outs

        def fold(ctx_part, bands):
            cur = bands[:, :, BLOCK:2 * BLOCK]
            prv = jnp.pad(bands[:, 1:, :BLOCK], ((0, 0), (0, 1), (0, 0), (0, 0)))
            nxt = jnp.pad(bands[:, :-1, 2 * BLOCK:], ((0, 0), (1, 0), (0, 0), (0, 0)))
            lat = (cur + prv + nxt).reshape(kv, n - c_len, d)
            return jnp.concatenate([ctx_part, lat], axis=1)

        return dq4, fold(dkc, dkb), fold(dvc, dvb), dsink

    f.defvjp(fwd, bwd)
    return f


def _cmul(ar, ai, br, bi):
    return ar * br - ai * bi, ar * bi + ai * br


def _scan_tables(ar, ai, desc):
    a1 = (ar, ai)
    a2 = _cmul(*a1, *a1)
    a4 = _cmul(*a2, *a2)
    pw = [a1]
    for _ in range(SUBLANES - 1):
        pw.append(_cmul(*pw[-1], *a1))
    row = jnp.arange(SUBLANES)[:, None]
    tabs = []
    for dist, (pr, pi) in ((1, a1), (2, a2), (4, a4)):
        keep = (row <= SUBLANES - 1 - dist) if desc else (row >= dist)
        tabs += [jnp.where(keep, pr[None, :], 0.0), jnp.where(keep, pi[None, :], 0.0)]
    order = pw[::-1] if desc else pw
    tabs += [jnp.stack([p[0] for p in order]), jnp.stack([p[1] for p in order])]
    return jnp.stack(tabs).astype(F32)


def _scan_call(b_re, b_im, tabs, order, chunk, prev=None):
    n, s_dim = b_re.shape
    nch = n // chunk
    ng = chunk // SUBLANES
    desc = order in ('Fb', 'R')
    with_da = prev is not None

    def chunk_of(i):
        if order == 'F':
            return i
        if order == 'Fb':
            return nch - 1 - i
        if order == 'R':
            return jnp.where(i == 0, 0, nch - i)
        return jnp.where(i == nch - 1, 0, i + 1)

    def body(*refs):
        br_ref, bi_ref, tab_ref = refs[0], refs[1], refs[2]
        if with_da:
            pr_ref, pi_ref, sr_ref, si_ref, dar_ref, dai_ref, cr_ref, ci_ref = refs[3:]
        else:
            sr_ref, si_ref, pr_ref, pi_ref, cr_ref, ci_ref = refs[3:]

        @pl.when(pl.program_id(0) == 0)
        def _():
            cr_ref[...] = jnp.zeros_like(cr_ref)
            ci_ref[...] = jnp.zeros_like(ci_ref)
            if with_da:
                dar_ref[...] = jnp.zeros_like(dar_ref)
                dai_ref[...] = jnp.zeros_like(dai_ref)

        sub = lax.broadcasted_iota(jnp.int32, (SUBLANES, s_dim), 0)
        edge = SUBLANES - 1 if desc else 0
        last = 0 if desc else SUBLANES - 1

        def step(t, carry):
            gi = (ng - 1 - t) if desc else t
            rows = pl.ds(pl.multiple_of(gi * SUBLANES, SUBLANES), SUBLANES)
            xr, xi = br_ref[rows, :], bi_ref[rows, :]
            for j, dist in enumerate((1, 2, 4)):
                shift = SUBLANES - dist if desc else dist
                rr, ri = pltpu.roll(xr, shift, 0), pltpu.roll(xi, shift, 0)
                mr, mi = tab_ref[2 * j], tab_ref[2 * j + 1]
                xr, xi = xr + mr * rr - mi * ri, xi + mr * ri + mi * rr
            cr, ci = cr_ref[...], ci_ref[...]
            pwr, pwi = tab_ref[6], tab_ref[7]
            sr = xr + pwr * cr - pwi * ci
            si = xi + pwr * ci + pwi * cr
            sr_ref[rows, :] = sr
            si_ref[rows, :] = si
            if with_da:
                pr, pi = pr_ref[rows, :], pi_ref[rows, :]
                dar_ref[...] += sr * pr + si * pi
                dai_ref[...] += si * pr - sr * pi
            else:
                shift1 = SUBLANES - 1 if desc else 1
                pr_ref[rows, :] = jnp.where(sub == edge, cr, pltpu.roll(sr, shift1, 0))
                pi_ref[rows, :] = jnp.where(sub == edge, ci, pltpu.roll(si, shift1, 0))
            cr_ref[...] = jnp.broadcast_to(sr[last:last + 1, :], (SUBLANES, s_dim))
            ci_ref[...] = jnp.broadcast_to(si[last:last + 1, :], (SUBLANES, s_dim))
            return carry

        lax.fori_loop(0, ng, step, 0)

    blk = pl.BlockSpec((chunk, s_dim), lambda i: (chunk_of(i), 0))
    tab_spec = pl.BlockSpec((8, SUBLANES, s_dim), lambda i: (0, 0, 0))
    acc = pl.BlockSpec((SUBLANES, s_dim), lambda i: (0, 0))
    big = jax.ShapeDtypeStruct((n, s_dim), F32)
    small = jax.ShapeDtypeStruct((SUBLANES, s_dim), F32)
    if with_da:
        in_specs, ins = [blk, blk, tab_spec, blk, blk], [b_re, b_im, tabs, prev[0], prev[1]]
        out_specs, out_shape = [blk, blk, acc, acc], [big, big, small, small]
    else:
        in_specs, ins = [blk, blk, tab_spec], [b_re, b_im, tabs]
        out_specs, out_shape = [blk, blk, blk, blk], [big, big, big, big]
    return pl.pallas_call(
        body, name='s5_scan_' + order, grid=(nch,), in_specs=in_specs, out_specs=out_specs, out_shape=out_shape,
        scratch_shapes=[pltpu.VMEM((SUBLANES, s_dim), F32), pltpu.VMEM((SUBLANES, s_dim), F32)],
        compiler_params=_cparams(('arbitrary',)),
    )(*ins)


def _make_scan(rev, chunk):
    def run(b_re, b_im, ar, ai):
        tabs = _scan_tables(ar, ai, desc=rev)
        return _scan_call(b_re, b_im, tabs, 'R' if rev else 'F', chunk)

    @jax.custom_vjp
    def f(b_re, b_im, ar, ai):
        return tuple(run(b_re, b_im, ar, ai)[:2])

    def fwd(b_re, b_im, ar, ai):
        s_re, s_im, p_re, p_im = run(b_re, b_im, ar, ai)
        return (s_re, s_im), (p_re, p_im, ar, ai)

    def bwd(res, g):
        p_re, p_im, ar, ai = res
        tabs = _scan_tables(ar, -ai, desc=not rev)
        db_re, db_im, dar, dai = _scan_call(g[0], g[1], tabs, 'Rb' if rev else 'Fb', chunk, prev=(p_re, p_im))
        return db_re, db_im, jnp.sum(dar, axis=0), jnp.sum(dai, axis=0)

    f.defvjp(fwd, bwd)
    return f


def _sqerr_call(y, t):
    n, d = y.shape
    rb = _tile(n, 512, SUBLANES)

    def body(y_ref, t_ref, o_ref):
        @pl.when(pl.program_id(0) == 0)
        def _():
            o_ref[...] = jnp.zeros_like(o_ref)

        e = y_ref[...] - t_ref[...]
        o_ref[...] += jnp.sum(e * e, axis=0, keepdims=True)

    row = pl.BlockSpec((rb, d), lambda i: (i, 0))
    return pl.pallas_call(
        body, name='sq_err', grid=(n // rb,), in_specs=[row, row], out_specs=pl.BlockSpec((1, d), lambda i: (0, 0)),
        out_shape=jax.ShapeDtypeStruct((1, d), F32), compiler_params=_cparams(('arbitrary',)),
    )(y, t)


@jax.custom_vjp
def loss_head(y, t):
    return 0.5 * jnp.sum(_sqerr_call(y, t)) / y.shape[1]


def _loss_head_fwd(y, t):
    return loss_head(y, t), (y, t)


def _loss_head_bwd(res, g):
    y, t = res
    return g * (y - t) / y.shape[1], None


loss_head.defvjp(_loss_head_fwd, _loss_head_bwd)


def _adamw_call(w, g, m, v):
    r, c = w.shape
    rb = _tile(r, max(SUBLANES, (256 * 1024) // max(c, LANES) // SUBLANES * SUBLANES), SUBLANES)

    def body(w_ref, g_ref, m_ref, v_ref, d_ref, nm_ref, nv_ref):
        gv = g_ref[...]
        nm = ADAM_B1 * m_ref[...] + (1.0 - ADAM_B1) * gv
        nv = ADAM_B2 * v_ref[...] + (1.0 - ADAM_B2) * (gv * gv)
        m_hat = nm / (1.0 - ADAM_B1 ** ADAM_STEP)
        v_hat = nv / (1.0 - ADAM_B2 ** ADAM_STEP)
        d_ref[...] = -ADAM_LR * (m_hat / (jnp.sqrt(v_hat) + ADAM_EPS) + ADAM_WD * w_ref[...])
        nm_ref[...] = nm
        nv_ref[...] = nv

    blk = pl.BlockSpec((rb, c), lambda i: (i, 0))
    shape = jax.ShapeDtypeStruct((r, c), F32)
    return pl.pallas_call(
        body, name='adamw', grid=(r // rb,), in_specs=[blk] * 4, out_specs=[blk] * 3, out_shape=[shape] * 3,
        compiler_params=_cparams(('parallel',)),
    )(w, g, m, v)


MESH = pl.DeviceIdType.MESH
HBM_SPEC = pl.BlockSpec(memory_space=pltpu.HBM)


def _all_gather(x):
    def body(x_ref, out_ref, send_sems, recv_sems, local_sem):
        x, y, c = lax.axis_index('x'), lax.axis_index('y'), lax.axis_index('c')
        me, sibling = (x, y, c), (x, y, 1 - c)
        chips = [(1 - x, y), (x, 1 - y), (1 - x, 1 - y)]

        def slot(px, py, pc):
            return out_ref.at[4 * px + 2 * py + pc]

        def copy(k, block, to, src=None):
            return pltpu.make_async_remote_copy(
                src_ref=slot(*block) if src is None else src, dst_ref=slot(*block),
                send_sem=send_sems.at[k], recv_sem=recv_sems.at[k], device_id=to, device_id_type=MESH)

        mine = pltpu.make_async_copy(x_ref, slot(*me), local_sem)
        mine.start()
        first = [copy(0, me, sibling, src=x_ref)]
        first += [copy(1 + j, me, (*chip, c), src=x_ref) for j, chip in enumerate(chips)]
        for cp in first:
            cp.start()
        passed = [copy(4 + j, (*chip, c), sibling) for j, chip in enumerate(chips)]
        for j, chip in enumerate(chips):
            copy(1 + j, (*chip, c), me).wait_recv()
            passed[j].start()
        copy(0, sibling, me).wait_recv()
        for j, chip in enumerate(chips):
            copy(4 + j, (*chip, 1 - c), me).wait_recv()
        for cp in first + passed:
            cp.wait_send()
        mine.wait()

    return pl.pallas_call(
        body, name='all_gather', out_shape=jax.ShapeDtypeStruct((N_DEV,) + x.shape, x.dtype),
        in_specs=[HBM_SPEC], out_specs=HBM_SPEC,
        scratch_shapes=[pltpu.SemaphoreType.DMA((7,)), pltpu.SemaphoreType.DMA((7,)), pltpu.SemaphoreType.DMA],
    )(x)


def _exchange_sibling(g_all):
    def body(g_ref, out_ref, send_sem, recv_sem):
        x, y, c = lax.axis_index('x'), lax.axis_index('y'), lax.axis_index('c')
        cp = pltpu.make_async_remote_copy(src_ref=g_ref.at[1 - c], dst_ref=out_ref, send_sem=send_sem, recv_sem=recv_sem,
                                          device_id=(x, y, 1 - c), device_id_type=MESH)
        cp.start()
        cp.wait()

    return pl.pallas_call(
        body, name='rs_sibling', out_shape=jax.ShapeDtypeStruct(g_all.shape[1:], g_all.dtype),
        in_specs=[HBM_SPEC], out_specs=HBM_SPEC,
        scratch_shapes=[pltpu.SemaphoreType.DMA, pltpu.SemaphoreType.DMA],
    )(g_all)


def _exchange_chips(p):
    def body(p_ref, out_ref, send_sems, recv_sems):
        x, y, c = lax.axis_index('x'), lax.axis_index('y'), lax.axis_index('c')
        chips = [(1 - x, y), (x, 1 - y), (1 - x, 1 - y)]
        copies = [pltpu.make_async_remote_copy(src_ref=p_ref.at[2 * px + py], dst_ref=out_ref.at[j],
                                               send_sem=send_sems.at[j], recv_sem=recv_sems.at[j],
                                               device_id=(px, py, c), device_id_type=MESH)
                  for j, (px, py) in enumerate(chips)]
        for cp in copies:
            cp.start()
        for cp in copies:
            cp.wait_recv()
        for cp in copies:
            cp.wait_send()

    return pl.pallas_call(
        body, name='rs_chips', out_shape=jax.ShapeDtypeStruct((3,) + p.shape[1:], p.dtype),
        in_specs=[HBM_SPEC], out_specs=HBM_SPEC,
        scratch_shapes=[pltpu.SemaphoreType.DMA((3,)), pltpu.SemaphoreType.DMA((3,))],
    )(p)


def _add_sibling(g_all, recv, c_idx):
    _, nchip, r, _ = g_all.shape
    rb = _tile(r, PACK_ROWS, SUBLANES)

    def body(c_ref, g_ref, r_ref, o_ref):
        o_ref[...] = g_ref[0] + r_ref[...]

    return pl.pallas_call(
        body, name='rs_add_sibling',
        grid_spec=pltpu.PrefetchScalarGridSpec(
            num_scalar_prefetch=1, grid=(nchip, r // rb),
            in_specs=[pl.BlockSpec((1, 1, rb, LANES), lambda k, i, c: (c[0], k, i, 0)),
                      pl.BlockSpec((1, rb, LANES), lambda k, i, c: (k, i, 0))],
            out_specs=pl.BlockSpec((1, rb, LANES), lambda k, i, c: (k, i, 0))),
        out_shape=jax.ShapeDtypeStruct(recv.shape, F32), compiler_params=_cparams(('parallel', 'parallel')),
    )(c_idx, g_all, recv)


def _add_chips(p, recv, chip_idx):
    _, r, _ = p.shape
    rb = _tile(r, PACK_ROWS, SUBLANES)

    def body(k_ref, p_ref, r0, r1, r2, o_ref):
        o_ref[...] = ((p_ref[0] + r0[0]) + r1[0]) + r2[0]

    rspec = lambda j: pl.BlockSpec((1, rb, LANES), lambda i, k: (j, i, 0))
    return pl.pallas_call(
        body, name='rs_add_chips',
        grid_spec=pltpu.PrefetchScalarGridSpec(
            num_scalar_prefetch=1, grid=(r // rb,),
            in_specs=[pl.BlockSpec((1, rb, LANES), lambda i, k: (k[0], i, 0)), rspec(0), rspec(1), rspec(2)],
            out_specs=pl.BlockSpec((rb, LANES), lambda i, k: (i, 0))),
        out_shape=jax.ShapeDtypeStruct((r, LANES), F32), compiler_params=_cparams(('parallel',)),
    )(chip_idx, p, recv, recv, recv)


def _reduce_scatter(g_all, c_idx, chip_idx):
    part = _add_sibling(g_all, _exchange_sibling(g_all), c_idx)
    return _add_chips(part, _exchange_chips(part), chip_idx)


def _pad_to(n, mult):
    return (n + mult - 1) // mult * mult


def _pack(pieces, lead, dtype):
    flat = []
    total = 0
    for p in pieces:
        f = p.reshape(lead + (-1,)).astype(dtype)
        n = _pad_to(f.shape[-1], 16 * LANES)
        flat.append(jnp.pad(f, [(0, 0)] * len(lead) + [(0, n - f.shape[-1])]))
        total += n
    full = _pad_to(total, PACK_ROWS * LANES)
    if full > total:
        flat.append(jnp.zeros(lead + (full - total,), dtype))
    return jnp.concatenate(flat, axis=-1).reshape(lead + (full // LANES, LANES))


def _unpack(buf, lead, shapes):
    flat = buf.reshape(lead + (-1,))
    out, off = [], 0
    for s in shapes:
        n = math.prod(s)
        out.append(flat[..., off:off + n].reshape(lead + tuple(s)))
        off += _pad_to(n, 16 * LANES)
    return out


def _rope_tables(c_len, t_len, n):
    quarter = n // 4
    inv = ROPE_BASE ** (-jnp.arange(0, 2 * quarter, 2, dtype=F32) / (2 * quarter))
    t = jnp.arange(t_len, dtype=jnp.int32)
    pos = jnp.stack([(t // GRID_W).astype(F32), (t % GRID_W).astype(F32)], axis=1)
    ang = pos[:, :, None] * inv[None, None, :]
    ang = jnp.concatenate([jnp.zeros((c_len, 2, quarter), F32), ang], axis=0)
    return jnp.cos(ang), jnp.sin(ang)


def _axial_rope(x, cos, sin):
    n_rows, h, n = x.shape
    xs = x.reshape(n_rows, h, 2, 2, n // 4)
    x1, x2 = xs[:, :, :, 0], xs[:, :, :, 1]
    c, s = cos[:, None], sin[:, None]
    return jnp.stack([x1 * c - x2 * s, x1 * s + x2 * c], axis=3).reshape(n_rows, h, n)


def _ssm_discretize(lam_re, lam_im, log_dt, b_re, b_im):
    dt = jnp.exp(log_dt)[:, None]
    mag = jnp.exp(lam_re * dt)
    a_re, a_im = mag * jnp.cos(lam_im * dt), mag * jnp.sin(lam_im * dt)
    den = lam_re * lam_re + lam_im * lam_im
    w_re = ((a_re - 1) * lam_re + a_im * lam_im) / den
    w_im = (a_im * lam_re - (a_re - 1) * lam_im) / den
    bb_re, bb_im = _cmul(w_re[..., None], w_im[..., None], b_re, b_im)
    return a_re, a_im, bb_re, bb_im


def _block_diag_in(b):
    g = b.shape[0]
    return jnp.einsum('gpm,gh->gmhp', b, jnp.eye(g, dtype=F32)).reshape(g * b.shape[2], g * b.shape[1])


def _block_diag_out(c):
    g = c.shape[0]
    return jnp.einsum('gmp,gh->gphm', c, jnp.eye(g, dtype=F32)).reshape(g * c.shape[2], g * c.shape[1])


def _w_in_layout(d_model):
    sizes = (MLA_Q_RANK, MLA_KV_RANK, MLA_ROPE, SSM_WIDTH, GQA_HEADS * GQA_D, GQA_KV * GQA_D, GQA_KV * GQA_D, 3 * d_model)
    starts = [0]
    for s in sizes[:-1]:
        starts.append(starts[-1] + s)
    names = ('cq', 'ckv', 'kr', 'u', 'gq', 'gk', 'gv', 'gates')
    orig = dict(zip(names, zip(starts, sizes)))
    order = ('cq', 'ckv', 'u', 'gq', 'gk', 'gv', 'gates', 'kr')
    return orig, order


def _permute_w_in(blocks, d_model):
    orig, order = _w_in_layout(d_model)
    w = blocks.reshape(-1, blocks.shape[2])
    rows = [w[orig[k][0]:orig[k][0] + orig[k][1]] for k in order]
    width = sum(orig[k][1] for k in order)
    return jnp.pad(jnp.concatenate(rows, axis=0), ((0, _pad_to(width, LANES) - width), (0, 0)))


def _unpermute_w_in(gp, d_model, r):
    orig, order = _w_in_layout(d_model)
    pos, off = {}, 0
    for k in order:
        pos[k] = off
        off += orig[k][1]
    names = sorted(orig, key=lambda k: orig[k][0])
    w = jnp.concatenate([gp[pos[k]:pos[k] + orig[k][1]] for k in names], axis=0)
    return w.reshape(N_DEV, r, gp.shape[1])


def _forward_loss(x_all, sinks, rp, wl, cc_in, target, c_len):
    n, d = x_all.shape
    t_len = n - c_len
    depth = len(wl)
    mla_attn = _make_mla(c_len)
    norm_mod = _make_norm(c_len, True)
    norm_tok = _make_norm(_tile(n, 512, SUBLANES), False)
    norm_out = _make_norm(_tile(t_len, 512, SUBLANES), False)
    half_res = _make_gated_res(c_len, 0.5)
    full_res = _make_gated_res(c_len, 1.0)
    gqa_band = _make_gqa(True, c_len)
    gqa_ctx = _make_gqa(False, c_len)
    scans = (_make_scan(False, c_len), _make_scan(True, c_len))
    cos_m, sin_m = _rope_tables(c_len, t_len, MLA_ROPE)
    cos_g, sin_g = _rope_tables(c_len, t_len, GQA_D)
    orig, order = _w_in_layout(d)
    offs, off = {}, 0
    for k in order:
        offs[k] = (off, orig[k][1])
        off += orig[k][1]

    cc = jnp.zeros((SUBLANES, d), F32).at[0].set(jax.nn.silu(rp['c_ctx'])).at[1].set(jax.nn.silu(cc_in))

    for l in range(depth):
        w, sk = wl[l], sinks[l]
        ctx_out = l < depth - 1

        def mm(h, name):
            return (matmul if name in ROW_SHARDED else matmul_t)(h, w[name], sk[name])

        def swiglu(h, name13, name2):
            a13 = mm(h, name13)
            f = a13.shape[1] // 2
            return mm(jax.nn.silu(a13[:, :f]) * a13[:, f:], name2)

        mod = mm(cc, 'ada_w') + rp['ada_b'][l][None, :]
        md = [mod[0:2, i * d:(i + 1) * d] for i in range(N_MOD)]
        x_all = half_res(x_all, swiglu(norm_mod(x_all, rp['norm_ffn1'][l], md[0], md[1]), 'ffn1_w13', 'ffn1_w2'), md[2])

        z = mm(norm_mod(x_all, rp['norm_mix'][l], md[3], md[4]), 'w_in')
        part = {k: z[:, o:o + s] for k, (o, s) in offs.items()}

        q3 = mm(norm_tok(part['cq'], rp['mla_q_norm'][l]), 'mla_w_uq').reshape(n, MLA_HEADS, LANES)
        q = jnp.concatenate([q3[..., :MLA_NOPE], _axial_rope(q3[..., MLA_NOPE:MLA_NOPE + MLA_ROPE], cos_m, sin_m),
                             q3[..., MLA_NOPE + MLA_ROPE:]], axis=-1).reshape(n, MLA_HEADS * LANES)
        kvp = mm(norm_tok(part['ckv'], rp['mla_kv_norm'][l]), 'mla_w_ukv')
        kr = _axial_rope(part['kr'].reshape(n, 1, MLA_ROPE), cos_m, sin_m).reshape(n, MLA_ROPE)
        kr = jnp.pad(kr, ((0, 0), (MLA_NOPE, LANES - MLA_NOPE - MLA_ROPE)))
        mla = mm(mla_attn(q, kvp, kr), 'mla_w_o')

        u = part['u']
        y = u * rp['ssm_d'][l][None, :]
        for direction in range(2):
            a_re, a_im, bb_re, bb_im = _ssm_discretize(
                rp['ssm_lambda_re'][l, direction], rp['ssm_lambda_im'][l, direction], rp['ssm_log_dt'][l, direction],
                rp['ssm_b_re'][l, direction], rp['ssm_b_im'][l, direction])
            s_re, s_im = scans[direction](matmul_d(u, _block_diag_in(bb_re)), matmul_d(u, _block_diag_in(bb_im)),
                                          a_re.reshape(-1), a_im.reshape(-1))
            y = y + (matmul_d(s_re, _block_diag_out(rp['ssm_c_re'][l, direction]))
                     - matmul_d(s_im, _block_diag_out(rp['ssm_c_im'][l, direction])))
        yg = mm(jax.nn.gelu(y), 'ssm_w_glu')
        ssm = yg[:, :d] * jax.nn.sigmoid(yg[:, d:])

        gq = _axial_rope(part['gq'].reshape(n, GQA_HEADS, GQA_D), cos_g, sin_g)
        gk = _axial_rope(part['gk'].reshape(n, GQA_KV, GQA_D), cos_g, sin_g)
        q4 = jnp.transpose(gq.reshape(n, GQA_KV, GQA_G, GQA_D), (1, 2, 0, 3))
        k2 = jnp.transpose(gk, (1, 0, 2))
        v2 = jnp.transpose(part['gv'].reshape(n, GQA_KV, GQA_D), (1, 0, 2))
        sink = rp['gqa_sink'][l].reshape(GQA_KV, GQA_G, 1, 1)
        sink_rows = lambda rows: jnp.broadcast_to(sink, (GQA_KV, GQA_G, rows, 1)).reshape(GQA_KV, GQA_G * rows, 1)
        g_lat = gqa_band(q4[:, :, c_len:], k2, v2, sink_rows(BLOCK))
        if ctx_out:
            g_ctx = gqa_ctx(q4[:, :, :c_len], k2[:, :c_len], v2[:, :c_len], sink_rows(c_len))
        else:
            g_ctx = jnp.zeros((GQA_KV, GQA_G, c_len, GQA_D), F32)
        go = jnp.transpose(jnp.concatenate([g_ctx, g_lat], axis=2), (2, 0, 1, 3)).reshape(n, GQA_HEADS * GQA_D)
        gqa = mm(go, 'gqa_w_o')

        gates = jax.nn.sigmoid(part['gates'])
        mixed = gates[:, :d] * mla + gates[:, d:2 * d] * ssm + gates[:, 2 * d:] * gqa
        x_all = full_res(x_all, mm(mixed, 'w_out'), md[5])
        x_all = half_res(x_all, swiglu(norm_mod(x_all, rp['norm_ffn2'][l], md[6], md[7]), 'ffn2_w13', 'ffn2_w2'), md[8])

    return loss_head(norm_out(x_all[c_len:], rp['final_norm']), target)


def kernel(x, c, ctx, c_ctx, ada_w, ada_b, norm_ffn1, norm_mix, norm_ffn2, ffn1_w13, ffn1_w2, ffn2_w13, ffn2_w2, w_in, mla_q_norm, mla_kv_norm, mla_w_uq, mla_w_ukv, mla_w_o, ssm_lambda_re, ssm_lambda_im, ssm_log_dt, ssm_b_re, ssm_b_im, ssm_c_re, ssm_c_im, ssm_d, ssm_w_glu, gqa_sink, gqa_w_o, w_out, final_norm, loss_target, m_c_ctx, m_ada_w, m_ada_b, m_norm_ffn1, m_norm_mix, m_norm_ffn2, m_ffn1_w13, m_ffn1_w2, m_ffn2_w13, m_ffn2_w2, m_w_in, m_mla_q_norm, m_mla_kv_norm, m_mla_w_uq, m_mla_w_ukv, m_mla_w_o, m_ssm_lambda_re, m_ssm_lambda_im, m_ssm_log_dt, m_ssm_b_re, m_ssm_b_im, m_ssm_c_re, m_ssm_c_im, m_ssm_d, m_ssm_w_glu, m_gqa_sink, m_gqa_w_o, m_w_out, m_final_norm, v_c_ctx, v_ada_w, v_ada_b, v_norm_ffn1, v_norm_mix, v_norm_ffn2, v_ffn1_w13, v_ffn1_w2, v_ffn2_w13, v_ffn2_w2, v_w_in, v_mla_q_norm, v_mla_kv_norm, v_mla_w_uq, v_mla_w_ukv, v_mla_w_o, v_ssm_lambda_re, v_ssm_lambda_im, v_ssm_log_dt, v_ssm_b_re, v_ssm_b_im, v_ssm_c_re, v_ssm_c_im, v_ssm_d, v_ssm_w_glu, v_gqa_sink, v_gqa_w_o, v_w_out, v_final_norm):
    args = dict(locals())
    weights = {k: args[k] for k in WEIGHTS}
    moments_m = {k: args['m_' + k] for k in WEIGHTS}
    moments_v = {k: args['v_' + k] for k in WEIGHTS}
    depth = ada_w.shape[0]
    d = x.shape[-1]
    c_len = ctx.shape[1]
    my_c = lax.axis_index('c')
    my_chip = 2 * lax.axis_index('x') + lax.axis_index('y')
    qk_w = MLA_NOPE + MLA_ROPE

    def as_rows(k, a):
        return a if k in ROW_SHARDED else a.T

    shard_shapes = [as_rows(k, weights[k][0]).shape for k in SHARDED]
    r_in = weights['w_in'].shape[2]
    layers = []
    for l in range(depth):
        gathered = _all_gather(_pack([as_rows(k, weights[k][l]) for k in SHARDED], (), BF16))
        full = {}
        for k, blocks in zip(SHARDED, _unpack(gathered, (N_DEV,), shard_shapes)):
            full[k] = _permute_w_in(blocks, d) if k == 'w_in' else blocks.reshape(N_DEV * blocks.shape[1], blocks.shape[2])
        full['mla_w_uq'] = jnp.pad(full['mla_w_uq'].reshape(MLA_HEADS, qk_w, MLA_Q_RANK),
                                   ((0, 0), (0, LANES - qk_w), (0, 0))).reshape(MLA_HEADS * LANES, MLA_Q_RANK)
        full['mla_w_o'] = jnp.pad(full['mla_w_o'].reshape(d, MLA_HEADS, MLA_V),
                                  ((0, 0), (0, 0), (LANES - MLA_V, 0))).reshape(d, MLA_HEADS * LANES)
        layers.append(full)
    sinks = [{k: jnp.zeros(v.shape, F32) for k, v in full.items()} for full in layers]

    rp = {k: weights[k] for k in REPLICATED}
    x_all = jnp.concatenate([ctx[0], x[0]], axis=0)
    loss_fn = functools.partial(_forward_loss, wl=layers, cc_in=c[0], target=loss_target[0], c_len=c_len)
    loss, vjp = jax.vjp(loss_fn, x_all, sinks, rp)
    g_x, g_layers, g_rp = vjp(jnp.ones((), F32))

    grads = {k: [] for k in SHARDED}
    for l in range(depth):
        gl = dict(g_layers[l])
        gl['mla_w_uq'] = gl['mla_w_uq'].reshape(MLA_HEADS, LANES, MLA_Q_RANK)[:, :qk_w].reshape(MLA_HEADS * qk_w, MLA_Q_RANK)
        gl['mla_w_o'] = gl['mla_w_o'].reshape(d, MLA_HEADS, LANES)[:, :, LANES - MLA_V:].reshape(d, MLA_HEADS * MLA_V)
        pieces = [_unpermute_w_in(gl[k], d, r_in) if k == 'w_in' else gl[k].reshape((N_DEV,) + shape)
                  for k, shape in zip(SHARDED, shard_shapes)]
        packed = _pack(pieces, (N_DEV,), F32)
        g_all = jnp.swapaxes(packed.reshape((4, 2) + packed.shape[1:]), 0, 1)
        mine = _reduce_scatter(g_all, my_c.reshape(1).astype(jnp.int32), my_chip.reshape(1).astype(jnp.int32))
        for k, g in zip(SHARDED, _unpack(mine, (), shard_shapes)):
            grads[k].append(as_rows(k, g))
    grads = {k: jnp.stack(v) for k, v in grads.items()}

    rep_shapes = [weights[k].shape for k in REPLICATED] + [(1,)]
    small = _pack([g_rp[k] for k in REPLICATED] + [loss.reshape(1)], (), F32)
    summed = _unpack(_sum_leading(_all_gather(small)), (), rep_shapes)
    for k, g in zip(REPLICATED, summed[:-1]):
        grads[k] = g
    loss_total = summed[-1].reshape(())

    delta, new_m, new_v = {}, {}, {}
    for k in SHARDED:
        shape = weights[k].shape
        as2d = lambda a: a.reshape(-1, shape[-1])
        outs = _adamw_call(as2d(weights[k]), as2d(grads[k]), as2d(moments_m[k]), as2d(moments_v[k]))
        delta[k], new_m[k], new_v[k] = (o.reshape(shape) for o in outs)
    rep_all = [weights[k].shape for k in REPLICATED]
    packs = [_pack([src[k] for k in REPLICATED], (), F32) for src in (weights, grads, moments_m, moments_v)]
    outs = [_unpack(o, (), rep_all) for o in _adamw_call(*packs)]
    for i, k in enumerate(REPLICATED):
        delta[k], new_m[k], new_v[k] = outs[0][i], outs[1][i], outs[2][i]

    return (loss_total, g_x[c_len:][None], *[grads[k] for k in WEIGHTS], *[delta[k] for k in WEIGHTS],
            *[new_m[k] for k in WEIGHTS], *[new_v[k] for k in WEIGHTS])
```

```python
import functools
import math

import jax
import jax.numpy as jnp
from jax import lax
from jax.experimental import pallas as pl
from jax.experimental.pallas import tpu as pltpu

F32 = jnp.float32
BF16 = jnp.bfloat16

MLA_HEADS, MLA_NOPE, MLA_ROPE, MLA_V = 8, 64, 32, 64
MLA_Q_RANK, MLA_KV_RANK = 384, 256
SSM_WIDTH, SSM_GROUP, SSM_STATE = 512, 16, 64
SSM_GROUPS = SSM_WIDTH // SSM_GROUP
GQA_HEADS, GQA_KV, GQA_D = 8, 2, 64
GQA_G = GQA_HEADS // GQA_KV
WINDOW, BLOCK, GRID_W = 128, 128, 64
N_MOD = 9
ROPE_BASE = 10000.0
EPS = 1e-6
NEG_INF = -1e30
ADAM_LR, ADAM_B1, ADAM_B2, ADAM_EPS, ADAM_WD, ADAM_STEP = 0.001, 0.9, 0.999, 1e-08, 0.01, 10

N_DEV = 8
LANES = 128
SUBLANES = 8
VMEM_LIMIT = 56 * 1024 * 1024
PACK_ROWS = 1024

SHARDED = ('ada_w', 'ffn1_w13', 'ffn1_w2', 'ffn2_w13', 'ffn2_w2', 'w_in', 'mla_w_uq', 'mla_w_ukv',
           'mla_w_o', 'ssm_w_glu', 'gqa_w_o', 'w_out')
ROW_SHARDED = ('ffn1_w2', 'ffn2_w2', 'w_out')
REPLICATED = ('c_ctx', 'ada_b', 'norm_ffn1', 'norm_mix', 'norm_ffn2', 'mla_q_norm', 'mla_kv_norm',
              'ssm_lambda_re', 'ssm_lambda_im', 'ssm_log_dt', 'ssm_b_re', 'ssm_b_im', 'ssm_c_re', 'ssm_c_im',
              'ssm_d', 'gqa_sink', 'final_norm')
WEIGHTS = ('c_ctx', 'ada_w', 'ada_b', 'norm_ffn1', 'norm_mix', 'norm_ffn2', 'ffn1_w13', 'ffn1_w2', 'ffn2_w13',
           'ffn2_w2', 'w_in', 'mla_q_norm', 'mla_kv_norm', 'mla_w_uq', 'mla_w_ukv', 'mla_w_o', 'ssm_lambda_re',
           'ssm_lambda_im', 'ssm_log_dt', 'ssm_b_re', 'ssm_b_im', 'ssm_c_re', 'ssm_c_im', 'ssm_d', 'ssm_w_glu',
           'gqa_sink', 'gqa_w_o', 'w_out', 'final_norm')


def _tile(n, target, mult):
    t = (min(target, n) // mult) * mult
    while t >= mult:
        if n % t == 0:
            return t
        t -= mult
    return n


def _cparams(sem):
    return pltpu.CompilerParams(dimension_semantics=sem, vmem_limit_bytes=VMEM_LIMIT)


def _mm(a, b, mode):
    if mode == 'nn':
        (M, K), N = a.shape, b.shape[1]
        tm, tn, tk = _tile(M, 1408, SUBLANES), _tile(N, 1024, LANES), _tile(K, 1408, LANES)
    elif mode == 'nt':
        (M, K), N = a.shape, b.shape[0]
        tm, tn, tk = _tile(M, 1408, SUBLANES), _tile(N, 1024, LANES), _tile(K, 1408, LANES)
    else:
        (K, M), N = a.shape, b.shape[1]
        tm, tn, tk = _tile(M, 1408, LANES), _tile(N, 1408, LANES), _tile(K, 768, 2 * SUBLANES)
    nk = K // tk
    dims = {'nn': (((1,), (0,)), ((), ())), 'nt': (((1,), (1,)), ((), ())), 'tn': (((0,), (0,)), ((), ()))}[mode]
    keep_a = nk == 1 and mode != 'tn' and N // tn > 1

    def body(a_ref, b_ref, o_ref, *scratch):
        if keep_a:
            @pl.when(pl.program_id(1) == 0)
            def _():
                scratch[0][...] = a_ref[...].astype(BF16)

            av = scratch[0][...]
        else:
            av = a_ref[...].astype(BF16)
        part = lax.dot_general(av, b_ref[...].astype(BF16), dims, preferred_element_type=F32)
        if nk == 1:
            o_ref[...] = part
        else:
            @pl.when(pl.program_id(2) == 0)
            def _():
                o_ref[...] = part

            @pl.when(pl.program_id(2) > 0)
            def _():
                o_ref[...] += part

    a_spec = pl.BlockSpec((tk, tm), lambda i, j, k: (k, i)) if mode == 'tn' else pl.BlockSpec((tm, tk), lambda i, j, k: (i, k))
    b_spec = pl.BlockSpec((tn, tk), lambda i, j, k: (j, k)) if mode == 'nt' else pl.BlockSpec((tk, tn), lambda i, j, k: (k, j))
    return pl.pallas_call(
        body, name='mm_' + mode, grid=(M // tm, N // tn, nk),
        in_specs=[a_spec, b_spec], out_specs=pl.BlockSpec((tm, tn), lambda i, j, k: (i, j)),
        out_shape=jax.ShapeDtypeStruct((M, N), F32),
        scratch_shapes=[pltpu.VMEM((tm, tk), BF16)] if keep_a else [],
        compiler_params=_cparams(('parallel', 'arbitrary', 'arbitrary')),
    )(a, b)


@jax.custom_vjp
def matmul_d(x, w):
    return _mm(x, w, 'nn')


def _matmul_d_fwd(x, w):
    return _mm(x, w, 'nn'), (x, w)


def _matmul_d_bwd(res, g):
    x, w = res
    return _mm(g, w, 'nt'), _mm(x, g, 'tn')


matmul_d.defvjp(_matmul_d_fwd, _matmul_d_bwd)


@jax.custom_vjp
def matmul(x, w, sink):
    return _mm(x, w, 'nn')


def _matmul_fwd(x, w, sink):
    return _mm(x, w, 'nn'), (x, w)


def _matmul_bwd(res, g):
    x, w = res
    return _mm(g, w, 'nt'), jnp.zeros_like(w), _mm(x, g, 'tn')


matmul.defvjp(_matmul_fwd, _matmul_bwd)


@jax.custom_vjp
def matmul_t(x, wt, sink):
    return _mm(x, wt, 'nt')


def _matmul_t_fwd(x, wt, sink):
    return _mm(x, wt, 'nt'), (x, wt)


def _matmul_t_bwd(res, g):
    x, wt = res
    return _mm(g, wt, 'nn'), jnp.zeros_like(wt), _mm(g, x, 'tn')


matmul_t.defvjp(_matmul_t_fwd, _matmul_t_bwd)


def _norm_fwd_call(x, g, sh, sc, rb):
    n, d = x.shape
    has_mod = sh is not None

    def body(*refs):
        x_ref, g_ref = refs[0], refs[1]
        o_ref = refs[-1]
        xv = x_ref[...]
        r = lax.rsqrt(jnp.mean(xv * xv, axis=-1, keepdims=True) + EPS)
        y = xv * r * g_ref[...]
        if has_mod:
            lat = pl.program_id(0) > 0
            shv = jnp.where(lat, refs[2][1:2, :], refs[2][0:1, :])
            scv = jnp.where(lat, refs[3][1:2, :], refs[3][0:1, :])
            y = y * (1.0 + scv) + shv
        o_ref[...] = y

    row = pl.BlockSpec((rb, d), lambda i: (i, 0))
    vec = pl.BlockSpec((1, d), lambda i: (0, 0))
    two = pl.BlockSpec((2, d), lambda i: (0, 0))
    ins = [x, g.reshape(1, d)] + ([sh, sc] if has_mod else [])
    return pl.pallas_call(
        body, name='norm_fwd', grid=(n // rb,), in_specs=[row, vec] + ([two, two] if has_mod else []),
        out_specs=row, out_shape=jax.ShapeDtypeStruct((n, d), F32), compiler_params=_cparams(('parallel',)),
    )(*ins)


def _norm_bwd_call(x, g, sh, sc, dy, rb):
    n, d = x.shape
    has_mod = sh is not None

    def body(*refs):
        x_ref, g_ref, dy_ref = refs[0], refs[1], refs[-3]
        dx_ref, acc_ref = refs[-2], refs[-1]
        i = pl.program_id(0)
        xv, dyv, gv = x_ref[...], dy_ref[...], g_ref[...]
        r = lax.rsqrt(jnp.mean(xv * xv, axis=-1, keepdims=True) + EPS)
        xh = xv * r
        if has_mod:
            lat = i > 0
            scv = jnp.where(lat, refs[3][1:2, :], refs[3][0:1, :])
            dyg = dyv * (1.0 + scv)
        else:
            dyg = dyv
        dxh = dyg * gv
        dx_ref[...] = r * (dxh - xh * jnp.mean(dxh * xh, axis=-1, keepdims=True))

        @pl.when(i == 0)
        def _():
            acc_ref[...] = jnp.zeros_like(acc_ref)

        acc_ref[0:1, :] += jnp.sum(dyg * xh, axis=0, keepdims=True)
        if has_mod:
            dsh = jnp.sum(dyv, axis=0, keepdims=True)
            dsc = jnp.sum(dyv * xh * gv, axis=0, keepdims=True)

            @pl.when(i == 0)
            def _():
                acc_ref[1:2, :] += dsh
                acc_ref[3:4, :] += dsc

            @pl.when(i > 0)
            def _():
                acc_ref[2:3, :] += dsh
                acc_ref[4:5, :] += dsc

    row = pl.BlockSpec((rb, d), lambda i: (i, 0))
    vec = pl.BlockSpec((1, d), lambda i: (0, 0))
    two = pl.BlockSpec((2, d), lambda i: (0, 0))
    ins = [x, g.reshape(1, d)] + ([sh, sc] if has_mod else []) + [dy]
    return pl.pallas_call(
        body, name='norm_bwd', grid=(n // rb,), in_specs=[row, vec] + ([two, two] if has_mod else []) + [row],
        out_specs=[row, pl.BlockSpec((SUBLANES, d), lambda i: (0, 0))],
        out_shape=[jax.ShapeDtypeStruct((n, d), F32), jax.ShapeDtypeStruct((SUBLANES, d), F32)],
        compiler_params=_cparams(('arbitrary',)),
    )(*ins)


def _make_norm(rb, has_mod):
    if has_mod:
        @jax.custom_vjp
        def f(x, g, sh, sc):
            return _norm_fwd_call(x, g, sh, sc, rb)

        def fwd(x, g, sh, sc):
            return _norm_fwd_call(x, g, sh, sc, rb), (x, g, sh, sc)

        def bwd(res, dy):
            x, g, sh, sc = res
            dx, acc = _norm_bwd_call(x, g, sh, sc, dy, rb)
            return dx, acc[0], acc[1:3], acc[3:5]
    else:
        @jax.custom_vjp
        def f(x, g):
            return _norm_fwd_call(x, g, None, None, rb)

        def fwd(x, g):
            return _norm_fwd_call(x, g, None, None, rb), (x, g)

        def bwd(res, dy):
            x, g = res
            dx, acc = _norm_bwd_call(x, g, None, None, dy, rb)
            return dx, acc[0]
    f.defvjp(fwd, bwd)
    return f


def _make_gated_res(rb, coef):
    def fwd_call(x, f, gate):
        n, d = x.shape

        def body(x_ref, f_ref, g_ref, o_ref):
            gv = jnp.where(pl.program_id(0) > 0, g_ref[1:2, :], g_ref[0:1, :])
            o_ref[...] = x_ref[...] + coef * gv * f_ref[...]

        row = pl.BlockSpec((rb, d), lambda i: (i, 0))
        return pl.pallas_call(
            body, name='gated_res_fwd', grid=(n // rb,), in_specs=[row, row, pl.BlockSpec((2, d), lambda i: (0, 0))],
            out_specs=row, out_shape=jax.ShapeDtypeStruct((n, d), F32), compiler_params=_cparams(('parallel',)),
        )(x, f, gate)

    def bwd_call(dy, f, gate):
        n, d = dy.shape

        def body(dy_ref, f_ref, g_ref, df_ref, acc_ref):
            i = pl.program_id(0)
            gv = jnp.where(i > 0, g_ref[1:2, :], g_ref[0:1, :])
            dyv = dy_ref[...]
            df_ref[...] = coef * gv * dyv
            part = coef * jnp.sum(dyv * f_ref[...], axis=0, keepdims=True)

            @pl.when(i == 0)
            def _():
                acc_ref[...] = jnp.zeros_like(acc_ref)
                acc_ref[0:1, :] += part

            @pl.when(i > 0)
            def _():
                acc_ref[1:2, :] += part

        row = pl.BlockSpec((rb, d), lambda i: (i, 0))
        return pl.pallas_call(
            body, name='gated_res_bwd', grid=(n // rb,), in_specs=[row, row, pl.BlockSpec((2, d), lambda i: (0, 0))],
            out_specs=[row, pl.BlockSpec((SUBLANES, d), lambda i: (0, 0))],
            out_shape=[jax.ShapeDtypeStruct((n, d), F32), jax.ShapeDtypeStruct((SUBLANES, d), F32)],
            compiler_params=_cparams(('arbitrary',)),
        )(dy, f, gate)

    @jax.custom_vjp
    def f(x, fv, gate):
        return fwd_call(x, fv, gate)

    def fwd(x, fv, gate):
        return fwd_call(x, fv, gate), (fv, gate)

    def bwd(res, dy):
        fv, gate = res
        df, acc = bwd_call(dy, fv, gate)
        return dy, df, acc[0:2]

    f.defvjp(fwd, bwd)
    return f


MLA_SCALE = (MLA_NOPE + MLA_ROPE) ** -0.5
NT_DIMS = (((1,), (1,)), ((), ()))


def _mla_keys(kv, kr):
    lane = lax.broadcasted_iota(jnp.int32, kv.shape, 1)
    return jnp.where(lane < MLA_NOPE, kv, kr)


def _chunks(start, stop, target):
    size = _tile(stop - start, target, LANES)
    return [(start + t * size, size) for t in range((stop - start) // size)]


MLA_CHUNK = 2816


def _mla_fwd_call(q, kv, kr, c_len):
    n = q.shape[0]
    h = q.shape[1] // LANES
    tq = c_len

    def body(q_ref, kv_ref, kr_ref, o_ref, lse_ref, kb):
        i = pl.program_id(1)

        @pl.when(i == 0)
        def _():
            kb[...] = _mla_keys(kv_ref[...], kr_ref[...])

        qv = q_ref[...].astype(BF16)

        def attend(nk):
            s = lax.dot_general(qv, kb[:nk, :], NT_DIMS, preferred_element_type=F32) * MLA_SCALE
            m = jnp.max(s, axis=-1, keepdims=True)
            p = jnp.exp(s - m)
            l = jnp.sum(p, axis=-1, keepdims=True)
            o_ref[...] = jnp.dot(p.astype(BF16), kv_ref[:nk, :], preferred_element_type=F32) / l
            lse_ref[0] = m + jnp.log(l)

        pl.when(i == 0)(lambda: attend(c_len))
        pl.when(i > 0)(lambda: attend(n))

    qspec = pl.BlockSpec((tq, LANES), lambda a, i: (i, a))
    return pl.pallas_call(
        body, name='mla_attn_fwd', grid=(h, n // tq),
        in_specs=[qspec, pl.BlockSpec((n, LANES), lambda a, i: (0, a)), pl.BlockSpec((n, LANES), lambda a, i: (0, 0))],
        out_specs=[qspec, pl.BlockSpec((1, tq, 1), lambda a, i: (a, i, 0))],
        out_shape=[jax.ShapeDtypeStruct((n, h * LANES), F32), jax.ShapeDtypeStruct((h, n, 1), F32)],
        scratch_shapes=[pltpu.VMEM((n, LANES), BF16)],
        compiler_params=_cparams(('arbitrary', 'arbitrary')),
    )(q, kv, kr)


def _mla_dq_call(q, kv, kr, o, lse, do, c_len):
    n = q.shape[0]
    h = q.shape[1] // LANES
    tq = c_len

    def body(q_ref, kv_ref, kr_ref, o_ref, lse_ref, do_ref, dq_ref, delta_ref, kb):
        i = pl.program_id(1)

        @pl.when(i == 0)
        def _():
            kb[...] = _mla_keys(kv_ref[...], kr_ref[...])

        dov = do_ref[...]
        delta = jnp.sum(dov * o_ref[...], axis=-1, keepdims=True)
        delta_ref[0] = delta
        qv, dob, lse = q_ref[...], dov.astype(BF16), lse_ref[0]

        def grad(chunks):
            acc = None
            for k0, kc in chunks:
                keys = kb[k0:k0 + kc, :]
                s = lax.dot_general(qv, keys, NT_DIMS, preferred_element_type=F32) * MLA_SCALE
                p = jnp.exp(s - lse)
                dp = lax.dot_general(dob, kv_ref[k0:k0 + kc, :], NT_DIMS, preferred_element_type=F32)
                ds = p * (dp - delta) * MLA_SCALE
                part = jnp.dot(ds.astype(BF16), keys, preferred_element_type=F32)
                acc = part if acc is None else acc + part
            dq_ref[...] = acc

        pl.when(i == 0)(lambda: grad([(0, c_len)]))
        pl.when(i > 0)(lambda: grad(_chunks(0, n, MLA_CHUNK)))

    qspec = pl.BlockSpec((tq, LANES), lambda a, i: (i, a))
    col = pl.BlockSpec((1, tq, 1), lambda a, i: (a, i, 0))
    return pl.pallas_call(
        body, name='mla_attn_dq', grid=(h, n // tq),
        in_specs=[qspec, pl.BlockSpec((n, LANES), lambda a, i: (0, a)), pl.BlockSpec((n, LANES), lambda a, i: (0, 0)),
                  qspec, col, qspec],
        out_specs=[qspec, col],
        out_shape=[jax.ShapeDtypeStruct((n, h * LANES), F32), jax.ShapeDtypeStruct((h, n, 1), F32)],
        scratch_shapes=[pltpu.VMEM((n, LANES), BF16)],
        compiler_params=_cparams(('arbitrary', 'arbitrary')),
    )(q, kv, kr, o, lse, do)


def _mla_dkv_call(q, kv, kr, lse_row, delta_row, do, c_len):
    n = q.shape[0]
    h = q.shape[1] // LANES
    tk = c_len

    def body(q_ref, kv_ref, kr_ref, lse_ref, delta_ref, do_ref, dkv_ref, dkf_ref):
        j = pl.program_id(1)
        vblk = kv_ref[...]
        kblk = _mla_keys(vblk, kr_ref[...])
        lane = lax.broadcasted_iota(jnp.int32, (tk, LANES), 1)

        def grad(chunks):
            dk = dv = None
            for q0, qc in chunks:
                qs, dos = q_ref[q0:q0 + qc, :], do_ref[q0:q0 + qc, :]
                st = lax.dot_general(kblk, qs, NT_DIMS, preferred_element_type=F32) * MLA_SCALE
                pt = jnp.exp(st - lse_ref[0, :, q0:q0 + qc])
                dv_part = jnp.dot(pt.astype(BF16), dos, preferred_element_type=F32)
                dpt = lax.dot_general(vblk, dos, NT_DIMS, preferred_element_type=F32)
                dst = pt * (dpt - delta_ref[0, :, q0:q0 + qc]) * MLA_SCALE
                dk_part = jnp.dot(dst.astype(BF16), qs, preferred_element_type=F32)
                dk = dk_part if dk is None else dk + dk_part
                dv = dv_part if dv is None else dv + dv_part
            dkv_ref[...] = jnp.where(lane < MLA_NOPE, dk, dv)
            dkf_ref[0] = dk

        latent = _chunks(c_len, n, MLA_CHUNK)
        pl.when(j == 0)(lambda: grad(latent + [(0, c_len)]))
        pl.when(j > 0)(lambda: grad(latent))

    full = pl.BlockSpec((n, LANES), lambda a, j: (0, a))
    row = pl.BlockSpec((1, 1, n), lambda a, j: (a, 0, 0))
    kspec = pl.BlockSpec((tk, LANES), lambda a, j: (j, a))
    return pl.pallas_call(
        body, name='mla_attn_dkv', grid=(h, n // tk),
        in_specs=[full, kspec, pl.BlockSpec((tk, LANES), lambda a, j: (j, 0)), row, row, full],
        out_specs=[kspec, pl.BlockSpec((1, tk, LANES), lambda a, j: (a, j, 0))],
        out_shape=[jax.ShapeDtypeStruct((n, h * LANES), F32), jax.ShapeDtypeStruct((h, n, LANES), F32)],
        compiler_params=_cparams(('arbitrary', 'arbitrary')),
    )(q, kv, kr, lse_row, delta_row, do)


def _sum_leading(g):
    nl, r, _ = g.shape
    rb = _tile(r, 512, SUBLANES)

    def body(g_ref, o_ref):
        acc = g_ref[0]
        for j in range(1, nl):
            acc = acc + g_ref[j]
        o_ref[...] = acc

    return pl.pallas_call(
        body, name='sum_leading', grid=(r // rb,), in_specs=[pl.BlockSpec((nl, rb, LANES), lambda i: (0, i, 0))],
        out_specs=pl.BlockSpec((rb, LANES), lambda i: (i, 0)), out_shape=jax.ShapeDtypeStruct((r, LANES), F32),
        compiler_params=_cparams(('parallel',)),
    )(g)


def _make_mla(c_len):
    @jax.custom_vjp
    def f(q, kv, kr):
        return _mla_fwd_call(q, kv.astype(BF16), kr.astype(BF16), c_len)[0]

    def fwd(q, kv, kr):
        kvb, krb = kv.astype(BF16), kr.astype(BF16)
        o, lse = _mla_fwd_call(q, kvb, krb, c_len)
        return o, (q.astype(BF16), kvb, krb, o, lse)

    def bwd(res, do):
        q, kv, kr, o, lse = res
        n = q.shape[0]
        h = q.shape[1] // LANES
        dq, delta = _mla_dq_call(q, kv, kr, o, lse, do, c_len)
        dkv, dk_full = _mla_dkv_call(q, kv, kr, lse.reshape(h, 1, n), delta.reshape(h, 1, n), do.astype(BF16), c_len)
        return dq, dkv, _sum_leading(dk_full)

    f.defvjp(fwd, bwd)
    return f


def _gqa_specs(band, c_len, nb, rows):
    cb = c_len // BLOCK
    q_spec = pl.BlockSpec((1, GQA_G, rows, GQA_D), lambda a, b: (a, 0, b, 0))
    ctx_spec = pl.BlockSpec((1, c_len, GQA_D), lambda a, b: (a, 0, 0))
    kv_specs = [ctx_spec]
    if band:
        kv_specs += [pl.BlockSpec((1, BLOCK, GQA_D), lambda a, b: (a, jnp.maximum(b - 1, 0) + cb, 0)),
                     pl.BlockSpec((1, BLOCK, GQA_D), lambda a, b: (a, b + cb, 0)),
                     pl.BlockSpec((1, BLOCK, GQA_D), lambda a, b: (a, jnp.minimum(b + 1, nb - 1) + cb, 0))]
    sink_spec = pl.BlockSpec((1, GQA_G * rows, 1), lambda a, b: (a, 0, 0))
    return q_spec, kv_specs, sink_spec


def _gqa_scores(q, kcat, sink, band, c_len, t_len, rows):
    scale = GQA_D ** -0.5
    s = lax.dot_general(q, kcat, (((1,), (1,)), ((), ())), preferred_element_type=F32) * scale
    if band:
        b = pl.program_id(1)
        shape = s.shape
        col = lax.broadcasted_iota(jnp.int32, shape, 1)
        qpos = b * BLOCK + (lax.broadcasted_iota(jnp.int32, shape, 0) & (BLOCK - 1))
        kpos = (b - 1) * BLOCK + (col - c_len)
        valid = (col < c_len) | ((jnp.abs(qpos - kpos) <= WINDOW) & (kpos >= 0) & (kpos < t_len))
        s = jnp.where(valid, s, NEG_INF)
    m = jnp.maximum(jnp.max(s, axis=-1, keepdims=True), sink)
    e = jnp.exp(s - m)
    es = jnp.exp(sink - m)
    den = es + jnp.sum(e, axis=-1, keepdims=True)
    return e / den, es / den


def _gqa_fwd_call(q4, k2, v2, sink_rows, band, c_len):
    kv, g, tq_all, d = q4.shape
    rows = BLOCK if band else tq_all
    nb = tq_all // rows
    t_len = k2.shape[1] - c_len
    nkv = 4 if band else 1
    q_spec, kv_specs, sink_spec = _gqa_specs(band, c_len, nb, rows)

    def body(*refs):
        q_ref, sink_ref, o_ref = refs[0], refs[1 + 2 * nkv], refs[-1]
        kcat = jnp.concatenate([r[0] for r in refs[1:1 + nkv]], axis=0).astype(BF16)
        vcat = jnp.concatenate([r[0] for r in refs[1 + nkv:1 + 2 * nkv]], axis=0).astype(BF16)
        q = q_ref[0].reshape(g * rows, d).astype(BF16)
        p, _ = _gqa_scores(q, kcat, sink_ref[0], band, c_len, t_len, rows)
        o_ref[0] = jnp.dot(p.astype(BF16), vcat, preferred_element_type=F32).reshape(g, rows, d)

    return pl.pallas_call(
        body, name='gqa_fwd_band' if band else 'gqa_fwd_ctx', grid=(kv, nb),
        in_specs=[q_spec] + kv_specs + kv_specs + [sink_spec], out_specs=q_spec,
        out_shape=jax.ShapeDtypeStruct(q4.shape, F32), compiler_params=_cparams(('parallel', 'parallel')),
    )(q4, *([k2] * nkv), *([v2] * nkv), sink_rows)


def _gqa_bwd_call(q4, k2, v2, sink_rows, do4, band, c_len):
    kv, g, tq_all, d = q4.shape
    rows = BLOCK if band else tq_all
    nb = tq_all // rows
    t_len = k2.shape[1] - c_len
    nkv = 4 if band else 1
    scale = GQA_D ** -0.5
    q_spec, kv_specs, sink_spec = _gqa_specs(band, c_len, nb, rows)

    def body(*refs):
        q_ref, sink_ref, do_ref = refs[0], refs[1 + 2 * nkv], refs[2 + 2 * nkv]
        outs = refs[3 + 2 * nkv:]
        dq_ref, dkc_ref, dvc_ref = outs[0], outs[1], outs[2]
        dsink_ref = outs[-1]
        b = pl.program_id(1)
        kcat = jnp.concatenate([r[0] for r in refs[1:1 + nkv]], axis=0).astype(BF16)
        vcat = jnp.concatenate([r[0] for r in refs[1 + nkv:1 + 2 * nkv]], axis=0).astype(BF16)
        q = q_ref[0].reshape(g * rows, d).astype(BF16)
        do = do_ref[0].reshape(g * rows, d).astype(BF16)
        p, p_sink = _gqa_scores(q, kcat, sink_ref[0], band, c_len, t_len, rows)
        dp = lax.dot_general(do, vcat, (((1,), (1,)), ((), ())), preferred_element_type=F32)
        rd = jnp.sum(p * dp, axis=-1, keepdims=True)
        ds = (p * (dp - rd) * scale).astype(BF16)
        dq_ref[0] = jnp.dot(ds, kcat, preferred_element_type=F32).reshape(g, rows, d)
        dkcat = lax.dot_general(ds, q, (((0,), (0,)), ((), ())), preferred_element_type=F32)
        dvcat = lax.dot_general(p.astype(BF16), do, (((0,), (0,)), ((), ())), preferred_element_type=F32)

        @pl.when(b == 0)
        def _():
            dkc_ref[...] = jnp.zeros_like(dkc_ref)
            dvc_ref[...] = jnp.zeros_like(dvc_ref)
            dsink_ref[...] = jnp.zeros_like(dsink_ref)

        dkc_ref[0] += dkcat[:c_len]
        dvc_ref[0] += dvcat[:c_len]
        dsink_ref[0] += -p_sink * rd
        if band:
            outs[3][0, 0] = dkcat[c_len:]
            outs[4][0, 0] = dvcat[c_len:]

    ctx_out = pl.BlockSpec((1, c_len, d), lambda a, b: (a, 0, 0))
    band_out = pl.BlockSpec((1, 1, 3 * BLOCK, d), lambda a, b: (a, b, 0, 0))
    out_specs = [q_spec, ctx_out, ctx_out] + ([band_out, band_out] if band else []) + [sink_spec]
    ctx_shape = jax.ShapeDtypeStruct((kv, c_len, d), F32)
    band_shape = jax.ShapeDtypeStruct((kv, nb, 3 * BLOCK, d), F32)
    out_shape = ([jax.ShapeDtypeStruct(q4.shape, F32), ctx_shape, ctx_shape] + ([band_shape, band_shape] if band else [])
                 + [jax.ShapeDtypeStruct(sink_rows.shape, F32)])
    return pl.pallas_call(
        body, name='gqa_bwd_band' if band else 'gqa_bwd_ctx', grid=(kv, nb),
        in_specs=[q_spec] + kv_specs + kv_specs + [sink_spec, q_spec], out_specs=out_specs, out_shape=out_shape,
        compiler_params=_cparams(('arbitrary', 'arbitrary')),
    )(q4, *([k2] * nkv), *([v2] * nkv), sink_rows, do4)


def _make_gqa(band, c_len):
    @jax.custom_vjp
    def f(q4, k2, v2, sink_rows):
        return _gqa_fwd_call(q4, k2, v2, sink_rows, band, c_len)

    def fwd(q4, k2, v2, sink_rows):
        return _gqa_fwd_call(q4, k2, v2, sink_rows, band, c_len), (q4, k2, v2, sink_rows)

    def bwd(res, do4):
        q4, k2, v2, sink_rows = res
        outs = _gqa_bwd_call(q4, k2, v2, sink_rows, do4, band, c_len)
        kv, n, d = k2.shape
        if not band:
            dq4, dkc, dvc, dsink = outs
            return dq4, dkc, dvc, dsink
        dq4, dkc, dvc, dkb, dvb, dsink = outs

        def fold(ctx_part, bands):
            cur = bands[:, :, BLOCK:2 * BLOCK]
            prv = jnp.pad(bands[:, 1:, :BLOCK], ((0, 0), (0, 1), (0, 0), (0, 0)))
            nxt = jnp.pad(bands[:, :-1, 2 * BLOCK:], ((0, 0), (1, 0), (0, 0), (0, 0)))
            lat = (cur + prv + nxt).reshape(kv, n - c_len, d)
            return jnp.concatenate([ctx_part, lat], axis=1)

        return dq4, fold(dkc, dkb), fold(dvc, dvb), dsink

    f.defvjp(fwd, bwd)
    return f


def _cmul(ar, ai, br, bi):
    return ar * br - ai * bi, ar * bi + ai * br


def _scan_tables(ar, ai, desc):
    a1 = (ar, ai)
    a2 = _cmul(*a1, *a1)
    a4 = _cmul(*a2, *a2)
    pw = [a1]
    for _ in range(SUBLANES - 1):
        pw.append(_cmul(*pw[-1], *a1))
    row = jnp.arange(SUBLANES)[:, None]
    tabs = []
    for dist, (pr, pi) in ((1, a1), (2, a2), (4, a4)):
        keep = (row <= SUBLANES - 1 - dist) if desc else (row >= dist)
        tabs += [jnp.where(keep, pr[None, :], 0.0), jnp.where(keep, pi[None, :], 0.0)]
    order = pw[::-1] if desc else pw
    tabs += [jnp.stack([p[0] for p in order]), jnp.stack([p[1] for p in order])]
    return jnp.stack(tabs).astype(F32)


def _scan_call(b_re, b_im, tabs, order, chunk, prev=None):
    n, s_dim = b_re.shape
    nch = n // chunk
    ng = chunk // SUBLANES
    desc = order in ('Fb', 'R')
    with_da = prev is not None

    def chunk_of(i):
        if order == 'F':
            return i
        if order == 'Fb':
            return nch - 1 - i
        if order == 'R':
            return jnp.where(i == 0, 0, nch - i)
        return jnp.where(i == nch - 1, 0, i + 1)

    def body(*refs):
        br_ref, bi_ref, tab_ref = refs[0], refs[1], refs[2]
        if with_da:
            pr_ref, pi_ref, sr_ref, si_ref, dar_ref, dai_ref, cr_ref, ci_ref = refs[3:]
        else:
            sr_ref, si_ref, pr_ref, pi_ref, cr_ref, ci_ref = refs[3:]

        @pl.when(pl.program_id(0) == 0)
        def _():
            cr_ref[...] = jnp.zeros_like(cr_ref)
            ci_ref[...] = jnp.zeros_like(ci_ref)
            if with_da:
                dar_ref[...] = jnp.zeros_like(dar_ref)
                dai_ref[...] = jnp.zeros_like(dai_ref)

        sub = lax.broadcasted_iota(jnp.int32, (SUBLANES, s_dim), 0)
        edge = SUBLANES - 1 if desc else 0
        last = 0 if desc else SUBLANES - 1

        def step(t, carry):
            gi = (ng - 1 - t) if desc else t
            rows = pl.ds(pl.multiple_of(gi * SUBLANES, SUBLANES), SUBLANES)
            xr, xi = br_ref[rows, :], bi_ref[rows, :]
            for j, dist in enumerate((1, 2, 4)):
                shift = SUBLANES - dist if desc else dist
                rr, ri = pltpu.roll(xr, shift, 0), pltpu.roll(xi, shift, 0)
                mr, mi = tab_ref[2 * j], tab_ref[2 * j + 1]
                xr, xi = xr + mr * rr - mi * ri, xi + mr * ri + mi * rr
            cr, ci = cr_ref[...], ci_ref[...]
            pwr, pwi = tab_ref[6], tab_ref[7]
            sr = xr + pwr * cr - pwi * ci
            si = xi + pwr * ci + pwi * cr
            sr_ref[rows, :] = sr
            si_ref[rows, :] = si
            if with_da:
                pr, pi = pr_ref[rows, :], pi_ref[rows, :]
                dar_ref[...] += sr * pr + si * pi
                dai_ref[...] += si * pr - sr * pi
            else:
                shift1 = SUBLANES - 1 if desc else 1
                pr_ref[rows, :] = jnp.where(sub == edge, cr, pltpu.roll(sr, shift1, 0))
                pi_ref[rows, :] = jnp.where(sub == edge, ci, pltpu.roll(si, shift1, 0))
            cr_ref[...] = jnp.broadcast_to(sr[last:last + 1, :], (SUBLANES, s_dim))
            ci_ref[...] = jnp.broadcast_to(si[last:last + 1, :], (SUBLANES, s_dim))
            return carry

        lax.fori_loop(0, ng, step, 0)

    blk = pl.BlockSpec((chunk, s_dim), lambda i: (chunk_of(i), 0))
    tab_spec = pl.BlockSpec((8, SUBLANES, s_dim), lambda i: (0, 0, 0))
    acc = pl.BlockSpec((SUBLANES, s_dim), lambda i: (0, 0))
    big = jax.ShapeDtypeStruct((n, s_dim), F32)
    small = jax.ShapeDtypeStruct((SUBLANES, s_dim), F32)
    if with_da:
        in_specs, ins = [blk, blk, tab_spec, blk, blk], [b_re, b_im, tabs, prev[0], prev[1]]
        out_specs, out_shape = [blk, blk, acc, acc], [big, big, small, small]
    else:
        in_specs, ins = [blk, blk, tab_spec], [b_re, b_im, tabs]
        out_specs, out_shape = [blk, blk, blk, blk], [big, big, big, big]
    return pl.pallas_call(
        body, name='s5_scan_' + order, grid=(nch,), in_specs=in_specs, out_specs=out_specs, out_shape=out_shape,
        scratch_shapes=[pltpu.VMEM((SUBLANES, s_dim), F32), pltpu.VMEM((SUBLANES, s_dim), F32)],
        compiler_params=_cparams(('arbitrary',)),
    )(*ins)


def _make_scan(rev, chunk):
    def run(b_re, b_im, ar, ai):
        tabs = _scan_tables(ar, ai, desc=rev)
        return _scan_call(b_re, b_im, tabs, 'R' if rev else 'F', chunk)

    @jax.custom_vjp
    def f(b_re, b_im, ar, ai):
        return tuple(run(b_re, b_im, ar, ai)[:2])

    def fwd(b_re, b_im, ar, ai):
        s_re, s_im, p_re, p_im = run(b_re, b_im, ar, ai)
        return (s_re, s_im), (p_re, p_im, ar, ai)

    def bwd(res, g):
        p_re, p_im, ar, ai = res
        tabs = _scan_tables(ar, -ai, desc=not rev)
        db_re, db_im, dar, dai = _scan_call(g[0], g[1], tabs, 'Rb' if rev else 'Fb', chunk, prev=(p_re, p_im))
        return db_re, db_im, jnp.sum(dar, axis=0), jnp.sum(dai, axis=0)

    f.defvjp(fwd, bwd)
    return f


def _sqerr_call(y, t):
    n, d = y.shape
    rb = _tile(n, 512, SUBLANES)

    def body(y_ref, t_ref, o_ref):
        @pl.when(pl.program_id(0) == 0)
        def _():
            o_ref[...] = jnp.zeros_like(o_ref)

        e = y_ref[...] - t_ref[...]
        o_ref[...] += jnp.sum(e * e, axis=0, keepdims=True)

    row = pl.BlockSpec((rb, d), lambda i: (i, 0))
    return pl.pallas_call(
        body, name='sq_err', grid=(n // rb,), in_specs=[row, row], out_specs=pl.BlockSpec((1, d), lambda i: (0, 0)),
        out_shape=jax.ShapeDtypeStruct((1, d), F32), compiler_params=_cparams(('arbitrary',)),
    )(y, t)


@jax.custom_vjp
def loss_head(y, t):
    return 0.5 * jnp.sum(_sqerr_call(y, t)) / y.shape[1]


def _loss_head_fwd(y, t):
    return loss_head(y, t), (y, t)


def _loss_head_bwd(res, g):
    y, t = res
    return g * (y - t) / y.shape[1], None


loss_head.defvjp(_loss_head_fwd, _loss_head_bwd)


def _adamw_call(w, g, m, v):
    r, c = w.shape
    rb = _tile(r, max(SUBLANES, (256 * 1024) // max(c, LANES) // SUBLANES * SUBLANES), SUBLANES)

    def body(w_ref, g_ref, m_ref, v_ref, d_ref, nm_ref, nv_ref):
        gv = g_ref[...]
        nm = ADAM_B1 * m_ref[...] + (1.0 - ADAM_B1) * gv
        nv = ADAM_B2 * v_ref[...] + (1.0 - ADAM_B2) * (gv * gv)
        m_hat = nm / (1.0 - ADAM_B1 ** ADAM_STEP)
        v_hat = nv / (1.0 - ADAM_B2 ** ADAM_STEP)
        d_ref[...] = -ADAM_LR * (m_hat / (jnp.sqrt(v_hat) + ADAM_EPS) + ADAM_WD * w_ref[...])
        nm_ref[...] = nm
        nv_ref[...] = nv

    blk = pl.BlockSpec((rb, c), lambda i: (i, 0))
    shape = jax.ShapeDtypeStruct((r, c), F32)
    return pl.pallas_call(
        body, name='adamw', grid=(r // rb,), in_specs=[blk] * 4, out_specs=[blk] * 3, out_shape=[shape] * 3,
        compiler_params=_cparams(('parallel',)),
    )(w, g, m, v)


MESH = pl.DeviceIdType.MESH
HBM_SPEC = pl.BlockSpec(memory_space=pltpu.HBM)


def _all_gather(x):
    def body(x_ref, out_ref, send_sems, recv_sems, local_sem):
        x, y, c = lax.axis_index('x'), lax.axis_index('y'), lax.axis_index('c')
        me, sibling = (x, y, c), (x, y, 1 - c)
        chips = [(1 - x, y), (x, 1 - y), (1 - x, 1 - y)]

        def slot(px, py, pc):
            return out_ref.at[4 * px + 2 * py + pc]

        def copy(k, block, to, src=None):
            return pltpu.make_async_remote_copy(
                src_ref=slot(*block) if src is None else src, dst_ref=slot(*block),
                send_sem=send_sems.at[k], recv_sem=recv_sems.at[k], device_id=to, device_id_type=MESH)

        mine = pltpu.make_async_copy(x_ref, slot(*me), local_sem)
        mine.start()
        first = [copy(0, me, sibling, src=x_ref)]
        first += [copy(1 + j, me, (*chip, c), src=x_ref) for j, chip in enumerate(chips)]
        for cp in first:
            cp.start()
        passed = [copy(4 + j, (*chip, c), sibling) for j, chip in enumerate(chips)]
        for j, chip in enumerate(chips):
            copy(1 + j, (*chip, c), me).wait_recv()
            passed[j].start()
        copy(0, sibling, me).wait_recv()
        for j, chip in enumerate(chips):
            copy(4 + j, (*chip, 1 - c), me).wait_recv()
        for cp in first + passed:
            cp.wait_send()
        mine.wait()

    return pl.pallas_call(
        body, name='all_gather', out_shape=jax.ShapeDtypeStruct((N_DEV,) + x.shape, x.dtype),
        in_specs=[HBM_SPEC], out_specs=HBM_SPEC,
        scratch_shapes=[pltpu.SemaphoreType.DMA((7,)), pltpu.SemaphoreType.DMA((7,)), pltpu.SemaphoreType.DMA],
    )(x)


def _exchange_sibling(g_all):
    def body(g_ref, out_ref, send_sem, recv_sem):
        x, y, c = lax.axis_index('x'), lax.axis_index('y'), lax.axis_index('c')
        cp = pltpu.make_async_remote_copy(src_ref=g_ref.at[1 - c], dst_ref=out_ref, send_sem=send_sem, recv_sem=recv_sem,
                                          device_id=(x, y, 1 - c), device_id_type=MESH)
        cp.start()
        cp.wait()

    return pl.pallas_call(
        body, name='rs_sibling', out_shape=jax.ShapeDtypeStruct(g_all.shape[1:], g_all.dtype),
        in_specs=[HBM_SPEC], out_specs=HBM_SPEC,
        scratch_shapes=[pltpu.SemaphoreType.DMA, pltpu.SemaphoreType.DMA],
    )(g_all)


def _exchange_chips(p):
    def body(p_ref, out_ref, send_sems, recv_sems):
        x, y, c = lax.axis_index('x'), lax.axis_index('y'), lax.axis_index('c')
        chips = [(1 - x, y), (x, 1 - y), (1 - x, 1 - y)]
        copies = [pltpu.make_async_remote_copy(src_ref=p_ref.at[2 * px + py], dst_ref=out_ref.at[j],
                                               send_sem=send_sems.at[j], recv_sem=recv_sems.at[j],
                                               device_id=(px, py, c), device_id_type=MESH)
                  for j, (px, py) in enumerate(chips)]
        for cp in copies:
            cp.start()
        for cp in copies:
            cp.wait_recv()
        for cp in copies:
            cp.wait_send()

    return pl.pallas_call(
        body, name='rs_chips', out_shape=jax.ShapeDtypeStruct((3,) + p.shape[1:], p.dtype),
        in_specs=[HBM_SPEC], out_specs=HBM_SPEC,
        scratch_shapes=[pltpu.SemaphoreType.DMA((3,)), pltpu.SemaphoreType.DMA((3,))],
    )(p)


def _add_sibling(g_all, recv, c_idx):
    _, nchip, r, _ = g_all.shape
    rb = _tile(r, PACK_ROWS, SUBLANES)

    def body(c_ref, g_ref, r_ref, o_ref, ob_ref):
        s = g_ref[0] + r_ref[...]
        o_ref[...] = s
        ob_ref[...] = s.astype(BF16)

    blk = pl.BlockSpec((1, rb, LANES), lambda k, i, c: (k, i, 0))
    return pl.pallas_call(
        body, name='rs_add_sibling',
        grid_spec=pltpu.PrefetchScalarGridSpec(
            num_scalar_prefetch=1, grid=(nchip, r // rb),
            in_specs=[pl.BlockSpec((1, 1, rb, LANES), lambda k, i, c: (c[0], k, i, 0)), blk],
            out_specs=[blk, blk]),
        out_shape=[jax.ShapeDtypeStruct(recv.shape, F32), jax.ShapeDtypeStruct(recv.shape, BF16)],
        compiler_params=_cparams(('parallel', 'parallel')),
    )(c_idx, g_all, recv)


def _add_chips(p, recv, chip_idx):
    _, r, _ = p.shape
    rb = _tile(r, PACK_ROWS, SUBLANES)

    def body(k_ref, p_ref, r0, r1, r2, o_ref):
        o_ref[...] = ((p_ref[0] + r0[0].astype(F32)) + r1[0].astype(F32)) + r2[0].astype(F32)

    rspec = lambda j: pl.BlockSpec((1, rb, LANES), lambda i, k: (j, i, 0))
    return pl.pallas_call(
        body, name='rs_add_chips',
        grid_spec=pltpu.PrefetchScalarGridSpec(
            num_scalar_prefetch=1, grid=(r // rb,),
            in_specs=[pl.BlockSpec((1, rb, LANES), lambda i, k: (k[0], i, 0)), rspec(0), rspec(1), rspec(2)],
            out_specs=pl.BlockSpec((rb, LANES), lambda i, k: (i, 0))),
        out_shape=jax.ShapeDtypeStruct((r, LANES), F32), compiler_params=_cparams(('parallel',)),
    )(chip_idx, p, recv, recv, recv)


def _reduce_scatter(g_all, c_idx, chip_idx):
    part, part_bf16 = _add_sibling(g_all, _exchange_sibling(g_all), c_idx)
    return _add_chips(part, _exchange_chips(part_bf16), chip_idx)


def _pad_to(n, mult):
    return (n + mult - 1) // mult * mult


def _pack(pieces, lead, dtype):
    flat = []
    total = 0
    for p in pieces:
        f = p.reshape(lead + (-1,)).astype(dtype)
        n = _pad_to(f.shape[-1], 16 * LANES)
        flat.append(jnp.pad(f, [(0, 0)] * len(lead) + [(0, n - f.shape[-1])]))
        total += n
    full = _pad_to(total, PACK_ROWS * LANES)
    if full > total:
        flat.append(jnp.zeros(lead + (full - total,), dtype))
    return jnp.concatenate(flat, axis=-1).reshape(lead + (full // LANES, LANES))


def _unpack(buf, lead, shapes):
    flat = buf.reshape(lead + (-1,))
    out, off = [], 0
    for s in shapes:
        n = math.prod(s)
        out.append(flat[..., off:off + n].reshape(lead + tuple(s)))
        off += _pad_to(n, 16 * LANES)
    return out


def _rope_tables(c_len, t_len, n):
    quarter = n // 4
    inv = ROPE_BASE ** (-jnp.arange(0, 2 * quarter, 2, dtype=F32) / (2 * quarter))
    t = jnp.arange(t_len, dtype=jnp.int32)
    pos = jnp.stack([(t // GRID_W).astype(F32), (t % GRID_W).astype(F32)], axis=1)
    ang = pos[:, :, None] * inv[None, None, :]
    ang = jnp.concatenate([jnp.zeros((c_len, 2, quarter), F32), ang], axis=0)
    return jnp.cos(ang), jnp.sin(ang)


def _axial_rope(x, cos, sin):
    n_rows, h, n = x.shape
    xs = x.reshape(n_rows, h, 2, 2, n // 4)
    x1, x2 = xs[:, :, :, 0], xs[:, :, :, 1]
    c, s = cos[:, None], sin[:, None]
    return jnp.stack([x1 * c - x2 * s, x1 * s + x2 * c], axis=3).reshape(n_rows, h, n)


def _ssm_discretize(lam_re, lam_im, log_dt, b_re, b_im):
    dt = jnp.exp(log_dt)[:, None]
    mag = jnp.exp(lam_re * dt)
    a_re, a_im = mag * jnp.cos(lam_im * dt), mag * jnp.sin(lam_im * dt)
    den = lam_re * lam_re + lam_im * lam_im
    w_re = ((a_re - 1) * lam_re + a_im * lam_im) / den
    w_im = (a_im * lam_re - (a_re - 1) * lam_im) / den
    bb_re, bb_im = _cmul(w_re[..., None], w_im[..., None], b_re, b_im)
    return a_re, a_im, bb_re, bb_im


def _block_diag_in(b):
    g = b.shape[0]
    return jnp.einsum('gpm,gh->gmhp', b, jnp.eye(g, dtype=F32)).reshape(g * b.shape[2], g * b.shape[1])


def _block_diag_out(c):
    g = c.shape[0]
    return jnp.einsum('gmp,gh->gphm', c, jnp.eye(g, dtype=F32)).reshape(g * c.shape[2], g * c.shape[1])


def _w_in_layout(d_model):
    sizes = (MLA_Q_RANK, MLA_KV_RANK, MLA_ROPE, SSM_WIDTH, GQA_HEADS * GQA_D, GQA_KV * GQA_D, GQA_KV * GQA_D, 3 * d_model)
    starts = [0]
    for s in sizes[:-1]:
        starts.append(starts[-1] + s)
    names = ('cq', 'ckv', 'kr', 'u', 'gq', 'gk', 'gv', 'gates')
    orig = dict(zip(names, zip(starts, sizes)))
    order = ('cq', 'ckv', 'u', 'gq', 'gk', 'gv', 'gates', 'kr')
    return orig, order


def _permute_w_in(blocks, d_model):
    orig, order = _w_in_layout(d_model)
    w = blocks.reshape(-1, blocks.shape[2])
    rows = [w[orig[k][0]:orig[k][0] + orig[k][1]] for k in order]
    width = sum(orig[k][1] for k in order)
    return jnp.pad(jnp.concatenate(rows, axis=0), ((0, _pad_to(width, LANES) - width), (0, 0)))


def _unpermute_w_in(gp, d_model, r):
    orig, order = _w_in_layout(d_model)
    pos, off = {}, 0
    for k in order:
        pos[k] = off
        off += orig[k][1]
    names = sorted(orig, key=lambda k: orig[k][0])
    w = jnp.concatenate([gp[pos[k]:pos[k] + orig[k][1]] for k in names], axis=0)
    return w.reshape(N_DEV, r, gp.shape[1])


def _forward_loss(x_all, sinks, rp, wl, cc_in, target, c_len):
    n, d = x_all.shape
    t_len = n - c_len
    depth = len(wl)
    mla_attn = _make_mla(c_len)
    norm_mod = _make_norm(c_len, True)
    norm_tok = _make_norm(_tile(n, 512, SUBLANES), False)
    norm_out = _make_norm(_tile(t_len, 512, SUBLANES), False)
    half_res = _make_gated_res(c_len, 0.5)
    full_res = _make_gated_res(c_len, 1.0)
    gqa_band = _make_gqa(True, c_len)
    gqa_ctx = _make_gqa(False, c_len)
    scans = (_make_scan(False, c_len), _make_scan(True, c_len))
    cos_m, sin_m = _rope_tables(c_len, t_len, MLA_ROPE)
    cos_g, sin_g = _rope_tables(c_len, t_len, GQA_D)
    orig, order = _w_in_layout(d)
    offs, off = {}, 0
    for k in order:
        offs[k] = (off, orig[k][1])
        off += orig[k][1]

    cc = jnp.zeros((SUBLANES, d), F32).at[0].set(jax.nn.silu(rp['c_ctx'])).at[1].set(jax.nn.silu(cc_in))

    for l in range(depth):
        w, sk = wl[l], sinks[l]
        ctx_out = l < depth - 1

        def mm(h, name):
            return (matmul if name in ROW_SHARDED else matmul_t)(h, w[name], sk[name])

        def swiglu(h, name13, name2):
            a13 = mm(h, name13)
            f = a13.shape[1] // 2
            return mm(jax.nn.silu(a13[:, :f]) * a13[:, f:], name2)

        mod = mm(cc, 'ada_w') + rp['ada_b'][l][None, :]
        md = [mod[0:2, i * d:(i + 1) * d] for i in range(N_MOD)]
        x_all = half_res(x_all, swiglu(norm_mod(x_all, rp['norm_ffn1'][l], md[0], md[1]), 'ffn1_w13', 'ffn1_w2'), md[2])

        z = mm(norm_mod(x_all, rp['norm_mix'][l], md[3], md[4]), 'w_in')
        part = {k: z[:, o:o + s] for k, (o, s) in offs.items()}

        q3 = mm(norm_tok(part['cq'], rp['mla_q_norm'][l]), 'mla_w_uq').reshape(n, MLA_HEADS, LANES)
        q = jnp.concatenate([q3[..., :MLA_NOPE], _axial_rope(q3[..., MLA_NOPE:MLA_NOPE + MLA_ROPE], cos_m, sin_m),
                             q3[..., MLA_NOPE + MLA_ROPE:]], axis=-1).reshape(n, MLA_HEADS * LANES)
        kvp = mm(norm_tok(part['ckv'], rp['mla_kv_norm'][l]), 'mla_w_ukv')
        kr = _axial_rope(part['kr'].reshape(n, 1, MLA_ROPE), cos_m, sin_m).reshape(n, MLA_ROPE)
        kr = jnp.pad(kr, ((0, 0), (MLA_NOPE, LANES - MLA_NOPE - MLA_ROPE)))
        mla = mm(mla_attn(q, kvp, kr), 'mla_w_o')

        u = part['u']
        y = u * rp['ssm_d'][l][None, :]
        for direction in range(2):
            a_re, a_im, bb_re, bb_im = _ssm_discretize(
                rp['ssm_lambda_re'][l, direction], rp['ssm_lambda_im'][l, direction], rp['ssm_log_dt'][l, direction],
                rp['ssm_b_re'][l, direction], rp['ssm_b_im'][l, direction])
            s_re, s_im = scans[direction](matmul_d(u, _block_diag_in(bb_re)), matmul_d(u, _block_diag_in(bb_im)),
                                          a_re.reshape(-1), a_im.reshape(-1))
            y = y + (matmul_d(s_re, _block_diag_out(rp['ssm_c_re'][l, direction]))
                     - matmul_d(s_im, _block_diag_out(rp['ssm_c_im'][l, direction])))
        yg = mm(jax.nn.gelu(y), 'ssm_w_glu')
        ssm = yg[:, :d] * jax.nn.sigmoid(yg[:, d:])

        gq = _axial_rope(part['gq'].reshape(n, GQA_HEADS, GQA_D), cos_g, sin_g)
        gk = _axial_rope(part['gk'].reshape(n, GQA_KV, GQA_D), cos_g, sin_g)
        q4 = jnp.transpose(gq.reshape(n, GQA_KV, GQA_G, GQA_D), (1, 2, 0, 3))
        k2 = jnp.transpose(gk, (1, 0, 2))
        v2 = jnp.transpose(part['gv'].reshape(n, GQA_KV, GQA_D), (1, 0, 2))
        sink = rp['gqa_sink'][l].reshape(GQA_KV, GQA_G, 1, 1)
        sink_rows = lambda rows: jnp.broadcast_to(sink, (GQA_KV, GQA_G, rows, 1)).reshape(GQA_KV, GQA_G * rows, 1)
        g_lat = gqa_band(q4[:, :, c_len:], k2, v2, sink_rows(BLOCK))
        if ctx_out:
            g_ctx = gqa_ctx(q4[:, :, :c_len], k2[:, :c_len], v2[:, :c_len], sink_rows(c_len))
        else:
            g_ctx = jnp.zeros((GQA_KV, GQA_G, c_len, GQA_D), F32)
        go = jnp.transpose(jnp.concatenate([g_ctx, g_lat], axis=2), (2, 0, 1, 3)).reshape(n, GQA_HEADS * GQA_D)
        gqa = mm(go, 'gqa_w_o')

        gates = jax.nn.sigmoid(part['gates'])
        mixed = gates[:, :d] * mla + gates[:, d:2 * d] * ssm + gates[:, 2 * d:] * gqa
        x_all = full_res(x_all, mm(mixed, 'w_out'), md[5])
        x_all = half_res(x_all, swiglu(norm_mod(x_all, rp['norm_ffn2'][l], md[6], md[7]), 'ffn2_w13', 'ffn2_w2'), md[8])

    return loss_head(norm_out(x_all[c_len:], rp['final_norm']), target)


def kernel(x, c, ctx, c_ctx, ada_w, ada_b, norm_ffn1, norm_mix, norm_ffn2, ffn1_w13, ffn1_w2, ffn2_w13, ffn2_w2, w_in, mla_q_norm, mla_kv_norm, mla_w_uq, mla_w_ukv, mla_w_o, ssm_lambda_re, ssm_lambda_im, ssm_log_dt, ssm_b_re, ssm_b_im, ssm_c_re, ssm_c_im, ssm_d, ssm_w_glu, gqa_sink, gqa_w_o, w_out, final_norm, loss_target, m_c_ctx, m_ada_w, m_ada_b, m_norm_ffn1, m_norm_mix, m_norm_ffn2, m_ffn1_w13, m_ffn1_w2, m_ffn2_w13, m_ffn2_w2, m_w_in, m_mla_q_norm, m_mla_kv_norm, m_mla_w_uq, m_mla_w_ukv, m_mla_w_o, m_ssm_lambda_re, m_ssm_lambda_im, m_ssm_log_dt, m_ssm_b_re, m_ssm_b_im, m_ssm_c_re, m_ssm_c_im, m_ssm_d, m_ssm_w_glu, m_gqa_sink, m_gqa_w_o, m_w_out, m_final_norm, v_c_ctx, v_ada_w, v_ada_b, v_norm_ffn1, v_norm_mix, v_norm_ffn2, v_ffn1_w13, v_ffn1_w2, v_ffn2_w13, v_ffn2_w2, v_w_in, v_mla_q_norm, v_mla_kv_norm, v_mla_w_uq, v_mla_w_ukv, v_mla_w_o, v_ssm_lambda_re, v_ssm_lambda_im, v_ssm_log_dt, v_ssm_b_re, v_ssm_b_im, v_ssm_c_re, v_ssm_c_im, v_ssm_d, v_ssm_w_glu, v_gqa_sink, v_gqa_w_o, v_w_out, v_final_norm):
    args = dict(locals())
    weights = {k: args[k] for k in WEIGHTS}
    moments_m = {k: args['m_' + k] for k in WEIGHTS}
    moments_v = {k: args['v_' + k] for k in WEIGHTS}
    depth = ada_w.shape[0]
    d = x.shape[-1]
    c_len = ctx.shape[1]
    my_c = lax.axis_index('c')
    my_chip = 2 * lax.axis_index('x') + lax.axis_index('y')
    qk_w = MLA_NOPE + MLA_ROPE

    def as_rows(k, a):
        return a if k in ROW_SHARDED else a.T

    shard_shapes = [as_rows(k, weights[k][0]).shape for k in SHARDED]
    r_in = weights['w_in'].shape[2]
    layers = []
    for l in range(depth):
        gathered = _all_gather(_pack([as_rows(k, weights[k][l]) for k in SHARDED], (), BF16))
        full = {}
        for k, blocks in zip(SHARDED, _unpack(gathered, (N_DEV,), shard_shapes)):
            full[k] = _permute_w_in(blocks, d) if k == 'w_in' else blocks.reshape(N_DEV * blocks.shape[1], blocks.shape[2])
        full['mla_w_uq'] = jnp.pad(full['mla_w_uq'].reshape(MLA_HEADS, qk_w, MLA_Q_RANK),
                                   ((0, 0), (0, LANES - qk_w), (0, 0))).reshape(MLA_HEADS * LANES, MLA_Q_RANK)
        full['mla_w_o'] = jnp.pad(full['mla_w_o'].reshape(d, MLA_HEADS, MLA_V),
                                  ((0, 0), (0, 0), (LANES - MLA_V, 0))).reshape(d, MLA_HEADS * LANES)
        layers.append(full)
    sinks = [{k: jnp.zeros(v.shape, F32) for k, v in full.items()} for full in layers]

    rp = {k: weights[k] for k in REPLICATED}
    x_all = jnp.concatenate([ctx[0], x[0]], axis=0)
    loss_fn = functools.partial(_forward_loss, wl=layers, cc_in=c[0], target=loss_target[0], c_len=c_len)
    loss, vjp = jax.vjp(loss_fn, x_all, sinks, rp)
    g_x, g_layers, g_rp = vjp(jnp.ones((), F32))

    grads = {k: [] for k in SHARDED}
    for l in range(depth):
        gl = dict(g_layers[l])
        gl['mla_w_uq'] = gl['mla_w_uq'].reshape(MLA_HEADS, LANES, MLA_Q_RANK)[:, :qk_w].reshape(MLA_HEADS * qk_w, MLA_Q_RANK)
        gl['mla_w_o'] = gl['mla_w_o'].reshape(d, MLA_HEADS, LANES)[:, :, LANES - MLA_V:].reshape(d, MLA_HEADS * MLA_V)
        pieces = [_unpermute_w_in(gl[k], d, r_in) if k == 'w_in' else gl[k].reshape((N_DEV,) + shape)
                  for k, shape in zip(SHARDED, shard_shapes)]
        packed = _pack(pieces, (N_DEV,), F32)
        g_all = jnp.swapaxes(packed.reshape((4, 2) + packed.shape[1:]), 0, 1)
        mine = _reduce_scatter(g_all, my_c.reshape(1).astype(jnp.int32), my_chip.reshape(1).astype(jnp.int32))
        for k, g in zip(SHARDED, _unpack(mine, (), shard_shapes)):
            grads[k].append(as_rows(k, g))
    grads = {k: jnp.stack(v) for k, v in grads.items()}

    rep_shapes = [weights[k].shape for k in REPLICATED] + [(1,)]
    small = _pack([g_rp[k] for k in REPLICATED] + [loss.reshape(1)], (), F32)
    summed = _unpack(_sum_leading(_all_gather(small)), (), rep_shapes)
    for k, g in zip(REPLICATED, summed[:-1]):
        grads[k] = g
    loss_total = summed[-1].reshape(())

    delta, new_m, new_v = {}, {}, {}
    for k in SHARDED:
        shape = weights[k].shape
        as2d = lambda a: a.reshape(-1, shape[-1])
        outs = _adamw_call(as2d(weights[k]), as2d(grads[k]), as2d(moments_m[k]), as2d(moments_v[k]))
        delta[k], new_m[k], new_v[k] = (o.reshape(shape) for o in outs)
    rep_all = [weights[k].shape for k in REPLICATED]
    packs = [_pack([src[k] for k in REPLICATED], (), F32) for src in (weights, grads, moments_m, moments_v)]
    outs = [_unpack(o, (), rep_all) for o in _adamw_call(*packs)]
    for i, k in enumerate(REPLICATED):
        delta[k], new_m[k], new_v[k] = outs[0][i], outs[1][i], outs[2][i]

    return (loss_total, g_x[c_len:][None], *[grads[k] for k in WEIGHTS], *[delta[k] for k in WEIGHTS],
            *[new_m[k] for k in WEIGHTS], *[new_v[k] for k in WEIGHTS])
```

```python
import functools
import math

import jax
import jax.numpy as jnp
from jax import lax
from jax.experimental import pallas as pl
from jax.experimental.pallas import tpu as pltpu

F32 = jnp.float32
BF16 = jnp.bfloat16

MLA_HEADS, MLA_NOPE, MLA_ROPE, MLA_V = 8, 64, 32, 64
MLA_Q_RANK, MLA_KV_RANK = 384, 256
SSM_WIDTH, SSM_GROUP, SSM_STATE = 512, 16, 64
SSM_GROUPS = SSM_WIDTH // SSM_GROUP
GQA_HEADS, GQA_KV, GQA_D = 8, 2, 64
GQA_G = GQA_HEADS // GQA_KV
WINDOW, BLOCK, GRID_W = 128, 128, 64
N_MOD = 9
ROPE_BASE = 10000.0
EPS = 1e-6
NEG_INF = -1e30
ADAM_LR, ADAM_B1, ADAM_B2, ADAM_EPS, ADAM_WD, ADAM_STEP = 0.001, 0.9, 0.999, 1e-08, 0.01, 10

N_DEV = 8
LANES = 128
SUBLANES = 8
VMEM_LIMIT = 56 * 1024 * 1024
PACK_ROWS = 1024

SHARDED = ('ada_w', 'ffn1_w13', 'ffn1_w2', 'ffn2_w13', 'ffn2_w2', 'w_in', 'mla_w_uq', 'mla_w_ukv',
           'mla_w_o', 'ssm_w_glu', 'gqa_w_o', 'w_out')
ROW_SHARDED = ('ffn1_w2', 'ffn2_w2', 'w_out')
REPLICATED = ('c_ctx', 'ada_b', 'norm_ffn1', 'norm_mix', 'norm_ffn2', 'mla_q_norm', 'mla_kv_norm',
              'ssm_lambda_re', 'ssm_lambda_im', 'ssm_log_dt', 'ssm_b_re', 'ssm_b_im', 'ssm_c_re', 'ssm_c_im',
              'ssm_d', 'gqa_sink', 'final_norm')
WEIGHTS = ('c_ctx', 'ada_w', 'ada_b', 'norm_ffn1', 'norm_mix', 'norm_ffn2', 'ffn1_w13', 'ffn1_w2', 'ffn2_w13',
           'ffn2_w2', 'w_in', 'mla_q_norm', 'mla_kv_norm', 'mla_w_uq', 'mla_w_ukv', 'mla_w_o', 'ssm_lambda_re',
           'ssm_lambda_im', 'ssm_log_dt', 'ssm_b_re', 'ssm_b_im', 'ssm_c_re', 'ssm_c_im', 'ssm_d', 'ssm_w_glu',
           'gqa_sink', 'gqa_w_o', 'w_out', 'final_norm')


def _tile(n, target, mult):
    t = (min(target, n) // mult) * mult
    while t >= mult:
        if n % t == 0:
            return t
        t -= mult
    return n


def _cparams(sem):
    return pltpu.CompilerParams(dimension_semantics=sem, vmem_limit_bytes=VMEM_LIMIT)


def _mm(a, b, mode):
    if mode == 'nn':
        (M, K), N = a.shape, b.shape[1]
        tm, tn, tk = _tile(M, 1408, SUBLANES), _tile(N, 1024, LANES), _tile(K, 1408, LANES)
    elif mode == 'nt':
        (M, K), N = a.shape, b.shape[0]
        tm, tn, tk = _tile(M, 1408, SUBLANES), _tile(N, 1024, LANES), _tile(K, 1408, LANES)
    else:
        (K, M), N = a.shape, b.shape[1]
        tm, tn, tk = _tile(M, 1408, LANES), _tile(N, 1408, LANES), _tile(K, 768, 2 * SUBLANES)
    nk = K // tk
    dims = {'nn': (((1,), (0,)), ((), ())), 'nt': (((1,), (1,)), ((), ())), 'tn': (((0,), (0,)), ((), ()))}[mode]
    keep_a = nk == 1 and mode != 'tn' and N // tn > 1

    def body(a_ref, b_ref, o_ref, *scratch):
        if keep_a:
            @pl.when(pl.program_id(1) == 0)
            def _():
                scratch[0][...] = a_ref[...].astype(BF16)

            av = scratch[0][...]
        else:
            av = a_ref[...].astype(BF16)
        part = lax.dot_general(av, b_ref[...].astype(BF16), dims, preferred_element_type=F32)
        if nk == 1:
            o_ref[...] = part
        else:
            @pl.when(pl.program_id(2) == 0)
            def _():
                o_ref[...] = part

            @pl.when(pl.program_id(2) > 0)
            def _():
                o_ref[...] += part

    a_spec = pl.BlockSpec((tk, tm), lambda i, j, k: (k, i)) if mode == 'tn' else pl.BlockSpec((tm, tk), lambda i, j, k: (i, k))
    b_spec = pl.BlockSpec((tn, tk), lambda i, j, k: (j, k)) if mode == 'nt' else pl.BlockSpec((tk, tn), lambda i, j, k: (k, j))
    return pl.pallas_call(
        body, name='mm_' + mode, grid=(M // tm, N // tn, nk),
        in_specs=[a_spec, b_spec], out_specs=pl.BlockSpec((tm, tn), lambda i, j, k: (i, j)),
        out_shape=jax.ShapeDtypeStruct((M, N), F32),
        scratch_shapes=[pltpu.VMEM((tm, tk), BF16)] if keep_a else [],
        compiler_params=_cparams(('parallel', 'arbitrary', 'arbitrary')),
    )(a, b)


@jax.custom_vjp
def matmul_d(x, w):
    return _mm(x, w, 'nn')


def _matmul_d_fwd(x, w):
    return _mm(x, w, 'nn'), (x, w)


def _matmul_d_bwd(res, g):
    x, w = res
    return _mm(g, w, 'nt'), _mm(x, g, 'tn')


matmul_d.defvjp(_matmul_d_fwd, _matmul_d_bwd)


@jax.custom_vjp
def matmul(x, w, sink):
    return _mm(x, w, 'nn')


def _matmul_fwd(x, w, sink):
    return _mm(x, w, 'nn'), (x, w)


def _matmul_bwd(res, g):
    x, w = res
    return _mm(g, w, 'nt'), jnp.zeros_like(w), _mm(x, g, 'tn')


matmul.defvjp(_matmul_fwd, _matmul_bwd)


@jax.custom_vjp
def matmul_t(x, wt, sink):
    return _mm(x, wt, 'nt')


def _matmul_t_fwd(x, wt, sink):
    return _mm(x, wt, 'nt'), (x, wt)


def _matmul_t_bwd(res, g):
    x, wt = res
    return _mm(g, wt, 'nn'), jnp.zeros_like(wt), _mm(g, x, 'tn')


matmul_t.defvjp(_matmul_t_fwd, _matmul_t_bwd)


def _norm_fwd_call(x, g, sh, sc, rb):
    n, d = x.shape
    has_mod = sh is not None

    def body(*refs):
        x_ref, g_ref = refs[0], refs[1]
        o_ref = refs[-1]
        xv = x_ref[...]
        r = lax.rsqrt(jnp.mean(xv * xv, axis=-1, keepdims=True) + EPS)
        y = xv * r * g_ref[...]
        if has_mod:
            lat = pl.program_id(0) > 0
            shv = jnp.where(lat, refs[2][1:2, :], refs[2][0:1, :])
            scv = jnp.where(lat, refs[3][1:2, :], refs[3][0:1, :])
            y = y * (1.0 + scv) + shv
        o_ref[...] = y

    row = pl.BlockSpec((rb, d), lambda i: (i, 0))
    vec = pl.BlockSpec((1, d), lambda i: (0, 0))
    two = pl.BlockSpec((2, d), lambda i: (0, 0))
    ins = [x, g.reshape(1, d)] + ([sh, sc] if has_mod else [])
    return pl.pallas_call(
        body, name='norm_fwd', grid=(n // rb,), in_specs=[row, vec] + ([two, two] if has_mod else []),
        out_specs=row, out_shape=jax.ShapeDtypeStruct((n, d), F32), compiler_params=_cparams(('parallel',)),
    )(*ins)


def _norm_bwd_call(x, g, sh, sc, dy, rb):
    n, d = x.shape
    has_mod = sh is not None

    def body(*refs):
        x_ref, g_ref, dy_ref = refs[0], refs[1], refs[-3]
        dx_ref, acc_ref = refs[-2], refs[-1]
        i = pl.program_id(0)
        xv, dyv, gv = x_ref[...], dy_ref[...], g_ref[...]
        r = lax.rsqrt(jnp.mean(xv * xv, axis=-1, keepdims=True) + EPS)
        xh = xv * r
        if has_mod:
            lat = i > 0
            scv = jnp.where(lat, refs[3][1:2, :], refs[3][0:1, :])
            dyg = dyv * (1.0 + scv)
        else:
            dyg = dyv
        dxh = dyg * gv
        dx_ref[...] = r * (dxh - xh * jnp.mean(dxh * xh, axis=-1, keepdims=True))

        @pl.when(i == 0)
        def _():
            acc_ref[...] = jnp.zeros_like(acc_ref)

        acc_ref[0:1, :] += jnp.sum(dyg * xh, axis=0, keepdims=True)
        if has_mod:
            dsh = jnp.sum(dyv, axis=0, keepdims=True)
            dsc = jnp.sum(dyv * xh * gv, axis=0, keepdims=True)

            @pl.when(i == 0)
            def _():
                acc_ref[1:2, :] += dsh
                acc_ref[3:4, :] += dsc

            @pl.when(i > 0)
            def _():
                acc_ref[2:3, :] += dsh
                acc_ref[4:5, :] += dsc

    row = pl.BlockSpec((rb, d), lambda i: (i, 0))
    vec = pl.BlockSpec((1, d), lambda i: (0, 0))
    two = pl.BlockSpec((2, d), lambda i: (0, 0))
    ins = [x, g.reshape(1, d)] + ([sh, sc] if has_mod else []) + [dy]
    return pl.pallas_call(
        body, name='norm_bwd', grid=(n // rb,), in_specs=[row, vec] + ([two, two] if has_mod else []) + [row],
        out_specs=[row, pl.BlockSpec((SUBLANES, d), lambda i: (0, 0))],
        out_shape=[jax.ShapeDtypeStruct((n, d), F32), jax.ShapeDtypeStruct((SUBLANES, d), F32)],
        compiler_params=_cparams(('arbitrary',)),
    )(*ins)


def _make_norm(rb, has_mod):
    if has_mod:
        @jax.custom_vjp
        def f(x, g, sh, sc):
            return _norm_fwd_call(x, g, sh, sc, rb)

        def fwd(x, g, sh, sc):
            return _norm_fwd_call(x, g, sh, sc, rb), (x, g, sh, sc)

        def bwd(res, dy):
            x, g, sh, sc = res
            dx, acc = _norm_bwd_call(x, g, sh, sc, dy, rb)
            return dx, acc[0], acc[1:3], acc[3:5]
    else:
        @jax.custom_vjp
        def f(x, g):
            return _norm_fwd_call(x, g, None, None, rb)

        def fwd(x, g):
            return _norm_fwd_call(x, g, None, None, rb), (x, g)

        def bwd(res, dy):
            x, g = res
            dx, acc = _norm_bwd_call(x, g, None, None, dy, rb)
            return dx, acc[0]
    f.defvjp(fwd, bwd)
    return f


def _make_gated_res(rb, coef):
    def fwd_call(x, f, gate):
        n, d = x.shape

        def body(x_ref, f_ref, g_ref, o_ref):
            gv = jnp.where(pl.program_id(0) > 0, g_ref[1:2, :], g_ref[0:1, :])
            o_ref[...] = x_ref[...] + coef * gv * f_ref[...]

        row = pl.BlockSpec((rb, d), lambda i: (i, 0))
        return pl.pallas_call(
            body, name='gated_res_fwd', grid=(n // rb,), in_specs=[row, row, pl.BlockSpec((2, d), lambda i: (0, 0))],
            out_specs=row, out_shape=jax.ShapeDtypeStruct((n, d), F32), compiler_params=_cparams(('parallel',)),
        )(x, f, gate)

    def bwd_call(dy, f, gate):
        n, d = dy.shape

        def body(dy_ref, f_ref, g_ref, df_ref, acc_ref):
            i = pl.program_id(0)
            gv = jnp.where(i > 0, g_ref[1:2, :], g_ref[0:1, :])
            dyv = dy_ref[...]
            df_ref[...] = coef * gv * dyv
            part = coef * jnp.sum(dyv * f_ref[...], axis=0, keepdims=True)

            @pl.when(i == 0)
            def _():
                acc_ref[...] = jnp.zeros_like(acc_ref)
                acc_ref[0:1, :] += part

            @pl.when(i > 0)
            def _():
                acc_ref[1:2, :] += part

        row = pl.BlockSpec((rb, d), lambda i: (i, 0))
        return pl.pallas_call(
            body, name='gated_res_bwd', grid=(n // rb,), in_specs=[row, row, pl.BlockSpec((2, d), lambda i: (0, 0))],
            out_specs=[row, pl.BlockSpec((SUBLANES, d), lambda i: (0, 0))],
            out_shape=[jax.ShapeDtypeStruct((n, d), F32), jax.ShapeDtypeStruct((SUBLANES, d), F32)],
            compiler_params=_cparams(('arbitrary',)),
        )(dy, f, gate)

    @jax.custom_vjp
    def f(x, fv, gate):
        return fwd_call(x, fv, gate)

    def fwd(x, fv, gate):
        return fwd_call(x, fv, gate), (fv, gate)

    def bwd(res, dy):
        fv, gate = res
        df, acc = bwd_call(dy, fv, gate)
        return dy, df, acc[0:2]

    f.defvjp(fwd, bwd)
    return f


MLA_SCALE = (MLA_NOPE + MLA_ROPE) ** -0.5
LOG2E = math.log2(math.e)
MLA_SCALE_LOG2E = MLA_SCALE * LOG2E
NT_DIMS = (((1,), (1,)), ((), ()))


def _mla_keys(kv, kr):
    lane = lax.broadcasted_iota(jnp.int32, kv.shape, 1)
    return jnp.where(lane < MLA_NOPE, kv, kr)


def _chunks(start, stop, target):
    size = _tile(stop - start, target, LANES)
    return [(start + t * size, size) for t in range((stop - start) // size)]


MLA_CHUNK = 2816


def _mla_fwd_call(q, kv, kr, c_len):
    n = q.shape[0]
    h = q.shape[1] // LANES
    tq = c_len

    def body(q_ref, kv_ref, kr_ref, o_ref, lse_ref, kb, vb):
        i = pl.program_id(1)

        @pl.when(i == 0)
        def _():
            kvv = kv_ref[...]
            kb[...] = _mla_keys(kvv, kr_ref[...])
            vb[...] = jnp.where(lax.broadcasted_iota(jnp.int32, kvv.shape, 1) < MLA_NOPE, jnp.ones_like(kvv), kvv)

        qv = q_ref[...].astype(BF16)

        def attend(nk):
            s = lax.dot_general(qv, kb[:nk, :], NT_DIMS, preferred_element_type=F32)
            m = jnp.max(s, axis=-1, keepdims=True)
            p = jnp.exp2((s - m) * MLA_SCALE_LOG2E)
            acc = jnp.dot(p.astype(BF16), vb[:nk, :], preferred_element_type=F32)
            l = acc[:, 0:1]
            o_ref[...] = acc / l
            lse_ref[0] = m * MLA_SCALE + jnp.log(l)

        pl.when(i == 0)(lambda: attend(c_len))
        pl.when(i > 0)(lambda: attend(n))

    qspec = pl.BlockSpec((tq, LANES), lambda a, i: (i, a))
    return pl.pallas_call(
        body, name='mla_attn_fwd', grid=(h, n // tq),
        in_specs=[qspec, pl.BlockSpec((n, LANES), lambda a, i: (0, a)), pl.BlockSpec((n, LANES), lambda a, i: (0, 0))],
        out_specs=[qspec, pl.BlockSpec((1, tq, 1), lambda a, i: (a, i, 0))],
        out_shape=[jax.ShapeDtypeStruct((n, h * LANES), F32), jax.ShapeDtypeStruct((h, n, 1), F32)],
        scratch_shapes=[pltpu.VMEM((n, LANES), BF16), pltpu.VMEM((n, LANES), BF16)],
        compiler_params=_cparams(('arbitrary', 'arbitrary')),
    )(q, kv, kr)


def _mla_dq_call(q, kv, kr, o, lse, do, c_len):
    n = q.shape[0]
    h = q.shape[1] // LANES
    tq = c_len

    def body(q_ref, kv_ref, kr_ref, o_ref, lse_ref, do_ref, dq_ref, delta_ref, kb):
        i = pl.program_id(1)

        @pl.when(i == 0)
        def _():
            kb[...] = _mla_keys(kv_ref[...], kr_ref[...])

        dov = do_ref[...]
        delta = jnp.sum(dov * o_ref[...], axis=-1, keepdims=True)
        delta_ref[0] = delta
        qv, dob, lse2 = q_ref[...], dov.astype(BF16), lse_ref[0] * LOG2E

        def grad(chunks):
            acc = None
            for k0, kc in chunks:
                keys = kb[k0:k0 + kc, :]
                s = lax.dot_general(qv, keys, NT_DIMS, preferred_element_type=F32)
                p = jnp.exp2(s * MLA_SCALE_LOG2E - lse2)
                dp = lax.dot_general(dob, kv_ref[k0:k0 + kc, :], NT_DIMS, preferred_element_type=F32)
                ds = p * (dp - delta)
                part = jnp.dot(ds.astype(BF16), keys, preferred_element_type=F32)
                acc = part if acc is None else acc + part
            dq_ref[...] = acc * MLA_SCALE

        pl.when(i == 0)(lambda: grad([(0, c_len)]))
        pl.when(i > 0)(lambda: grad(_chunks(0, n, MLA_CHUNK)))

    qspec = pl.BlockSpec((tq, LANES), lambda a, i: (i, a))
    col = pl.BlockSpec((1, tq, 1), lambda a, i: (a, i, 0))
    return pl.pallas_call(
        body, name='mla_attn_dq', grid=(h, n // tq),
        in_specs=[qspec, pl.BlockSpec((n, LANES), lambda a, i: (0, a)), pl.BlockSpec((n, LANES), lambda a, i: (0, 0)),
                  qspec, col, qspec],
        out_specs=[qspec, col],
        out_shape=[jax.ShapeDtypeStruct((n, h * LANES), F32), jax.ShapeDtypeStruct((h, n, 1), F32)],
        scratch_shapes=[pltpu.VMEM((n, LANES), BF16)],
        compiler_params=_cparams(('arbitrary', 'arbitrary')),
    )(q, kv, kr, o, lse, do)


def _mla_dkv_call(q, kv, kr, lse_row, delta_row, do, c_len):
    n = q.shape[0]
    h = q.shape[1] // LANES
    tk = c_len

    def body(q_ref, kv_ref, kr_ref, lse_ref, delta_ref, do_ref, dkv_ref, dkf_ref):
        j = pl.program_id(1)
        vblk = kv_ref[...]
        kblk = _mla_keys(vblk, kr_ref[...])
        lane = lax.broadcasted_iota(jnp.int32, (tk, LANES), 1)

        def grad(chunks):
            dk = dv = None
            for q0, qc in chunks:
                qs, dos = q_ref[q0:q0 + qc, :], do_ref[q0:q0 + qc, :]
                st = lax.dot_general(kblk, qs, NT_DIMS, preferred_element_type=F32)
                pt = jnp.exp2(st * MLA_SCALE_LOG2E - lse_ref[0, :, q0:q0 + qc])
                dv_part = jnp.dot(pt.astype(BF16), dos, preferred_element_type=F32)
                dpt = lax.dot_general(vblk, dos, NT_DIMS, preferred_element_type=F32)
                dst = pt * (dpt - delta_ref[0, :, q0:q0 + qc])
                dk_part = jnp.dot(dst.astype(BF16), qs, preferred_element_type=F32)
                dk = dk_part if dk is None else dk + dk_part
                dv = dv_part if dv is None else dv + dv_part
            dk = dk * MLA_SCALE
            dkv_ref[...] = jnp.where(lane < MLA_NOPE, dk, dv)
            dkf_ref[0] = dk

        latent = _chunks(c_len, n, MLA_CHUNK)
        pl.when(j == 0)(lambda: grad(latent + [(0, c_len)]))
        pl.when(j > 0)(lambda: grad(latent))

    full = pl.BlockSpec((n, LANES), lambda a, j: (0, a))
    row = pl.BlockSpec((1, 1, n), lambda a, j: (a, 0, 0))
    kspec = pl.BlockSpec((tk, LANES), lambda a, j: (j, a))
    return pl.pallas_call(
        body, name='mla_attn_dkv', grid=(h, n // tk),
        in_specs=[full, kspec, pl.BlockSpec((tk, LANES), lambda a, j: (j, 0)), row, row, full],
        out_specs=[kspec, pl.BlockSpec((1, tk, LANES), lambda a, j: (a, j, 0))],
        out_shape=[jax.ShapeDtypeStruct((n, h * LANES), F32), jax.ShapeDtypeStruct((h, n, LANES), F32)],
        compiler_params=_cparams(('arbitrary', 'arbitrary')),
    )(q, kv, kr, lse_row, delta_row, do)


def _sum_leading(g):
    nl, r, _ = g.shape
    rb = _tile(r, 512, SUBLANES)

    def body(g_ref, o_ref):
        acc = g_ref[0]
        for j in range(1, nl):
            acc = acc + g_ref[j]
        o_ref[...] = acc

    return pl.pallas_call(
        body, name='sum_leading', grid=(r // rb,), in_specs=[pl.BlockSpec((nl, rb, LANES), lambda i: (0, i, 0))],
        out_specs=pl.BlockSpec((rb, LANES), lambda i: (i, 0)), out_shape=jax.ShapeDtypeStruct((r, LANES), F32),
        compiler_params=_cparams(('parallel',)),
    )(g)


def _make_mla(c_len):
    @jax.custom_vjp
    def f(q, kv, kr):
        return _mla_fwd_call(q, kv.astype(BF16), kr.astype(BF16), c_len)[0]

    def fwd(q, kv, kr):
        kvb, krb = kv.astype(BF16), kr.astype(BF16)
        o, lse = _mla_fwd_call(q, kvb, krb, c_len)
        return o, (q.astype(BF16), kvb, krb, o, lse)

    def bwd(res, do):
        q, kv, kr, o, lse = res
        n = q.shape[0]
        h = q.shape[1] // LANES
        dq, delta = _mla_dq_call(q, kv, kr, o, lse, do, c_len)
        dkv, dk_full = _mla_dkv_call(q, kv, kr, (lse * LOG2E).reshape(h, 1, n), delta.reshape(h, 1, n), do.astype(BF16), c_len)
        return dq, dkv, _sum_leading(dk_full)

    f.defvjp(fwd, bwd)
    return f


def _gqa_specs(band, c_len, nb, rows):
    cb = c_len // BLOCK
    q_spec = pl.BlockSpec((1, GQA_G, rows, GQA_D), lambda a, b: (a, 0, b, 0))
    ctx_spec = pl.BlockSpec((1, c_len, GQA_D), lambda a, b: (a, 0, 0))
    kv_specs = [ctx_spec]
    if band:
        kv_specs += [pl.BlockSpec((1, BLOCK, GQA_D), lambda a, b: (a, jnp.maximum(b - 1, 0) + cb, 0)),
                     pl.BlockSpec((1, BLOCK, GQA_D), lambda a, b: (a, b + cb, 0)),
                     pl.BlockSpec((1, BLOCK, GQA_D), lambda a, b: (a, jnp.minimum(b + 1, nb - 1) + cb, 0))]
    sink_spec = pl.BlockSpec((1, GQA_G * rows, 1), lambda a, b: (a, 0, 0))
    return q_spec, kv_specs, sink_spec


def _gqa_scores(q, kcat, sink, band, c_len, t_len, rows):
    scale = GQA_D ** -0.5
    s = lax.dot_general(q, kcat, (((1,), (1,)), ((), ())), preferred_element_type=F32) * scale
    if band:
        b = pl.program_id(1)
        shape = s.shape
        col = lax.broadcasted_iota(jnp.int32, shape, 1)
        qpos = b * BLOCK + (lax.broadcasted_iota(jnp.int32, shape, 0) & (BLOCK - 1))
        kpos = (b - 1) * BLOCK + (col - c_len)
        valid = (col < c_len) | ((jnp.abs(qpos - kpos) <= WINDOW) & (kpos >= 0) & (kpos < t_len))
        s = jnp.where(valid, s, NEG_INF)
    m = jnp.maximum(jnp.max(s, axis=-1, keepdims=True), sink)
    e = jnp.exp(s - m)
    es = jnp.exp(sink - m)
    den = es + jnp.sum(e, axis=-1, keepdims=True)
    return e / den, es / den


def _gqa_fwd_call(q4, k2, v2, sink_rows, band, c_len):
    kv, g, tq_all, d = q4.shape
    rows = BLOCK if band else tq_all
    nb = tq_all // rows
    t_len = k2.shape[1] - c_len
    nkv = 4 if band else 1
    q_spec, kv_specs, sink_spec = _gqa_specs(band, c_len, nb, rows)

    def body(*refs):
        q_ref, sink_ref, o_ref = refs[0], refs[1 + 2 * nkv], refs[-1]
        kcat = jnp.concatenate([r[0] for r in refs[1:1 + nkv]], axis=0).astype(BF16)
        vcat = jnp.concatenate([r[0] for r in refs[1 + nkv:1 + 2 * nkv]], axis=0).astype(BF16)
        q = q_ref[0].reshape(g * rows, d).astype(BF16)
        p, _ = _gqa_scores(q, kcat, sink_ref[0], band, c_len, t_len, rows)
        o_ref[0] = jnp.dot(p.astype(BF16), vcat, preferred_element_type=F32).reshape(g, rows, d)

    return pl.pallas_call(
        body, name='gqa_fwd_band' if band else 'gqa_fwd_ctx', grid=(kv, nb),
        in_specs=[q_spec] + kv_specs + kv_specs + [sink_spec], out_specs=q_spec,
        out_shape=jax.ShapeDtypeStruct(q4.shape, F32), compiler_params=_cparams(('parallel', 'parallel')),
    )(q4, *([k2] * nkv), *([v2] * nkv), sink_rows)


def _gqa_bwd_call(q4, k2, v2, sink_rows, do4, band, c_len):
    kv, g, tq_all, d = q4.shape
    rows = BLOCK if band else tq_all
    nb = tq_all // rows
    t_len = k2.shape[1] - c_len
    nkv = 4 if band else 1
    scale = GQA_D ** -0.5
    q_spec, kv_specs, sink_spec = _gqa_specs(band, c_len, nb, rows)

    def body(*refs):
        q_ref, sink_ref, do_ref = refs[0], refs[1 + 2 * nkv], refs[2 + 2 * nkv]
        outs = refs[3 + 2 * nkv:]
        dq_ref, dkc_ref, dvc_ref = outs[0], outs[1], outs[2]
        dsink_ref = outs[-1]
        b = pl.program_id(1)
        kcat = jnp.concatenate([r[0] for r in refs[1:1 + nkv]], axis=0).astype(BF16)
        vcat = jnp.concatenate([r[0] for r in refs[1 + nkv:1 + 2 * nkv]], axis=0).astype(BF16)
        q = q_ref[0].reshape(g * rows, d).astype(BF16)
        do = do_ref[0].reshape(g * rows, d).astype(BF16)
        p, p_sink = _gqa_scores(q, kcat, sink_ref[0], band, c_len, t_len, rows)
        dp = lax.dot_general(do, vcat, (((1,), (1,)), ((), ())), preferred_element_type=F32)
        rd = jnp.sum(p * dp, axis=-1, keepdims=True)
        ds = (p * (dp - rd) * scale).astype(BF16)
        dq_ref[0] = jnp.dot(ds, kcat, preferred_element_type=F32).reshape(g, rows, d)
        dkcat = lax.dot_general(ds, q, (((0,), (0,)), ((), ())), preferred_element_type=F32)
        dvcat = lax.dot_general(p.astype(BF16), do, (((0,), (0,)), ((), ())), preferred_element_type=F32)

        @pl.when(b == 0)
        def _():
            dkc_ref[...] = jnp.zeros_like(dkc_ref)
            dvc_ref[...] = jnp.zeros_like(dvc_ref)
            dsink_ref[...] = jnp.zeros_like(dsink_ref)

        dkc_ref[0] += dkcat[:c_len]
        dvc_ref[0] += dvcat[:c_len]
        dsink_ref[0] += -p_sink * rd
        if band:
            outs[3][0, 0] = dkcat[c_len:]
            outs[4][0, 0] = dvcat[c_len:]

    ctx_out = pl.BlockSpec((1, c_len, d), lambda a, b: (a, 0, 0))
    band_out = pl.BlockSpec((1, 1, 3 * BLOCK, d), lambda a, b: (a, b, 0, 0))
    out_specs = [q_spec, ctx_out, ctx_out] + ([band_out, band_out] if band else []) + [sink_spec]
    ctx_shape = jax.ShapeDtypeStruct((kv, c_len, d), F32)
    band_shape = jax.ShapeDtypeStruct((kv, nb, 3 * BLOCK, d), F32)
    out_shape = ([jax.ShapeDtypeStruct(q4.shape, F32), ctx_shape, ctx_shape] + ([band_shape, band_shape] if band else [])
                 + [jax.ShapeDtypeStruct(sink_rows.shape, F32)])
    return pl.pallas_call(
        body, name='gqa_bwd_band' if band else 'gqa_bwd_ctx', grid=(kv, nb),
        in_specs=[q_spec] + kv_specs + kv_specs + [sink_spec, q_spec], out_specs=out_specs, out_shape=out_shape,
        compiler_params=_cparams(('arbitrary', 'arbitrary')),
    )(q4, *([k2] * nkv), *([v2] * nkv), sink_rows, do4)


def _make_gqa(band, c_len):
    @jax.custom_vjp
    def f(q4, k2, v2, sink_rows):
        return _gqa_fwd_call(q4, k2, v2, sink_rows, band, c_len)

    def fwd(q4, k2, v2, sink_rows):
        return _gqa_fwd_call(q4, k2, v2, sink_rows, band, c_len), (q4, k2, v2, sink_rows)

    def bwd(res, do4):
        q4, k2, v2, sink_rows = res
        outs = _gqa_bwd_call(q4, k2, v2, sink_rows, do4, band, c_len)
        kv, n, d = k2.shape
        if not band:
            dq4, dkc, dvc, dsink = outs
            return dq4, dkc, dvc, dsink
        dq4, dkc, dvc, dkb, dvb, dsink = outs

        def fold(ctx_part, bands):
            cur = bands[:, :, BLOCK:2 * BLOCK]
            prv = jnp.pad(bands[:, 1:, :BLOCK], ((0, 0), (0, 1), (0, 0), (0, 0)))
            nxt = jnp.pad(bands[:, :-1, 2 * BLOCK:], ((0, 0), (1, 0), (0, 0), (0, 0)))
            lat = (cur + prv + nxt).reshape(kv, n - c_len, d)
            return jnp.concatenate([ctx_part, lat], axis=1)

        return dq4, fold(dkc, dkb), fold(dvc, dvb), dsink

    f.defvjp(fwd, bwd)
    return f


def _cmul(ar, ai, br, bi):
    return ar * br - ai * bi, ar * bi + ai * br


def _scan_tables(ar, ai, desc):
    a1 = (ar, ai)
    a2 = _cmul(*a1, *a1)
    a4 = _cmul(*a2, *a2)
    pw = [a1]
    for _ in range(SUBLANES - 1):
        pw.append(_cmul(*pw[-1], *a1))
    row = jnp.arange(SUBLANES)[:, None]
    tabs = []
    for dist, (pr, pi) in ((1, a1), (2, a2), (4, a4)):
        keep = (row <= SUBLANES - 1 - dist) if desc else (row >= dist)
        tabs += [jnp.where(keep, pr[None, :], 0.0), jnp.where(keep, pi[None, :], 0.0)]
    order = pw[::-1] if desc else pw
    tabs += [jnp.stack([p[0] for p in order]), jnp.stack([p[1] for p in order])]
    return jnp.stack(tabs).astype(F32)


def _scan_call(b_re, b_im, tabs, order, chunk, prev=None):
    n, s_dim = b_re.shape
    nch = n // chunk
    ng = chunk // SUBLANES
    desc = order in ('Fb', 'R')
    with_da = prev is not None

    def chunk_of(i):
        if order == 'F':
            return i
        if order == 'Fb':
            return nch - 1 - i
        if order == 'R':
            return jnp.where(i == 0, 0, nch - i)
        return jnp.where(i == nch - 1, 0, i + 1)

    def body(*refs):
        br_ref, bi_ref, tab_ref = refs[0], refs[1], refs[2]
        if with_da:
            pr_ref, pi_ref, sr_ref, si_ref, dar_ref, dai_ref, cr_ref, ci_ref = refs[3:]
        else:
            sr_ref, si_ref, pr_ref, pi_ref, cr_ref, ci_ref = refs[3:]

        @pl.when(pl.program_id(0) == 0)
        def _():
            cr_ref[...] = jnp.zeros_like(cr_ref)
            ci_ref[...] = jnp.zeros_like(ci_ref)
            if with_da:
                dar_ref[...] = jnp.zeros_like(dar_ref)
                dai_ref[...] = jnp.zeros_like(dai_ref)

        sub = lax.broadcasted_iota(jnp.int32, (SUBLANES, s_dim), 0)
        edge = SUBLANES - 1 if desc else 0
        last = 0 if desc else SUBLANES - 1

        def step(t, carry):
            gi = (ng - 1 - t) if desc else t
            rows = pl.ds(pl.multiple_of(gi * SUBLANES, SUBLANES), SUBLANES)
            xr, xi = br_ref[rows, :], bi_ref[rows, :]
            for j, dist in enumerate((1, 2, 4)):
                shift = SUBLANES - dist if desc else dist
                rr, ri = pltpu.roll(xr, shift, 0), pltpu.roll(xi, shift, 0)
                mr, mi = tab_ref[2 * j], tab_ref[2 * j + 1]
                xr, xi = xr + mr * rr - mi * ri, xi + mr * ri + mi * rr
            cr, ci = cr_ref[...], ci_ref[...]
            pwr, pwi = tab_ref[6], tab_ref[7]
            sr = xr + pwr * cr - pwi * ci
            si = xi + pwr * ci + pwi * cr
            sr_ref[rows, :] = sr
            si_ref[rows, :] = si
            if with_da:
                pr, pi = pr_ref[rows, :], pi_ref[rows, :]
                dar_ref[...] += sr * pr + si * pi
                dai_ref[...] += si * pr - sr * pi
            else:
                shift1 = SUBLANES - 1 if desc else 1
                pr_ref[rows, :] = jnp.where(sub == edge, cr, pltpu.roll(sr, shift1, 0))
                pi_ref[rows, :] = jnp.where(sub == edge, ci, pltpu.roll(si, shift1, 0))
            cr_ref[...] = jnp.broadcast_to(sr[last:last + 1, :], (SUBLANES, s_dim))
            ci_ref[...] = jnp.broadcast_to(si[last:last + 1, :], (SUBLANES, s_dim))
            return carry

        lax.fori_loop(0, ng, step, 0)

    blk = pl.BlockSpec((chunk, s_dim), lambda i: (chunk_of(i), 0))
    tab_spec = pl.BlockSpec((8, SUBLANES, s_dim), lambda i: (0, 0, 0))
    acc = pl.BlockSpec((SUBLANES, s_dim), lambda i: (0, 0))
    big = jax.ShapeDtypeStruct((n, s_dim), F32)
    small = jax.ShapeDtypeStruct((SUBLANES, s_dim), F32)
    if with_da:
        in_specs, ins = [blk, blk, tab_spec, blk, blk], [b_re, b_im, tabs, prev[0], prev[1]]
        out_specs, out_shape = [blk, blk, acc, acc], [big, big, small, small]
    else:
        in_specs, ins = [blk, blk, tab_spec], [b_re, b_im, tabs]
        out_specs, out_shape = [blk, blk, blk, blk], [big, big, big, big]
    return pl.pallas_call(
        body, name='s5_scan_' + order, grid=(nch,), in_specs=in_specs, out_specs=out_specs, out_shape=out_shape,
        scratch_shapes=[pltpu.VMEM((SUBLANES, s_dim), F32), pltpu.VMEM((SUBLANES, s_dim), F32)],
        compiler_params=_cparams(('arbitrary',)),
    )(*ins)


def _make_scan(rev, chunk):
    def run(b_re, b_im, ar, ai):
        tabs = _scan_tables(ar, ai, desc=rev)
        return _scan_call(b_re, b_im, tabs, 'R' if rev else 'F', chunk)

    @jax.custom_vjp
    def f(b_re, b_im, ar, ai):
        return tuple(run(b_re, b_im, ar, ai)[:2])

    def fwd(b_re, b_im, ar, ai):
        s_re, s_im, p_re, p_im = run(b_re, b_im, ar, ai)
        return (s_re, s_im), (p_re, p_im, ar, ai)

    def bwd(res, g):
        p_re, p_im, ar, ai = res
        tabs = _scan_tables(ar, -ai, desc=not rev)
        db_re, db_im, dar, dai = _scan_call(g[0], g[1], tabs, 'Rb' if rev else 'Fb', chunk, prev=(p_re, p_im))
        return db_re, db_im, jnp.sum(dar, axis=0), jnp.sum(dai, axis=0)

    f.defvjp(fwd, bwd)
    return f


def _sqerr_call(y, t):
    n, d = y.shape
    rb = _tile(n, 512, SUBLANES)

    def body(y_ref, t_ref, o_ref):
        @pl.when(pl.program_id(0) == 0)
        def _():
            o_ref[...] = jnp.zeros_like(o_ref)

        e = y_ref[...] - t_ref[...]
        o_ref[...] += jnp.sum(e * e, axis=0, keepdims=True)

    row = pl.BlockSpec((rb, d), lambda i: (i, 0))
    return pl.pallas_call(
        body, name='sq_err', grid=(n // rb,), in_specs=[row, row], out_specs=pl.BlockSpec((1, d), lambda i: (0, 0)),
        out_shape=jax.ShapeDtypeStruct((1, d), F32), compiler_params=_cparams(('arbitrary',)),
    )(y, t)


@jax.custom_vjp
def loss_head(y, t):
    return 0.5 * jnp.sum(_sqerr_call(y, t)) / y.shape[1]


def _loss_head_fwd(y, t):
    return loss_head(y, t), (y, t)


def _loss_head_bwd(res, g):
    y, t = res
    return g * (y - t) / y.shape[1], None


loss_head.defvjp(_loss_head_fwd, _loss_head_bwd)


def _adamw_call(w, g, m, v):
    r, c = w.shape
    rb = _tile(r, max(SUBLANES, (256 * 1024) // max(c, LANES) // SUBLANES * SUBLANES), SUBLANES)

    def body(w_ref, g_ref, m_ref, v_ref, d_ref, nm_ref, nv_ref):
        gv = g_ref[...]
        nm = ADAM_B1 * m_ref[...] + (1.0 - ADAM_B1) * gv
        nv = ADAM_B2 * v_ref[...] + (1.0 - ADAM_B2) * (gv * gv)
        m_hat = nm / (1.0 - ADAM_B1 ** ADAM_STEP)
        v_hat = nv / (1.0 - ADAM_B2 ** ADAM_STEP)
        d_ref[...] = -ADAM_LR * (m_hat / (jnp.sqrt(v_hat) + ADAM_EPS) + ADAM_WD * w_ref[...])
        nm_ref[...] = nm
        nv_ref[...] = nv

    blk = pl.BlockSpec((rb, c), lambda i: (i, 0))
    shape = jax.ShapeDtypeStruct((r, c), F32)
    return pl.pallas_call(
        body, name='adamw', grid=(r // rb,), in_specs=[blk] * 4, out_specs=[blk] * 3, out_shape=[shape] * 3,
        compiler_params=_cparams(('parallel',)),
    )(w, g, m, v)


MESH = pl.DeviceIdType.MESH
HBM_SPEC = pl.BlockSpec(memory_space=pltpu.HBM)


def _all_gather(x):
    def body(x_ref, out_ref, send_sems, recv_sems, local_sem):
        x, y, c = lax.axis_index('x'), lax.axis_index('y'), lax.axis_index('c')
        me, sibling = (x, y, c), (x, y, 1 - c)
        chips = [(1 - x, y), (x, 1 - y), (1 - x, 1 - y)]

        def slot(px, py, pc):
            return out_ref.at[4 * px + 2 * py + pc]

        def copy(k, block, to, src=None):
            return pltpu.make_async_remote_copy(
                src_ref=slot(*block) if src is None else src, dst_ref=slot(*block),
                send_sem=send_sems.at[k], recv_sem=recv_sems.at[k], device_id=to, device_id_type=MESH)

        mine = pltpu.make_async_copy(x_ref, slot(*me), local_sem)
        mine.start()
        first = [copy(0, me, sibling, src=x_ref)]
        first += [copy(1 + j, me, (*chip, c), src=x_ref) for j, chip in enumerate(chips)]
        for cp in first:
            cp.start()
        passed = [copy(4 + j, (*chip, c), sibling) for j, chip in enumerate(chips)]
        for j, chip in enumerate(chips):
            copy(1 + j, (*chip, c), me).wait_recv()
            passed[j].start()
        copy(0, sibling, me).wait_recv()
        for j, chip in enumerate(chips):
            copy(4 + j, (*chip, 1 - c), me).wait_recv()
        for cp in first + passed:
            cp.wait_send()
        mine.wait()

    return pl.pallas_call(
        body, name='all_gather', out_shape=jax.ShapeDtypeStruct((N_DEV,) + x.shape, x.dtype),
        in_specs=[HBM_SPEC], out_specs=HBM_SPEC,
        scratch_shapes=[pltpu.SemaphoreType.DMA((7,)), pltpu.SemaphoreType.DMA((7,)), pltpu.SemaphoreType.DMA],
    )(x)


def _exchange_sibling(g_all):
    def body(g_ref, out_ref, send_sem, recv_sem):
        x, y, c = lax.axis_index('x'), lax.axis_index('y'), lax.axis_index('c')
        cp = pltpu.make_async_remote_copy(src_ref=g_ref.at[1 - c], dst_ref=out_ref, send_sem=send_sem, recv_sem=recv_sem,
                                          device_id=(x, y, 1 - c), device_id_type=MESH)
        cp.start()
        cp.wait()

    return pl.pallas_call(
        body, name='rs_sibling', out_shape=jax.ShapeDtypeStruct(g_all.shape[1:], g_all.dtype),
        in_specs=[HBM_SPEC], out_specs=HBM_SPEC,
        scratch_shapes=[pltpu.SemaphoreType.DMA, pltpu.SemaphoreType.DMA],
    )(g_all)


def _exchange_chips(p):
    def body(p_ref, out_ref, send_sems, recv_sems):
        x, y, c = lax.axis_index('x'), lax.axis_index('y'), lax.axis_index('c')
        chips = [(1 - x, y), (x, 1 - y), (1 - x, 1 - y)]
        copies = [pltpu.make_async_remote_copy(src_ref=p_ref.at[2 * px + py], dst_ref=out_ref.at[j],
                                               send_sem=send_sems.at[j], recv_sem=recv_sems.at[j],
                                               device_id=(px, py, c), device_id_type=MESH)
                  for j, (px, py) in enumerate(chips)]
        for cp in copies:
            cp.start()
        for cp in copies:
            cp.wait_recv()
        for cp in copies:
            cp.wait_send()

    return pl.pallas_call(
        body, name='rs_chips', out_shape=jax.ShapeDtypeStruct((3,) + p.shape[1:], p.dtype),
        in_specs=[HBM_SPEC], out_specs=HBM_SPEC,
        scratch_shapes=[pltpu.SemaphoreType.DMA((3,)), pltpu.SemaphoreType.DMA((3,))],
    )(p)


def _add_sibling(g_all, recv, c_idx):
    _, nchip, r, _ = g_all.shape
    rb = _tile(r, PACK_ROWS, SUBLANES)

    def body(c_ref, g_ref, r_ref, o_ref, ob_ref):
        s = g_ref[0] + r_ref[...]
        o_ref[...] = s
        ob_ref[...] = s.astype(BF16)

    blk = pl.BlockSpec((1, rb, LANES), lambda k, i, c: (k, i, 0))
    return pl.pallas_call(
        body, name='rs_add_sibling',
        grid_spec=pltpu.PrefetchScalarGridSpec(
            num_scalar_prefetch=1, grid=(nchip, r // rb),
            in_specs=[pl.BlockSpec((1, 1, rb, LANES), lambda k, i, c: (c[0], k, i, 0)), blk],
            out_specs=[blk, blk]),
        out_shape=[jax.ShapeDtypeStruct(recv.shape, F32), jax.ShapeDtypeStruct(recv.shape, BF16)],
        compiler_params=_cparams(('parallel', 'parallel')),
    )(c_idx, g_all, recv)


def _add_chips(p, recv, chip_idx):
    _, r, _ = p.shape
    rb = _tile(r, PACK_ROWS, SUBLANES)

    def body(k_ref, p_ref, r0, r1, r2, o_ref):
        o_ref[...] = ((p_ref[0] + r0[0].astype(F32)) + r1[0].astype(F32)) + r2[0].astype(F32)

    rspec = lambda j: pl.BlockSpec((1, rb, LANES), lambda i, k: (j, i, 0))
    return pl.pallas_call(
        body, name='rs_add_chips',
        grid_spec=pltpu.PrefetchScalarGridSpec(
            num_scalar_prefetch=1, grid=(r // rb,),
            in_specs=[pl.BlockSpec((1, rb, LANES), lambda i, k: (k[0], i, 0)), rspec(0), rspec(1), rspec(2)],
            out_specs=pl.BlockSpec((rb, LANES), lambda i, k: (i, 0))),
        out_shape=jax.ShapeDtypeStruct((r, LANES), F32), compiler_params=_cparams(('parallel',)),
    )(chip_idx, p, recv, recv, recv)


def _reduce_scatter(g_all, c_idx, chip_idx):
    part, part_bf16 = _add_sibling(g_all, _exchange_sibling(g_all), c_idx)
    return _add_chips(part, _exchange_chips(part_bf16), chip_idx)


def _pad_to(n, mult):
    return (n + mult - 1) // mult * mult


def _pack(pieces, lead, dtype):
    flat = []
    total = 0
    for p in pieces:
        f = p.reshape(lead + (-1,)).astype(dtype)
        n = _pad_to(f.shape[-1], 16 * LANES)
        flat.append(jnp.pad(f, [(0, 0)] * len(lead) + [(0, n - f.shape[-1])]))
        total += n
    full = _pad_to(total, PACK_ROWS * LANES)
    if full > total:
        flat.append(jnp.zeros(lead + (full - total,), dtype))
    return jnp.concatenate(flat, axis=-1).reshape(lead + (full // LANES, LANES))


def _unpack(buf, lead, shapes):
    flat = buf.reshape(lead + (-1,))
    out, off = [], 0
    for s in shapes:
        n = math.prod(s)
        out.append(flat[..., off:off + n].reshape(lead + tuple(s)))
        off += _pad_to(n, 16 * LANES)
    return out


def _make_split(sizes, width):
    starts = [sum(sizes[:i]) for i in range(len(sizes))]
    tail = width - sum(sizes)

    @jax.custom_vjp
    def f(z):
        return tuple(z[:, o:o + s] for o, s in zip(starts, sizes))

    def fwd(z):
        return f(z), None

    def bwd(_, gs):
        pieces = list(gs) + ([jnp.zeros((gs[0].shape[0], tail), gs[0].dtype)] if tail else [])
        return (jnp.concatenate(pieces, axis=1),)

    f.defvjp(fwd, bwd)
    return f


def _rope_tables(c_len, t_len, n):
    quarter = n // 4
    inv = ROPE_BASE ** (-jnp.arange(0, 2 * quarter, 2, dtype=F32) / (2 * quarter))
    t = jnp.arange(t_len, dtype=jnp.int32)
    pos = jnp.stack([(t // GRID_W).astype(F32), (t % GRID_W).astype(F32)], axis=1)
    ang = pos[:, :, None] * inv[None, None, :]
    ang = jnp.concatenate([jnp.zeros((c_len, 2, quarter), F32), ang], axis=0)
    return jnp.cos(ang), jnp.sin(ang)


def _axial_rope(x, cos, sin):
    n_rows, h, n = x.shape
    xs = x.reshape(n_rows, h, 2, 2, n // 4)
    x1, x2 = xs[:, :, :, 0], xs[:, :, :, 1]
    c, s = cos[:, None], sin[:, None]
    return jnp.stack([x1 * c - x2 * s, x1 * s + x2 * c], axis=3).reshape(n_rows, h, n)


def _ssm_discretize(lam_re, lam_im, log_dt, b_re, b_im):
    dt = jnp.exp(log_dt)[:, None]
    mag = jnp.exp(lam_re * dt)
    a_re, a_im = mag * jnp.cos(lam_im * dt), mag * jnp.sin(lam_im * dt)
    den = lam_re * lam_re + lam_im * lam_im
    w_re = ((a_re - 1) * lam_re + a_im * lam_im) / den
    w_im = (a_im * lam_re - (a_re - 1) * lam_im) / den
    bb_re, bb_im = _cmul(w_re[..., None], w_im[..., None], b_re, b_im)
    return a_re, a_im, bb_re, bb_im


def _block_diag_in(b):
    g = b.shape[0]
    return jnp.einsum('gpm,gh->gmhp', b, jnp.eye(g, dtype=F32)).reshape(g * b.shape[2], g * b.shape[1])


def _block_diag_out(c):
    g = c.shape[0]
    return jnp.einsum('gmp,gh->gphm', c, jnp.eye(g, dtype=F32)).reshape(g * c.shape[2], g * c.shape[1])


def _w_in_layout(d_model):
    sizes = (MLA_Q_RANK, MLA_KV_RANK, MLA_ROPE, SSM_WIDTH, GQA_HEADS * GQA_D, GQA_KV * GQA_D, GQA_KV * GQA_D, 3 * d_model)
    starts = [0]
    for s in sizes[:-1]:
        starts.append(starts[-1] + s)
    names = ('cq', 'ckv', 'kr', 'u', 'gq', 'gk', 'gv', 'gates')
    orig = dict(zip(names, zip(starts, sizes)))
    order = ('cq', 'ckv', 'u', 'gq', 'gk', 'gv', 'gates', 'kr')
    return orig, order


def _permute_w_in(blocks, d_model):
    orig, order = _w_in_layout(d_model)
    w = blocks.reshape(-1, blocks.shape[2])
    rows = [w[orig[k][0]:orig[k][0] + orig[k][1]] for k in order]
    width = sum(orig[k][1] for k in order)
    return jnp.pad(jnp.concatenate(rows, axis=0), ((0, _pad_to(width, LANES) - width), (0, 0)))


def _unpermute_w_in(gp, d_model, r):
    orig, order = _w_in_layout(d_model)
    pos, off = {}, 0
    for k in order:
        pos[k] = off
        off += orig[k][1]
    names = sorted(orig, key=lambda k: orig[k][0])
    w = jnp.concatenate([gp[pos[k]:pos[k] + orig[k][1]] for k in names], axis=0)
    return w.reshape(N_DEV, r, gp.shape[1])


def _forward_loss(x_all, sinks, rp, wl, cc_in, target, c_len):
    n, d = x_all.shape
    t_len = n - c_len
    depth = len(wl)
    mla_attn = _make_mla(c_len)
    norm_mod = _make_norm(c_len, True)
    norm_tok = _make_norm(_tile(n, 512, SUBLANES), False)
    norm_out = _make_norm(_tile(t_len, 512, SUBLANES), False)
    half_res = _make_gated_res(c_len, 0.5)
    full_res = _make_gated_res(c_len, 1.0)
    gqa_band = _make_gqa(True, c_len)
    gqa_ctx = _make_gqa(False, c_len)
    scans = (_make_scan(False, c_len), _make_scan(True, c_len))
    cos_m, sin_m = _rope_tables(c_len, t_len, MLA_ROPE)
    cos_g, sin_g = _rope_tables(c_len, t_len, GQA_D)
    orig, order = _w_in_layout(d)
    split_in = _make_split([orig[k][1] for k in order], _pad_to(sum(orig[k][1] for k in order), LANES))
    halves = lambda a: _make_split([a.shape[1] // 2] * 2, a.shape[1])(a)

    cc = jnp.zeros((SUBLANES, d), F32).at[0].set(jax.nn.silu(rp['c_ctx'])).at[1].set(jax.nn.silu(cc_in))

    for l in range(depth):
        w, sk = wl[l], sinks[l]
        ctx_out = l < depth - 1

        def mm(h, name):
            return (matmul if name in ROW_SHARDED else matmul_t)(h, w[name], sk[name])

        def swiglu(h, name13, name2):
            a, b = halves(mm(h, name13))
            return mm(jax.nn.silu(a) * b, name2)

        mod = mm(cc, 'ada_w') + rp['ada_b'][l][None, :]
        md = [mod[0:2, i * d:(i + 1) * d] for i in range(N_MOD)]
        x_all = half_res(x_all, swiglu(norm_mod(x_all, rp['norm_ffn1'][l], md[0], md[1]), 'ffn1_w13', 'ffn1_w2'), md[2])

        z = mm(norm_mod(x_all, rp['norm_mix'][l], md[3], md[4]), 'w_in')
        part = dict(zip(order, split_in(z)))

        q3 = mm(norm_tok(part['cq'], rp['mla_q_norm'][l]), 'mla_w_uq').reshape(n, MLA_HEADS, LANES)
        q = jnp.concatenate([q3[..., :MLA_NOPE], _axial_rope(q3[..., MLA_NOPE:MLA_NOPE + MLA_ROPE], cos_m, sin_m),
                             q3[..., MLA_NOPE + MLA_ROPE:]], axis=-1).reshape(n, MLA_HEADS * LANES)
        kvp = mm(norm_tok(part['ckv'], rp['mla_kv_norm'][l]), 'mla_w_ukv')
        kr = _axial_rope(part['kr'].reshape(n, 1, MLA_ROPE), cos_m, sin_m).reshape(n, MLA_ROPE)
        kr = jnp.pad(kr, ((0, 0), (MLA_NOPE, LANES - MLA_NOPE - MLA_ROPE)))
        mla = mm(mla_attn(q, kvp, kr), 'mla_w_o')

        u = part['u']
        y = u * rp['ssm_d'][l][None, :]
        for direction in range(2):
            a_re, a_im, bb_re, bb_im = _ssm_discretize(
                rp['ssm_lambda_re'][l, direction], rp['ssm_lambda_im'][l, direction], rp['ssm_log_dt'][l, direction],
                rp['ssm_b_re'][l, direction], rp['ssm_b_im'][l, direction])
            s_re, s_im = scans[direction](matmul_d(u, _block_diag_in(bb_re)), matmul_d(u, _block_diag_in(bb_im)),
                                          a_re.reshape(-1), a_im.reshape(-1))
            y = y + (matmul_d(s_re, _block_diag_out(rp['ssm_c_re'][l, direction]))
                     - matmul_d(s_im, _block_diag_out(rp['ssm_c_im'][l, direction])))
        yg = mm(jax.nn.gelu(y), 'ssm_w_glu')
        ya, ygate = halves(yg)
        ssm = ya * jax.nn.sigmoid(ygate)

        gq = _axial_rope(part['gq'].reshape(n, GQA_HEADS, GQA_D), cos_g, sin_g)
        gk = _axial_rope(part['gk'].reshape(n, GQA_KV, GQA_D), cos_g, sin_g)
        q4 = jnp.transpose(gq.reshape(n, GQA_KV, GQA_G, GQA_D), (1, 2, 0, 3))
        k2 = jnp.transpose(gk, (1, 0, 2))
        v2 = jnp.transpose(part['gv'].reshape(n, GQA_KV, GQA_D), (1, 0, 2))
        sink = rp['gqa_sink'][l].reshape(GQA_KV, GQA_G, 1, 1)
        sink_rows = lambda rows: jnp.broadcast_to(sink, (GQA_KV, GQA_G, rows, 1)).reshape(GQA_KV, GQA_G * rows, 1)
        g_lat = gqa_band(q4[:, :, c_len:], k2, v2, sink_rows(BLOCK))
        if ctx_out:
            g_ctx = gqa_ctx(q4[:, :, :c_len], k2[:, :c_len], v2[:, :c_len], sink_rows(c_len))
        else:
            g_ctx = jnp.zeros((GQA_KV, GQA_G, c_len, GQA_D), F32)
        go = jnp.transpose(jnp.concatenate([g_ctx, g_lat], axis=2), (2, 0, 1, 3)).reshape(n, GQA_HEADS * GQA_D)
        gqa = mm(go, 'gqa_w_o')

        g0, g1, g2 = _make_split([d] * 3, 3 * d)(jax.nn.sigmoid(part['gates']))
        mixed = g0 * mla + g1 * ssm + g2 * gqa
        x_all = full_res(x_all, mm(mixed, 'w_out'), md[5])
        x_all = half_res(x_all, swiglu(norm_mod(x_all, rp['norm_ffn2'][l], md[6], md[7]), 'ffn2_w13', 'ffn2_w2'), md[8])

    return loss_head(norm_out(x_all[c_len:], rp['final_norm']), target)


def kernel(x, c, ctx, c_ctx, ada_w, ada_b, norm_ffn1, norm_mix, norm_ffn2, ffn1_w13, ffn1_w2, ffn2_w13, ffn2_w2, w_in, mla_q_norm, mla_kv_norm, mla_w_uq, mla_w_ukv, mla_w_o, ssm_lambda_re, ssm_lambda_im, ssm_log_dt, ssm_b_re, ssm_b_im, ssm_c_re, ssm_c_im, ssm_d, ssm_w_glu, gqa_sink, gqa_w_o, w_out, final_norm, loss_target, m_c_ctx, m_ada_w, m_ada_b, m_norm_ffn1, m_norm_mix, m_norm_ffn2, m_ffn1_w13, m_ffn1_w2, m_ffn2_w13, m_ffn2_w2, m_w_in, m_mla_q_norm, m_mla_kv_norm, m_mla_w_uq, m_mla_w_ukv, m_mla_w_o, m_ssm_lambda_re, m_ssm_lambda_im, m_ssm_log_dt, m_ssm_b_re, m_ssm_b_im, m_ssm_c_re, m_ssm_c_im, m_ssm_d, m_ssm_w_glu, m_gqa_sink, m_gqa_w_o, m_w_out, m_final_norm, v_c_ctx, v_ada_w, v_ada_b, v_norm_ffn1, v_norm_mix, v_norm_ffn2, v_ffn1_w13, v_ffn1_w2, v_ffn2_w13, v_ffn2_w2, v_w_in, v_mla_q_norm, v_mla_kv_norm, v_mla_w_uq, v_mla_w_ukv, v_mla_w_o, v_ssm_lambda_re, v_ssm_lambda_im, v_ssm_log_dt, v_ssm_b_re, v_ssm_b_im, v_ssm_c_re, v_ssm_c_im, v_ssm_d, v_ssm_w_glu, v_gqa_sink, v_gqa_w_o, v_w_out, v_final_norm):
    args = dict(locals())
    weights = {k: args[k] for k in WEIGHTS}
    moments_m = {k: args['m_' + k] for k in WEIGHTS}
    moments_v = {k: args['v_' + k] for k in WEIGHTS}
    depth = ada_w.shape[0]
    d = x.shape[-1]
    c_len = ctx.shape[1]
    my_c = lax.axis_index('c')
    my_chip = 2 * lax.axis_index('x') + lax.axis_index('y')
    qk_w = MLA_NOPE + MLA_ROPE

    def as_rows(k, a):
        return a if k in ROW_SHARDED else a.T

    shard_shapes = [as_rows(k, weights[k][0]).shape for k in SHARDED]
    r_in = weights['w_in'].shape[2]
    layers = []
    for l in range(depth):
        gathered = _all_gather(_pack([as_rows(k, weights[k][l]) for k in SHARDED], (), BF16))
        full = {}
        for k, blocks in zip(SHARDED, _unpack(gathered, (N_DEV,), shard_shapes)):
            full[k] = _permute_w_in(blocks, d) if k == 'w_in' else blocks.reshape(N_DEV * blocks.shape[1], blocks.shape[2])
        full['mla_w_uq'] = jnp.pad(full['mla_w_uq'].reshape(MLA_HEADS, qk_w, MLA_Q_RANK),
                                   ((0, 0), (0, LANES - qk_w), (0, 0))).reshape(MLA_HEADS * LANES, MLA_Q_RANK)
        full['mla_w_o'] = jnp.pad(full['mla_w_o'].reshape(d, MLA_HEADS, MLA_V),
                                  ((0, 0), (0, 0), (LANES - MLA_V, 0))).reshape(d, MLA_HEADS * LANES)
        layers.append(full)
    sinks = [{k: jnp.zeros(v.shape, F32) for k, v in full.items()} for full in layers]

    rp = {k: weights[k] for k in REPLICATED}
    x_all = jnp.concatenate([ctx[0], x[0]], axis=0)
    loss_fn = functools.partial(_forward_loss, wl=layers, cc_in=c[0], target=loss_target[0], c_len=c_len)
    loss, vjp = jax.vjp(loss_fn, x_all, sinks, rp)
    g_x, g_layers, g_rp = vjp(jnp.ones((), F32))

    grads = {k: [] for k in SHARDED}
    for l in range(depth):
        gl = dict(g_layers[l])
        gl['mla_w_uq'] = gl['mla_w_uq'].reshape(MLA_HEADS, LANES, MLA_Q_RANK)[:, :qk_w].reshape(MLA_HEADS * qk_w, MLA_Q_RANK)
        gl['mla_w_o'] = gl['mla_w_o'].reshape(d, MLA_HEADS, LANES)[:, :, LANES - MLA_V:].reshape(d, MLA_HEADS * MLA_V)
        pieces = [_unpermute_w_in(gl[k], d, r_in) if k == 'w_in' else gl[k].reshape((N_DEV,) + shape)
                  for k, shape in zip(SHARDED, shard_shapes)]
        packed = _pack(pieces, (N_DEV,), F32)
        g_all = jnp.swapaxes(packed.reshape((4, 2) + packed.shape[1:]), 0, 1)
        mine = _reduce_scatter(g_all, my_c.reshape(1).astype(jnp.int32), my_chip.reshape(1).astype(jnp.int32))
        for k, g in zip(SHARDED, _unpack(mine, (), shard_shapes)):
            grads[k].append(as_rows(k, g))
    grads = {k: jnp.stack(v) for k, v in grads.items()}

    rep_shapes = [weights[k].shape for k in REPLICATED] + [(1,)]
    small = _pack([g_rp[k] for k in REPLICATED] + [loss.reshape(1)], (), F32)
    summed = _unpack(_sum_leading(_all_gather(small)), (), rep_shapes)
    for k, g in zip(REPLICATED, summed[:-1]):
        grads[k] = g
    loss_total = summed[-1].reshape(())

    delta, new_m, new_v = {}, {}, {}
    for k in SHARDED:
        shape = weights[k].shape
        as2d = lambda a: a.reshape(-1, shape[-1])
        outs = _adamw_call(as2d(weights[k]), as2d(grads[k]), as2d(moments_m[k]), as2d(moments_v[k]))
        delta[k], new_m[k], new_v[k] = (o.reshape(shape) for o in outs)
    rep_all = [weights[k].shape for k in REPLICATED]
    packs = [_pack([src[k] for k in REPLICATED], (), F32) for src in (weights, grads, moments_m, moments_v)]
    outs = [_unpack(o, (), rep_all) for o in _adamw_call(*packs)]
    for i, k in enumerate(REPLICATED):
        delta[k], new_m[k], new_v[k] = outs[0][i], outs[1][i], outs[2][i]

    return (loss_total, g_x[c_len:][None], *[grads[k] for k in WEIGHTS], *[delta[k] for k in WEIGHTS],
            *[new_m[k] for k in WEIGHTS], *[new_v[k] for k in WEIGHTS])
```

```python
import functools
import math

import jax
import jax.numpy as jnp
from jax import lax
from jax.experimental import pallas as pl
from jax.experimental.pallas import tpu as pltpu

F32 = jnp.float32
BF16 = jnp.bfloat16

MLA_HEADS, MLA_NOPE, MLA_ROPE, MLA_V = 8, 64, 32, 64
MLA_Q_RANK, MLA_KV_RANK = 384, 256
SSM_WIDTH, SSM_GROUP, SSM_STATE = 512, 16, 64
SSM_GROUPS = SSM_WIDTH // SSM_GROUP
GQA_HEADS, GQA_KV, GQA_D = 8, 2, 64
GQA_G = GQA_HEADS // GQA_KV
WINDOW, BLOCK, GRID_W = 128, 128, 64
N_MOD = 9
ROPE_BASE = 10000.0
EPS = 1e-6
NEG_INF = -1e30
ADAM_LR, ADAM_B1, ADAM_B2, ADAM_EPS, ADAM_WD, ADAM_STEP = 0.001, 0.9, 0.999, 1e-08, 0.01, 10

N_DEV = 8
LANES = 128
SUBLANES = 8
VMEM_LIMIT = 56 * 1024 * 1024
PACK_ROWS = 1024

SHARDED = ('ada_w', 'ffn1_w13', 'ffn1_w2', 'ffn2_w13', 'ffn2_w2', 'w_in', 'mla_w_uq', 'mla_w_ukv',
           'mla_w_o', 'ssm_w_glu', 'gqa_w_o', 'w_out')
ROW_SHARDED = ('ffn1_w2', 'ffn2_w2', 'w_out')
REPLICATED = ('c_ctx', 'ada_b', 'norm_ffn1', 'norm_mix', 'norm_ffn2', 'mla_q_norm', 'mla_kv_norm',
              'ssm_lambda_re', 'ssm_lambda_im', 'ssm_log_dt', 'ssm_b_re', 'ssm_b_im', 'ssm_c_re', 'ssm_c_im',
              'ssm_d', 'gqa_sink', 'final_norm')
WEIGHTS = ('c_ctx', 'ada_w', 'ada_b', 'norm_ffn1', 'norm_mix', 'norm_ffn2', 'ffn1_w13', 'ffn1_w2', 'ffn2_w13',
           'ffn2_w2', 'w_in', 'mla_q_norm', 'mla_kv_norm', 'mla_w_uq', 'mla_w_ukv', 'mla_w_o', 'ssm_lambda_re',
           'ssm_lambda_im', 'ssm_log_dt', 'ssm_b_re', 'ssm_b_im', 'ssm_c_re', 'ssm_c_im', 'ssm_d', 'ssm_w_glu',
           'gqa_sink', 'gqa_w_o', 'w_out', 'final_norm')


def _tile(n, target, mult):
    t = (min(target, n) // mult) * mult
    while t >= mult:
        if n % t == 0:
            return t
        t -= mult
    return n


def _cparams(sem):
    return pltpu.CompilerParams(dimension_semantics=sem, vmem_limit_bytes=VMEM_LIMIT)


def _mm(a, b, mode):
    if mode == 'nn':
        (M, K), N = a.shape, b.shape[1]
        tm, tn, tk = _tile(M, 1408, SUBLANES), _tile(N, 1024, LANES), _tile(K, 1408, LANES)
    elif mode == 'nt':
        (M, K), N = a.shape, b.shape[0]
        tm, tn, tk = _tile(M, 1408, SUBLANES), _tile(N, 1024, LANES), _tile(K, 1408, LANES)
    else:
        (K, M), N = a.shape, b.shape[1]
        tm, tn, tk = _tile(M, 1408, LANES), _tile(N, 1408, LANES), _tile(K, 768, 2 * SUBLANES)
    nk = K // tk
    dims = {'nn': (((1,), (0,)), ((), ())), 'nt': (((1,), (1,)), ((), ())), 'tn': (((0,), (0,)), ((), ()))}[mode]
    keep_a = nk == 1 and mode != 'tn' and N // tn > 1

    def body(a_ref, b_ref, o_ref, *scratch):
        if keep_a:
            @pl.when(pl.program_id(1) == 0)
            def _():
                scratch[0][...] = a_ref[...].astype(BF16)

            av = scratch[0][...]
        else:
            av = a_ref[...].astype(BF16)
        part = lax.dot_general(av, b_ref[...].astype(BF16), dims, preferred_element_type=F32)
        if nk == 1:
            o_ref[...] = part
        else:
            @pl.when(pl.program_id(2) == 0)
            def _():
                o_ref[...] = part

            @pl.when(pl.program_id(2) > 0)
            def _():
                o_ref[...] += part

    a_spec = pl.BlockSpec((tk, tm), lambda i, j, k: (k, i)) if mode == 'tn' else pl.BlockSpec((tm, tk), lambda i, j, k: (i, k))
    b_spec = pl.BlockSpec((tn, tk), lambda i, j, k: (j, k)) if mode == 'nt' else pl.BlockSpec((tk, tn), lambda i, j, k: (k, j))
    return pl.pallas_call(
        body, name='mm_' + mode, grid=(M // tm, N // tn, nk),
        in_specs=[a_spec, b_spec], out_specs=pl.BlockSpec((tm, tn), lambda i, j, k: (i, j)),
        out_shape=jax.ShapeDtypeStruct((M, N), F32),
        scratch_shapes=[pltpu.VMEM((tm, tk), BF16)] if keep_a else [],
        compiler_params=_cparams(('parallel', 'arbitrary', 'arbitrary')),
    )(a, b)


@jax.custom_vjp
def matmul_d(x, w):
    return _mm(x, w, 'nn')


def _matmul_d_fwd(x, w):
    return _mm(x, w, 'nn'), (x, w)


def _matmul_d_bwd(res, g):
    x, w = res
    return _mm(g, w, 'nt'), _mm(x, g, 'tn')


matmul_d.defvjp(_matmul_d_fwd, _matmul_d_bwd)


@jax.custom_vjp
def matmul(x, w, sink):
    return _mm(x, w, 'nn')


def _matmul_fwd(x, w, sink):
    return _mm(x, w, 'nn'), (x, w)


def _matmul_bwd(res, g):
    x, w = res
    return _mm(g, w, 'nt'), jnp.zeros_like(w), _mm(x, g, 'tn')


matmul.defvjp(_matmul_fwd, _matmul_bwd)


@jax.custom_vjp
def matmul_t(x, wt, sink):
    return _mm(x, wt, 'nt')


def _matmul_t_fwd(x, wt, sink):
    return _mm(x, wt, 'nt'), (x, wt)


def _matmul_t_bwd(res, g):
    x, wt = res
    return _mm(g, wt, 'nn'), jnp.zeros_like(wt), _mm(g, x, 'tn')


matmul_t.defvjp(_matmul_t_fwd, _matmul_t_bwd)


def _norm_fwd_call(x, g, sh, sc, rb):
    n, d = x.shape
    has_mod = sh is not None

    def body(*refs):
        x_ref, g_ref = refs[0], refs[1]
        o_ref = refs[-1]
        xv = x_ref[...]
        r = lax.rsqrt(jnp.mean(xv * xv, axis=-1, keepdims=True) + EPS)
        y = xv * r * g_ref[...]
        if has_mod:
            lat = pl.program_id(0) > 0
            shv = jnp.where(lat, refs[2][1:2, :], refs[2][0:1, :])
            scv = jnp.where(lat, refs[3][1:2, :], refs[3][0:1, :])
            y = y * (1.0 + scv) + shv
        o_ref[...] = y

    row = pl.BlockSpec((rb, d), lambda i: (i, 0))
    vec = pl.BlockSpec((1, d), lambda i: (0, 0))
    two = pl.BlockSpec((2, d), lambda i: (0, 0))
    ins = [x, g.reshape(1, d)] + ([sh, sc] if has_mod else [])
    return pl.pallas_call(
        body, name='norm_fwd', grid=(n // rb,), in_specs=[row, vec] + ([two, two] if has_mod else []),
        out_specs=row, out_shape=jax.ShapeDtypeStruct((n, d), F32), compiler_params=_cparams(('parallel',)),
    )(*ins)


def _norm_bwd_call(x, g, sh, sc, dy, rb):
    n, d = x.shape
    has_mod = sh is not None

    def body(*refs):
        x_ref, g_ref, dy_ref = refs[0], refs[1], refs[-3]
        dx_ref, acc_ref = refs[-2], refs[-1]
        i = pl.program_id(0)
        xv, dyv, gv = x_ref[...], dy_ref[...], g_ref[...]
        r = lax.rsqrt(jnp.mean(xv * xv, axis=-1, keepdims=True) + EPS)
        xh = xv * r
        if has_mod:
            lat = i > 0
            scv = jnp.where(lat, refs[3][1:2, :], refs[3][0:1, :])
            dyg = dyv * (1.0 + scv)
        else:
            dyg = dyv
        dxh = dyg * gv
        dx_ref[...] = r * (dxh - xh * jnp.mean(dxh * xh, axis=-1, keepdims=True))

        @pl.when(i == 0)
        def _():
            acc_ref[...] = jnp.zeros_like(acc_ref)

        acc_ref[0:1, :] += jnp.sum(dyg * xh, axis=0, keepdims=True)
        if has_mod:
            dsh = jnp.sum(dyv, axis=0, keepdims=True)
            dsc = jnp.sum(dyv * xh * gv, axis=0, keepdims=True)

            @pl.when(i == 0)
            def _():
                acc_ref[1:2, :] += dsh
                acc_ref[3:4, :] += dsc

            @pl.when(i > 0)
            def _():
                acc_ref[2:3, :] += dsh
                acc_ref[4:5, :] += dsc

    row = pl.BlockSpec((rb, d), lambda i: (i, 0))
    vec = pl.BlockSpec((1, d), lambda i: (0, 0))
    two = pl.BlockSpec((2, d), lambda i: (0, 0))
    ins = [x, g.reshape(1, d)] + ([sh, sc] if has_mod else []) + [dy]
    return pl.pallas_call(
        body, name='norm_bwd', grid=(n // rb,), in_specs=[row, vec] + ([two, two] if has_mod else []) + [row],
        out_specs=[row, pl.BlockSpec((SUBLANES, d), lambda i: (0, 0))],
        out_shape=[jax.ShapeDtypeStruct((n, d), F32), jax.ShapeDtypeStruct((SUBLANES, d), F32)],
        compiler_params=_cparams(('arbitrary',)),
    )(*ins)


def _make_norm(rb, has_mod):
    if has_mod:
        @jax.custom_vjp
        def f(x, g, sh, sc):
            return _norm_fwd_call(x, g, sh, sc, rb)

        def fwd(x, g, sh, sc):
            return _norm_fwd_call(x, g, sh, sc, rb), (x, g, sh, sc)

        def bwd(res, dy):
            x, g, sh, sc = res
            dx, acc = _norm_bwd_call(x, g, sh, sc, dy, rb)
            return dx, acc[0], acc[1:3], acc[3:5]
    else:
        @jax.custom_vjp
        def f(x, g):
            return _norm_fwd_call(x, g, None, None, rb)

        def fwd(x, g):
            return _norm_fwd_call(x, g, None, None, rb), (x, g)

        def bwd(res, dy):
            x, g = res
            dx, acc = _norm_bwd_call(x, g, None, None, dy, rb)
            return dx, acc[0]
    f.defvjp(fwd, bwd)
    return f


def _make_gated_res(rb, coef):
    def fwd_call(x, f, gate):
        n, d = x.shape

        def body(x_ref, f_ref, g_ref, o_ref):
            gv = jnp.where(pl.program_id(0) > 0, g_ref[1:2, :], g_ref[0:1, :])
            o_ref[...] = x_ref[...] + coef * gv * f_ref[...]

        row = pl.BlockSpec((rb, d), lambda i: (i, 0))
        return pl.pallas_call(
            body, name='gated_res_fwd', grid=(n // rb,), in_specs=[row, row, pl.BlockSpec((2, d), lambda i: (0, 0))],
            out_specs=row, out_shape=jax.ShapeDtypeStruct((n, d), F32), compiler_params=_cparams(('parallel',)),
        )(x, f, gate)

    def bwd_call(dy, f, gate):
        n, d = dy.shape

        def body(dy_ref, f_ref, g_ref, df_ref, acc_ref):
            i = pl.program_id(0)
            gv = jnp.where(i > 0, g_ref[1:2, :], g_ref[0:1, :])
            dyv = dy_ref[...]
            df_ref[...] = coef * gv * dyv
            part = coef * jnp.sum(dyv * f_ref[...], axis=0, keepdims=True)

            @pl.when(i == 0)
            def _():
                acc_ref[...] = jnp.zeros_like(acc_ref)
                acc_ref[0:1, :] += part

            @pl.when(i > 0)
            def _():
                acc_ref[1:2, :] += part

        row = pl.BlockSpec((rb, d), lambda i: (i, 0))
        return pl.pallas_call(
            body, name='gated_res_bwd', grid=(n // rb,), in_specs=[row, row, pl.BlockSpec((2, d), lambda i: (0, 0))],
            out_specs=[row, pl.BlockSpec((SUBLANES, d), lambda i: (0, 0))],
            out_shape=[jax.ShapeDtypeStruct((n, d), F32), jax.ShapeDtypeStruct((SUBLANES, d), F32)],
            compiler_params=_cparams(('arbitrary',)),
        )(dy, f, gate)

    @jax.custom_vjp
    def f(x, fv, gate):
        return fwd_call(x, fv, gate)

    def fwd(x, fv, gate):
        return fwd_call(x, fv, gate), (fv, gate)

    def bwd(res, dy):
        fv, gate = res
        df, acc = bwd_call(dy, fv, gate)
        return dy, df, acc[0:2]

    f.defvjp(fwd, bwd)
    return f


MLA_SCALE = (MLA_NOPE + MLA_ROPE) ** -0.5
LOG2E = math.log2(math.e)
MLA_SCALE_LOG2E = MLA_SCALE * LOG2E
NT_DIMS = (((1,), (1,)), ((), ()))


def _mla_keys(kv, kr):
    lane = lax.broadcasted_iota(jnp.int32, kv.shape, 1)
    return jnp.where(lane < MLA_NOPE, kv, kr)


def _chunks(start, stop, target):
    size = _tile(stop - start, target, LANES)
    return [(start + t * size, size) for t in range((stop - start) // size)]


MLA_CHUNK = 2816


def _mla_fwd_call(q, kv, kr, c_len):
    n = q.shape[0]
    h = q.shape[1] // LANES
    tq = c_len

    def body(q_ref, kv_ref, kr_ref, o_ref, lse_ref, kb, vb):
        i = pl.program_id(1)

        @pl.when(i == 0)
        def _():
            kvv = kv_ref[...]
            kb[...] = _mla_keys(kvv, kr_ref[...])
            vb[...] = jnp.where(lax.broadcasted_iota(jnp.int32, kvv.shape, 1) < MLA_NOPE, jnp.ones_like(kvv), kvv)

        qv = q_ref[...].astype(BF16)

        def attend(nk):
            s = lax.dot_general(qv, kb[:nk, :], NT_DIMS, preferred_element_type=F32)
            m = jnp.max(s, axis=-1, keepdims=True)
            p = jnp.exp2((s - m) * MLA_SCALE_LOG2E)
            acc = jnp.dot(p.astype(BF16), vb[:nk, :], preferred_element_type=F32)
            l = acc[:, 0:1]
            o_ref[...] = acc / l
            lse_ref[0] = m * MLA_SCALE + jnp.log(l)

        pl.when(i == 0)(lambda: attend(c_len))
        pl.when(i > 0)(lambda: attend(n))

    qspec = pl.BlockSpec((tq, LANES), lambda a, i: (i, a))
    return pl.pallas_call(
        body, name='mla_attn_fwd', grid=(h, n // tq),
        in_specs=[qspec, pl.BlockSpec((n, LANES), lambda a, i: (0, a)), pl.BlockSpec((n, LANES), lambda a, i: (0, 0))],
        out_specs=[qspec, pl.BlockSpec((1, tq, 1), lambda a, i: (a, i, 0))],
        out_shape=[jax.ShapeDtypeStruct((n, h * LANES), F32), jax.ShapeDtypeStruct((h, n, 1), F32)],
        scratch_shapes=[pltpu.VMEM((n, LANES), BF16), pltpu.VMEM((n, LANES), BF16)],
        compiler_params=_cparams(('arbitrary', 'arbitrary')),
    )(q, kv, kr)


TN_DIMS = (((0,), (0,)), ((), ()))


def _mla_bwd_call(q, kv, kr, o, lse, do, c_len):
    n = q.shape[0]
    h = q.shape[1] // LANES
    tq = c_len
    nq = n // tq

    def body(q_ref, kv_ref, kr_ref, o_ref, lse_ref, do_ref, dq_ref, dv_ref, dk_ref, kb):
        i = pl.program_id(1)

        @pl.when(i == 0)
        def _():
            kb[...] = _mla_keys(kv_ref[...], kr_ref[...])
            dv_ref[...] = jnp.zeros_like(dv_ref)
            dk_ref[...] = jnp.zeros_like(dk_ref)

        dov = do_ref[...]
        delta = jnp.sum(dov * o_ref[...], axis=-1, keepdims=True)
        qv, dob, lse2 = q_ref[...], dov.astype(BF16), lse_ref[0] * LOG2E

        def grad(chunks):
            acc = None
            for k0, kc in chunks:
                keys = kb[k0:k0 + kc, :]
                s = lax.dot_general(qv, keys, NT_DIMS, preferred_element_type=F32)
                p = jnp.exp2(s * MLA_SCALE_LOG2E - lse2)
                dp = lax.dot_general(dob, kv_ref[k0:k0 + kc, :], NT_DIMS, preferred_element_type=F32)
                ds = (p * (dp - delta)).astype(BF16)
                part = jnp.dot(ds, keys, preferred_element_type=F32)
                acc = part if acc is None else acc + part
                dk_ref[0, k0:k0 + kc, :] += lax.dot_general(ds, qv, TN_DIMS, preferred_element_type=F32)
                dv_ref[k0:k0 + kc, :] += lax.dot_general(p.astype(BF16), dob, TN_DIMS, preferred_element_type=F32)
            dq_ref[...] = acc * MLA_SCALE

        pl.when(i == 0)(lambda: grad([(0, c_len)]))
        pl.when(i > 0)(lambda: grad(_chunks(0, n, MLA_CHUNK)))

        @pl.when(i == nq - 1)
        def _():
            dk = dk_ref[0] * MLA_SCALE
            dk_ref[0] = dk
            dv_ref[...] = jnp.where(lax.broadcasted_iota(jnp.int32, dk.shape, 1) < MLA_NOPE, dk, dv_ref[...])

    qspec = pl.BlockSpec((tq, LANES), lambda a, i: (i, a))
    full = pl.BlockSpec((n, LANES), lambda a, i: (0, a))
    return pl.pallas_call(
        body, name='mla_attn_bwd', grid=(h, nq),
        in_specs=[qspec, full, pl.BlockSpec((n, LANES), lambda a, i: (0, 0)), qspec,
                  pl.BlockSpec((1, tq, 1), lambda a, i: (a, i, 0)), qspec],
        out_specs=[qspec, full, pl.BlockSpec((1, n, LANES), lambda a, i: (a, 0, 0))],
        out_shape=[jax.ShapeDtypeStruct((n, h * LANES), F32), jax.ShapeDtypeStruct((n, h * LANES), F32),
                   jax.ShapeDtypeStruct((h, n, LANES), F32)],
        scratch_shapes=[pltpu.VMEM((n, LANES), BF16)],
        compiler_params=_cparams(('arbitrary', 'arbitrary')),
    )(q, kv, kr, o, lse, do)


def _sum_leading(g):
    nl, r, _ = g.shape
    rb = _tile(r, 512, SUBLANES)

    def body(g_ref, o_ref):
        acc = g_ref[0]
        for j in range(1, nl):
            acc = acc + g_ref[j]
        o_ref[...] = acc

    return pl.pallas_call(
        body, name='sum_leading', grid=(r // rb,), in_specs=[pl.BlockSpec((nl, rb, LANES), lambda i: (0, i, 0))],
        out_specs=pl.BlockSpec((rb, LANES), lambda i: (i, 0)), out_shape=jax.ShapeDtypeStruct((r, LANES), F32),
        compiler_params=_cparams(('parallel',)),
    )(g)


def _make_mla(c_len):
    @jax.custom_vjp
    def f(q, kv, kr):
        return _mla_fwd_call(q, kv.astype(BF16), kr.astype(BF16), c_len)[0]

    def fwd(q, kv, kr):
        kvb, krb = kv.astype(BF16), kr.astype(BF16)
        o, lse = _mla_fwd_call(q, kvb, krb, c_len)
        return o, (q.astype(BF16), kvb, krb, o, lse)

    def bwd(res, do):
        q, kv, kr, o, lse = res
        dq, dkv, dk_full = _mla_bwd_call(q, kv, kr, o, lse, do, c_len)
        return dq, dkv, _sum_leading(dk_full)

    f.defvjp(fwd, bwd)
    return f


def _gqa_specs(band, c_len, nb, rows):
    cb = c_len // BLOCK
    q_spec = pl.BlockSpec((1, GQA_G, rows, GQA_D), lambda a, b: (a, 0, b, 0))
    ctx_spec = pl.BlockSpec((1, c_len, GQA_D), lambda a, b: (a, 0, 0))
    kv_specs = [ctx_spec]
    if band:
        kv_specs += [pl.BlockSpec((1, BLOCK, GQA_D), lambda a, b: (a, jnp.maximum(b - 1, 0) + cb, 0)),
                     pl.BlockSpec((1, BLOCK, GQA_D), lambda a, b: (a, b + cb, 0)),
                     pl.BlockSpec((1, BLOCK, GQA_D), lambda a, b: (a, jnp.minimum(b + 1, nb - 1) + cb, 0))]
    sink_spec = pl.BlockSpec((1, GQA_G * rows, 1), lambda a, b: (a, 0, 0))
    return q_spec, kv_specs, sink_spec


def _gqa_scores(q, kcat, sink, band, c_len, t_len, rows):
    scale = GQA_D ** -0.5
    s = lax.dot_general(q, kcat, (((1,), (1,)), ((), ())), preferred_element_type=F32) * scale
    if band:
        b = pl.program_id(1)
        shape = s.shape
        col = lax.broadcasted_iota(jnp.int32, shape, 1)
        qpos = b * BLOCK + (lax.broadcasted_iota(jnp.int32, shape, 0) & (BLOCK - 1))
        kpos = (b - 1) * BLOCK + (col - c_len)
        valid = (col < c_len) | ((jnp.abs(qpos - kpos) <= WINDOW) & (kpos >= 0) & (kpos < t_len))
        s = jnp.where(valid, s, NEG_INF)
    m = jnp.maximum(jnp.max(s, axis=-1, keepdims=True), sink)
    e = jnp.exp(s - m)
    es = jnp.exp(sink - m)
    den = es + jnp.sum(e, axis=-1, keepdims=True)
    return e / den, es / den


def _gqa_fwd_call(q4, k2, v2, sink_rows, band, c_len):
    kv, g, tq_all, d = q4.shape
    rows = BLOCK if band else tq_all
    nb = tq_all // rows
    t_len = k2.shape[1] - c_len
    nkv = 4 if band else 1
    q_spec, kv_specs, sink_spec = _gqa_specs(band, c_len, nb, rows)

    def body(*refs):
        q_ref, sink_ref, o_ref = refs[0], refs[1 + 2 * nkv], refs[-1]
        kcat = jnp.concatenate([r[0] for r in refs[1:1 + nkv]], axis=0).astype(BF16)
        vcat = jnp.concatenate([r[0] for r in refs[1 + nkv:1 + 2 * nkv]], axis=0).astype(BF16)
        q = q_ref[0].reshape(g * rows, d).astype(BF16)
        p, _ = _gqa_scores(q, kcat, sink_ref[0], band, c_len, t_len, rows)
        o_ref[0] = jnp.dot(p.astype(BF16), vcat, preferred_element_type=F32).reshape(g, rows, d)

    return pl.pallas_call(
        body, name='gqa_fwd_band' if band else 'gqa_fwd_ctx', grid=(kv, nb),
        in_specs=[q_spec] + kv_specs + kv_specs + [sink_spec], out_specs=q_spec,
        out_shape=jax.ShapeDtypeStruct(q4.shape, F32), compiler_params=_cparams(('parallel', 'parallel')),
    )(q4, *([k2] * nkv), *([v2] * nkv), sink_rows)


def _gqa_bwd_call(q4, k2, v2, sink_rows, do4, band, c_len):
    kv, g, tq_all, d = q4.shape
    rows = BLOCK if band else tq_all
    nb = tq_all // rows
    t_len = k2.shape[1] - c_len
    nkv = 4 if band else 1
    scale = GQA_D ** -0.5
    q_spec, kv_specs, sink_spec = _gqa_specs(band, c_len, nb, rows)

    def body(*refs):
        q_ref, sink_ref, do_ref = refs[0], refs[1 + 2 * nkv], refs[2 + 2 * nkv]
        outs = refs[3 + 2 * nkv:]
        dq_ref, dkc_ref, dvc_ref = outs[0], outs[1], outs[2]
        dsink_ref = outs[-1]
        b = pl.program_id(1)
        kcat = jnp.concatenate([r[0] for r in refs[1:1 + nkv]], axis=0).astype(BF16)
        vcat = jnp.concatenate([r[0] for r in refs[1 + nkv:1 + 2 * nkv]], axis=0).astype(BF16)
        q = q_ref[0].reshape(g * rows, d).astype(BF16)
        do = do_ref[0].reshape(g * rows, d).astype(BF16)
        p, p_sink = _gqa_scores(q, kcat, sink_ref[0], band, c_len, t_len, rows)
        dp = lax.dot_general(do, vcat, (((1,), (1,)), ((), ())), preferred_element_type=F32)
        rd = jnp.sum(p * dp, axis=-1, keepdims=True)
        ds = (p * (dp - rd) * scale).astype(BF16)
        dq_ref[0] = jnp.dot(ds, kcat, preferred_element_type=F32).reshape(g, rows, d)
        dkcat = lax.dot_general(ds, q, (((0,), (0,)), ((), ())), preferred_element_type=F32)
        dvcat = lax.dot_general(p.astype(BF16), do, (((0,), (0,)), ((), ())), preferred_element_type=F32)

        @pl.when(b == 0)
        def _():
            dkc_ref[...] = jnp.zeros_like(dkc_ref)
            dvc_ref[...] = jnp.zeros_like(dvc_ref)
            dsink_ref[...] = jnp.zeros_like(dsink_ref)

        dkc_ref[0] += dkcat[:c_len]
        dvc_ref[0] += dvcat[:c_len]
        dsink_ref[0] += -p_sink * rd
        if band:
            outs[3][0, 0] = dkcat[c_len:]
            outs[4][0, 0] = dvcat[c_len:]

    ctx_out = pl.BlockSpec((1, c_len, d), lambda a, b: (a, 0, 0))
    band_out = pl.BlockSpec((1, 1, 3 * BLOCK, d), lambda a, b: (a, b, 0, 0))
    out_specs = [q_spec, ctx_out, ctx_out] + ([band_out, band_out] if band else []) + [sink_spec]
    ctx_shape = jax.ShapeDtypeStruct((kv, c_len, d), F32)
    band_shape = jax.ShapeDtypeStruct((kv, nb, 3 * BLOCK, d), F32)
    out_shape = ([jax.ShapeDtypeStruct(q4.shape, F32), ctx_shape, ctx_shape] + ([band_shape, band_shape] if band else [])
                 + [jax.ShapeDtypeStruct(sink_rows.shape, F32)])
    return pl.pallas_call(
        body, name='gqa_bwd_band' if band else 'gqa_bwd_ctx', grid=(kv, nb),
        in_specs=[q_spec] + kv_specs + kv_specs + [sink_spec, q_spec], out_specs=out_specs, out_shape=out_shape,
        compiler_params=_cparams(('arbitrary', 'arbitrary')),
    )(q4, *([k2] * nkv), *([v2] * nkv), sink_rows, do4)


def _make_gqa(band, c_len):
    @jax.custom_vjp
    def f(q4, k2, v2, sink_rows):
        return _gqa_fwd_call(q4, k2, v2, sink_rows, band, c_len)

    def fwd(q4, k2, v2, sink_rows):
        return _gqa_fwd_call(q4, k2, v2, sink_rows, band, c_len), (q4, k2, v2, sink_rows)

    def bwd(res, do4):
        q4, k2, v2, sink_rows = res
        outs = _gqa_bwd_call(q4, k2, v2, sink_rows, do4, band, c_len)
        kv, n, d = k2.shape
        if not band:
            dq4, dkc, dvc, dsink = outs
            return dq4, dkc, dvc, dsink
        dq4, dkc, dvc, dkb, dvb, dsink = outs

        def fold(ctx_part, bands):
            cur = bands[:, :, BLOCK:2 * BLOCK]
            prv = jnp.pad(bands[:, 1:, :BLOCK], ((0, 0), (0, 1), (0, 0), (0, 0)))
            nxt = jnp.pad(bands[:, :-1, 2 * BLOCK:], ((0, 0), (1, 0), (0, 0), (0, 0)))
            lat = (cur + prv + nxt).reshape(kv, n - c_len, d)
            return jnp.concatenate([ctx_part, lat], axis=1)

        return dq4, fold(dkc, dkb), fold(dvc, dvb), dsink

    f.defvjp(fwd, bwd)
    return f


def _cmul(ar, ai, br, bi):
    return ar * br - ai * bi, ar * bi + ai * br


def _scan_tables(ar, ai, desc):
    a1 = (ar, ai)
    a2 = _cmul(*a1, *a1)
    a4 = _cmul(*a2, *a2)
    pw = [a1]
    for _ in range(SUBLANES - 1):
        pw.append(_cmul(*pw[-1], *a1))
    row = jnp.arange(SUBLANES)[:, None]
    tabs = []
    for dist, (pr, pi) in ((1, a1), (2, a2), (4, a4)):
        keep = (row <= SUBLANES - 1 - dist) if desc else (row >= dist)
        tabs += [jnp.where(keep, pr[None, :], 0.0), jnp.where(keep, pi[None, :], 0.0)]
    order = pw[::-1] if desc else pw
    tabs += [jnp.stack([p[0] for p in order]), jnp.stack([p[1] for p in order])]
    return jnp.stack(tabs).astype(F32)


def _scan_call(b_re, b_im, tabs, order, chunk, prev=None):
    n, s_dim = b_re.shape
    nch = n // chunk
    ng = chunk // SUBLANES
    desc = order in ('Fb', 'R')
    with_da = prev is not None

    def chunk_of(i):
        if order == 'F':
            return i
        if order == 'Fb':
            return nch - 1 - i
        if order == 'R':
            return jnp.where(i == 0, 0, nch - i)
        return jnp.where(i == nch - 1, 0, i + 1)

    def body(*refs):
        br_ref, bi_ref, tab_ref = refs[0], refs[1], refs[2]
        if with_da:
            pr_ref, pi_ref, sr_ref, si_ref, dar_ref, dai_ref, cr_ref, ci_ref = refs[3:]
        else:
            sr_ref, si_ref, pr_ref, pi_ref, cr_ref, ci_ref = refs[3:]

        @pl.when(pl.program_id(0) == 0)
        def _():
            cr_ref[...] = jnp.zeros_like(cr_ref)
            ci_ref[...] = jnp.zeros_like(ci_ref)
            if with_da:
                dar_ref[...] = jnp.zeros_like(dar_ref)
                dai_ref[...] = jnp.zeros_like(dai_ref)

        sub = lax.broadcasted_iota(jnp.int32, (SUBLANES, s_dim), 0)
        edge = SUBLANES - 1 if desc else 0
        last = 0 if desc else SUBLANES - 1

        def step(t, carry):
            gi = (ng - 1 - t) if desc else t
            rows = pl.ds(pl.multiple_of(gi * SUBLANES, SUBLANES), SUBLANES)
            xr, xi = br_ref[rows, :], bi_ref[rows, :]
            for j, dist in enumerate((1, 2, 4)):
                shift = SUBLANES - dist if desc else dist
                rr, ri = pltpu.roll(xr, shift, 0), pltpu.roll(xi, shift, 0)
                mr, mi = tab_ref[2 * j], tab_ref[2 * j + 1]
                xr, xi = xr + mr * rr - mi * ri, xi + mr * ri + mi * rr
            cr, ci = cr_ref[...], ci_ref[...]
            pwr, pwi = tab_ref[6], tab_ref[7]
            sr = xr + pwr * cr - pwi * ci
            si = xi + pwr * ci + pwi * cr
            sr_ref[rows, :] = sr
            si_ref[rows, :] = si
            if with_da:
                pr, pi = pr_ref[rows, :], pi_ref[rows, :]
                dar_ref[...] += sr * pr + si * pi
                dai_ref[...] += si * pr - sr * pi
            else:
                shift1 = SUBLANES - 1 if desc else 1
                pr_ref[rows, :] = jnp.where(sub == edge, cr, pltpu.roll(sr, shift1, 0))
                pi_ref[rows, :] = jnp.where(sub == edge, ci, pltpu.roll(si, shift1, 0))
            cr_ref[...] = jnp.broadcast_to(sr[last:last + 1, :], (SUBLANES, s_dim))
            ci_ref[...] = jnp.broadcast_to(si[last:last + 1, :], (SUBLANES, s_dim))
            return carry

        lax.fori_loop(0, ng, step, 0)

    blk = pl.BlockSpec((chunk, s_dim), lambda i: (chunk_of(i), 0))
    tab_spec = pl.BlockSpec((8, SUBLANES, s_dim), lambda i: (0, 0, 0))
    acc = pl.BlockSpec((SUBLANES, s_dim), lambda i: (0, 0))
    big = jax.ShapeDtypeStruct((n, s_dim), F32)
    small = jax.ShapeDtypeStruct((SUBLANES, s_dim), F32)
    if with_da:
        in_specs, ins = [blk, blk, tab_spec, blk, blk], [b_re, b_im, tabs, prev[0], prev[1]]
        out_specs, out_shape = [blk, blk, acc, acc], [big, big, small, small]
    else:
        in_specs, ins = [blk, blk, tab_spec], [b_re, b_im, tabs]
        out_specs, out_shape = [blk, blk, blk, blk], [big, big, big, big]
    return pl.pallas_call(
        body, name='s5_scan_' + order, grid=(nch,), in_specs=in_specs, out_specs=out_specs, out_shape=out_shape,
        scratch_shapes=[pltpu.VMEM((SUBLANES, s_dim), F32), pltpu.VMEM((SUBLANES, s_dim), F32)],
        compiler_params=_cparams(('arbitrary',)),
    )(*ins)


def _make_scan(rev, chunk):
    def run(b_re, b_im, ar, ai):
        tabs = _scan_tables(ar, ai, desc=rev)
        return _scan_call(b_re, b_im, tabs, 'R' if rev else 'F', chunk)

    @jax.custom_vjp
    def f(b_re, b_im, ar, ai):
        return tuple(run(b_re, b_im, ar, ai)[:2])

    def fwd(b_re, b_im, ar, ai):
        s_re, s_im, p_re, p_im = run(b_re, b_im, ar, ai)
        return (s_re, s_im), (p_re, p_im, ar, ai)

    def bwd(res, g):
        p_re, p_im, ar, ai = res
        tabs = _scan_tables(ar, -ai, desc=not rev)
        db_re, db_im, dar, dai = _scan_call(g[0], g[1], tabs, 'Rb' if rev else 'Fb', chunk, prev=(p_re, p_im))
        return db_re, db_im, jnp.sum(dar, axis=0), jnp.sum(dai, axis=0)

    f.defvjp(fwd, bwd)
    return f


def _sqerr_call(y, t):
    n, d = y.shape
    rb = _tile(n, 512, SUBLANES)

    def body(y_ref, t_ref, o_ref):
        @pl.when(pl.program_id(0) == 0)
        def _():
            o_ref[...] = jnp.zeros_like(o_ref)

        e = y_ref[...] - t_ref[...]
        o_ref[...] += jnp.sum(e * e, axis=0, keepdims=True)

    row = pl.BlockSpec((rb, d), lambda i: (i, 0))
    return pl.pallas_call(
        body, name='sq_err', grid=(n // rb,), in_specs=[row, row], out_specs=pl.BlockSpec((1, d), lambda i: (0, 0)),
        out_shape=jax.ShapeDtypeStruct((1, d), F32), compiler_params=_cparams(('arbitrary',)),
    )(y, t)


@jax.custom_vjp
def loss_head(y, t):
    return 0.5 * jnp.sum(_sqerr_call(y, t)) / y.shape[1]


def _loss_head_fwd(y, t):
    return loss_head(y, t), (y, t)


def _loss_head_bwd(res, g):
    y, t = res
    return g * (y - t) / y.shape[1], None


loss_head.defvjp(_loss_head_fwd, _loss_head_bwd)


def _adamw_call(w, g, m, v):
    r, c = w.shape
    rb = _tile(r, max(SUBLANES, (256 * 1024) // max(c, LANES) // SUBLANES * SUBLANES), SUBLANES)

    def body(w_ref, g_ref, m_ref, v_ref, d_ref, nm_ref, nv_ref):
        gv = g_ref[...]
        nm = ADAM_B1 * m_ref[...] + (1.0 - ADAM_B1) * gv
        nv = ADAM_B2 * v_ref[...] + (1.0 - ADAM_B2) * (gv * gv)
        m_hat = nm / (1.0 - ADAM_B1 ** ADAM_STEP)
        v_hat = nv / (1.0 - ADAM_B2 ** ADAM_STEP)
        d_ref[...] = -ADAM_LR * (m_hat / (jnp.sqrt(v_hat) + ADAM_EPS) + ADAM_WD * w_ref[...])
        nm_ref[...] = nm
        nv_ref[...] = nv

    blk = pl.BlockSpec((rb, c), lambda i: (i, 0))
    shape = jax.ShapeDtypeStruct((r, c), F32)
    return pl.pallas_call(
        body, name='adamw', grid=(r // rb,), in_specs=[blk] * 4, out_specs=[blk] * 3, out_shape=[shape] * 3,
        compiler_params=_cparams(('parallel',)),
    )(w, g, m, v)


MESH = pl.DeviceIdType.MESH
HBM_SPEC = pl.BlockSpec(memory_space=pltpu.HBM)


def _all_gather(x):
    def body(x_ref, out_ref, send_sems, recv_sems, local_sem):
        x, y, c = lax.axis_index('x'), lax.axis_index('y'), lax.axis_index('c')
        me, sibling = (x, y, c), (x, y, 1 - c)
        chips = [(1 - x, y), (x, 1 - y), (1 - x, 1 - y)]

        def slot(px, py, pc):
            return out_ref.at[4 * px + 2 * py + pc]

        def copy(k, block, to, src=None):
            return pltpu.make_async_remote_copy(
                src_ref=slot(*block) if src is None else src, dst_ref=slot(*block),
                send_sem=send_sems.at[k], recv_sem=recv_sems.at[k], device_id=to, device_id_type=MESH)

        mine = pltpu.make_async_copy(x_ref, slot(*me), local_sem)
        mine.start()
        first = [copy(0, me, sibling, src=x_ref)]
        first += [copy(1 + j, me, (*chip, c), src=x_ref) for j, chip in enumerate(chips)]
        for cp in first:
            cp.start()
        passed = [copy(4 + j, (*chip, c), sibling) for j, chip in enumerate(chips)]
        for j, chip in enumerate(chips):
            copy(1 + j, (*chip, c), me).wait_recv()
            passed[j].start()
        copy(0, sibling, me).wait_recv()
        for j, chip in enumerate(chips):
            copy(4 + j, (*chip, 1 - c), me).wait_recv()
        for cp in first + passed:
            cp.wait_send()
        mine.wait()

    return pl.pallas_call(
        body, name='all_gather', out_shape=jax.ShapeDtypeStruct((N_DEV,) + x.shape, x.dtype),
        in_specs=[HBM_SPEC], out_specs=HBM_SPEC,
        scratch_shapes=[pltpu.SemaphoreType.DMA((7,)), pltpu.SemaphoreType.DMA((7,)), pltpu.SemaphoreType.DMA],
    )(x)


def _exchange_sibling(g_all):
    def body(g_ref, out_ref, send_sem, recv_sem):
        x, y, c = lax.axis_index('x'), lax.axis_index('y'), lax.axis_index('c')
        cp = pltpu.make_async_remote_copy(src_ref=g_ref.at[1 - c], dst_ref=out_ref, send_sem=send_sem, recv_sem=recv_sem,
                                          device_id=(x, y, 1 - c), device_id_type=MESH)
        cp.start()
        cp.wait()

    return pl.pallas_call(
        body, name='rs_sibling', out_shape=jax.ShapeDtypeStruct(g_all.shape[1:], g_all.dtype),
        in_specs=[HBM_SPEC], out_specs=HBM_SPEC,
        scratch_shapes=[pltpu.SemaphoreType.DMA, pltpu.SemaphoreType.DMA],
    )(g_all)


def _exchange_chips(p):
    def body(p_ref, out_ref, send_sems, recv_sems):
        x, y, c = lax.axis_index('x'), lax.axis_index('y'), lax.axis_index('c')
        chips = [(1 - x, y), (x, 1 - y), (1 - x, 1 - y)]
        copies = [pltpu.make_async_remote_copy(src_ref=p_ref.at[2 * px + py], dst_ref=out_ref.at[j],
                                               send_sem=send_sems.at[j], recv_sem=recv_sems.at[j],
                                               device_id=(px, py, c), device_id_type=MESH)
                  for j, (px, py) in enumerate(chips)]
        for cp in copies:
            cp.start()
        for cp in copies:
            cp.wait_recv()
        for cp in copies:
            cp.wait_send()

    return pl.pallas_call(
        body, name='rs_chips', out_shape=jax.ShapeDtypeStruct((3,) + p.shape[1:], p.dtype),
        in_specs=[HBM_SPEC], out_specs=HBM_SPEC,
        scratch_shapes=[pltpu.SemaphoreType.DMA((3,)), pltpu.SemaphoreType.DMA((3,))],
    )(p)


def _add_sibling(g_all, recv, c_idx):
    _, nchip, r, _ = g_all.shape
    rb = _tile(r, PACK_ROWS, SUBLANES)

    def body(c_ref, g_ref, r_ref, o_ref, ob_ref):
        s = g_ref[0] + r_ref[...]
        o_ref[...] = s
        ob_ref[...] = s.astype(BF16)

    blk = pl.BlockSpec((1, rb, LANES), lambda k, i, c: (k, i, 0))
    return pl.pallas_call(
        body, name='rs_add_sibling',
        grid_spec=pltpu.PrefetchScalarGridSpec(
            num_scalar_prefetch=1, grid=(nchip, r // rb),
            in_specs=[pl.BlockSpec((1, 1, rb, LANES), lambda k, i, c: (c[0], k, i, 0)), blk],
            out_specs=[blk, blk]),
        out_shape=[jax.ShapeDtypeStruct(recv.shape, F32), jax.ShapeDtypeStruct(recv.shape, BF16)],
        compiler_params=_cparams(('parallel', 'parallel')),
    )(c_idx, g_all, recv)


def _add_chips(p, recv, chip_idx):
    _, r, _ = p.shape
    rb = _tile(r, PACK_ROWS, SUBLANES)

    def body(k_ref, p_ref, r0, r1, r2, o_ref):
        o_ref[...] = ((p_ref[0] + r0[0].astype(F32)) + r1[0].astype(F32)) + r2[0].astype(F32)

    rspec = lambda j: pl.BlockSpec((1, rb, LANES), lambda i, k: (j, i, 0))
    return pl.pallas_call(
        body, name='rs_add_chips',
        grid_spec=pltpu.PrefetchScalarGridSpec(
            num_scalar_prefetch=1, grid=(r // rb,),
            in_specs=[pl.BlockSpec((1, rb, LANES), lambda i, k: (k[0], i, 0)), rspec(0), rspec(1), rspec(2)],
            out_specs=pl.BlockSpec((rb, LANES), lambda i, k: (i, 0))),
        out_shape=jax.ShapeDtypeStruct((r, LANES), F32), compiler_params=_cparams(('parallel',)),
    )(chip_idx, p, recv, recv, recv)


def _reduce_scatter(g_all, c_idx, chip_idx):
    part, part_bf16 = _add_sibling(g_all, _exchange_sibling(g_all), c_idx)
    return _add_chips(part, _exchange_chips(part_bf16), chip_idx)


def _pad_to(n, mult):
    return (n + mult - 1) // mult * mult


def _pack(pieces, lead, dtype):
    flat = []
    total = 0
    for p in pieces:
        f = p.reshape(lead + (-1,)).astype(dtype)
        n = _pad_to(f.shape[-1], 16 * LANES)
        flat.append(jnp.pad(f, [(0, 0)] * len(lead) + [(0, n - f.shape[-1])]))
        total += n
    full = _pad_to(total, PACK_ROWS * LANES)
    if full > total:
        flat.append(jnp.zeros(lead + (full - total,), dtype))
    return jnp.concatenate(flat, axis=-1).reshape(lead + (full // LANES, LANES))


def _unpack(buf, lead, shapes):
    flat = buf.reshape(lead + (-1,))
    out, off = [], 0
    for s in shapes:
        n = math.prod(s)
        out.append(flat[..., off:off + n].reshape(lead + tuple(s)))
        off += _pad_to(n, 16 * LANES)
    return out


def _make_split(sizes, width):
    starts = [sum(sizes[:i]) for i in range(len(sizes))]
    tail = width - sum(sizes)

    @jax.custom_vjp
    def f(z):
        return tuple(z[:, o:o + s] for o, s in zip(starts, sizes))

    def fwd(z):
        return f(z), None

    def bwd(_, gs):
        pieces = list(gs) + ([jnp.zeros((gs[0].shape[0], tail), gs[0].dtype)] if tail else [])
        return (jnp.concatenate(pieces, axis=1),)

    f.defvjp(fwd, bwd)
    return f


def _rope_tables(c_len, t_len, n):
    quarter = n // 4
    inv = ROPE_BASE ** (-jnp.arange(0, 2 * quarter, 2, dtype=F32) / (2 * quarter))
    t = jnp.arange(t_len, dtype=jnp.int32)
    pos = jnp.stack([(t // GRID_W).astype(F32), (t % GRID_W).astype(F32)], axis=1)
    ang = pos[:, :, None] * inv[None, None, :]
    ang = jnp.concatenate([jnp.zeros((c_len, 2, quarter), F32), ang], axis=0)
    return jnp.cos(ang), jnp.sin(ang)


def _axial_rope(x, cos, sin):
    n_rows, h, n = x.shape
    xs = x.reshape(n_rows, h, 2, 2, n // 4)
    x1, x2 = xs[:, :, :, 0], xs[:, :, :, 1]
    c, s = cos[:, None], sin[:, None]
    return jnp.stack([x1 * c - x2 * s, x1 * s + x2 * c], axis=3).reshape(n_rows, h, n)


def _ssm_discretize(lam_re, lam_im, log_dt, b_re, b_im):
    dt = jnp.exp(log_dt)[:, None]
    mag = jnp.exp(lam_re * dt)
    a_re, a_im = mag * jnp.cos(lam_im * dt), mag * jnp.sin(lam_im * dt)
    den = lam_re * lam_re + lam_im * lam_im
    w_re = ((a_re - 1) * lam_re + a_im * lam_im) / den
    w_im = (a_im * lam_re - (a_re - 1) * lam_im) / den
    bb_re, bb_im = _cmul(w_re[..., None], w_im[..., None], b_re, b_im)
    return a_re, a_im, bb_re, bb_im


def _block_diag_in(b):
    g = b.shape[0]
    return jnp.einsum('gpm,gh->gmhp', b, jnp.eye(g, dtype=F32)).reshape(g * b.shape[2], g * b.shape[1])


def _block_diag_out(c):
    g = c.shape[0]
    return jnp.einsum('gmp,gh->gphm', c, jnp.eye(g, dtype=F32)).reshape(g * c.shape[2], g * c.shape[1])


def _w_in_layout(d_model):
    sizes = (MLA_Q_RANK, MLA_KV_RANK, MLA_ROPE, SSM_WIDTH, GQA_HEADS * GQA_D, GQA_KV * GQA_D, GQA_KV * GQA_D, 3 * d_model)
    starts = [0]
    for s in sizes[:-1]:
        starts.append(starts[-1] + s)
    names = ('cq', 'ckv', 'kr', 'u', 'gq', 'gk', 'gv', 'gates')
    orig = dict(zip(names, zip(starts, sizes)))
    order = ('cq', 'ckv', 'u', 'gq', 'gk', 'gv', 'gates', 'kr')
    return orig, order


def _permute_w_in(blocks, d_model):
    orig, order = _w_in_layout(d_model)
    w = blocks.reshape(-1, blocks.shape[2])
    rows = [w[orig[k][0]:orig[k][0] + orig[k][1]] for k in order]
    width = sum(orig[k][1] for k in order)
    return jnp.pad(jnp.concatenate(rows, axis=0), ((0, _pad_to(width, LANES) - width), (0, 0)))


def _unpermute_w_in(gp, d_model, r):
    orig, order = _w_in_layout(d_model)
    pos, off = {}, 0
    for k in order:
        pos[k] = off
        off += orig[k][1]
    names = sorted(orig, key=lambda k: orig[k][0])
    w = jnp.concatenate([gp[pos[k]:pos[k] + orig[k][1]] for k in names], axis=0)
    return w.reshape(N_DEV, r, gp.shape[1])


def _forward_loss(x_all, sinks, rp, wl, cc_in, target, c_len):
    n, d = x_all.shape
    t_len = n - c_len
    depth = len(wl)
    mla_attn = _make_mla(c_len)
    norm_mod = _make_norm(c_len, True)
    norm_tok = _make_norm(_tile(n, 512, SUBLANES), False)
    norm_out = _make_norm(_tile(t_len, 512, SUBLANES), False)
    half_res = _make_gated_res(c_len, 0.5)
    full_res = _make_gated_res(c_len, 1.0)
    gqa_band = _make_gqa(True, c_len)
    gqa_ctx = _make_gqa(False, c_len)
    scans = (_make_scan(False, c_len), _make_scan(True, c_len))
    cos_m, sin_m = _rope_tables(c_len, t_len, MLA_ROPE)
    cos_g, sin_g = _rope_tables(c_len, t_len, GQA_D)
    orig, order = _w_in_layout(d)
    split_in = _make_split([orig[k][1] for k in order], _pad_to(sum(orig[k][1] for k in order), LANES))
    halves = lambda a: _make_split([a.shape[1] // 2] * 2, a.shape[1])(a)

    cc = jnp.zeros((SUBLANES, d), F32).at[0].set(jax.nn.silu(rp['c_ctx'])).at[1].set(jax.nn.silu(cc_in))

    for l in range(depth):
        w, sk = wl[l], sinks[l]
        ctx_out = l < depth - 1

        def mm(h, name):
            return (matmul if name in ROW_SHARDED else matmul_t)(h, w[name], sk[name])

        def swiglu(h, name13, name2):
            a, b = halves(mm(h, name13))
            return mm(jax.nn.silu(a) * b, name2)

        mod = mm(cc, 'ada_w') + rp['ada_b'][l][None, :]
        md = [mod[0:2, i * d:(i + 1) * d] for i in range(N_MOD)]
        x_all = half_res(x_all, swiglu(norm_mod(x_all, rp['norm_ffn1'][l], md[0], md[1]), 'ffn1_w13', 'ffn1_w2'), md[2])

        z = mm(norm_mod(x_all, rp['norm_mix'][l], md[3], md[4]), 'w_in')
        part = dict(zip(order, split_in(z)))

        q3 = mm(norm_tok(part['cq'], rp['mla_q_norm'][l]), 'mla_w_uq').reshape(n, MLA_HEADS, LANES)
        q = jnp.concatenate([q3[..., :MLA_NOPE], _axial_rope(q3[..., MLA_NOPE:MLA_NOPE + MLA_ROPE], cos_m, sin_m),
                             q3[..., MLA_NOPE + MLA_ROPE:]], axis=-1).reshape(n, MLA_HEADS * LANES)
        kvp = mm(norm_tok(part['ckv'], rp['mla_kv_norm'][l]), 'mla_w_ukv')
        kr = _axial_rope(part['kr'].reshape(n, 1, MLA_ROPE), cos_m, sin_m).reshape(n, MLA_ROPE)
        kr = jnp.pad(kr, ((0, 0), (MLA_NOPE, LANES - MLA_NOPE - MLA_ROPE)))
        mla = mm(mla_attn(q, kvp, kr), 'mla_w_o')

        u = part['u']
        y = u * rp['ssm_d'][l][None, :]
        for direction in range(2):
            a_re, a_im, bb_re, bb_im = _ssm_discretize(
                rp['ssm_lambda_re'][l, direction], rp['ssm_lambda_im'][l, direction], rp['ssm_log_dt'][l, direction],
                rp['ssm_b_re'][l, direction], rp['ssm_b_im'][l, direction])
            s_re, s_im = scans[direction](matmul_d(u, _block_diag_in(bb_re)), matmul_d(u, _block_diag_in(bb_im)),
                                          a_re.reshape(-1), a_im.reshape(-1))
            y = y + (matmul_d(s_re, _block_diag_out(rp['ssm_c_re'][l, direction]))
                     - matmul_d(s_im, _block_diag_out(rp['ssm_c_im'][l, direction])))
        yg = mm(jax.nn.gelu(y), 'ssm_w_glu')
        ya, ygate = halves(yg)
        ssm = ya * jax.nn.sigmoid(ygate)

        gq = _axial_rope(part['gq'].reshape(n, GQA_HEADS, GQA_D), cos_g, sin_g)
        gk = _axial_rope(part['gk'].reshape(n, GQA_KV, GQA_D), cos_g, sin_g)
        q4 = jnp.transpose(gq.reshape(n, GQA_KV, GQA_G, GQA_D), (1, 2, 0, 3))
        k2 = jnp.transpose(gk, (1, 0, 2))
        v2 = jnp.transpose(part['gv'].reshape(n, GQA_KV, GQA_D), (1, 0, 2))
        sink = rp['gqa_sink'][l].reshape(GQA_KV, GQA_G, 1, 1)
        sink_rows = lambda rows: jnp.broadcast_to(sink, (GQA_KV, GQA_G, rows, 1)).reshape(GQA_KV, GQA_G * rows, 1)
        g_lat = gqa_band(q4[:, :, c_len:], k2, v2, sink_rows(BLOCK))
        if ctx_out:
            g_ctx = gqa_ctx(q4[:, :, :c_len], k2[:, :c_len], v2[:, :c_len], sink_rows(c_len))
        else:
            g_ctx = jnp.zeros((GQA_KV, GQA_G, c_len, GQA_D), F32)
        go = jnp.transpose(jnp.concatenate([g_ctx, g_lat], axis=2), (2, 0, 1, 3)).reshape(n, GQA_HEADS * GQA_D)
        gqa = mm(go, 'gqa_w_o')

        g0, g1, g2 = _make_split([d] * 3, 3 * d)(jax.nn.sigmoid(part['gates']))
        mixed = g0 * mla + g1 * ssm + g2 * gqa
        x_all = full_res(x_all, mm(mixed, 'w_out'), md[5])
        x_all = half_res(x_all, swiglu(norm_mod(x_all, rp['norm_ffn2'][l], md[6], md[7]), 'ffn2_w13', 'ffn2_w2'), md[8])

    return loss_head(norm_out(x_all[c_len:], rp['final_norm']), target)


def kernel(x, c, ctx, c_ctx, ada_w, ada_b, norm_ffn1, norm_mix, norm_ffn2, ffn1_w13, ffn1_w2, ffn2_w13, ffn2_w2, w_in, mla_q_norm, mla_kv_norm, mla_w_uq, mla_w_ukv, mla_w_o, ssm_lambda_re, ssm_lambda_im, ssm_log_dt, ssm_b_re, ssm_b_im, ssm_c_re, ssm_c_im, ssm_d, ssm_w_glu, gqa_sink, gqa_w_o, w_out, final_norm, loss_target, m_c_ctx, m_ada_w, m_ada_b, m_norm_ffn1, m_norm_mix, m_norm_ffn2, m_ffn1_w13, m_ffn1_w2, m_ffn2_w13, m_ffn2_w2, m_w_in, m_mla_q_norm, m_mla_kv_norm, m_mla_w_uq, m_mla_w_ukv, m_mla_w_o, m_ssm_lambda_re, m_ssm_lambda_im, m_ssm_log_dt, m_ssm_b_re, m_ssm_b_im, m_ssm_c_re, m_ssm_c_im, m_ssm_d, m_ssm_w_glu, m_gqa_sink, m_gqa_w_o, m_w_out, m_final_norm, v_c_ctx, v_ada_w, v_ada_b, v_norm_ffn1, v_norm_mix, v_norm_ffn2, v_ffn1_w13, v_ffn1_w2, v_ffn2_w13, v_ffn2_w2, v_w_in, v_mla_q_norm, v_mla_kv_norm, v_mla_w_uq, v_mla_w_ukv, v_mla_w_o, v_ssm_lambda_re, v_ssm_lambda_im, v_ssm_log_dt, v_ssm_b_re, v_ssm_b_im, v_ssm_c_re, v_ssm_c_im, v_ssm_d, v_ssm_w_glu, v_gqa_sink, v_gqa_w_o, v_w_out, v_final_norm):
    args = dict(locals())
    weights = {k: args[k] for k in WEIGHTS}
    moments_m = {k: args['m_' + k] for k in WEIGHTS}
    moments_v = {k: args['v_' + k] for k in WEIGHTS}
    depth = ada_w.shape[0]
    d = x.shape[-1]
    c_len = ctx.shape[1]
    my_c = lax.axis_index('c')
    my_chip = 2 * lax.axis_index('x') + lax.axis_index('y')
    qk_w = MLA_NOPE + MLA_ROPE

    def as_rows(k, a):
        return a if k in ROW_SHARDED else a.T

    shard_shapes = [as_rows(k, weights[k][0]).shape for k in SHARDED]
    r_in = weights['w_in'].shape[2]
    layers = []
    for l in range(depth):
        gathered = _all_gather(_pack([as_rows(k, weights[k][l]) for k in SHARDED], (), BF16))
        full = {}
        for k, blocks in zip(SHARDED, _unpack(gathered, (N_DEV,), shard_shapes)):
            full[k] = _permute_w_in(blocks, d) if k == 'w_in' else blocks.reshape(N_DEV * blocks.shape[1], blocks.shape[2])
        full['mla_w_uq'] = jnp.pad(full['mla_w_uq'].reshape(MLA_HEADS, qk_w, MLA_Q_RANK),
                                   ((0, 0), (0, LANES - qk_w), (0, 0))).reshape(MLA_HEADS * LANES, MLA_Q_RANK)
        full['mla_w_o'] = jnp.pad(full['mla_w_o'].reshape(d, MLA_HEADS, MLA_V),
                                  ((0, 0), (0, 0), (LANES - MLA_V, 0))).reshape(d, MLA_HEADS * LANES)
        layers.append(full)
    sinks = [{k: jnp.zeros(v.shape, F32) for k, v in full.items()} for full in layers]

    rp = {k: weights[k] for k in REPLICATED}
    x_all = jnp.concatenate([ctx[0], x[0]], axis=0)
    loss_fn = functools.partial(_forward_loss, wl=layers, cc_in=c[0], target=loss_target[0], c_len=c_len)
    loss, vjp = jax.vjp(loss_fn, x_all, sinks, rp)
    g_x, g_layers, g_rp = vjp(jnp.ones((), F32))

    grads = {k: [] for k in SHARDED}
    for l in range(depth):
        gl = dict(g_layers[l])
        gl['mla_w_uq'] = gl['mla_w_uq'].reshape(MLA_HEADS, LANES, MLA_Q_RANK)[:, :qk_w].reshape(MLA_HEADS * qk_w, MLA_Q_RANK)
        gl['mla_w_o'] = gl['mla_w_o'].reshape(d, MLA_HEADS, LANES)[:, :, LANES - MLA_V:].reshape(d, MLA_HEADS * MLA_V)
        pieces = [_unpermute_w_in(gl[k], d, r_in) if k == 'w_in' else gl[k].reshape((N_DEV,) + shape)
                  for k, shape in zip(SHARDED, shard_shapes)]
        packed = _pack(pieces, (N_DEV,), F32)
        g_all = jnp.swapaxes(packed.reshape((4, 2) + packed.shape[1:]), 0, 1)
        mine = _reduce_scatter(g_all, my_c.reshape(1).astype(jnp.int32), my_chip.reshape(1).astype(jnp.int32))
        for k, g in zip(SHARDED, _unpack(mine, (), shard_shapes)):
            grads[k].append(as_rows(k, g))
    grads = {k: jnp.stack(v) for k, v in grads.items()}

    rep_shapes = [weights[k].shape for k in REPLICATED] + [(1,)]
    small = _pack([g_rp[k] for k in REPLICATED] + [loss.reshape(1)], (), F32)
    summed = _unpack(_sum_leading(_all_gather(small)), (), rep_shapes)
    for k, g in zip(REPLICATED, summed[:-1]):
        grads[k] = g
    loss_total = summed[-1].reshape(())

    delta, new_m, new_v = {}, {}, {}
    for k in SHARDED:
        shape = weights[k].shape
        as2d = lambda a: a.reshape(-1, shape[-1])
        outs = _adamw_call(as2d(weights[k]), as2d(grads[k]), as2d(moments_m[k]), as2d(moments_v[k]))
        delta[k], new_m[k], new_v[k] = (o.reshape(shape) for o in outs)
    rep_all = [weights[k].shape for k in REPLICATED]
    packs = [_pack([src[k] for k in REPLICATED], (), F32) for src in (weights, grads, moments_m, moments_v)]
    outs = [_unpack(o, (), rep_all) for o in _adamw_call(*packs)]
    for i, k in enumerate(REPLICATED):
        delta[k], new_m[k], new_v[k] = outs[0][i], outs[1][i], outs[2][i]

    return (loss_total, g_x[c_len:][None], *[grads[k] for k in WEIGHTS], *[delta[k] for k in WEIGHTS],
            *[new_m[k] for k in WEIGHTS], *[new_v[k] for k in WEIGHTS])
```

```python
import functools
import math

import jax
import jax.numpy as jnp
from jax import lax
from jax.experimental import pallas as pl
from jax.experimental.pallas import tpu as pltpu

F32 = jnp.float32
BF16 = jnp.bfloat16

MLA_HEADS, MLA_NOPE, MLA_ROPE, MLA_V = 8, 64, 32, 64
MLA_Q_RANK, MLA_KV_RANK = 384, 256
SSM_WIDTH, SSM_GROUP, SSM_STATE = 512, 16, 64
SSM_GROUPS = SSM_WIDTH // SSM_GROUP
GQA_HEADS, GQA_KV, GQA_D = 8, 2, 64
GQA_G = GQA_HEADS // GQA_KV
WINDOW, BLOCK, GRID_W = 128, 128, 64
N_MOD = 9
ROPE_BASE = 10000.0
EPS = 1e-6
NEG_INF = -1e30
ADAM_LR, ADAM_B1, ADAM_B2, ADAM_EPS, ADAM_WD, ADAM_STEP = 0.001, 0.9, 0.999, 1e-08, 0.01, 10

N_DEV = 8
LANES = 128
SUBLANES = 8
VMEM_LIMIT = 56 * 1024 * 1024
PACK_ROWS = 1024

SHARDED = ('ada_w', 'ffn1_w13', 'ffn1_w2', 'ffn2_w13', 'ffn2_w2', 'w_in', 'mla_w_uq', 'mla_w_ukv',
           'mla_w_o', 'ssm_w_glu', 'gqa_w_o', 'w_out')
ROW_SHARDED = ('ffn1_w2', 'ffn2_w2', 'w_out')
REPLICATED = ('c_ctx', 'ada_b', 'norm_ffn1', 'norm_mix', 'norm_ffn2', 'mla_q_norm', 'mla_kv_norm',
              'ssm_lambda_re', 'ssm_lambda_im', 'ssm_log_dt', 'ssm_b_re', 'ssm_b_im', 'ssm_c_re', 'ssm_c_im',
              'ssm_d', 'gqa_sink', 'final_norm')
WEIGHTS = ('c_ctx', 'ada_w', 'ada_b', 'norm_ffn1', 'norm_mix', 'norm_ffn2', 'ffn1_w13', 'ffn1_w2', 'ffn2_w13',
           'ffn2_w2', 'w_in', 'mla_q_norm', 'mla_kv_norm', 'mla_w_uq', 'mla_w_ukv', 'mla_w_o', 'ssm_lambda_re',
           'ssm_lambda_im', 'ssm_log_dt', 'ssm_b_re', 'ssm_b_im', 'ssm_c_re', 'ssm_c_im', 'ssm_d', 'ssm_w_glu',
           'gqa_sink', 'gqa_w_o', 'w_out', 'final_norm')


def _tile(n, target, mult):
    t = (min(target, n) // mult) * mult
    while t >= mult:
        if n % t == 0:
            return t
        t -= mult
    return n


def _cparams(sem):
    return pltpu.CompilerParams(dimension_semantics=sem, vmem_limit_bytes=VMEM_LIMIT)


def _mm(a, b, mode):
    if mode == 'nn':
        (M, K), N = a.shape, b.shape[1]
        tm, tn, tk = _tile(M, 1408, SUBLANES), _tile(N, 1024, LANES), _tile(K, 1408, LANES)
    elif mode == 'nt':
        (M, K), N = a.shape, b.shape[0]
        tm, tn, tk = _tile(M, 1408, SUBLANES), _tile(N, 1024, LANES), _tile(K, 1408, LANES)
    else:
        (K, M), N = a.shape, b.shape[1]
        tm, tn, tk = _tile(M, 1408, LANES), _tile(N, 1408, LANES), _tile(K, 768, 2 * SUBLANES)
    nk = K // tk
    dims = {'nn': (((1,), (0,)), ((), ())), 'nt': (((1,), (1,)), ((), ())), 'tn': (((0,), (0,)), ((), ()))}[mode]
    keep_a = nk == 1 and mode != 'tn' and N // tn > 1 and a.dtype != BF16

    def body(a_ref, b_ref, o_ref, *scratch):
        if keep_a:
            @pl.when(pl.program_id(1) == 0)
            def _():
                scratch[0][...] = a_ref[...].astype(BF16)

            av = scratch[0][...]
        else:
            av = a_ref[...].astype(BF16)
        part = lax.dot_general(av, b_ref[...].astype(BF16), dims, preferred_element_type=F32)
        if nk == 1:
            o_ref[...] = part
        else:
            @pl.when(pl.program_id(2) == 0)
            def _():
                o_ref[...] = part

            @pl.when(pl.program_id(2) > 0)
            def _():
                o_ref[...] += part

    a_spec = pl.BlockSpec((tk, tm), lambda i, j, k: (k, i)) if mode == 'tn' else pl.BlockSpec((tm, tk), lambda i, j, k: (i, k))
    b_spec = pl.BlockSpec((tn, tk), lambda i, j, k: (j, k)) if mode == 'nt' else pl.BlockSpec((tk, tn), lambda i, j, k: (k, j))
    return pl.pallas_call(
        body, name='mm_' + mode, grid=(M // tm, N // tn, nk),
        in_specs=[a_spec, b_spec], out_specs=pl.BlockSpec((tm, tn), lambda i, j, k: (i, j)),
        out_shape=jax.ShapeDtypeStruct((M, N), F32),
        scratch_shapes=[pltpu.VMEM((tm, tk), BF16)] if keep_a else [],
        compiler_params=_cparams(('parallel', 'arbitrary', 'arbitrary')),
    )(a, b)


@jax.custom_vjp
def matmul_d(x, w):
    return _mm(x, w, 'nn')


def _matmul_d_fwd(x, w):
    return _mm(x, w, 'nn'), (x, w)


def _matmul_d_bwd(res, g):
    x, w = res
    return _mm(g, w, 'nt'), _mm(x, g, 'tn')


matmul_d.defvjp(_matmul_d_fwd, _matmul_d_bwd)


@jax.custom_vjp
def matmul(x, w, sink):
    return _mm(x, w, 'nn')


def _matmul_fwd(x, w, sink):
    return _mm(x, w, 'nn'), (x, w)


def _matmul_bwd(res, g):
    x, w = res
    return _mm(g, w, 'nt'), jnp.zeros_like(w), _mm(x, g, 'tn')


matmul.defvjp(_matmul_fwd, _matmul_bwd)


@jax.custom_vjp
def matmul_t(x, wt, sink):
    return _mm(x, wt, 'nt')


def _matmul_t_fwd(x, wt, sink):
    return _mm(x, wt, 'nt'), (x, wt)


def _matmul_t_bwd(res, g):
    x, wt = res
    return _mm(g, wt, 'nn'), jnp.zeros_like(wt), _mm(g, x, 'tn')


matmul_t.defvjp(_matmul_t_fwd, _matmul_t_bwd)


def _ffn_up_call(x, w13t):
    m, k = x.shape
    f = w13t.shape[0] // 2
    tm, tn = _tile(m, 1408, SUBLANES), _tile(f, 256, LANES)
    nf = f // tn

    def body(x_ref, wa_ref, wb_ref, a_ref, b_ref, act_ref, xb):
        @pl.when(pl.program_id(1) == 0)
        def _():
            xb[...] = x_ref[...].astype(BF16)

        a = lax.dot_general(xb[...], wa_ref[...], NT_DIMS, preferred_element_type=F32)
        b = lax.dot_general(xb[...], wb_ref[...], NT_DIMS, preferred_element_type=F32)
        a_ref[...] = a
        b_ref[...] = b
        act_ref[...] = (a * jax.nn.sigmoid(a) * b).astype(BF16)

    out = pl.BlockSpec((tm, tn), lambda i, j: (i, j))
    return pl.pallas_call(
        body, name='ffn_up', grid=(m // tm, nf),
        in_specs=[pl.BlockSpec((tm, k), lambda i, j: (i, 0)), pl.BlockSpec((tn, k), lambda i, j: (j, 0)),
                  pl.BlockSpec((tn, k), lambda i, j: (j + nf, 0))],
        out_specs=[out, out, out],
        out_shape=[jax.ShapeDtypeStruct((m, f), F32), jax.ShapeDtypeStruct((m, f), F32), jax.ShapeDtypeStruct((m, f), BF16)],
        scratch_shapes=[pltpu.VMEM((tm, k), BF16)],
        compiler_params=_cparams(('parallel', 'arbitrary')),
    )(x, w13t, w13t)


@jax.custom_vjp
def swiglu_ffn(x, w13t, sink13, w2, sink2):
    return _mm(_ffn_up_call(x, w13t)[2], w2, 'nn')


def _swiglu_ffn_fwd(x, w13t, sink13, w2, sink2):
    a, b, act = _ffn_up_call(x, w13t)
    return _mm(act, w2, 'nn'), (x, a, b, act, w13t, w2)


def _swiglu_ffn_bwd(res, g):
    x, a, b, act, w13t, w2 = res
    dact = _mm(g, w2, 'nt')
    sa = jax.nn.sigmoid(a)
    d13 = jnp.concatenate([dact * b * (sa * (1.0 + a * (1.0 - sa))), dact * (a * sa)], axis=1)
    return _mm(d13, w13t, 'nn'), jnp.zeros_like(w13t), _mm(d13, x, 'tn'), jnp.zeros_like(w2), _mm(act, g, 'tn')


swiglu_ffn.defvjp(_swiglu_ffn_fwd, _swiglu_ffn_bwd)


def _norm_fwd_call(x, g, sh, sc, rb):
    n, d = x.shape
    has_mod = sh is not None

    def body(*refs):
        x_ref, g_ref = refs[0], refs[1]
        o_ref = refs[-1]
        xv = x_ref[...]
        r = lax.rsqrt(jnp.mean(xv * xv, axis=-1, keepdims=True) + EPS)
        y = xv * r * g_ref[...]
        if has_mod:
            lat = pl.program_id(0) > 0
            shv = jnp.where(lat, refs[2][1:2, :], refs[2][0:1, :])
            scv = jnp.where(lat, refs[3][1:2, :], refs[3][0:1, :])
            y = y * (1.0 + scv) + shv
        o_ref[...] = y

    row = pl.BlockSpec((rb, d), lambda i: (i, 0))
    vec = pl.BlockSpec((1, d), lambda i: (0, 0))
    two = pl.BlockSpec((2, d), lambda i: (0, 0))
    ins = [x, g.reshape(1, d)] + ([sh, sc] if has_mod else [])
    return pl.pallas_call(
        body, name='norm_fwd', grid=(n // rb,), in_specs=[row, vec] + ([two, two] if has_mod else []),
        out_specs=row, out_shape=jax.ShapeDtypeStruct((n, d), F32), compiler_params=_cparams(('parallel',)),
    )(*ins)


def _norm_bwd_call(x, g, sh, sc, dy, rb):
    n, d = x.shape
    has_mod = sh is not None

    def body(*refs):
        x_ref, g_ref, dy_ref = refs[0], refs[1], refs[-3]
        dx_ref, acc_ref = refs[-2], refs[-1]
        i = pl.program_id(0)
        xv, dyv, gv = x_ref[...], dy_ref[...], g_ref[...]
        r = lax.rsqrt(jnp.mean(xv * xv, axis=-1, keepdims=True) + EPS)
        xh = xv * r
        if has_mod:
            lat = i > 0
            scv = jnp.where(lat, refs[3][1:2, :], refs[3][0:1, :])
            dyg = dyv * (1.0 + scv)
        else:
            dyg = dyv
        dxh = dyg * gv
        dx_ref[...] = r * (dxh - xh * jnp.mean(dxh * xh, axis=-1, keepdims=True))

        @pl.when(i == 0)
        def _():
            acc_ref[...] = jnp.zeros_like(acc_ref)

        acc_ref[0:1, :] += jnp.sum(dyg * xh, axis=0, keepdims=True)
        if has_mod:
            dsh = jnp.sum(dyv, axis=0, keepdims=True)
            dsc = jnp.sum(dyv * xh * gv, axis=0, keepdims=True)

            @pl.when(i == 0)
            def _():
                acc_ref[1:2, :] += dsh
                acc_ref[3:4, :] += dsc

            @pl.when(i > 0)
            def _():
                acc_ref[2:3, :] += dsh
                acc_ref[4:5, :] += dsc

    row = pl.BlockSpec((rb, d), lambda i: (i, 0))
    vec = pl.BlockSpec((1, d), lambda i: (0, 0))
    two = pl.BlockSpec((2, d), lambda i: (0, 0))
    ins = [x, g.reshape(1, d)] + ([sh, sc] if has_mod else []) + [dy]
    return pl.pallas_call(
        body, name='norm_bwd', grid=(n // rb,), in_specs=[row, vec] + ([two, two] if has_mod else []) + [row],
        out_specs=[row, pl.BlockSpec((SUBLANES, d), lambda i: (0, 0))],
        out_shape=[jax.ShapeDtypeStruct((n, d), F32), jax.ShapeDtypeStruct((SUBLANES, d), F32)],
        compiler_params=_cparams(('arbitrary',)),
    )(*ins)


def _make_norm(rb, has_mod):
    if has_mod:
        @jax.custom_vjp
        def f(x, g, sh, sc):
            return _norm_fwd_call(x, g, sh, sc, rb)

        def fwd(x, g, sh, sc):
            return _norm_fwd_call(x, g, sh, sc, rb), (x, g, sh, sc)

        def bwd(res, dy):
            x, g, sh, sc = res
            dx, acc = _norm_bwd_call(x, g, sh, sc, dy, rb)
            return dx, acc[0], acc[1:3], acc[3:5]
    else:
        @jax.custom_vjp
        def f(x, g):
            return _norm_fwd_call(x, g, None, None, rb)

        def fwd(x, g):
            return _norm_fwd_call(x, g, None, None, rb), (x, g)

        def bwd(res, dy):
            x, g = res
            dx, acc = _norm_bwd_call(x, g, None, None, dy, rb)
            return dx, acc[0]
    f.defvjp(fwd, bwd)
    return f


def _make_gated_res(rb, coef):
    def fwd_call(x, f, gate):
        n, d = x.shape

        def body(x_ref, f_ref, g_ref, o_ref):
            gv = jnp.where(pl.program_id(0) > 0, g_ref[1:2, :], g_ref[0:1, :])
            o_ref[...] = x_ref[...] + coef * gv * f_ref[...]

        row = pl.BlockSpec((rb, d), lambda i: (i, 0))
        return pl.pallas_call(
            body, name='gated_res_fwd', grid=(n // rb,), in_specs=[row, row, pl.BlockSpec((2, d), lambda i: (0, 0))],
            out_specs=row, out_shape=jax.ShapeDtypeStruct((n, d), F32), compiler_params=_cparams(('parallel',)),
        )(x, f, gate)

    def bwd_call(dy, f, gate):
        n, d = dy.shape

        def body(dy_ref, f_ref, g_ref, df_ref, acc_ref):
            i = pl.program_id(0)
            gv = jnp.where(i > 0, g_ref[1:2, :], g_ref[0:1, :])
            dyv = dy_ref[...]
            df_ref[...] = coef * gv * dyv
            part = coef * jnp.sum(dyv * f_ref[...], axis=0, keepdims=True)

            @pl.when(i == 0)
            def _():
                acc_ref[...] = jnp.zeros_like(acc_ref)
                acc_ref[0:1, :] += part

            @pl.when(i > 0)
            def _():
                acc_ref[1:2, :] += part

        row = pl.BlockSpec((rb, d), lambda i: (i, 0))
        return pl.pallas_call(
            body, name='gated_res_bwd', grid=(n // rb,), in_specs=[row, row, pl.BlockSpec((2, d), lambda i: (0, 0))],
            out_specs=[row, pl.BlockSpec((SUBLANES, d), lambda i: (0, 0))],
            out_shape=[jax.ShapeDtypeStruct((n, d), F32), jax.ShapeDtypeStruct((SUBLANES, d), F32)],
            compiler_params=_cparams(('arbitrary',)),
        )(dy, f, gate)

    @jax.custom_vjp
    def f(x, fv, gate):
        return fwd_call(x, fv, gate)

    def fwd(x, fv, gate):
        return fwd_call(x, fv, gate), (fv, gate)

    def bwd(res, dy):
        fv, gate = res
        df, acc = bwd_call(dy, fv, gate)
        return dy, df, acc[0:2]

    f.defvjp(fwd, bwd)
    return f


MLA_SCALE = (MLA_NOPE + MLA_ROPE) ** -0.5
LOG2E = math.log2(math.e)
MLA_SCALE_LOG2E = MLA_SCALE * LOG2E
NT_DIMS = (((1,), (1,)), ((), ()))


def _mla_keys(kv, kr):
    lane = lax.broadcasted_iota(jnp.int32, kv.shape, 1)
    return jnp.where(lane < MLA_NOPE, kv, kr)


def _chunks(start, stop, target):
    size = _tile(stop - start, target, LANES)
    return [(start + t * size, size) for t in range((stop - start) // size)]


MLA_CHUNK = 2816


def _mla_fwd_call(q, kv, kr, c_len):
    n = q.shape[0]
    h = q.shape[1] // LANES
    tq = c_len

    def body(q_ref, kv_ref, kr_ref, o_ref, lse_ref, kb, vb):
        i = pl.program_id(1)

        @pl.when(i == 0)
        def _():
            kvv = kv_ref[...]
            kb[...] = _mla_keys(kvv, kr_ref[...])
            vb[...] = jnp.where(lax.broadcasted_iota(jnp.int32, kvv.shape, 1) < MLA_NOPE, jnp.ones_like(kvv), kvv)

        qv = q_ref[...].astype(BF16)

        def attend(nk):
            s = lax.dot_general(qv, kb[:nk, :], NT_DIMS, preferred_element_type=F32)
            m = jnp.max(s, axis=-1, keepdims=True)
            p = jnp.exp2((s - m) * MLA_SCALE_LOG2E)
            acc = jnp.dot(p.astype(BF16), vb[:nk, :], preferred_element_type=F32)
            l = acc[:, 0:1]
            o_ref[...] = acc / l
            lse_ref[0] = m * MLA_SCALE + jnp.log(l)

        pl.when(i == 0)(lambda: attend(c_len))
        pl.when(i > 0)(lambda: attend(n))

    qspec = pl.BlockSpec((tq, LANES), lambda a, i: (i, a))
    return pl.pallas_call(
        body, name='mla_attn_fwd', grid=(h, n // tq),
        in_specs=[qspec, pl.BlockSpec((n, LANES), lambda a, i: (0, a)), pl.BlockSpec((n, LANES), lambda a, i: (0, 0))],
        out_specs=[qspec, pl.BlockSpec((1, tq, 1), lambda a, i: (a, i, 0))],
        out_shape=[jax.ShapeDtypeStruct((n, h * LANES), F32), jax.ShapeDtypeStruct((h, n, 1), F32)],
        scratch_shapes=[pltpu.VMEM((n, LANES), BF16), pltpu.VMEM((n, LANES), BF16)],
        compiler_params=_cparams(('arbitrary', 'arbitrary')),
    )(q, kv, kr)


TN_DIMS = (((0,), (0,)), ((), ()))


def _mla_bwd_call(q, kv, kr, o, lse, do, c_len):
    n = q.shape[0]
    h = q.shape[1] // LANES
    tq = c_len
    nq = n // tq

    def body(q_ref, kv_ref, kr_ref, o_ref, lse_ref, do_ref, dq_ref, dv_ref, dk_ref, kb):
        i = pl.program_id(1)

        @pl.when(i == 0)
        def _():
            kb[...] = _mla_keys(kv_ref[...], kr_ref[...])
            dv_ref[...] = jnp.zeros_like(dv_ref)
            dk_ref[...] = jnp.zeros_like(dk_ref)

        dov = do_ref[...]
        delta = jnp.sum(dov * o_ref[...], axis=-1, keepdims=True)
        qv, dob, lse2 = q_ref[...], dov.astype(BF16), lse_ref[0] * LOG2E

        def grad(chunks):
            acc = None
            for k0, kc in chunks:
                keys = kb[k0:k0 + kc, :]
                s = lax.dot_general(qv, keys, NT_DIMS, preferred_element_type=F32)
                p = jnp.exp2(s * MLA_SCALE_LOG2E - lse2)
                dp = lax.dot_general(dob, kv_ref[k0:k0 + kc, :], NT_DIMS, preferred_element_type=F32)
                ds = (p * (dp - delta)).astype(BF16)
                part = jnp.dot(ds, keys, preferred_element_type=F32)
                acc = part if acc is None else acc + part
                dk_ref[0, k0:k0 + kc, :] += lax.dot_general(ds, qv, TN_DIMS, preferred_element_type=F32)
                dv_ref[k0:k0 + kc, :] += lax.dot_general(p.astype(BF16), dob, TN_DIMS, preferred_element_type=F32)
            dq_ref[...] = acc * MLA_SCALE

        pl.when(i == 0)(lambda: grad([(0, c_len)]))
        pl.when(i > 0)(lambda: grad(_chunks(0, n, MLA_CHUNK)))

        @pl.when(i == nq - 1)
        def _():
            dk = dk_ref[0] * MLA_SCALE
            dk_ref[0] = dk
            dv_ref[...] = jnp.where(lax.broadcasted_iota(jnp.int32, dk.shape, 1) < MLA_NOPE, dk, dv_ref[...])

    qspec = pl.BlockSpec((tq, LANES), lambda a, i: (i, a))
    full = pl.BlockSpec((n, LANES), lambda a, i: (0, a))
    return pl.pallas_call(
        body, name='mla_attn_bwd', grid=(h, nq),
        in_specs=[qspec, full, pl.BlockSpec((n, LANES), lambda a, i: (0, 0)), qspec,
                  pl.BlockSpec((1, tq, 1), lambda a, i: (a, i, 0)), qspec],
        out_specs=[qspec, full, pl.BlockSpec((1, n, LANES), lambda a, i: (a, 0, 0))],
        out_shape=[jax.ShapeDtypeStruct((n, h * LANES), F32), jax.ShapeDtypeStruct((n, h * LANES), F32),
                   jax.ShapeDtypeStruct((h, n, LANES), F32)],
        scratch_shapes=[pltpu.VMEM((n, LANES), BF16)],
        compiler_params=_cparams(('arbitrary', 'arbitrary')),
    )(q, kv, kr, o, lse, do)


def _sum_leading(g):
    nl, r, _ = g.shape
    rb = _tile(r, 512, SUBLANES)

    def body(g_ref, o_ref):
        acc = g_ref[0]
        for j in range(1, nl):
            acc = acc + g_ref[j]
        o_ref[...] = acc

    return pl.pallas_call(
        body, name='sum_leading', grid=(r // rb,), in_specs=[pl.BlockSpec((nl, rb, LANES), lambda i: (0, i, 0))],
        out_specs=pl.BlockSpec((rb, LANES), lambda i: (i, 0)), out_shape=jax.ShapeDtypeStruct((r, LANES), F32),
        compiler_params=_cparams(('parallel',)),
    )(g)


def _make_mla(c_len):
    @jax.custom_vjp
    def f(q, kv, kr):
        return _mla_fwd_call(q, kv.astype(BF16), kr.astype(BF16), c_len)[0]

    def fwd(q, kv, kr):
        kvb, krb = kv.astype(BF16), kr.astype(BF16)
        o, lse = _mla_fwd_call(q, kvb, krb, c_len)
        return o, (q.astype(BF16), kvb, krb, o, lse)

    def bwd(res, do):
        q, kv, kr, o, lse = res
        dq, dkv, dk_full = _mla_bwd_call(q, kv, kr, o, lse, do, c_len)
        return dq, dkv, _sum_leading(dk_full)

    f.defvjp(fwd, bwd)
    return f


def _gqa_specs(band, c_len, nb, rows):
    cb = c_len // BLOCK
    q_spec = pl.BlockSpec((1, GQA_G, rows, GQA_D), lambda a, b: (a, 0, b, 0))
    ctx_spec = pl.BlockSpec((1, c_len, GQA_D), lambda a, b: (a, 0, 0))
    kv_specs = [ctx_spec]
    if band:
        kv_specs += [pl.BlockSpec((1, BLOCK, GQA_D), lambda a, b: (a, jnp.maximum(b - 1, 0) + cb, 0)),
                     pl.BlockSpec((1, BLOCK, GQA_D), lambda a, b: (a, b + cb, 0)),
                     pl.BlockSpec((1, BLOCK, GQA_D), lambda a, b: (a, jnp.minimum(b + 1, nb - 1) + cb, 0))]
    sink_spec = pl.BlockSpec((1, GQA_G * rows, 1), lambda a, b: (a, 0, 0))
    return q_spec, kv_specs, sink_spec


def _gqa_scores(q, kcat, sink, band, c_len, t_len, rows):
    scale = GQA_D ** -0.5
    s = lax.dot_general(q, kcat, (((1,), (1,)), ((), ())), preferred_element_type=F32) * scale
    if band:
        b = pl.program_id(1)
        shape = s.shape
        col = lax.broadcasted_iota(jnp.int32, shape, 1)
        qpos = b * BLOCK + (lax.broadcasted_iota(jnp.int32, shape, 0) & (BLOCK - 1))
        kpos = (b - 1) * BLOCK + (col - c_len)
        valid = (col < c_len) | ((jnp.abs(qpos - kpos) <= WINDOW) & (kpos >= 0) & (kpos < t_len))
        s = jnp.where(valid, s, NEG_INF)
    m = jnp.maximum(jnp.max(s, axis=-1, keepdims=True), sink)
    e = jnp.exp(s - m)
    es = jnp.exp(sink - m)
    den = es + jnp.sum(e, axis=-1, keepdims=True)
    return e / den, es / den


def _gqa_fwd_call(q4, k2, v2, sink_rows, band, c_len):
    kv, g, tq_all, d = q4.shape
    rows = BLOCK if band else tq_all
    nb = tq_all // rows
    t_len = k2.shape[1] - c_len
    nkv = 4 if band else 1
    q_spec, kv_specs, sink_spec = _gqa_specs(band, c_len, nb, rows)

    def body(*refs):
        q_ref, sink_ref, o_ref = refs[0], refs[1 + 2 * nkv], refs[-1]
        kcat = jnp.concatenate([r[0] for r in refs[1:1 + nkv]], axis=0).astype(BF16)
        vcat = jnp.concatenate([r[0] for r in refs[1 + nkv:1 + 2 * nkv]], axis=0).astype(BF16)
        q = q_ref[0].reshape(g * rows, d).astype(BF16)
        p, _ = _gqa_scores(q, kcat, sink_ref[0], band, c_len, t_len, rows)
        o_ref[0] = jnp.dot(p.astype(BF16), vcat, preferred_element_type=F32).reshape(g, rows, d)

    return pl.pallas_call(
        body, name='gqa_fwd_band' if band else 'gqa_fwd_ctx', grid=(kv, nb),
        in_specs=[q_spec] + kv_specs + kv_specs + [sink_spec], out_specs=q_spec,
        out_shape=jax.ShapeDtypeStruct(q4.shape, F32), compiler_params=_cparams(('parallel', 'parallel')),
    )(q4, *([k2] * nkv), *([v2] * nkv), sink_rows)


def _gqa_bwd_call(q4, k2, v2, sink_rows, do4, band, c_len):
    kv, g, tq_all, d = q4.shape
    rows = BLOCK if band else tq_all
    nb = tq_all // rows
    t_len = k2.shape[1] - c_len
    nkv = 4 if band else 1
    scale = GQA_D ** -0.5
    q_spec, kv_specs, sink_spec = _gqa_specs(band, c_len, nb, rows)

    def body(*refs):
        q_ref, sink_ref, do_ref = refs[0], refs[1 + 2 * nkv], refs[2 + 2 * nkv]
        outs = refs[3 + 2 * nkv:]
        dq_ref, dkc_ref, dvc_ref = outs[0], outs[1], outs[2]
        dsink_ref = outs[-1]
        b = pl.program_id(1)
        kcat = jnp.concatenate([r[0] for r in refs[1:1 + nkv]], axis=0).astype(BF16)
        vcat = jnp.concatenate([r[0] for r in refs[1 + nkv:1 + 2 * nkv]], axis=0).astype(BF16)
        q = q_ref[0].reshape(g * rows, d).astype(BF16)
        do = do_ref[0].reshape(g * rows, d).astype(BF16)
        p, p_sink = _gqa_scores(q, kcat, sink_ref[0], band, c_len, t_len, rows)
        dp = lax.dot_general(do, vcat, (((1,), (1,)), ((), ())), preferred_element_type=F32)
        rd = jnp.sum(p * dp, axis=-1, keepdims=True)
        ds = (p * (dp - rd) * scale).astype(BF16)
        dq_ref[0] = jnp.dot(ds, kcat, preferred_element_type=F32).reshape(g, rows, d)
        dkcat = lax.dot_general(ds, q, (((0,), (0,)), ((), ())), preferred_element_type=F32)
        dvcat = lax.dot_general(p.astype(BF16), do, (((0,), (0,)), ((), ())), preferred_element_type=F32)

        @pl.when(b == 0)
        def _():
            dkc_ref[...] = jnp.zeros_like(dkc_ref)
            dvc_ref[...] = jnp.zeros_like(dvc_ref)
            dsink_ref[...] = jnp.zeros_like(dsink_ref)

        dkc_ref[0] += dkcat[:c_len]
        dvc_ref[0] += dvcat[:c_len]
        dsink_ref[0] += -p_sink * rd
        if band:
            outs[3][0, 0] = dkcat[c_len:]
            outs[4][0, 0] = dvcat[c_len:]

    ctx_out = pl.BlockSpec((1, c_len, d), lambda a, b: (a, 0, 0))
    band_out = pl.BlockSpec((1, 1, 3 * BLOCK, d), lambda a, b: (a, b, 0, 0))
    out_specs = [q_spec, ctx_out, ctx_out] + ([band_out, band_out] if band else []) + [sink_spec]
    ctx_shape = jax.ShapeDtypeStruct((kv, c_len, d), F32)
    band_shape = jax.ShapeDtypeStruct((kv, nb, 3 * BLOCK, d), F32)
    out_shape = ([jax.ShapeDtypeStruct(q4.shape, F32), ctx_shape, ctx_shape] + ([band_shape, band_shape] if band else [])
                 + [jax.ShapeDtypeStruct(sink_rows.shape, F32)])
    return pl.pallas_call(
        body, name='gqa_bwd_band' if band else 'gqa_bwd_ctx', grid=(kv, nb),
        in_specs=[q_spec] + kv_specs + kv_specs + [sink_spec, q_spec], out_specs=out_specs, out_shape=out_shape,
        compiler_params=_cparams(('arbitrary', 'arbitrary')),
    )(q4, *([k2] * nkv), *([v2] * nkv), sink_rows, do4)


def _make_gqa(band, c_len):
    @jax.custom_vjp
    def f(q4, k2, v2, sink_rows):
        return _gqa_fwd_call(q4, k2, v2, sink_rows, band, c_len)

    def fwd(q4, k2, v2, sink_rows):
        return _gqa_fwd_call(q4, k2, v2, sink_rows, band, c_len), (q4, k2, v2, sink_rows)

    def bwd(res, do4):
        q4, k2, v2, sink_rows = res
        outs = _gqa_bwd_call(q4, k2, v2, sink_rows, do4, band, c_len)
        kv, n, d = k2.shape
        if not band:
            dq4, dkc, dvc, dsink = outs
            return dq4, dkc, dvc, dsink
        dq4, dkc, dvc, dkb, dvb, dsink = outs

        def fold(ctx_part, bands):
            cur = bands[:, :, BLOCK:2 * BLOCK]
            prv = jnp.pad(bands[:, 1:, :BLOCK], ((0, 0), (0, 1), (0, 0), (0, 0)))
            nxt = jnp.pad(bands[:, :-1, 2 * BLOCK:], ((0, 0), (1, 0), (0, 0), (0, 0)))
            lat = (cur + prv + nxt).reshape(kv, n - c_len, d)
            return jnp.concatenate([ctx_part, lat], axis=1)

        return dq4, fold(dkc, dkb), fold(dvc, dvb), dsink

    f.defvjp(fwd, bwd)
    return f


def _cmul(ar, ai, br, bi):
    return ar * br - ai * bi, ar * bi + ai * br


def _scan_tables(ar, ai, desc):
    a1 = (ar, ai)
    a2 = _cmul(*a1, *a1)
    a4 = _cmul(*a2, *a2)
    pw = [a1]
    for _ in range(SUBLANES - 1):
        pw.append(_cmul(*pw[-1], *a1))
    row = jnp.arange(SUBLANES)[:, None]
    tabs = []
    for dist, (pr, pi) in ((1, a1), (2, a2), (4, a4)):
        keep = (row <= SUBLANES - 1 - dist) if desc else (row >= dist)
        tabs += [jnp.where(keep, pr[None, :], 0.0), jnp.where(keep, pi[None, :], 0.0)]
    order = pw[::-1] if desc else pw
    tabs += [jnp.stack([p[0] for p in order]), jnp.stack([p[1] for p in order])]
    return jnp.stack(tabs).astype(F32)


def _scan_call(b_re, b_im, tabs, order, chunk, prev=None):
    n, s_dim = b_re.shape
    nch = n // chunk
    ng = chunk // SUBLANES
    desc = order in ('Fb', 'R')
    with_da = prev is not None

    def chunk_of(i):
        if order == 'F':
            return i
        if order == 'Fb':
            return nch - 1 - i
        if order == 'R':
            return jnp.where(i == 0, 0, nch - i)
        return jnp.where(i == nch - 1, 0, i + 1)

    def body(*refs):
        br_ref, bi_ref, tab_ref = refs[0], refs[1], refs[2]
        if with_da:
            pr_ref, pi_ref, sr_ref, si_ref, dar_ref, dai_ref, cr_ref, ci_ref = refs[3:]
        else:
            sr_ref, si_ref, pr_ref, pi_ref, cr_ref, ci_ref = refs[3:]

        @pl.when(pl.program_id(0) == 0)
        def _():
            cr_ref[...] = jnp.zeros_like(cr_ref)
            ci_ref[...] = jnp.zeros_like(ci_ref)
            if with_da:
                dar_ref[...] = jnp.zeros_like(dar_ref)
                dai_ref[...] = jnp.zeros_like(dai_ref)

        sub = lax.broadcasted_iota(jnp.int32, (SUBLANES, s_dim), 0)
        edge = SUBLANES - 1 if desc else 0
        last = 0 if desc else SUBLANES - 1

        def step(t, carry):
            gi = (ng - 1 - t) if desc else t
            rows = pl.ds(pl.multiple_of(gi * SUBLANES, SUBLANES), SUBLANES)
            xr, xi = br_ref[rows, :], bi_ref[rows, :]
            for j, dist in enumerate((1, 2, 4)):
                shift = SUBLANES - dist if desc else dist
                rr, ri = pltpu.roll(xr, shift, 0), pltpu.roll(xi, shift, 0)
                mr, mi = tab_ref[2 * j], tab_ref[2 * j + 1]
                xr, xi = xr + mr * rr - mi * ri, xi + mr * ri + mi * rr
            cr, ci = cr_ref[...], ci_ref[...]
            pwr, pwi = tab_ref[6], tab_ref[7]
            sr = xr + pwr * cr - pwi * ci
            si = xi + pwr * ci + pwi * cr
            sr_ref[rows, :] = sr
            si_ref[rows, :] = si
            if with_da:
                pr, pi = pr_ref[rows, :], pi_ref[rows, :]
                dar_ref[...] += sr * pr + si * pi
                dai_ref[...] += si * pr - sr * pi
            else:
                shift1 = SUBLANES - 1 if desc else 1
                pr_ref[rows, :] = jnp.where(sub == edge, cr, pltpu.roll(sr, shift1, 0))
                pi_ref[rows, :] = jnp.where(sub == edge, ci, pltpu.roll(si, shift1, 0))
            cr_ref[...] = jnp.broadcast_to(sr[last:last + 1, :], (SUBLANES, s_dim))
            ci_ref[...] = jnp.broadcast_to(si[last:last + 1, :], (SUBLANES, s_dim))
            return carry

        lax.fori_loop(0, ng, step, 0)

    blk = pl.BlockSpec((chunk, s_dim), lambda i: (chunk_of(i), 0))
    tab_spec = pl.BlockSpec((8, SUBLANES, s_dim), lambda i: (0, 0, 0))
    acc = pl.BlockSpec((SUBLANES, s_dim), lambda i: (0, 0))
    big = jax.ShapeDtypeStruct((n, s_dim), F32)
    small = jax.ShapeDtypeStruct((SUBLANES, s_dim), F32)
    if with_da:
        in_specs, ins = [blk, blk, tab_spec, blk, blk], [b_re, b_im, tabs, prev[0], prev[1]]
        out_specs, out_shape = [blk, blk, acc, acc], [big, big, small, small]
    else:
        in_specs, ins = [blk, blk, tab_spec], [b_re, b_im, tabs]
        out_specs, out_shape = [blk, blk, blk, blk], [big, big, big, big]
    return pl.pallas_call(
        body, name='s5_scan_' + order, grid=(nch,), in_specs=in_specs, out_specs=out_specs, out_shape=out_shape,
        scratch_shapes=[pltpu.VMEM((SUBLANES, s_dim), F32), pltpu.VMEM((SUBLANES, s_dim), F32)],
        compiler_params=_cparams(('arbitrary',)),
    )(*ins)


def _make_scan(rev, chunk):
    def run(b_re, b_im, ar, ai):
        tabs = _scan_tables(ar, ai, desc=rev)
        return _scan_call(b_re, b_im, tabs, 'R' if rev else 'F', chunk)

    @jax.custom_vjp
    def f(b_re, b_im, ar, ai):
        return tuple(run(b_re, b_im, ar, ai)[:2])

    def fwd(b_re, b_im, ar, ai):
        s_re, s_im, p_re, p_im = run(b_re, b_im, ar, ai)
        return (s_re, s_im), (p_re, p_im, ar, ai)

    def bwd(res, g):
        p_re, p_im, ar, ai = res
        tabs = _scan_tables(ar, -ai, desc=not rev)
        db_re, db_im, dar, dai = _scan_call(g[0], g[1], tabs, 'Rb' if rev else 'Fb', chunk, prev=(p_re, p_im))
        return db_re, db_im, jnp.sum(dar, axis=0), jnp.sum(dai, axis=0)

    f.defvjp(fwd, bwd)
    return f


def _sqerr_call(y, t):
    n, d = y.shape
    rb = _tile(n, 512, SUBLANES)

    def body(y_ref, t_ref, o_ref):
        @pl.when(pl.program_id(0) == 0)
        def _():
            o_ref[...] = jnp.zeros_like(o_ref)

        e = y_ref[...] - t_ref[...]
        o_ref[...] += jnp.sum(e * e, axis=0, keepdims=True)

    row = pl.BlockSpec((rb, d), lambda i: (i, 0))
    return pl.pallas_call(
        body, name='sq_err', grid=(n // rb,), in_specs=[row, row], out_specs=pl.BlockSpec((1, d), lambda i: (0, 0)),
        out_shape=jax.ShapeDtypeStruct((1, d), F32), compiler_params=_cparams(('arbitrary',)),
    )(y, t)


@jax.custom_vjp
def loss_head(y, t):
    return 0.5 * jnp.sum(_sqerr_call(y, t)) / y.shape[1]


def _loss_head_fwd(y, t):
    return loss_head(y, t), (y, t)


def _loss_head_bwd(res, g):
    y, t = res
    return g * (y - t) / y.shape[1], None


loss_head.defvjp(_loss_head_fwd, _loss_head_bwd)


def _adamw_call(w, g, m, v):
    r, c = w.shape
    rb = _tile(r, max(SUBLANES, (256 * 1024) // max(c, LANES) // SUBLANES * SUBLANES), SUBLANES)

    def body(w_ref, g_ref, m_ref, v_ref, d_ref, nm_ref, nv_ref):
        gv = g_ref[...]
        nm = ADAM_B1 * m_ref[...] + (1.0 - ADAM_B1) * gv
        nv = ADAM_B2 * v_ref[...] + (1.0 - ADAM_B2) * (gv * gv)
        m_hat = nm / (1.0 - ADAM_B1 ** ADAM_STEP)
        v_hat = nv / (1.0 - ADAM_B2 ** ADAM_STEP)
        d_ref[...] = -ADAM_LR * (m_hat / (jnp.sqrt(v_hat) + ADAM_EPS) + ADAM_WD * w_ref[...])
        nm_ref[...] = nm
        nv_ref[...] = nv

    blk = pl.BlockSpec((rb, c), lambda i: (i, 0))
    shape = jax.ShapeDtypeStruct((r, c), F32)
    return pl.pallas_call(
        body, name='adamw', grid=(r // rb,), in_specs=[blk] * 4, out_specs=[blk] * 3, out_shape=[shape] * 3,
        compiler_params=_cparams(('parallel',)),
    )(w, g, m, v)


MESH = pl.DeviceIdType.MESH
HBM_SPEC = pl.BlockSpec(memory_space=pltpu.HBM)


def _all_gather(x):
    def body(x_ref, out_ref, send_sems, recv_sems, local_sem):
        x, y, c = lax.axis_index('x'), lax.axis_index('y'), lax.axis_index('c')
        me, sibling = (x, y, c), (x, y, 1 - c)
        chips = [(1 - x, y), (x, 1 - y), (1 - x, 1 - y)]

        def slot(px, py, pc):
            return out_ref.at[4 * px + 2 * py + pc]

        def copy(k, block, to, src=None):
            return pltpu.make_async_remote_copy(
                src_ref=slot(*block) if src is None else src, dst_ref=slot(*block),
                send_sem=send_sems.at[k], recv_sem=recv_sems.at[k], device_id=to, device_id_type=MESH)

        mine = pltpu.make_async_copy(x_ref, slot(*me), local_sem)
        mine.start()
        first = [copy(0, me, sibling, src=x_ref)]
        first += [copy(1 + j, me, (*chip, c), src=x_ref) for j, chip in enumerate(chips)]
        for cp in first:
            cp.start()
        passed = [copy(4 + j, (*chip, c), sibling) for j, chip in enumerate(chips)]
        for j, chip in enumerate(chips):
            copy(1 + j, (*chip, c), me).wait_recv()
            passed[j].start()
        copy(0, sibling, me).wait_recv()
        for j, chip in enumerate(chips):
            copy(4 + j, (*chip, 1 - c), me).wait_recv()
        for cp in first + passed:
            cp.wait_send()
        mine.wait()

    return pl.pallas_call(
        body, name='all_gather', out_shape=jax.ShapeDtypeStruct((N_DEV,) + x.shape, x.dtype),
        in_specs=[HBM_SPEC], out_specs=HBM_SPEC,
        scratch_shapes=[pltpu.SemaphoreType.DMA((7,)), pltpu.SemaphoreType.DMA((7,)), pltpu.SemaphoreType.DMA],
    )(x)


def _exchange_sibling(g_all):
    def body(g_ref, out_ref, send_sem, recv_sem):
        x, y, c = lax.axis_index('x'), lax.axis_index('y'), lax.axis_index('c')
        cp = pltpu.make_async_remote_copy(src_ref=g_ref.at[1 - c], dst_ref=out_ref, send_sem=send_sem, recv_sem=recv_sem,
                                          device_id=(x, y, 1 - c), device_id_type=MESH)
        cp.start()
        cp.wait()

    return pl.pallas_call(
        body, name='rs_sibling', out_shape=jax.ShapeDtypeStruct(g_all.shape[1:], g_all.dtype),
        in_specs=[HBM_SPEC], out_specs=HBM_SPEC,
        scratch_shapes=[pltpu.SemaphoreType.DMA, pltpu.SemaphoreType.DMA],
    )(g_all)


def _exchange_chips(p):
    def body(p_ref, out_ref, send_sems, recv_sems):
        x, y, c = lax.axis_index('x'), lax.axis_index('y'), lax.axis_index('c')
        chips = [(1 - x, y), (x, 1 - y), (1 - x, 1 - y)]
        copies = [pltpu.make_async_remote_copy(src_ref=p_ref.at[2 * px + py], dst_ref=out_ref.at[j],
                                               send_sem=send_sems.at[j], recv_sem=recv_sems.at[j],
                                               device_id=(px, py, c), device_id_type=MESH)
                  for j, (px, py) in enumerate(chips)]
        for cp in copies:
            cp.start()
        for cp in copies:
            cp.wait_recv()
        for cp in copies:
            cp.wait_send()

    return pl.pallas_call(
        body, name='rs_chips', out_shape=jax.ShapeDtypeStruct((3,) + p.shape[1:], p.dtype),
        in_specs=[HBM_SPEC], out_specs=HBM_SPEC,
        scratch_shapes=[pltpu.SemaphoreType.DMA((3,)), pltpu.SemaphoreType.DMA((3,))],
    )(p)


def _add_sibling(g_all, recv, c_idx):
    _, nchip, r, _ = g_all.shape
    rb = _tile(r, PACK_ROWS, SUBLANES)

    def body(c_ref, g_ref, r_ref, o_ref, ob_ref):
        s = g_ref[0] + r_ref[...]
        o_ref[...] = s
        ob_ref[...] = s.astype(BF16)

    blk = pl.BlockSpec((1, rb, LANES), lambda k, i, c: (k, i, 0))
    return pl.pallas_call(
        body, name='rs_add_sibling',
        grid_spec=pltpu.PrefetchScalarGridSpec(
            num_scalar_prefetch=1, grid=(nchip, r // rb),
            in_specs=[pl.BlockSpec((1, 1, rb, LANES), lambda k, i, c: (c[0], k, i, 0)), blk],
            out_specs=[blk, blk]),
        out_shape=[jax.ShapeDtypeStruct(recv.shape, F32), jax.ShapeDtypeStruct(recv.shape, BF16)],
        compiler_params=_cparams(('parallel', 'parallel')),
    )(c_idx, g_all, recv)


def _add_chips(p, recv, chip_idx):
    _, r, _ = p.shape
    rb = _tile(r, PACK_ROWS, SUBLANES)

    def body(k_ref, p_ref, r0, r1, r2, o_ref):
        o_ref[...] = ((p_ref[0] + r0[0].astype(F32)) + r1[0].astype(F32)) + r2[0].astype(F32)

    rspec = lambda j: pl.BlockSpec((1, rb, LANES), lambda i, k: (j, i, 0))
    return pl.pallas_call(
        body, name='rs_add_chips',
        grid_spec=pltpu.PrefetchScalarGridSpec(
            num_scalar_prefetch=1, grid=(r // rb,),
            in_specs=[pl.BlockSpec((1, rb, LANES), lambda i, k: (k[0], i, 0)), rspec(0), rspec(1), rspec(2)],
            out_specs=pl.BlockSpec((rb, LANES), lambda i, k: (i, 0))),
        out_shape=jax.ShapeDtypeStruct((r, LANES), F32), compiler_params=_cparams(('parallel',)),
    )(chip_idx, p, recv, recv, recv)


def _reduce_scatter(g_all, c_idx, chip_idx):
    part, part_bf16 = _add_sibling(g_all, _exchange_sibling(g_all), c_idx)
    return _add_chips(part, _exchange_chips(part_bf16), chip_idx)


def _pad_to(n, mult):
    return (n + mult - 1) // mult * mult


def _pack(pieces, lead, dtype):
    flat = []
    total = 0
    for p in pieces:
        f = p.reshape(lead + (-1,)).astype(dtype)
        n = _pad_to(f.shape[-1], 16 * LANES)
        flat.append(jnp.pad(f, [(0, 0)] * len(lead) + [(0, n - f.shape[-1])]))
        total += n
    full = _pad_to(total, PACK_ROWS * LANES)
    if full > total:
        flat.append(jnp.zeros(lead + (full - total,), dtype))
    return jnp.concatenate(flat, axis=-1).reshape(lead + (full // LANES, LANES))


def _unpack(buf, lead, shapes):
    flat = buf.reshape(lead + (-1,))
    out, off = [], 0
    for s in shapes:
        n = math.prod(s)
        out.append(flat[..., off:off + n].reshape(lead + tuple(s)))
        off += _pad_to(n, 16 * LANES)
    return out


def _make_split(sizes, width):
    starts = [sum(sizes[:i]) for i in range(len(sizes))]
    tail = width - sum(sizes)

    @jax.custom_vjp
    def f(z):
        return tuple(z[:, o:o + s] for o, s in zip(starts, sizes))

    def fwd(z):
        return f(z), None

    def bwd(_, gs):
        pieces = list(gs) + ([jnp.zeros((gs[0].shape[0], tail), gs[0].dtype)] if tail else [])
        return (jnp.concatenate(pieces, axis=1),)

    f.defvjp(fwd, bwd)
    return f


def _rope_tables(c_len, t_len, n):
    quarter = n // 4
    inv = ROPE_BASE ** (-jnp.arange(0, 2 * quarter, 2, dtype=F32) / (2 * quarter))
    t = jnp.arange(t_len, dtype=jnp.int32)
    pos = jnp.stack([(t // GRID_W).astype(F32), (t % GRID_W).astype(F32)], axis=1)
    ang = pos[:, :, None] * inv[None, None, :]
    ang = jnp.concatenate([jnp.zeros((c_len, 2, quarter), F32), ang], axis=0)
    return jnp.cos(ang), jnp.sin(ang)


def _axial_rope(x, cos, sin):
    n_rows, h, n = x.shape
    xs = x.reshape(n_rows, h, 2, 2, n // 4)
    x1, x2 = xs[:, :, :, 0], xs[:, :, :, 1]
    c, s = cos[:, None], sin[:, None]
    return jnp.stack([x1 * c - x2 * s, x1 * s + x2 * c], axis=3).reshape(n_rows, h, n)


def _ssm_discretize(lam_re, lam_im, log_dt, b_re, b_im):
    dt = jnp.exp(log_dt)[:, None]
    mag = jnp.exp(lam_re * dt)
    a_re, a_im = mag * jnp.cos(lam_im * dt), mag * jnp.sin(lam_im * dt)
    den = lam_re * lam_re + lam_im * lam_im
    w_re = ((a_re - 1) * lam_re + a_im * lam_im) / den
    w_im = (a_im * lam_re - (a_re - 1) * lam_im) / den
    bb_re, bb_im = _cmul(w_re[..., None], w_im[..., None], b_re, b_im)
    return a_re, a_im, bb_re, bb_im


def _block_diag_in(b):
    g = b.shape[0]
    return jnp.einsum('gpm,gh->gmhp', b, jnp.eye(g, dtype=F32)).reshape(g * b.shape[2], g * b.shape[1])


def _block_diag_out(c):
    g = c.shape[0]
    return jnp.einsum('gmp,gh->gphm', c, jnp.eye(g, dtype=F32)).reshape(g * c.shape[2], g * c.shape[1])


def _w_in_layout(d_model):
    sizes = (MLA_Q_RANK, MLA_KV_RANK, MLA_ROPE, SSM_WIDTH, GQA_HEADS * GQA_D, GQA_KV * GQA_D, GQA_KV * GQA_D, 3 * d_model)
    starts = [0]
    for s in sizes[:-1]:
        starts.append(starts[-1] + s)
    names = ('cq', 'ckv', 'kr', 'u', 'gq', 'gk', 'gv', 'gates')
    orig = dict(zip(names, zip(starts, sizes)))
    order = ('cq', 'ckv', 'u', 'gq', 'gk', 'gv', 'gates', 'kr')
    return orig, order


def _permute_w_in(blocks, d_model):
    orig, order = _w_in_layout(d_model)
    w = blocks.reshape(-1, blocks.shape[2])
    rows = [w[orig[k][0]:orig[k][0] + orig[k][1]] for k in order]
    width = sum(orig[k][1] for k in order)
    return jnp.pad(jnp.concatenate(rows, axis=0), ((0, _pad_to(width, LANES) - width), (0, 0)))


def _unpermute_w_in(gp, d_model, r):
    orig, order = _w_in_layout(d_model)
    pos, off = {}, 0
    for k in order:
        pos[k] = off
        off += orig[k][1]
    names = sorted(orig, key=lambda k: orig[k][0])
    w = jnp.concatenate([gp[pos[k]:pos[k] + orig[k][1]] for k in names], axis=0)
    return w.reshape(N_DEV, r, gp.shape[1])


def _forward_loss(x_all, sinks, rp, wl, cc_in, target, c_len):
    n, d = x_all.shape
    t_len = n - c_len
    depth = len(wl)
    mla_attn = _make_mla(c_len)
    norm_mod = _make_norm(c_len, True)
    norm_tok = _make_norm(_tile(n, 512, SUBLANES), False)
    norm_out = _make_norm(_tile(t_len, 512, SUBLANES), False)
    half_res = _make_gated_res(c_len, 0.5)
    full_res = _make_gated_res(c_len, 1.0)
    gqa_band = _make_gqa(True, c_len)
    gqa_ctx = _make_gqa(False, c_len)
    scans = (_make_scan(False, c_len), _make_scan(True, c_len))
    cos_m, sin_m = _rope_tables(c_len, t_len, MLA_ROPE)
    cos_g, sin_g = _rope_tables(c_len, t_len, GQA_D)
    orig, order = _w_in_layout(d)
    split_in = _make_split([orig[k][1] for k in order], _pad_to(sum(orig[k][1] for k in order), LANES))
    halves = lambda a: _make_split([a.shape[1] // 2] * 2, a.shape[1])(a)

    cc = jnp.zeros((SUBLANES, d), F32).at[0].set(jax.nn.silu(rp['c_ctx'])).at[1].set(jax.nn.silu(cc_in))

    for l in range(depth):
        w, sk = wl[l], sinks[l]
        ctx_out = l < depth - 1

        def mm(h, name):
            return (matmul if name in ROW_SHARDED else matmul_t)(h, w[name], sk[name])

        def swiglu(h, name13, name2):
            return swiglu_ffn(h, w[name13], sk[name13], w[name2], sk[name2])

        mod = mm(cc, 'ada_w') + rp['ada_b'][l][None, :]
        md = [mod[0:2, i * d:(i + 1) * d] for i in range(N_MOD)]
        x_all = half_res(x_all, swiglu(norm_mod(x_all, rp['norm_ffn1'][l], md[0], md[1]), 'ffn1_w13', 'ffn1_w2'), md[2])

        z = mm(norm_mod(x_all, rp['norm_mix'][l], md[3], md[4]), 'w_in')
        part = dict(zip(order, split_in(z)))

        q3 = mm(norm_tok(part['cq'], rp['mla_q_norm'][l]), 'mla_w_uq').reshape(n, MLA_HEADS, LANES)
        q = jnp.concatenate([q3[..., :MLA_NOPE], _axial_rope(q3[..., MLA_NOPE:MLA_NOPE + MLA_ROPE], cos_m, sin_m),
                             q3[..., MLA_NOPE + MLA_ROPE:]], axis=-1).reshape(n, MLA_HEADS * LANES)
        kvp = mm(norm_tok(part['ckv'], rp['mla_kv_norm'][l]), 'mla_w_ukv')
        kr = _axial_rope(part['kr'].reshape(n, 1, MLA_ROPE), cos_m, sin_m).reshape(n, MLA_ROPE)
        kr = jnp.pad(kr, ((0, 0), (MLA_NOPE, LANES - MLA_NOPE - MLA_ROPE)))
        mla = mm(mla_attn(q, kvp, kr), 'mla_w_o')

        u = part['u']
        y = u * rp['ssm_d'][l][None, :]
        for direction in range(2):
            a_re, a_im, bb_re, bb_im = _ssm_discretize(
                rp['ssm_lambda_re'][l, direction], rp['ssm_lambda_im'][l, direction], rp['ssm_log_dt'][l, direction],
                rp['ssm_b_re'][l, direction], rp['ssm_b_im'][l, direction])
            s_re, s_im = scans[direction](matmul_d(u, _block_diag_in(bb_re)), matmul_d(u, _block_diag_in(bb_im)),
                                          a_re.reshape(-1), a_im.reshape(-1))
            y = y + (matmul_d(s_re, _block_diag_out(rp['ssm_c_re'][l, direction]))
                     - matmul_d(s_im, _block_diag_out(rp['ssm_c_im'][l, direction])))
        yg = mm(jax.nn.gelu(y), 'ssm_w_glu')
        ya, ygate = halves(yg)
        ssm = ya * jax.nn.sigmoid(ygate)

        gq = _axial_rope(part['gq'].reshape(n, GQA_HEADS, GQA_D), cos_g, sin_g)
        gk = _axial_rope(part['gk'].reshape(n, GQA_KV, GQA_D), cos_g, sin_g)
        q4 = jnp.transpose(gq.reshape(n, GQA_KV, GQA_G, GQA_D), (1, 2, 0, 3))
        k2 = jnp.transpose(gk, (1, 0, 2))
        v2 = jnp.transpose(part['gv'].reshape(n, GQA_KV, GQA_D), (1, 0, 2))
        sink = rp['gqa_sink'][l].reshape(GQA_KV, GQA_G, 1, 1)
        sink_rows = lambda rows: jnp.broadcast_to(sink, (GQA_KV, GQA_G, rows, 1)).reshape(GQA_KV, GQA_G * rows, 1)
        g_lat = gqa_band(q4[:, :, c_len:], k2, v2, sink_rows(BLOCK))
        if ctx_out:
            g_ctx = gqa_ctx(q4[:, :, :c_len], k2[:, :c_len], v2[:, :c_len], sink_rows(c_len))
        else:
            g_ctx = jnp.zeros((GQA_KV, GQA_G, c_len, GQA_D), F32)
        go = jnp.transpose(jnp.concatenate([g_ctx, g_lat], axis=2), (2, 0, 1, 3)).reshape(n, GQA_HEADS * GQA_D)
        gqa = mm(go, 'gqa_w_o')

        g0, g1, g2 = _make_split([d] * 3, 3 * d)(jax.nn.sigmoid(part['gates']))
        mixed = g0 * mla + g1 * ssm + g2 * gqa
        x_all = full_res(x_all, mm(mixed, 'w_out'), md[5])
        x_all = half_res(x_all, swiglu(norm_mod(x_all, rp['norm_ffn2'][l], md[6], md[7]), 'ffn2_w13', 'ffn2_w2'), md[8])

    return loss_head(norm_out(x_all[c_len:], rp['final_norm']), target)


def kernel(x, c, ctx, c_ctx, ada_w, ada_b, norm_ffn1, norm_mix, norm_ffn2, ffn1_w13, ffn1_w2, ffn2_w13, ffn2_w2, w_in, mla_q_norm, mla_kv_norm, mla_w_uq, mla_w_ukv, mla_w_o, ssm_lambda_re, ssm_lambda_im, ssm_log_dt, ssm_b_re, ssm_b_im, ssm_c_re, ssm_c_im, ssm_d, ssm_w_glu, gqa_sink, gqa_w_o, w_out, final_norm, loss_target, m_c_ctx, m_ada_w, m_ada_b, m_norm_ffn1, m_norm_mix, m_norm_ffn2, m_ffn1_w13, m_ffn1_w2, m_ffn2_w13, m_ffn2_w2, m_w_in, m_mla_q_norm, m_mla_kv_norm, m_mla_w_uq, m_mla_w_ukv, m_mla_w_o, m_ssm_lambda_re, m_ssm_lambda_im, m_ssm_log_dt, m_ssm_b_re, m_ssm_b_im, m_ssm_c_re, m_ssm_c_im, m_ssm_d, m_ssm_w_glu, m_gqa_sink, m_gqa_w_o, m_w_out, m_final_norm, v_c_ctx, v_ada_w, v_ada_b, v_norm_ffn1, v_norm_mix, v_norm_ffn2, v_ffn1_w13, v_ffn1_w2, v_ffn2_w13, v_ffn2_w2, v_w_in, v_mla_q_norm, v_mla_kv_norm, v_mla_w_uq, v_mla_w_ukv, v_mla_w_o, v_ssm_lambda_re, v_ssm_lambda_im, v_ssm_log_dt, v_ssm_b_re, v_ssm_b_im, v_ssm_c_re, v_ssm_c_im, v_ssm_d, v_ssm_w_glu, v_gqa_sink, v_gqa_w_o, v_w_out, v_final_norm):
    args = dict(locals())
    weights = {k: args[k] for k in WEIGHTS}
    moments_m = {k: args['m_' + k] for k in WEIGHTS}
    moments_v = {k: args['v_' + k] for k in WEIGHTS}
    depth = ada_w.shape[0]
    d = x.shape[-1]
    c_len = ctx.shape[1]
    my_c = lax.axis_index('c')
    my_chip = 2 * lax.axis_index('x') + lax.axis_index('y')
    qk_w = MLA_NOPE + MLA_ROPE

    def as_rows(k, a):
        return a if k in ROW_SHARDED else a.T

    shard_shapes = [as_rows(k, weights[k][0]).shape for k in SHARDED]
    r_in = weights['w_in'].shape[2]
    layers = []
    for l in range(depth):
        gathered = _all_gather(_pack([as_rows(k, weights[k][l]) for k in SHARDED], (), BF16))
        full = {}
        for k, blocks in zip(SHARDED, _unpack(gathered, (N_DEV,), shard_shapes)):
            full[k] = _permute_w_in(blocks, d) if k == 'w_in' else blocks.reshape(N_DEV * blocks.shape[1], blocks.shape[2])
        full['mla_w_uq'] = jnp.pad(full['mla_w_uq'].reshape(MLA_HEADS, qk_w, MLA_Q_RANK),
                                   ((0, 0), (0, LANES - qk_w), (0, 0))).reshape(MLA_HEADS * LANES, MLA_Q_RANK)
        full['mla_w_o'] = jnp.pad(full['mla_w_o'].reshape(d, MLA_HEADS, MLA_V),
                                  ((0, 0), (0, 0), (LANES - MLA_V, 0))).reshape(d, MLA_HEADS * LANES)
        layers.append(full)
    sinks = [{k: jnp.zeros(v.shape, F32) for k, v in full.items()} for full in layers]

    rp = {k: weights[k] for k in REPLICATED}
    x_all = jnp.concatenate([ctx[0], x[0]], axis=0)
    loss_fn = functools.partial(_forward_loss, wl=layers, cc_in=c[0], target=loss_target[0], c_len=c_len)
    loss, vjp = jax.vjp(loss_fn, x_all, sinks, rp)
    g_x, g_layers, g_rp = vjp(jnp.ones((), F32))

    grads = {k: [] for k in SHARDED}
    for l in range(depth):
        gl = dict(g_layers[l])
        gl['mla_w_uq'] = gl['mla_w_uq'].reshape(MLA_HEADS, LANES, MLA_Q_RANK)[:, :qk_w].reshape(MLA_HEADS * qk_w, MLA_Q_RANK)
        gl['mla_w_o'] = gl['mla_w_o'].reshape(d, MLA_HEADS, LANES)[:, :, LANES - MLA_V:].reshape(d, MLA_HEADS * MLA_V)
        pieces = [_unpermute_w_in(gl[k], d, r_in) if k == 'w_in' else gl[k].reshape((N_DEV,) + shape)
                  for k, shape in zip(SHARDED, shard_shapes)]
        packed = _pack(pieces, (N_DEV,), F32)
        g_all = jnp.swapaxes(packed.reshape((4, 2) + packed.shape[1:]), 0, 1)
        mine = _reduce_scatter(g_all, my_c.reshape(1).astype(jnp.int32), my_chip.reshape(1).astype(jnp.int32))
        for k, g in zip(SHARDED, _unpack(mine, (), shard_shapes)):
            grads[k].append(as_rows(k, g))
    grads = {k: jnp.stack(v) for k, v in grads.items()}

    rep_shapes = [weights[k].shape for k in REPLICATED] + [(1,)]
    small = _pack([g_rp[k] for k in REPLICATED] + [loss.reshape(1)], (), F32)
    summed = _unpack(_sum_leading(_all_gather(small)), (), rep_shapes)
    for k, g in zip(REPLICATED, summed[:-1]):
        grads[k] = g
    loss_total = summed[-1].reshape(())

    delta, new_m, new_v = {}, {}, {}
    for k in SHARDED:
        shape = weights[k].shape
        as2d = lambda a: a.reshape(-1, shape[-1])
        outs = _adamw_call(as2d(weights[k]), as2d(grads[k]), as2d(moments_m[k]), as2d(moments_v[k]))
        delta[k], new_m[k], new_v[k] = (o.reshape(shape) for o in outs)
    rep_all = [weights[k].shape for k in REPLICATED]
    packs = [_pack([src[k] for k in REPLICATED], (), F32) for src in (weights, grads, moments_m, moments_v)]
    outs = [_unpack(o, (), rep_all) for o in _adamw_call(*packs)]
    for i, k in enumerate(REPLICATED):
        delta[k], new_m[k], new_v[k] = outs[0][i], outs[1][i], outs[2][i]

    return (loss_total, g_x[c_len:][None], *[grads[k] for k in WEIGHTS], *[delta[k] for k in WEIGHTS],
            *[new_m[k] for k in WEIGHTS], *[new_v[k] for k in WEIGHTS])
```

```python
import functools
import math

import jax
import jax.numpy as jnp
from jax import lax
from jax.experimental import pallas as pl
from jax.experimental.pallas import tpu as pltpu

F32 = jnp.float32
BF16 = jnp.bfloat16

MLA_HEADS, MLA_NOPE, MLA_ROPE, MLA_V = 8, 64, 32, 64
MLA_Q_RANK, MLA_KV_RANK = 384, 256
SSM_WIDTH, SSM_GROUP, SSM_STATE = 512, 16, 64
SSM_GROUPS = SSM_WIDTH // SSM_GROUP
GQA_HEADS, GQA_KV, GQA_D = 8, 2, 64
GQA_G = GQA_HEADS // GQA_KV
WINDOW, BLOCK, GRID_W = 128, 128, 64
N_MOD = 9
ROPE_BASE = 10000.0
EPS = 1e-6
NEG_INF = -1e30
ADAM_LR, ADAM_B1, ADAM_B2, ADAM_EPS, ADAM_WD, ADAM_STEP = 0.001, 0.9, 0.999, 1e-08, 0.01, 10

N_DEV = 8
LANES = 128
SUBLANES = 8
VMEM_LIMIT = 56 * 1024 * 1024
PACK_ROWS = 1024

SHARDED = ('ada_w', 'ffn1_w13', 'ffn1_w2', 'ffn2_w13', 'ffn2_w2', 'w_in', 'mla_w_uq', 'mla_w_ukv',
           'mla_w_o', 'ssm_w_glu', 'gqa_w_o', 'w_out')
ROW_SHARDED = ('ffn1_w2', 'ffn2_w2', 'w_out')
REPLICATED = ('c_ctx', 'ada_b', 'norm_ffn1', 'norm_mix', 'norm_ffn2', 'mla_q_norm', 'mla_kv_norm',
              'ssm_lambda_re', 'ssm_lambda_im', 'ssm_log_dt', 'ssm_b_re', 'ssm_b_im', 'ssm_c_re', 'ssm_c_im',
              'ssm_d', 'gqa_sink', 'final_norm')
WEIGHTS = ('c_ctx', 'ada_w', 'ada_b', 'norm_ffn1', 'norm_mix', 'norm_ffn2', 'ffn1_w13', 'ffn1_w2', 'ffn2_w13',
           'ffn2_w2', 'w_in', 'mla_q_norm', 'mla_kv_norm', 'mla_w_uq', 'mla_w_ukv', 'mla_w_o', 'ssm_lambda_re',
           'ssm_lambda_im', 'ssm_log_dt', 'ssm_b_re', 'ssm_b_im', 'ssm_c_re', 'ssm_c_im', 'ssm_d', 'ssm_w_glu',
           'gqa_sink', 'gqa_w_o', 'w_out', 'final_norm')


def _tile(n, target, mult):
    t = (min(target, n) // mult) * mult
    while t >= mult:
        if n % t == 0:
            return t
        t -= mult
    return n


def _cparams(sem):
    return pltpu.CompilerParams(dimension_semantics=sem, vmem_limit_bytes=VMEM_LIMIT)


def _mm(a, b, mode):
    if mode == 'nn':
        (M, K), N = a.shape, b.shape[1]
        tm, tn, tk = _tile(M, 1408, SUBLANES), _tile(N, 1024, LANES), _tile(K, 1408, LANES)
    elif mode == 'nt':
        (M, K), N = a.shape, b.shape[0]
        tm, tn, tk = _tile(M, 1408, SUBLANES), _tile(N, 1024, LANES), _tile(K, 1408, LANES)
    else:
        (K, M), N = a.shape, b.shape[1]
        tm, tn, tk = _tile(M, 1408, LANES), _tile(N, 1408, LANES), _tile(K, 768, 2 * SUBLANES)
    nk = K // tk
    dims = {'nn': (((1,), (0,)), ((), ())), 'nt': (((1,), (1,)), ((), ())), 'tn': (((0,), (0,)), ((), ()))}[mode]
    keep_a = nk == 1 and mode != 'tn' and N // tn > 1 and a.dtype != BF16

    def body(a_ref, b_ref, o_ref, *scratch):
        if keep_a:
            @pl.when(pl.program_id(1) == 0)
            def _():
                scratch[0][...] = a_ref[...].astype(BF16)

            av = scratch[0][...]
        else:
            av = a_ref[...].astype(BF16)
        part = lax.dot_general(av, b_ref[...].astype(BF16), dims, preferred_element_type=F32)
        if nk == 1:
            o_ref[...] = part
        else:
            @pl.when(pl.program_id(2) == 0)
            def _():
                o_ref[...] = part

            @pl.when(pl.program_id(2) > 0)
            def _():
                o_ref[...] += part

    a_spec = pl.BlockSpec((tk, tm), lambda i, j, k: (k, i)) if mode == 'tn' else pl.BlockSpec((tm, tk), lambda i, j, k: (i, k))
    b_spec = pl.BlockSpec((tn, tk), lambda i, j, k: (j, k)) if mode == 'nt' else pl.BlockSpec((tk, tn), lambda i, j, k: (k, j))
    return pl.pallas_call(
        body, name='mm_' + mode, grid=(M // tm, N // tn, nk),
        in_specs=[a_spec, b_spec], out_specs=pl.BlockSpec((tm, tn), lambda i, j, k: (i, j)),
        out_shape=jax.ShapeDtypeStruct((M, N), F32),
        scratch_shapes=[pltpu.VMEM((tm, tk), BF16)] if keep_a else [],
        compiler_params=_cparams(('parallel', 'arbitrary', 'arbitrary')),
    )(a, b)


@jax.custom_vjp
def matmul_d(x, w):
    return _mm(x, w, 'nn')


def _matmul_d_fwd(x, w):
    return _mm(x, w, 'nn'), (x, w)


def _matmul_d_bwd(res, g):
    x, w = res
    return _mm(g, w, 'nt'), _mm(x, g, 'tn')


matmul_d.defvjp(_matmul_d_fwd, _matmul_d_bwd)


@jax.custom_vjp
def matmul(x, w, sink):
    return _mm(x, w, 'nn')


def _matmul_fwd(x, w, sink):
    return _mm(x, w, 'nn'), (x, w)


def _matmul_bwd(res, g):
    x, w = res
    return _mm(g, w, 'nt'), jnp.zeros_like(w), _mm(x, g, 'tn')


matmul.defvjp(_matmul_fwd, _matmul_bwd)


@jax.custom_vjp
def matmul_t(x, wt, sink):
    return _mm(x, wt, 'nt')


def _matmul_t_fwd(x, wt, sink):
    return _mm(x, wt, 'nt'), (x, wt)


def _matmul_t_bwd(res, g):
    x, wt = res
    return _mm(g, wt, 'nn'), jnp.zeros_like(wt), _mm(g, x, 'tn')


matmul_t.defvjp(_matmul_t_fwd, _matmul_t_bwd)


def _ffn_up_call(x, w13t):
    m, k = x.shape
    f = w13t.shape[0] // 2
    tm, tn = _tile(m, 1408, SUBLANES), _tile(f, 256, LANES)
    nf = f // tn

    def body(x_ref, wa_ref, wb_ref, a_ref, b_ref, act_ref, xb):
        @pl.when(pl.program_id(1) == 0)
        def _():
            xb[...] = x_ref[...].astype(BF16)

        a = lax.dot_general(xb[...], wa_ref[...], NT_DIMS, preferred_element_type=F32)
        b = lax.dot_general(xb[...], wb_ref[...], NT_DIMS, preferred_element_type=F32)
        a_ref[...] = a
        b_ref[...] = b
        act_ref[...] = (a * jax.nn.sigmoid(a) * b).astype(BF16)

    out = pl.BlockSpec((tm, tn), lambda i, j: (i, j))
    return pl.pallas_call(
        body, name='ffn_up', grid=(m // tm, nf),
        in_specs=[pl.BlockSpec((tm, k), lambda i, j: (i, 0)), pl.BlockSpec((tn, k), lambda i, j: (j, 0)),
                  pl.BlockSpec((tn, k), lambda i, j: (j + nf, 0))],
        out_specs=[out, out, out],
        out_shape=[jax.ShapeDtypeStruct((m, f), F32), jax.ShapeDtypeStruct((m, f), F32), jax.ShapeDtypeStruct((m, f), BF16)],
        scratch_shapes=[pltpu.VMEM((tm, k), BF16)],
        compiler_params=_cparams(('parallel', 'arbitrary')),
    )(x, w13t, w13t)


@jax.custom_vjp
def swiglu_ffn(x, w13t, sink13, w2, sink2):
    return _mm(_ffn_up_call(x, w13t)[2], w2, 'nn')


def _swiglu_ffn_fwd(x, w13t, sink13, w2, sink2):
    a, b, act = _ffn_up_call(x, w13t)
    return _mm(act, w2, 'nn'), (x, a, b, act, w13t, w2)


def _ffn_down_bwd_call(g, w2, a, b):
    m, d = g.shape
    f = w2.shape[0]
    tm, tn = _tile(m, 1408, SUBLANES), _tile(f, 256, LANES)
    nf = f // tn

    def body(g_ref, w_ref, a_ref, b_ref, o_ref, gb):
        j = pl.program_id(1)

        @pl.when(j == 0)
        def _():
            gb[...] = g_ref[...].astype(BF16)

        dact = lax.dot_general(gb[...], w_ref[...], NT_DIMS, preferred_element_type=F32)
        av = a_ref[...]
        sa = jax.nn.sigmoid(av)
        o_ref[...] = jnp.where(j < nf, dact * b_ref[...] * (sa * (1.0 + av * (1.0 - sa))), dact * (av * sa))

    half = pl.BlockSpec((tm, tn), lambda i, j: (i, lax.rem(j, nf)))
    return pl.pallas_call(
        body, name='ffn_down_bwd', grid=(m // tm, 2 * nf),
        in_specs=[pl.BlockSpec((tm, d), lambda i, j: (i, 0)), pl.BlockSpec((tn, d), lambda i, j: (lax.rem(j, nf), 0)), half, half],
        out_specs=pl.BlockSpec((tm, tn), lambda i, j: (i, j)),
        out_shape=jax.ShapeDtypeStruct((m, 2 * f), F32),
        scratch_shapes=[pltpu.VMEM((tm, d), BF16)],
        compiler_params=_cparams(('parallel', 'arbitrary')),
    )(g, w2, a, b)


def _swiglu_ffn_bwd(res, g):
    x, a, b, act, w13t, w2 = res
    d13 = _ffn_down_bwd_call(g, w2, a, b)
    return _mm(d13, w13t, 'nn'), jnp.zeros_like(w13t), _mm(d13, x, 'tn'), jnp.zeros_like(w2), _mm(act, g, 'tn')


swiglu_ffn.defvjp(_swiglu_ffn_fwd, _swiglu_ffn_bwd)


def _norm_fwd_call(x, g, sh, sc, rb):
    n, d = x.shape
    has_mod = sh is not None

    def body(*refs):
        x_ref, g_ref = refs[0], refs[1]
        o_ref = refs[-1]
        xv = x_ref[...]
        r = lax.rsqrt(jnp.mean(xv * xv, axis=-1, keepdims=True) + EPS)
        y = xv * r * g_ref[...]
        if has_mod:
            lat = pl.program_id(0) > 0
            shv = jnp.where(lat, refs[2][1:2, :], refs[2][0:1, :])
            scv = jnp.where(lat, refs[3][1:2, :], refs[3][0:1, :])
            y = y * (1.0 + scv) + shv
        o_ref[...] = y

    row = pl.BlockSpec((rb, d), lambda i: (i, 0))
    vec = pl.BlockSpec((1, d), lambda i: (0, 0))
    two = pl.BlockSpec((2, d), lambda i: (0, 0))
    ins = [x, g.reshape(1, d)] + ([sh, sc] if has_mod else [])
    return pl.pallas_call(
        body, name='norm_fwd', grid=(n // rb,), in_specs=[row, vec] + ([two, two] if has_mod else []),
        out_specs=row, out_shape=jax.ShapeDtypeStruct((n, d), F32), compiler_params=_cparams(('parallel',)),
    )(*ins)


def _norm_bwd_call(x, g, sh, sc, dy, rb):
    n, d = x.shape
    has_mod = sh is not None

    def body(*refs):
        x_ref, g_ref, dy_ref = refs[0], refs[1], refs[-3]
        dx_ref, acc_ref = refs[-2], refs[-1]
        i = pl.program_id(0)
        xv, dyv, gv = x_ref[...], dy_ref[...], g_ref[...]
        r = lax.rsqrt(jnp.mean(xv * xv, axis=-1, keepdims=True) + EPS)
        xh = xv * r
        if has_mod:
            lat = i > 0
            scv = jnp.where(lat, refs[3][1:2, :], refs[3][0:1, :])
            dyg = dyv * (1.0 + scv)
        else:
            dyg = dyv
        dxh = dyg * gv
        dx_ref[...] = r * (dxh - xh * jnp.mean(dxh * xh, axis=-1, keepdims=True))

        @pl.when(i == 0)
        def _():
            acc_ref[...] = jnp.zeros_like(acc_ref)

        acc_ref[0:1, :] += jnp.sum(dyg * xh, axis=0, keepdims=True)
        if has_mod:
            dsh = jnp.sum(dyv, axis=0, keepdims=True)
            dsc = jnp.sum(dyv * xh * gv, axis=0, keepdims=True)

            @pl.when(i == 0)
            def _():
                acc_ref[1:2, :] += dsh
                acc_ref[3:4, :] += dsc

            @pl.when(i > 0)
            def _():
                acc_ref[2:3, :] += dsh
                acc_ref[4:5, :] += dsc

    row = pl.BlockSpec((rb, d), lambda i: (i, 0))
    vec = pl.BlockSpec((1, d), lambda i: (0, 0))
    two = pl.BlockSpec((2, d), lambda i: (0, 0))
    ins = [x, g.reshape(1, d)] + ([sh, sc] if has_mod else []) + [dy]
    return pl.pallas_call(
        body, name='norm_bwd', grid=(n // rb,), in_specs=[row, vec] + ([two, two] if has_mod else []) + [row],
        out_specs=[row, pl.BlockSpec((SUBLANES, d), lambda i: (0, 0))],
        out_shape=[jax.ShapeDtypeStruct((n, d), F32), jax.ShapeDtypeStruct((SUBLANES, d), F32)],
        compiler_params=_cparams(('arbitrary',)),
    )(*ins)


def _make_norm(rb, has_mod):
    if has_mod:
        @jax.custom_vjp
        def f(x, g, sh, sc):
            return _norm_fwd_call(x, g, sh, sc, rb)

        def fwd(x, g, sh, sc):
            return _norm_fwd_call(x, g, sh, sc, rb), (x, g, sh, sc)

        def bwd(res, dy):
            x, g, sh, sc = res
            dx, acc = _norm_bwd_call(x, g, sh, sc, dy, rb)
            return dx, acc[0], acc[1:3], acc[3:5]
    else:
        @jax.custom_vjp
        def f(x, g):
            return _norm_fwd_call(x, g, None, None, rb)

        def fwd(x, g):
            return _norm_fwd_call(x, g, None, None, rb), (x, g)

        def bwd(res, dy):
            x, g = res
            dx, acc = _norm_bwd_call(x, g, None, None, dy, rb)
            return dx, acc[0]
    f.defvjp(fwd, bwd)
    return f


def _make_gated_res(rb, coef):
    def fwd_call(x, f, gate):
        n, d = x.shape

        def body(x_ref, f_ref, g_ref, o_ref):
            gv = jnp.where(pl.program_id(0) > 0, g_ref[1:2, :], g_ref[0:1, :])
            o_ref[...] = x_ref[...] + coef * gv * f_ref[...]

        row = pl.BlockSpec((rb, d), lambda i: (i, 0))
        return pl.pallas_call(
            body, name='gated_res_fwd', grid=(n // rb,), in_specs=[row, row, pl.BlockSpec((2, d), lambda i: (0, 0))],
            out_specs=row, out_shape=jax.ShapeDtypeStruct((n, d), F32), compiler_params=_cparams(('parallel',)),
        )(x, f, gate)

    def bwd_call(dy, f, gate):
        n, d = dy.shape

        def body(dy_ref, f_ref, g_ref, df_ref, acc_ref):
            i = pl.program_id(0)
            gv = jnp.where(i > 0, g_ref[1:2, :], g_ref[0:1, :])
            dyv = dy_ref[...]
            df_ref[...] = coef * gv * dyv
            part = coef * jnp.sum(dyv * f_ref[...], axis=0, keepdims=True)

            @pl.when(i == 0)
            def _():
                acc_ref[...] = jnp.zeros_like(acc_ref)
                acc_ref[0:1, :] += part

            @pl.when(i > 0)
            def _():
                acc_ref[1:2, :] += part

        row = pl.BlockSpec((rb, d), lambda i: (i, 0))
        return pl.pallas_call(
            body, name='gated_res_bwd', grid=(n // rb,), in_specs=[row, row, pl.BlockSpec((2, d), lambda i: (0, 0))],
            out_specs=[row, pl.BlockSpec((SUBLANES, d), lambda i: (0, 0))],
            out_shape=[jax.ShapeDtypeStruct((n, d), F32), jax.ShapeDtypeStruct((SUBLANES, d), F32)],
            compiler_params=_cparams(('arbitrary',)),
        )(dy, f, gate)

    @jax.custom_vjp
    def f(x, fv, gate):
        return fwd_call(x, fv, gate)

    def fwd(x, fv, gate):
        return fwd_call(x, fv, gate), (fv, gate)

    def bwd(res, dy):
        fv, gate = res
        df, acc = bwd_call(dy, fv, gate)
        return dy, df, acc[0:2]

    f.defvjp(fwd, bwd)
    return f


MLA_SCALE = (MLA_NOPE + MLA_ROPE) ** -0.5
LOG2E = math.log2(math.e)
MLA_SCALE_LOG2E = MLA_SCALE * LOG2E
NT_DIMS = (((1,), (1,)), ((), ()))


def _mla_keys(kv, kr):
    lane = lax.broadcasted_iota(jnp.int32, kv.shape, 1)
    return jnp.where(lane < MLA_NOPE, kv, kr)


def _chunks(start, stop, target):
    size = _tile(stop - start, target, LANES)
    return [(start + t * size, size) for t in range((stop - start) // size)]


MLA_CHUNK = 2816


def _mla_fwd_call(q, kv, kr, c_len):
    n = q.shape[0]
    h = q.shape[1] // LANES
    tq = c_len

    def body(q_ref, kv_ref, kr_ref, o_ref, lse_ref, kb, vb):
        i = pl.program_id(1)

        @pl.when(i == 0)
        def _():
            kvv = kv_ref[...]
            kb[...] = _mla_keys(kvv, kr_ref[...])
            vb[...] = jnp.where(lax.broadcasted_iota(jnp.int32, kvv.shape, 1) < MLA_NOPE, jnp.ones_like(kvv), kvv)

        qv = q_ref[...].astype(BF16)

        def attend(nk):
            s = lax.dot_general(qv, kb[:nk, :], NT_DIMS, preferred_element_type=F32)
            m = jnp.max(s, axis=-1, keepdims=True)
            p = jnp.exp2((s - m) * MLA_SCALE_LOG2E)
            acc = jnp.dot(p.astype(BF16), vb[:nk, :], preferred_element_type=F32)
            l = acc[:, 0:1]
            o_ref[...] = acc / l
            lse_ref[0] = m * MLA_SCALE + jnp.log(l)

        pl.when(i == 0)(lambda: attend(c_len))
        pl.when(i > 0)(lambda: attend(n))

    qspec = pl.BlockSpec((tq, LANES), lambda a, i: (i, a))
    return pl.pallas_call(
        body, name='mla_attn_fwd', grid=(h, n // tq),
        in_specs=[qspec, pl.BlockSpec((n, LANES), lambda a, i: (0, a)), pl.BlockSpec((n, LANES), lambda a, i: (0, 0))],
        out_specs=[qspec, pl.BlockSpec((1, tq, 1), lambda a, i: (a, i, 0))],
        out_shape=[jax.ShapeDtypeStruct((n, h * LANES), F32), jax.ShapeDtypeStruct((h, n, 1), F32)],
        scratch_shapes=[pltpu.VMEM((n, LANES), BF16), pltpu.VMEM((n, LANES), BF16)],
        compiler_params=_cparams(('arbitrary', 'arbitrary')),
    )(q, kv, kr)


TN_DIMS = (((0,), (0,)), ((), ()))


def _mla_bwd_call(q, kv, kr, o, lse, do, c_len):
    n = q.shape[0]
    h = q.shape[1] // LANES
    tq = c_len
    nq = n // tq

    def body(q_ref, kv_ref, kr_ref, o_ref, lse_ref, do_ref, dq_ref, dv_ref, dk_ref, kb):
        i = pl.program_id(1)

        @pl.when(i == 0)
        def _():
            kb[...] = _mla_keys(kv_ref[...], kr_ref[...])
            dv_ref[...] = jnp.zeros_like(dv_ref)
            dk_ref[...] = jnp.zeros_like(dk_ref)

        dov = do_ref[...]
        delta = jnp.sum(dov * o_ref[...], axis=-1, keepdims=True)
        qv, dob, lse2 = q_ref[...], dov.astype(BF16), lse_ref[0] * LOG2E

        def grad(chunks):
            acc = None
            for k0, kc in chunks:
                keys = kb[k0:k0 + kc, :]
                s = lax.dot_general(qv, keys, NT_DIMS, preferred_element_type=F32)
                p = jnp.exp2(s * MLA_SCALE_LOG2E - lse2)
                dp = lax.dot_general(dob, kv_ref[k0:k0 + kc, :], NT_DIMS, preferred_element_type=F32)
                ds = (p * (dp - delta)).astype(BF16)
                part = jnp.dot(ds, keys, preferred_element_type=F32)
                acc = part if acc is None else acc + part
                dk_ref[0, k0:k0 + kc, :] += lax.dot_general(ds, qv, TN_DIMS, preferred_element_type=F32)
                dv_ref[k0:k0 + kc, :] += lax.dot_general(p.astype(BF16), dob, TN_DIMS, preferred_element_type=F32)
            dq_ref[...] = acc * MLA_SCALE

        pl.when(i == 0)(lambda: grad([(0, c_len)]))
        pl.when(i > 0)(lambda: grad(_chunks(0, n, MLA_CHUNK)))

        @pl.when(i == nq - 1)
        def _():
            dk = dk_ref[0] * MLA_SCALE
            dk_ref[0] = dk
            dv_ref[...] = jnp.where(lax.broadcasted_iota(jnp.int32, dk.shape, 1) < MLA_NOPE, dk, dv_ref[...])

    qspec = pl.BlockSpec((tq, LANES), lambda a, i: (i, a))
    full = pl.BlockSpec((n, LANES), lambda a, i: (0, a))
    return pl.pallas_call(
        body, name='mla_attn_bwd', grid=(h, nq),
        in_specs=[qspec, full, pl.BlockSpec((n, LANES), lambda a, i: (0, 0)), qspec,
                  pl.BlockSpec((1, tq, 1), lambda a, i: (a, i, 0)), qspec],
        out_specs=[qspec, full, pl.BlockSpec((1, n, LANES), lambda a, i: (a, 0, 0))],
        out_shape=[jax.ShapeDtypeStruct((n, h * LANES), F32), jax.ShapeDtypeStruct((n, h * LANES), F32),
                   jax.ShapeDtypeStruct((h, n, LANES), F32)],
        scratch_shapes=[pltpu.VMEM((n, LANES), BF16)],
        compiler_params=_cparams(('arbitrary', 'arbitrary')),
    )(q, kv, kr, o, lse, do)


def _sum_leading(g):
    nl, r, _ = g.shape
    rb = _tile(r, 512, SUBLANES)

    def body(g_ref, o_ref):
        acc = g_ref[0]
        for j in range(1, nl):
            acc = acc + g_ref[j]
        o_ref[...] = acc

    return pl.pallas_call(
        body, name='sum_leading', grid=(r // rb,), in_specs=[pl.BlockSpec((nl, rb, LANES), lambda i: (0, i, 0))],
        out_specs=pl.BlockSpec((rb, LANES), lambda i: (i, 0)), out_shape=jax.ShapeDtypeStruct((r, LANES), F32),
        compiler_params=_cparams(('parallel',)),
    )(g)


def _make_mla(c_len):
    @jax.custom_vjp
    def f(q, kv, kr):
        return _mla_fwd_call(q, kv.astype(BF16), kr.astype(BF16), c_len)[0]

    def fwd(q, kv, kr):
        kvb, krb = kv.astype(BF16), kr.astype(BF16)
        o, lse = _mla_fwd_call(q, kvb, krb, c_len)
        return o, (q.astype(BF16), kvb, krb, o, lse)

    def bwd(res, do):
        q, kv, kr, o, lse = res
        dq, dkv, dk_full = _mla_bwd_call(q, kv, kr, o, lse, do, c_len)
        return dq, dkv, _sum_leading(dk_full)

    f.defvjp(fwd, bwd)
    return f


def _gqa_specs(band, c_len, nb, rows):
    cb = c_len // BLOCK
    q_spec = pl.BlockSpec((1, GQA_G, rows, GQA_D), lambda a, b: (a, 0, b, 0))
    ctx_spec = pl.BlockSpec((1, c_len, GQA_D), lambda a, b: (a, 0, 0))
    kv_specs = [ctx_spec]
    if band:
        kv_specs += [pl.BlockSpec((1, BLOCK, GQA_D), lambda a, b: (a, jnp.maximum(b - 1, 0) + cb, 0)),
                     pl.BlockSpec((1, BLOCK, GQA_D), lambda a, b: (a, b + cb, 0)),
                     pl.BlockSpec((1, BLOCK, GQA_D), lambda a, b: (a, jnp.minimum(b + 1, nb - 1) + cb, 0))]
    sink_spec = pl.BlockSpec((1, GQA_G * rows, 1), lambda a, b: (a, 0, 0))
    return q_spec, kv_specs, sink_spec


def _gqa_scores(q, kcat, sink, band, c_len, t_len, rows):
    scale = GQA_D ** -0.5
    s = lax.dot_general(q, kcat, (((1,), (1,)), ((), ())), preferred_element_type=F32) * scale
    if band:
        b = pl.program_id(1)
        shape = s.shape
        col = lax.broadcasted_iota(jnp.int32, shape, 1)
        qpos = b * BLOCK + (lax.broadcasted_iota(jnp.int32, shape, 0) & (BLOCK - 1))
        kpos = (b - 1) * BLOCK + (col - c_len)
        valid = (col < c_len) | ((jnp.abs(qpos - kpos) <= WINDOW) & (kpos >= 0) & (kpos < t_len))
        s = jnp.where(valid, s, NEG_INF)
    m = jnp.maximum(jnp.max(s, axis=-1, keepdims=True), sink)
    e = jnp.exp(s - m)
    es = jnp.exp(sink - m)
    den = es + jnp.sum(e, axis=-1, keepdims=True)
    return e / den, es / den


def _gqa_fwd_call(q4, k2, v2, sink_rows, band, c_len):
    kv, g, tq_all, d = q4.shape
    rows = BLOCK if band else tq_all
    nb = tq_all // rows
    t_len = k2.shape[1] - c_len
    nkv = 4 if band else 1
    q_spec, kv_specs, sink_spec = _gqa_specs(band, c_len, nb, rows)

    def body(*refs):
        q_ref, sink_ref, o_ref = refs[0], refs[1 + 2 * nkv], refs[-1]
        kcat = jnp.concatenate([r[0] for r in refs[1:1 + nkv]], axis=0).astype(BF16)
        vcat = jnp.concatenate([r[0] for r in refs[1 + nkv:1 + 2 * nkv]], axis=0).astype(BF16)
        q = q_ref[0].reshape(g * rows, d).astype(BF16)
        p, _ = _gqa_scores(q, kcat, sink_ref[0], band, c_len, t_len, rows)
        o_ref[0] = jnp.dot(p.astype(BF16), vcat, preferred_element_type=F32).reshape(g, rows, d)

    return pl.pallas_call(
        body, name='gqa_fwd_band' if band else 'gqa_fwd_ctx', grid=(kv, nb),
        in_specs=[q_spec] + kv_specs + kv_specs + [sink_spec], out_specs=q_spec,
        out_shape=jax.ShapeDtypeStruct(q4.shape, F32), compiler_params=_cparams(('parallel', 'parallel')),
    )(q4, *([k2] * nkv), *([v2] * nkv), sink_rows)


def _gqa_bwd_call(q4, k2, v2, sink_rows, do4, band, c_len):
    kv, g, tq_all, d = q4.shape
    rows = BLOCK if band else tq_all
    nb = tq_all // rows
    t_len = k2.shape[1] - c_len
    nkv = 4 if band else 1
    scale = GQA_D ** -0.5
    q_spec, kv_specs, sink_spec = _gqa_specs(band, c_len, nb, rows)

    def body(*refs):
        q_ref, sink_ref, do_ref = refs[0], refs[1 + 2 * nkv], refs[2 + 2 * nkv]
        outs = refs[3 + 2 * nkv:]
        dq_ref, dkc_ref, dvc_ref = outs[0], outs[1], outs[2]
        dsink_ref = outs[-1]
        b = pl.program_id(1)
        kcat = jnp.concatenate([r[0] for r in refs[1:1 + nkv]], axis=0).astype(BF16)
        vcat = jnp.concatenate([r[0] for r in refs[1 + nkv:1 + 2 * nkv]], axis=0).astype(BF16)
        q = q_ref[0].reshape(g * rows, d).astype(BF16)
        do = do_ref[0].reshape(g * rows, d).astype(BF16)
        p, p_sink = _gqa_scores(q, kcat, sink_ref[0], band, c_len, t_len, rows)
        dp = lax.dot_general(do, vcat, (((1,), (1,)), ((), ())), preferred_element_type=F32)
        rd = jnp.sum(p * dp, axis=-1, keepdims=True)
        ds = (p * (dp - rd) * scale).astype(BF16)
        dq_ref[0] = jnp.dot(ds, kcat, preferred_element_type=F32).reshape(g, rows, d)
        dkcat = lax.dot_general(ds, q, (((0,), (0,)), ((), ())), preferred_element_type=F32)
        dvcat = lax.dot_general(p.astype(BF16), do, (((0,), (0,)), ((), ())), preferred_element_type=F32)

        @pl.when(b == 0)
        def _():
            dkc_ref[...] = jnp.zeros_like(dkc_ref)
            dvc_ref[...] = jnp.zeros_like(dvc_ref)
            dsink_ref[...] = jnp.zeros_like(dsink_ref)

        dkc_ref[0] += dkcat[:c_len]
        dvc_ref[0] += dvcat[:c_len]
        dsink_ref[0] += -p_sink * rd
        if band:
            outs[3][0, 0] = dkcat[c_len:]
            outs[4][0, 0] = dvcat[c_len:]

    ctx_out = pl.BlockSpec((1, c_len, d), lambda a, b: (a, 0, 0))
    band_out = pl.BlockSpec((1, 1, 3 * BLOCK, d), lambda a, b: (a, b, 0, 0))
    out_specs = [q_spec, ctx_out, ctx_out] + ([band_out, band_out] if band else []) + [sink_spec]
    ctx_shape = jax.ShapeDtypeStruct((kv, c_len, d), F32)
    band_shape = jax.ShapeDtypeStruct((kv, nb, 3 * BLOCK, d), F32)
    out_shape = ([jax.ShapeDtypeStruct(q4.shape, F32), ctx_shape, ctx_shape] + ([band_shape, band_shape] if band else [])
                 + [jax.ShapeDtypeStruct(sink_rows.shape, F32)])
    return pl.pallas_call(
        body, name='gqa_bwd_band' if band else 'gqa_bwd_ctx', grid=(kv, nb),
        in_specs=[q_spec] + kv_specs + kv_specs + [sink_spec, q_spec], out_specs=out_specs, out_shape=out_shape,
        compiler_params=_cparams(('arbitrary', 'arbitrary')),
    )(q4, *([k2] * nkv), *([v2] * nkv), sink_rows, do4)


def _make_gqa(band, c_len):
    @jax.custom_vjp
    def f(q4, k2, v2, sink_rows):
        return _gqa_fwd_call(q4, k2, v2, sink_rows, band, c_len)

    def fwd(q4, k2, v2, sink_rows):
        return _gqa_fwd_call(q4, k2, v2, sink_rows, band, c_len), (q4, k2, v2, sink_rows)

    def bwd(res, do4):
        q4, k2, v2, sink_rows = res
        outs = _gqa_bwd_call(q4, k2, v2, sink_rows, do4, band, c_len)
        kv, n, d = k2.shape
        if not band:
            dq4, dkc, dvc, dsink = outs
            return dq4, dkc, dvc, dsink
        dq4, dkc, dvc, dkb, dvb, dsink = outs

        def fold(ctx_part, bands):
            cur = bands[:, :, BLOCK:2 * BLOCK]
            prv = jnp.pad(bands[:, 1:, :BLOCK], ((0, 0), (0, 1), (0, 0), (0, 0)))
            nxt = jnp.pad(bands[:, :-1, 2 * BLOCK:], ((0, 0), (1, 0), (0, 0), (0, 0)))
            lat = (cur + prv + nxt).reshape(kv, n - c_len, d)
            return jnp.concatenate([ctx_part, lat], axis=1)

        return dq4, fold(dkc, dkb), fold(dvc, dvb), dsink

    f.defvjp(fwd, bwd)
    return f


def _cmul(ar, ai, br, bi):
    return ar * br - ai * bi, ar * bi + ai * br


def _scan_tables(ar, ai, desc):
    a1 = (ar, ai)
    a2 = _cmul(*a1, *a1)
    a4 = _cmul(*a2, *a2)
    pw = [a1]
    for _ in range(SUBLANES - 1):
        pw.append(_cmul(*pw[-1], *a1))
    row = jnp.arange(SUBLANES)[:, None]
    tabs = []
    for dist, (pr, pi) in ((1, a1), (2, a2), (4, a4)):
        keep = (row <= SUBLANES - 1 - dist) if desc else (row >= dist)
        tabs += [jnp.where(keep, pr[None, :], 0.0), jnp.where(keep, pi[None, :], 0.0)]
    order = pw[::-1] if desc else pw
    tabs += [jnp.stack([p[0] for p in order]), jnp.stack([p[1] for p in order])]
    return jnp.stack(tabs).astype(F32)


def _scan_call(b_re, b_im, tabs, order, chunk, prev=None):
    n, s_dim = b_re.shape
    nch = n // chunk
    ng = chunk // SUBLANES
    desc = order in ('Fb', 'R')
    with_da = prev is not None

    def chunk_of(i):
        if order == 'F':
            return i
        if order == 'Fb':
            return nch - 1 - i
        if order == 'R':
            return jnp.where(i == 0, 0, nch - i)
        return jnp.where(i == nch - 1, 0, i + 1)

    def body(*refs):
        br_ref, bi_ref, tab_ref = refs[0], refs[1], refs[2]
        if with_da:
            pr_ref, pi_ref, sr_ref, si_ref, dar_ref, dai_ref, cr_ref, ci_ref = refs[3:]
        else:
            sr_ref, si_ref, pr_ref, pi_ref, cr_ref, ci_ref = refs[3:]

        @pl.when(pl.program_id(0) == 0)
        def _():
            cr_ref[...] = jnp.zeros_like(cr_ref)
            ci_ref[...] = jnp.zeros_like(ci_ref)
            if with_da:
                dar_ref[...] = jnp.zeros_like(dar_ref)
                dai_ref[...] = jnp.zeros_like(dai_ref)

        sub = lax.broadcasted_iota(jnp.int32, (SUBLANES, s_dim), 0)
        edge = SUBLANES - 1 if desc else 0
        last = 0 if desc else SUBLANES - 1

        def step(t, carry):
            gi = (ng - 1 - t) if desc else t
            rows = pl.ds(pl.multiple_of(gi * SUBLANES, SUBLANES), SUBLANES)
            xr, xi = br_ref[rows, :], bi_ref[rows, :]
            for j, dist in enumerate((1, 2, 4)):
                shift = SUBLANES - dist if desc else dist
                rr, ri = pltpu.roll(xr, shift, 0), pltpu.roll(xi, shift, 0)
                mr, mi = tab_ref[2 * j], tab_ref[2 * j + 1]
                xr, xi = xr + mr * rr - mi * ri, xi + mr * ri + mi * rr
            cr, ci = cr_ref[...], ci_ref[...]
            pwr, pwi = tab_ref[6], tab_ref[7]
            sr = xr + pwr * cr - pwi * ci
            si = xi + pwr * ci + pwi * cr
            sr_ref[rows, :] = sr
            si_ref[rows, :] = si
            if with_da:
                pr, pi = pr_ref[rows, :], pi_ref[rows, :]
                dar_ref[...] += sr * pr + si * pi
                dai_ref[...] += si * pr - sr * pi
            else:
                shift1 = SUBLANES - 1 if desc else 1
                pr_ref[rows, :] = jnp.where(sub == edge, cr, pltpu.roll(sr, shift1, 0))
                pi_ref[rows, :] = jnp.where(sub == edge, ci, pltpu.roll(si, shift1, 0))
            cr_ref[...] = jnp.broadcast_to(sr[last:last + 1, :], (SUBLANES, s_dim))
            ci_ref[...] = jnp.broadcast_to(si[last:last + 1, :], (SUBLANES, s_dim))
            return carry

        lax.fori_loop(0, ng, step, 0)

    blk = pl.BlockSpec((chunk, s_dim), lambda i: (chunk_of(i), 0))
    tab_spec = pl.BlockSpec((8, SUBLANES, s_dim), lambda i: (0, 0, 0))
    acc = pl.BlockSpec((SUBLANES, s_dim), lambda i: (0, 0))
    big = jax.ShapeDtypeStruct((n, s_dim), F32)
    small = jax.ShapeDtypeStruct((SUBLANES, s_dim), F32)
    if with_da:
        in_specs, ins = [blk, blk, tab_spec, blk, blk], [b_re, b_im, tabs, prev[0], prev[1]]
        out_specs, out_shape = [blk, blk, acc, acc], [big, big, small, small]
    else:
        in_specs, ins = [blk, blk, tab_spec], [b_re, b_im, tabs]
        out_specs, out_shape = [blk, blk, blk, blk], [big, big, big, big]
    return pl.pallas_call(
        body, name='s5_scan_' + order, grid=(nch,), in_specs=in_specs, out_specs=out_specs, out_shape=out_shape,
        scratch_shapes=[pltpu.VMEM((SUBLANES, s_dim), F32), pltpu.VMEM((SUBLANES, s_dim), F32)],
        compiler_params=_cparams(('arbitrary',)),
    )(*ins)


def _make_scan(rev, chunk):
    def run(b_re, b_im, ar, ai):
        tabs = _scan_tables(ar, ai, desc=rev)
        return _scan_call(b_re, b_im, tabs, 'R' if rev else 'F', chunk)

    @jax.custom_vjp
    def f(b_re, b_im, ar, ai):
        return tuple(run(b_re, b_im, ar, ai)[:2])

    def fwd(b_re, b_im, ar, ai):
        s_re, s_im, p_re, p_im = run(b_re, b_im, ar, ai)
        return (s_re, s_im), (p_re, p_im, ar, ai)

    def bwd(res, g):
        p_re, p_im, ar, ai = res
        tabs = _scan_tables(ar, -ai, desc=not rev)
        db_re, db_im, dar, dai = _scan_call(g[0], g[1], tabs, 'Rb' if rev else 'Fb', chunk, prev=(p_re, p_im))
        return db_re, db_im, jnp.sum(dar, axis=0), jnp.sum(dai, axis=0)

    f.defvjp(fwd, bwd)
    return f


def _sqerr_call(y, t):
    n, d = y.shape
    rb = _tile(n, 512, SUBLANES)

    def body(y_ref, t_ref, o_ref):
        @pl.when(pl.program_id(0) == 0)
        def _():
            o_ref[...] = jnp.zeros_like(o_ref)

        e = y_ref[...] - t_ref[...]
        o_ref[...] += jnp.sum(e * e, axis=0, keepdims=True)

    row = pl.BlockSpec((rb, d), lambda i: (i, 0))
    return pl.pallas_call(
        body, name='sq_err', grid=(n // rb,), in_specs=[row, row], out_specs=pl.BlockSpec((1, d), lambda i: (0, 0)),
        out_shape=jax.ShapeDtypeStruct((1, d), F32), compiler_params=_cparams(('arbitrary',)),
    )(y, t)


@jax.custom_vjp
def loss_head(y, t):
    return 0.5 * jnp.sum(_sqerr_call(y, t)) / y.shape[1]


def _loss_head_fwd(y, t):
    return loss_head(y, t), (y, t)


def _loss_head_bwd(res, g):
    y, t = res
    return g * (y - t) / y.shape[1], None


loss_head.defvjp(_loss_head_fwd, _loss_head_bwd)


def _adamw_call(w, g, m, v):
    r, c = w.shape
    rb = _tile(r, max(SUBLANES, (256 * 1024) // max(c, LANES) // SUBLANES * SUBLANES), SUBLANES)

    def body(w_ref, g_ref, m_ref, v_ref, d_ref, nm_ref, nv_ref):
        gv = g_ref[...]
        nm = ADAM_B1 * m_ref[...] + (1.0 - ADAM_B1) * gv
        nv = ADAM_B2 * v_ref[...] + (1.0 - ADAM_B2) * (gv * gv)
        m_hat = nm / (1.0 - ADAM_B1 ** ADAM_STEP)
        v_hat = nv / (1.0 - ADAM_B2 ** ADAM_STEP)
        d_ref[...] = -ADAM_LR * (m_hat / (jnp.sqrt(v_hat) + ADAM_EPS) + ADAM_WD * w_ref[...])
        nm_ref[...] = nm
        nv_ref[...] = nv

    blk = pl.BlockSpec((rb, c), lambda i: (i, 0))
    shape = jax.ShapeDtypeStruct((r, c), F32)
    return pl.pallas_call(
        body, name='adamw', grid=(r // rb,), in_specs=[blk] * 4, out_specs=[blk] * 3, out_shape=[shape] * 3,
        compiler_params=_cparams(('parallel',)),
    )(w, g, m, v)


MESH = pl.DeviceIdType.MESH
HBM_SPEC = pl.BlockSpec(memory_space=pltpu.HBM)


def _all_gather(x):
    def body(x_ref, out_ref, send_sems, recv_sems, local_sem):
        x, y, c = lax.axis_index('x'), lax.axis_index('y'), lax.axis_index('c')
        me, sibling = (x, y, c), (x, y, 1 - c)
        chips = [(1 - x, y), (x, 1 - y), (1 - x, 1 - y)]

        def slot(px, py, pc):
            return out_ref.at[4 * px + 2 * py + pc]

        def copy(k, block, to, src=None):
            return pltpu.make_async_remote_copy(
                src_ref=slot(*block) if src is None else src, dst_ref=slot(*block),
                send_sem=send_sems.at[k], recv_sem=recv_sems.at[k], device_id=to, device_id_type=MESH)

        mine = pltpu.make_async_copy(x_ref, slot(*me), local_sem)
        mine.start()
        first = [copy(0, me, sibling, src=x_ref)]
        first += [copy(1 + j, me, (*chip, c), src=x_ref) for j, chip in enumerate(chips)]
        for cp in first:
            cp.start()
        passed = [copy(4 + j, (*chip, c), sibling) for j, chip in enumerate(chips)]
        for j, chip in enumerate(chips):
            copy(1 + j, (*chip, c), me).wait_recv()
            passed[j].start()
        copy(0, sibling, me).wait_recv()
        for j, chip in enumerate(chips):
            copy(4 + j, (*chip, 1 - c), me).wait_recv()
        for cp in first + passed:
            cp.wait_send()
        mine.wait()

    return pl.pallas_call(
        body, name='all_gather', out_shape=jax.ShapeDtypeStruct((N_DEV,) + x.shape, x.dtype),
        in_specs=[HBM_SPEC], out_specs=HBM_SPEC,
        scratch_shapes=[pltpu.SemaphoreType.DMA((7,)), pltpu.SemaphoreType.DMA((7,)), pltpu.SemaphoreType.DMA],
    )(x)


def _exchange_sibling(g_all):
    def body(g_ref, out_ref, send_sem, recv_sem):
        x, y, c = lax.axis_index('x'), lax.axis_index('y'), lax.axis_index('c')
        cp = pltpu.make_async_remote_copy(src_ref=g_ref.at[1 - c], dst_ref=out_ref, send_sem=send_sem, recv_sem=recv_sem,
                                          device_id=(x, y, 1 - c), device_id_type=MESH)
        cp.start()
        cp.wait()

    return pl.pallas_call(
        body, name='rs_sibling', out_shape=jax.ShapeDtypeStruct(g_all.shape[1:], g_all.dtype),
        in_specs=[HBM_SPEC], out_specs=HBM_SPEC,
        scratch_shapes=[pltpu.SemaphoreType.DMA, pltpu.SemaphoreType.DMA],
    )(g_all)


def _exchange_chips(p):
    def body(p_ref, out_ref, send_sems, recv_sems):
        x, y, c = lax.axis_index('x'), lax.axis_index('y'), lax.axis_index('c')
        chips = [(1 - x, y), (x, 1 - y), (1 - x, 1 - y)]
        copies = [pltpu.make_async_remote_copy(src_ref=p_ref.at[2 * px + py], dst_ref=out_ref.at[j],
                                               send_sem=send_sems.at[j], recv_sem=recv_sems.at[j],
                                               device_id=(px, py, c), device_id_type=MESH)
                  for j, (px, py) in enumerate(chips)]
        for cp in copies:
            cp.start()
        for cp in copies:
            cp.wait_recv()
        for cp in copies:
            cp.wait_send()

    return pl.pallas_call(
        body, name='rs_chips', out_shape=jax.ShapeDtypeStruct((3,) + p.shape[1:], p.dtype),
        in_specs=[HBM_SPEC], out_specs=HBM_SPEC,
        scratch_shapes=[pltpu.SemaphoreType.DMA((3,)), pltpu.SemaphoreType.DMA((3,))],
    )(p)


def _add_sibling(g_all, recv, c_idx):
    _, nchip, r, _ = g_all.shape
    rb = _tile(r, PACK_ROWS, SUBLANES)

    def body(c_ref, g_ref, r_ref, o_ref, ob_ref):
        s = g_ref[0] + r_ref[...]
        o_ref[...] = s
        ob_ref[...] = s.astype(BF16)

    blk = pl.BlockSpec((1, rb, LANES), lambda k, i, c: (k, i, 0))
    return pl.pallas_call(
        body, name='rs_add_sibling',
        grid_spec=pltpu.PrefetchScalarGridSpec(
            num_scalar_prefetch=1, grid=(nchip, r // rb),
            in_specs=[pl.BlockSpec((1, 1, rb, LANES), lambda k, i, c: (c[0], k, i, 0)), blk],
            out_specs=[blk, blk]),
        out_shape=[jax.ShapeDtypeStruct(recv.shape, F32), jax.ShapeDtypeStruct(recv.shape, BF16)],
        compiler_params=_cparams(('parallel', 'parallel')),
    )(c_idx, g_all, recv)


def _add_chips(p, recv, chip_idx):
    _, r, _ = p.shape
    rb = _tile(r, PACK_ROWS, SUBLANES)

    def body(k_ref, p_ref, r0, r1, r2, o_ref):
        o_ref[...] = ((p_ref[0] + r0[0].astype(F32)) + r1[0].astype(F32)) + r2[0].astype(F32)

    rspec = lambda j: pl.BlockSpec((1, rb, LANES), lambda i, k: (j, i, 0))
    return pl.pallas_call(
        body, name='rs_add_chips',
        grid_spec=pltpu.PrefetchScalarGridSpec(
            num_scalar_prefetch=1, grid=(r // rb,),
            in_specs=[pl.BlockSpec((1, rb, LANES), lambda i, k: (k[0], i, 0)), rspec(0), rspec(1), rspec(2)],
            out_specs=pl.BlockSpec((rb, LANES), lambda i, k: (i, 0))),
        out_shape=jax.ShapeDtypeStruct((r, LANES), F32), compiler_params=_cparams(('parallel',)),
    )(chip_idx, p, recv, recv, recv)


def _reduce_scatter(g_all, c_idx, chip_idx):
    part, part_bf16 = _add_sibling(g_all, _exchange_sibling(g_all), c_idx)
    return _add_chips(part, _exchange_chips(part_bf16), chip_idx)


def _pad_to(n, mult):
    return (n + mult - 1) // mult * mult


def _pack(pieces, lead, dtype):
    flat = []
    total = 0
    for p in pieces:
        f = p.reshape(lead + (-1,)).astype(dtype)
        n = _pad_to(f.shape[-1], 16 * LANES)
        flat.append(jnp.pad(f, [(0, 0)] * len(lead) + [(0, n - f.shape[-1])]))
        total += n
    full = _pad_to(total, PACK_ROWS * LANES)
    if full > total:
        flat.append(jnp.zeros(lead + (full - total,), dtype))
    return jnp.concatenate(flat, axis=-1).reshape(lead + (full // LANES, LANES))


def _unpack(buf, lead, shapes):
    flat = buf.reshape(lead + (-1,))
    out, off = [], 0
    for s in shapes:
        n = math.prod(s)
        out.append(flat[..., off:off + n].reshape(lead + tuple(s)))
        off += _pad_to(n, 16 * LANES)
    return out


def _make_split(sizes, width):
    starts = [sum(sizes[:i]) for i in range(len(sizes))]
    tail = width - sum(sizes)

    @jax.custom_vjp
    def f(z):
        return tuple(z[:, o:o + s] for o, s in zip(starts, sizes))

    def fwd(z):
        return f(z), None

    def bwd(_, gs):
        pieces = list(gs) + ([jnp.zeros((gs[0].shape[0], tail), gs[0].dtype)] if tail else [])
        return (jnp.concatenate(pieces, axis=1),)

    f.defvjp(fwd, bwd)
    return f


def _rope_tables(c_len, t_len, n):
    quarter = n // 4
    inv = ROPE_BASE ** (-jnp.arange(0, 2 * quarter, 2, dtype=F32) / (2 * quarter))
    t = jnp.arange(t_len, dtype=jnp.int32)
    pos = jnp.stack([(t // GRID_W).astype(F32), (t % GRID_W).astype(F32)], axis=1)
    ang = pos[:, :, None] * inv[None, None, :]
    ang = jnp.concatenate([jnp.zeros((c_len, 2, quarter), F32), ang], axis=0)
    return jnp.cos(ang), jnp.sin(ang)


def _axial_rope(x, cos, sin):
    n_rows, h, n = x.shape
    xs = x.reshape(n_rows, h, 2, 2, n // 4)
    x1, x2 = xs[:, :, :, 0], xs[:, :, :, 1]
    c, s = cos[:, None], sin[:, None]
    return jnp.stack([x1 * c - x2 * s, x1 * s + x2 * c], axis=3).reshape(n_rows, h, n)


def _ssm_discretize(lam_re, lam_im, log_dt, b_re, b_im):
    dt = jnp.exp(log_dt)[:, None]
    mag = jnp.exp(lam_re * dt)
    a_re, a_im = mag * jnp.cos(lam_im * dt), mag * jnp.sin(lam_im * dt)
    den = lam_re * lam_re + lam_im * lam_im
    w_re = ((a_re - 1) * lam_re + a_im * lam_im) / den
    w_im = (a_im * lam_re - (a_re - 1) * lam_im) / den
    bb_re, bb_im = _cmul(w_re[..., None], w_im[..., None], b_re, b_im)
    return a_re, a_im, bb_re, bb_im


def _block_diag_in(b):
    g = b.shape[0]
    return jnp.einsum('gpm,gh->gmhp', b, jnp.eye(g, dtype=F32)).reshape(g * b.shape[2], g * b.shape[1])


def _block_diag_out(c):
    g = c.shape[0]
    return jnp.einsum('gmp,gh->gphm', c, jnp.eye(g, dtype=F32)).reshape(g * c.shape[2], g * c.shape[1])


def _w_in_layout(d_model):
    sizes = (MLA_Q_RANK, MLA_KV_RANK, MLA_ROPE, SSM_WIDTH, GQA_HEADS * GQA_D, GQA_KV * GQA_D, GQA_KV * GQA_D, 3 * d_model)
    starts = [0]
    for s in sizes[:-1]:
        starts.append(starts[-1] + s)
    names = ('cq', 'ckv', 'kr', 'u', 'gq', 'gk', 'gv', 'gates')
    orig = dict(zip(names, zip(starts, sizes)))
    order = ('cq', 'ckv', 'u', 'gq', 'gk', 'gv', 'gates', 'kr')
    return orig, order


def _permute_w_in(blocks, d_model):
    orig, order = _w_in_layout(d_model)
    w = blocks.reshape(-1, blocks.shape[2])
    rows = [w[orig[k][0]:orig[k][0] + orig[k][1]] for k in order]
    width = sum(orig[k][1] for k in order)
    return jnp.pad(jnp.concatenate(rows, axis=0), ((0, _pad_to(width, LANES) - width), (0, 0)))


def _unpermute_w_in(gp, d_model, r):
    orig, order = _w_in_layout(d_model)
    pos, off = {}, 0
    for k in order:
        pos[k] = off
        off += orig[k][1]
    names = sorted(orig, key=lambda k: orig[k][0])
    w = jnp.concatenate([gp[pos[k]:pos[k] + orig[k][1]] for k in names], axis=0)
    return w.reshape(N_DEV, r, gp.shape[1])


def _forward_loss(x_all, sinks, rp, wl, cc_in, target, c_len):
    n, d = x_all.shape
    t_len = n - c_len
    depth = len(wl)
    mla_attn = _make_mla(c_len)
    norm_mod = _make_norm(c_len, True)
    norm_tok = _make_norm(_tile(n, 512, SUBLANES), False)
    norm_out = _make_norm(_tile(t_len, 512, SUBLANES), False)
    half_res = _make_gated_res(c_len, 0.5)
    full_res = _make_gated_res(c_len, 1.0)
    gqa_band = _make_gqa(True, c_len)
    gqa_ctx = _make_gqa(False, c_len)
    scans = (_make_scan(False, c_len), _make_scan(True, c_len))
    cos_m, sin_m = _rope_tables(c_len, t_len, MLA_ROPE)
    cos_g, sin_g = _rope_tables(c_len, t_len, GQA_D)
    orig, order = _w_in_layout(d)
    split_in = _make_split([orig[k][1] for k in order], _pad_to(sum(orig[k][1] for k in order), LANES))
    halves = lambda a: _make_split([a.shape[1] // 2] * 2, a.shape[1])(a)

    cc = jnp.zeros((SUBLANES, d), F32).at[0].set(jax.nn.silu(rp['c_ctx'])).at[1].set(jax.nn.silu(cc_in))

    for l in range(depth):
        w, sk = wl[l], sinks[l]
        ctx_out = l < depth - 1

        def mm(h, name):
            return (matmul if name in ROW_SHARDED else matmul_t)(h, w[name], sk[name])

        def swiglu(h, name13, name2):
            return swiglu_ffn(h, w[name13], sk[name13], w[name2], sk[name2])

        mod = mm(cc, 'ada_w') + rp['ada_b'][l][None, :]
        md = [mod[0:2, i * d:(i + 1) * d] for i in range(N_MOD)]
        x_all = half_res(x_all, swiglu(norm_mod(x_all, rp['norm_ffn1'][l], md[0], md[1]), 'ffn1_w13', 'ffn1_w2'), md[2])

        z = mm(norm_mod(x_all, rp['norm_mix'][l], md[3], md[4]), 'w_in')
        part = dict(zip(order, split_in(z)))

        q3 = mm(norm_tok(part['cq'], rp['mla_q_norm'][l]), 'mla_w_uq').reshape(n, MLA_HEADS, LANES)
        q = jnp.concatenate([q3[..., :MLA_NOPE], _axial_rope(q3[..., MLA_NOPE:MLA_NOPE + MLA_ROPE], cos_m, sin_m),
                             q3[..., MLA_NOPE + MLA_ROPE:]], axis=-1).reshape(n, MLA_HEADS * LANES)
        kvp = mm(norm_tok(part['ckv'], rp['mla_kv_norm'][l]), 'mla_w_ukv')
        kr = _axial_rope(part['kr'].reshape(n, 1, MLA_ROPE), cos_m, sin_m).reshape(n, MLA_ROPE)
        kr = jnp.pad(kr, ((0, 0), (MLA_NOPE, LANES - MLA_NOPE - MLA_ROPE)))
        mla = mm(mla_attn(q, kvp, kr), 'mla_w_o')

        u = part['u']
        y = u * rp['ssm_d'][l][None, :]
        for direction in range(2):
            a_re, a_im, bb_re, bb_im = _ssm_discretize(
                rp['ssm_lambda_re'][l, direction], rp['ssm_lambda_im'][l, direction], rp['ssm_log_dt'][l, direction],
                rp['ssm_b_re'][l, direction], rp['ssm_b_im'][l, direction])
            s_re, s_im = scans[direction](matmul_d(u, _block_diag_in(bb_re)), matmul_d(u, _block_diag_in(bb_im)),
                                          a_re.reshape(-1), a_im.reshape(-1))
            y = y + (matmul_d(s_re, _block_diag_out(rp['ssm_c_re'][l, direction]))
                     - matmul_d(s_im, _block_diag_out(rp['ssm_c_im'][l, direction])))
        yg = mm(jax.nn.gelu(y), 'ssm_w_glu')
        ya, ygate = halves(yg)
        ssm = ya * jax.nn.sigmoid(ygate)

        gq = _axial_rope(part['gq'].reshape(n, GQA_HEADS, GQA_D), cos_g, sin_g)
        gk = _axial_rope(part['gk'].reshape(n, GQA_KV, GQA_D), cos_g, sin_g)
        q4 = jnp.transpose(gq.reshape(n, GQA_KV, GQA_G, GQA_D), (1, 2, 0, 3))
        k2 = jnp.transpose(gk, (1, 0, 2))
        v2 = jnp.transpose(part['gv'].reshape(n, GQA_KV, GQA_D), (1, 0, 2))
        sink = rp['gqa_sink'][l].reshape(GQA_KV, GQA_G, 1, 1)
        sink_rows = lambda rows: jnp.broadcast_to(sink, (GQA_KV, GQA_G, rows, 1)).reshape(GQA_KV, GQA_G * rows, 1)
        g_lat = gqa_band(q4[:, :, c_len:], k2, v2, sink_rows(BLOCK))
        if ctx_out:
            g_ctx = gqa_ctx(q4[:, :, :c_len], k2[:, :c_len], v2[:, :c_len], sink_rows(c_len))
        else:
            g_ctx = jnp.zeros((GQA_KV, GQA_G, c_len, GQA_D), F32)
        go = jnp.transpose(jnp.concatenate([g_ctx, g_lat], axis=2), (2, 0, 1, 3)).reshape(n, GQA_HEADS * GQA_D)
        gqa = mm(go, 'gqa_w_o')

        g0, g1, g2 = _make_split([d] * 3, 3 * d)(jax.nn.sigmoid(part['gates']))
        mixed = g0 * mla + g1 * ssm + g2 * gqa
        x_all = full_res(x_all, mm(mixed, 'w_out'), md[5])
        x_all = half_res(x_all, swiglu(norm_mod(x_all, rp['norm_ffn2'][l], md[6], md[7]), 'ffn2_w13', 'ffn2_w2'), md[8])

    return loss_head(norm_out(x_all[c_len:], rp['final_norm']), target)


def kernel(x, c, ctx, c_ctx, ada_w, ada_b, norm_ffn1, norm_mix, norm_ffn2, ffn1_w13, ffn1_w2, ffn2_w13, ffn2_w2, w_in, mla_q_norm, mla_kv_norm, mla_w_uq, mla_w_ukv, mla_w_o, ssm_lambda_re, ssm_lambda_im, ssm_log_dt, ssm_b_re, ssm_b_im, ssm_c_re, ssm_c_im, ssm_d, ssm_w_glu, gqa_sink, gqa_w_o, w_out, final_norm, loss_target, m_c_ctx, m_ada_w, m_ada_b, m_norm_ffn1, m_norm_mix, m_norm_ffn2, m_ffn1_w13, m_ffn1_w2, m_ffn2_w13, m_ffn2_w2, m_w_in, m_mla_q_norm, m_mla_kv_norm, m_mla_w_uq, m_mla_w_ukv, m_mla_w_o, m_ssm_lambda_re, m_ssm_lambda_im, m_ssm_log_dt, m_ssm_b_re, m_ssm_b_im, m_ssm_c_re, m_ssm_c_im, m_ssm_d, m_ssm_w_glu, m_gqa_sink, m_gqa_w_o, m_w_out, m_final_norm, v_c_ctx, v_ada_w, v_ada_b, v_norm_ffn1, v_norm_mix, v_norm_ffn2, v_ffn1_w13, v_ffn1_w2, v_ffn2_w13, v_ffn2_w2, v_w_in, v_mla_q_norm, v_mla_kv_norm, v_mla_w_uq, v_mla_w_ukv, v_mla_w_o, v_ssm_lambda_re, v_ssm_lambda_im, v_ssm_log_dt, v_ssm_b_re, v_ssm_b_im, v_ssm_c_re, v_ssm_c_im, v_ssm_d, v_ssm_w_glu, v_gqa_sink, v_gqa_w_o, v_w_out, v_final_norm):
    args = dict(locals())
    weights = {k: args[k] for k in WEIGHTS}
    moments_m = {k: args['m_' + k] for k in WEIGHTS}
    moments_v = {k: args['v_' + k] for k in WEIGHTS}
    depth = ada_w.shape[0]
    d = x.shape[-1]
    c_len = ctx.shape[1]
    my_c = lax.axis_index('c')
    my_chip = 2 * lax.axis_index('x') + lax.axis_index('y')
    qk_w = MLA_NOPE + MLA_ROPE

    def as_rows(k, a):
        return a if k in ROW_SHARDED else a.T

    shard_shapes = [as_rows(k, weights[k][0]).shape for k in SHARDED]
    r_in = weights['w_in'].shape[2]
    layers = []
    for l in range(depth):
        gathered = _all_gather(_pack([as_rows(k, weights[k][l]) for k in SHARDED], (), BF16))
        full = {}
        for k, blocks in zip(SHARDED, _unpack(gathered, (N_DEV,), shard_shapes)):
            full[k] = _permute_w_in(blocks, d) if k == 'w_in' else blocks.reshape(N_DEV * blocks.shape[1], blocks.shape[2])
        full['mla_w_uq'] = jnp.pad(full['mla_w_uq'].reshape(MLA_HEADS, qk_w, MLA_Q_RANK),
                                   ((0, 0), (0, LANES - qk_w), (0, 0))).reshape(MLA_HEADS * LANES, MLA_Q_RANK)
        full['mla_w_o'] = jnp.pad(full['mla_w_o'].reshape(d, MLA_HEADS, MLA_V),
                                  ((0, 0), (0, 0), (LANES - MLA_V, 0))).reshape(d, MLA_HEADS * LANES)
        layers.append(full)
    sinks = [{k: jnp.zeros(v.shape, F32) for k, v in full.items()} for full in layers]

    rp = {k: weights[k] for k in REPLICATED}
    x_all = jnp.concatenate([ctx[0], x[0]], axis=0)
    loss_fn = functools.partial(_forward_loss, wl=layers, cc_in=c[0], target=loss_target[0], c_len=c_len)
    loss, vjp = jax.vjp(loss_fn, x_all, sinks, rp)
    g_x, g_layers, g_rp = vjp(jnp.ones((), F32))

    grads = {k: [] for k in SHARDED}
    for l in range(depth):
        gl = dict(g_layers[l])
        gl['mla_w_uq'] = gl['mla_w_uq'].reshape(MLA_HEADS, LANES, MLA_Q_RANK)[:, :qk_w].reshape(MLA_HEADS * qk_w, MLA_Q_RANK)
        gl['mla_w_o'] = gl['mla_w_o'].reshape(d, MLA_HEADS, LANES)[:, :, LANES - MLA_V:].reshape(d, MLA_HEADS * MLA_V)
        pieces = [_unpermute_w_in(gl[k], d, r_in) if k == 'w_in' else gl[k].reshape((N_DEV,) + shape)
                  for k, shape in zip(SHARDED, shard_shapes)]
        packed = _pack(pieces, (N_DEV,), F32)
        g_all = jnp.swapaxes(packed.reshape((4, 2) + packed.shape[1:]), 0, 1)
        mine = _reduce_scatter(g_all, my_c.reshape(1).astype(jnp.int32), my_chip.reshape(1).astype(jnp.int32))
        for k, g in zip(SHARDED, _unpack(mine, (), shard_shapes)):
            grads[k].append(as_rows(k, g))
    grads = {k: jnp.stack(v) for k, v in grads.items()}

    rep_shapes = [weights[k].shape for k in REPLICATED] + [(1,)]
    small = _pack([g_rp[k] for k in REPLICATED] + [loss.reshape(1)], (), F32)
    summed = _unpack(_sum_leading(_all_gather(small)), (), rep_shapes)
    for k, g in zip(REPLICATED, summed[:-1]):
        grads[k] = g
    loss_total = summed[-1].reshape(())

    delta, new_m, new_v = {}, {}, {}
    for k in SHARDED:
        shape = weights[k].shape
        as2d = lambda a: a.reshape(-1, shape[-1])
        outs = _adamw_call(as2d(weights[k]), as2d(grads[k]), as2d(moments_m[k]), as2d(moments_v[k]))
        delta[k], new_m[k], new_v[k] = (o.reshape(shape) for o in outs)
    rep_all = [weights[k].shape for k in REPLICATED]
    packs = [_pack([src[k] for k in REPLICATED], (), F32) for src in (weights, grads, moments_m, moments_v)]
    outs = [_unpack(o, (), rep_all) for o in _adamw_call(*packs)]
    for i, k in enumerate(REPLICATED):
        delta[k], new_m[k], new_v[k] = outs[0][i], outs[1][i], outs[2][i]

    return (loss_total, g_x[c_len:][None], *[grads[k] for k in WEIGHTS], *[delta[k] for k in WEIGHTS],
            *[new_m[k] for k in WEIGHTS], *[new_v[k] for k in WEIGHTS])
```

```python
import functools
import math

import jax
import jax.numpy as jnp
from jax import lax
from jax.experimental import pallas as pl
from jax.experimental.pallas import tpu as pltpu

F32 = jnp.float32
BF16 = jnp.bfloat16

MLA_HEADS, MLA_NOPE, MLA_ROPE, MLA_V = 8, 64, 32, 64
MLA_Q_RANK, MLA_KV_RANK = 384, 256
SSM_WIDTH, SSM_GROUP, SSM_STATE = 512, 16, 64
SSM_GROUPS = SSM_WIDTH // SSM_GROUP
GQA_HEADS, GQA_KV, GQA_D = 8, 2, 64
GQA_G = GQA_HEADS // GQA_KV
WINDOW, BLOCK, GRID_W = 128, 128, 64
N_MOD = 9
ROPE_BASE = 10000.0
EPS = 1e-6
NEG_INF = -1e30
ADAM_LR, ADAM_B1, ADAM_B2, ADAM_EPS, ADAM_WD, ADAM_STEP = 0.001, 0.9, 0.999, 1e-08, 0.01, 10

N_DEV = 8
LANES = 128
SUBLANES = 8
VMEM_LIMIT = 56 * 1024 * 1024
PACK_ROWS = 1024

SHARDED = ('ada_w', 'ffn1_w13', 'ffn1_w2', 'ffn2_w13', 'ffn2_w2', 'w_in', 'mla_w_uq', 'mla_w_ukv',
           'mla_w_o', 'ssm_w_glu', 'gqa_w_o', 'w_out')
ROW_SHARDED = ('ffn1_w2', 'ffn2_w2', 'w_out')
REPLICATED = ('c_ctx', 'ada_b', 'norm_ffn1', 'norm_mix', 'norm_ffn2', 'mla_q_norm', 'mla_kv_norm',
              'ssm_lambda_re', 'ssm_lambda_im', 'ssm_log_dt', 'ssm_b_re', 'ssm_b_im', 'ssm_c_re', 'ssm_c_im',
              'ssm_d', 'gqa_sink', 'final_norm')
WEIGHTS = ('c_ctx', 'ada_w', 'ada_b', 'norm_ffn1', 'norm_mix', 'norm_ffn2', 'ffn1_w13', 'ffn1_w2', 'ffn2_w13',
           'ffn2_w2', 'w_in', 'mla_q_norm', 'mla_kv_norm', 'mla_w_uq', 'mla_w_ukv', 'mla_w_o', 'ssm_lambda_re',
           'ssm_lambda_im', 'ssm_log_dt', 'ssm_b_re', 'ssm_b_im', 'ssm_c_re', 'ssm_c_im', 'ssm_d', 'ssm_w_glu',
           'gqa_sink', 'gqa_w_o', 'w_out', 'final_norm')


def _tile(n, target, mult):
    t = (min(target, n) // mult) * mult
    while t >= mult:
        if n % t == 0:
            return t
        t -= mult
    return n


def _cparams(sem):
    return pltpu.CompilerParams(dimension_semantics=sem, vmem_limit_bytes=VMEM_LIMIT)


def _mm(a, b, mode):
    if mode == 'nn':
        (M, K), N = a.shape, b.shape[1]
        tm, tn, tk = _tile(M, 1408, SUBLANES), _tile(N, 1024, LANES), _tile(K, 1408, LANES)
    elif mode == 'nt':
        (M, K), N = a.shape, b.shape[0]
        tm, tn, tk = _tile(M, 1408, SUBLANES), _tile(N, 1024, LANES), _tile(K, 1408, LANES)
    else:
        (K, M), N = a.shape, b.shape[1]
        tm, tn, tk = _tile(M, 1408, LANES), _tile(N, 1408, LANES), _tile(K, 768, 2 * SUBLANES)
    nk = K // tk
    dims = {'nn': (((1,), (0,)), ((), ())), 'nt': (((1,), (1,)), ((), ())), 'tn': (((0,), (0,)), ((), ()))}[mode]
    keep_a = nk == 1 and mode != 'tn' and N // tn > 1 and a.dtype != BF16

    def body(a_ref, b_ref, o_ref, *scratch):
        if keep_a:
            @pl.when(pl.program_id(1) == 0)
            def _():
                scratch[0][...] = a_ref[...].astype(BF16)

            av = scratch[0][...]
        else:
            av = a_ref[...].astype(BF16)
        part = lax.dot_general(av, b_ref[...].astype(BF16), dims, preferred_element_type=F32)
        if nk == 1:
            o_ref[...] = part
        else:
            @pl.when(pl.program_id(2) == 0)
            def _():
                o_ref[...] = part

            @pl.when(pl.program_id(2) > 0)
            def _():
                o_ref[...] += part

    a_spec = pl.BlockSpec((tk, tm), lambda i, j, k: (k, i)) if mode == 'tn' else pl.BlockSpec((tm, tk), lambda i, j, k: (i, k))
    b_spec = pl.BlockSpec((tn, tk), lambda i, j, k: (j, k)) if mode == 'nt' else pl.BlockSpec((tk, tn), lambda i, j, k: (k, j))
    return pl.pallas_call(
        body, name='mm_' + mode, grid=(M // tm, N // tn, nk),
        in_specs=[a_spec, b_spec], out_specs=pl.BlockSpec((tm, tn), lambda i, j, k: (i, j)),
        out_shape=jax.ShapeDtypeStruct((M, N), F32),
        scratch_shapes=[pltpu.VMEM((tm, tk), BF16)] if keep_a else [],
        compiler_params=_cparams(('parallel', 'arbitrary', 'arbitrary')),
    )(a, b)


@jax.custom_vjp
def matmul_d(x, w):
    return _mm(x, w, 'nn')


def _matmul_d_fwd(x, w):
    return _mm(x, w, 'nn'), (x, w)


def _matmul_d_bwd(res, g):
    x, w = res
    return _mm(g, w, 'nt'), _mm(x, g, 'tn')


matmul_d.defvjp(_matmul_d_fwd, _matmul_d_bwd)


@jax.custom_vjp
def matmul(x, w, sink):
    return _mm(x, w, 'nn')


def _matmul_fwd(x, w, sink):
    return _mm(x, w, 'nn'), (x, w)


def _matmul_bwd(res, g):
    x, w = res
    return _mm(g, w, 'nt'), jnp.zeros_like(w), _mm(x, g, 'tn')


matmul.defvjp(_matmul_fwd, _matmul_bwd)


@jax.custom_vjp
def matmul_t(x, wt, sink):
    return _mm(x, wt, 'nt')


def _matmul_t_fwd(x, wt, sink):
    return _mm(x, wt, 'nt'), (x, wt)


def _matmul_t_bwd(res, g):
    x, wt = res
    return _mm(g, wt, 'nn'), jnp.zeros_like(wt), _mm(g, x, 'tn')


matmul_t.defvjp(_matmul_t_fwd, _matmul_t_bwd)


def _ffn_up_call(x, w13t):
    m, k = x.shape
    f = w13t.shape[0] // 2
    tm, tn = _tile(m, 1408, SUBLANES), _tile(f, 256, LANES)
    nf = f // tn

    def body(x_ref, wa_ref, wb_ref, a_ref, b_ref, act_ref, xb):
        @pl.when(pl.program_id(1) == 0)
        def _():
            xb[...] = x_ref[...].astype(BF16)

        a = lax.dot_general(xb[...], wa_ref[...], NT_DIMS, preferred_element_type=F32)
        b = lax.dot_general(xb[...], wb_ref[...], NT_DIMS, preferred_element_type=F32)
        a_ref[...] = a
        b_ref[...] = b
        act_ref[...] = (a * jax.nn.sigmoid(a) * b).astype(BF16)

    out = pl.BlockSpec((tm, tn), lambda i, j: (i, j))
    return pl.pallas_call(
        body, name='ffn_up', grid=(m // tm, nf),
        in_specs=[pl.BlockSpec((tm, k), lambda i, j: (i, 0)), pl.BlockSpec((tn, k), lambda i, j: (j, 0)),
                  pl.BlockSpec((tn, k), lambda i, j: (j + nf, 0))],
        out_specs=[out, out, out],
        out_shape=[jax.ShapeDtypeStruct((m, f), F32), jax.ShapeDtypeStruct((m, f), F32), jax.ShapeDtypeStruct((m, f), BF16)],
        scratch_shapes=[pltpu.VMEM((tm, k), BF16)],
        compiler_params=_cparams(('parallel', 'arbitrary')),
    )(x, w13t, w13t)


@jax.custom_vjp
def swiglu_ffn(x, w13t, sink13, w2, sink2):
    return _mm(_ffn_up_call(x, w13t)[2], w2, 'nn')


def _swiglu_ffn_fwd(x, w13t, sink13, w2, sink2):
    a, b, act = _ffn_up_call(x, w13t)
    return _mm(act, w2, 'nn'), (x, a, b, act, w13t, w2)


def _ffn_down_bwd_call(g, w2, a, b):
    m, d = g.shape
    f = w2.shape[0]
    tm, tn = _tile(m, 1408, SUBLANES), _tile(f, 256, LANES)
    nf = f // tn

    def body(g_ref, w_ref, a_ref, b_ref, o_ref, gb):
        j = pl.program_id(1)

        @pl.when(j == 0)
        def _():
            gb[...] = g_ref[...].astype(BF16)

        dact = lax.dot_general(gb[...], w_ref[...], NT_DIMS, preferred_element_type=F32)
        av = a_ref[...]
        sa = jax.nn.sigmoid(av)

        @pl.when(j < nf)
        def _():
            o_ref[...] = dact * b_ref[...] * (sa * (1.0 + av * (1.0 - sa)))

        @pl.when(j >= nf)
        def _():
            o_ref[...] = dact * (av * sa)

    half = pl.BlockSpec((tm, tn), lambda i, j: (i, lax.rem(j, nf)))
    return pl.pallas_call(
        body, name='ffn_down_bwd', grid=(m // tm, 2 * nf),
        in_specs=[pl.BlockSpec((tm, d), lambda i, j: (i, 0)), pl.BlockSpec((tn, d), lambda i, j: (lax.rem(j, nf), 0)), half, half],
        out_specs=pl.BlockSpec((tm, tn), lambda i, j: (i, j)),
        out_shape=jax.ShapeDtypeStruct((m, 2 * f), F32),
        scratch_shapes=[pltpu.VMEM((tm, d), BF16)],
        compiler_params=_cparams(('parallel', 'arbitrary')),
    )(g, w2, a, b)


def _swiglu_ffn_bwd(res, g):
    x, a, b, act, w13t, w2 = res
    d13 = _ffn_down_bwd_call(g, w2, a, b)
    return _mm(d13, w13t, 'nn'), jnp.zeros_like(w13t), _mm(d13, x, 'tn'), jnp.zeros_like(w2), _mm(act, g, 'tn')


swiglu_ffn.defvjp(_swiglu_ffn_fwd, _swiglu_ffn_bwd)


def _norm_fwd_call(x, g, sh, sc, rb):
    n, d = x.shape
    has_mod = sh is not None

    def body(*refs):
        x_ref, g_ref = refs[0], refs[1]
        o_ref = refs[-1]
        xv = x_ref[...]
        r = lax.rsqrt(jnp.mean(xv * xv, axis=-1, keepdims=True) + EPS)
        y = xv * r * g_ref[...]
        if has_mod:
            lat = pl.program_id(0) > 0
            shv = jnp.where(lat, refs[2][1:2, :], refs[2][0:1, :])
            scv = jnp.where(lat, refs[3][1:2, :], refs[3][0:1, :])
            y = y * (1.0 + scv) + shv
        o_ref[...] = y

    row = pl.BlockSpec((rb, d), lambda i: (i, 0))
    vec = pl.BlockSpec((1, d), lambda i: (0, 0))
    two = pl.BlockSpec((2, d), lambda i: (0, 0))
    ins = [x, g.reshape(1, d)] + ([sh, sc] if has_mod else [])
    return pl.pallas_call(
        body, name='norm_fwd', grid=(n // rb,), in_specs=[row, vec] + ([two, two] if has_mod else []),
        out_specs=row, out_shape=jax.ShapeDtypeStruct((n, d), F32), compiler_params=_cparams(('parallel',)),
    )(*ins)


def _norm_bwd_call(x, g, sh, sc, dy, rb):
    n, d = x.shape
    has_mod = sh is not None

    def body(*refs):
        x_ref, g_ref, dy_ref = refs[0], refs[1], refs[-3]
        dx_ref, acc_ref = refs[-2], refs[-1]
        i = pl.program_id(0)
        xv, dyv, gv = x_ref[...], dy_ref[...], g_ref[...]
        r = lax.rsqrt(jnp.mean(xv * xv, axis=-1, keepdims=True) + EPS)
        xh = xv * r
        if has_mod:
            lat = i > 0
            scv = jnp.where(lat, refs[3][1:2, :], refs[3][0:1, :])
            dyg = dyv * (1.0 + scv)
        else:
            dyg = dyv
        dxh = dyg * gv
        dx_ref[...] = r * (dxh - xh * jnp.mean(dxh * xh, axis=-1, keepdims=True))

        @pl.when(i == 0)
        def _():
            acc_ref[...] = jnp.zeros_like(acc_ref)

        acc_ref[0:1, :] += jnp.sum(dyg * xh, axis=0, keepdims=True)
        if has_mod:
            dsh = jnp.sum(dyv, axis=0, keepdims=True)
            dsc = jnp.sum(dyv * xh * gv, axis=0, keepdims=True)

            @pl.when(i == 0)
            def _():
                acc_ref[1:2, :] += dsh
                acc_ref[3:4, :] += dsc

            @pl.when(i > 0)
            def _():
                acc_ref[2:3, :] += dsh
                acc_ref[4:5, :] += dsc

    row = pl.BlockSpec((rb, d), lambda i: (i, 0))
    vec = pl.BlockSpec((1, d), lambda i: (0, 0))
    two = pl.BlockSpec((2, d), lambda i: (0, 0))
    ins = [x, g.reshape(1, d)] + ([sh, sc] if has_mod else []) + [dy]
    return pl.pallas_call(
        body, name='norm_bwd', grid=(n // rb,), in_specs=[row, vec] + ([two, two] if has_mod else []) + [row],
        out_specs=[row, pl.BlockSpec((SUBLANES, d), lambda i: (0, 0))],
        out_shape=[jax.ShapeDtypeStruct((n, d), F32), jax.ShapeDtypeStruct((SUBLANES, d), F32)],
        compiler_params=_cparams(('arbitrary',)),
    )(*ins)


def _make_norm(rb, has_mod):
    if has_mod:
        @jax.custom_vjp
        def f(x, g, sh, sc):
            return _norm_fwd_call(x, g, sh, sc, rb)

        def fwd(x, g, sh, sc):
            return _norm_fwd_call(x, g, sh, sc, rb), (x, g, sh, sc)

        def bwd(res, dy):
            x, g, sh, sc = res
            dx, acc = _norm_bwd_call(x, g, sh, sc, dy, rb)
            return dx, acc[0], acc[1:3], acc[3:5]
    else:
        @jax.custom_vjp
        def f(x, g):
            return _norm_fwd_call(x, g, None, None, rb)

        def fwd(x, g):
            return _norm_fwd_call(x, g, None, None, rb), (x, g)

        def bwd(res, dy):
            x, g = res
            dx, acc = _norm_bwd_call(x, g, None, None, dy, rb)
            return dx, acc[0]
    f.defvjp(fwd, bwd)
    return f


def _make_gated_res(rb, coef):
    def fwd_call(x, f, gate):
        n, d = x.shape

        def body(x_ref, f_ref, g_ref, o_ref):
            gv = jnp.where(pl.program_id(0) > 0, g_ref[1:2, :], g_ref[0:1, :])
            o_ref[...] = x_ref[...] + coef * gv * f_ref[...]

        row = pl.BlockSpec((rb, d), lambda i: (i, 0))
        return pl.pallas_call(
            body, name='gated_res_fwd', grid=(n // rb,), in_specs=[row, row, pl.BlockSpec((2, d), lambda i: (0, 0))],
            out_specs=row, out_shape=jax.ShapeDtypeStruct((n, d), F32), compiler_params=_cparams(('parallel',)),
        )(x, f, gate)

    def bwd_call(dy, f, gate):
        n, d = dy.shape

        def body(dy_ref, f_ref, g_ref, df_ref, acc_ref):
            i = pl.program_id(0)
            gv = jnp.where(i > 0, g_ref[1:2, :], g_ref[0:1, :])
            dyv = dy_ref[...]
            df_ref[...] = coef * gv * dyv
            part = coef * jnp.sum(dyv * f_ref[...], axis=0, keepdims=True)

            @pl.when(i == 0)
            def _():
                acc_ref[...] = jnp.zeros_like(acc_ref)
                acc_ref[0:1, :] += part

            @pl.when(i > 0)
            def _():
                acc_ref[1:2, :] += part

        row = pl.BlockSpec((rb, d), lambda i: (i, 0))
        return pl.pallas_call(
            body, name='gated_res_bwd', grid=(n // rb,), in_specs=[row, row, pl.BlockSpec((2, d), lambda i: (0, 0))],
            out_specs=[row, pl.BlockSpec((SUBLANES, d), lambda i: (0, 0))],
            out_shape=[jax.ShapeDtypeStruct((n, d), F32), jax.ShapeDtypeStruct((SUBLANES, d), F32)],
            compiler_params=_cparams(('arbitrary',)),
        )(dy, f, gate)

    @jax.custom_vjp
    def f(x, fv, gate):
        return fwd_call(x, fv, gate)

    def fwd(x, fv, gate):
        return fwd_call(x, fv, gate), (fv, gate)

    def bwd(res, dy):
        fv, gate = res
        df, acc = bwd_call(dy, fv, gate)
        return dy, df, acc[0:2]

    f.defvjp(fwd, bwd)
    return f


MLA_SCALE = (MLA_NOPE + MLA_ROPE) ** -0.5
LOG2E = math.log2(math.e)
MLA_SCALE_LOG2E = MLA_SCALE * LOG2E
NT_DIMS = (((1,), (1,)), ((), ()))


def _mla_keys(kv, kr):
    lane = lax.broadcasted_iota(jnp.int32, kv.shape, 1)
    return jnp.where(lane < MLA_NOPE, kv, kr)


def _chunks(start, stop, target):
    size = _tile(stop - start, target, LANES)
    return [(start + t * size, size) for t in range((stop - start) // size)]


MLA_CHUNK = 2816


def _mla_fwd_call(q, kv, kr, c_len):
    n = q.shape[0]
    h = q.shape[1] // LANES
    tq = c_len

    def body(q_ref, kv_ref, kr_ref, o_ref, lse_ref, kb, vb):
        i = pl.program_id(1)

        @pl.when(i == 0)
        def _():
            kvv = kv_ref[...]
            kb[...] = _mla_keys(kvv, kr_ref[...])
            vb[...] = jnp.where(lax.broadcasted_iota(jnp.int32, kvv.shape, 1) < MLA_NOPE, jnp.ones_like(kvv), kvv)

        qv = q_ref[...].astype(BF16)

        def attend(nk):
            s = lax.dot_general(qv, kb[:nk, :], NT_DIMS, preferred_element_type=F32)
            m = jnp.max(s, axis=-1, keepdims=True)
            p = jnp.exp2((s - m) * MLA_SCALE_LOG2E)
            acc = jnp.dot(p.astype(BF16), vb[:nk, :], preferred_element_type=F32)
            l = acc[:, 0:1]
            o_ref[...] = acc / l
            lse_ref[0] = m * MLA_SCALE + jnp.log(l)

        pl.when(i == 0)(lambda: attend(c_len))
        pl.when(i > 0)(lambda: attend(n))

    qspec = pl.BlockSpec((tq, LANES), lambda a, i: (i, a))
    return pl.pallas_call(
        body, name='mla_attn_fwd', grid=(h, n // tq),
        in_specs=[qspec, pl.BlockSpec((n, LANES), lambda a, i: (0, a)), pl.BlockSpec((n, LANES), lambda a, i: (0, 0))],
        out_specs=[qspec, pl.BlockSpec((1, tq, 1), lambda a, i: (a, i, 0))],
        out_shape=[jax.ShapeDtypeStruct((n, h * LANES), F32), jax.ShapeDtypeStruct((h, n, 1), F32)],
        scratch_shapes=[pltpu.VMEM((n, LANES), BF16), pltpu.VMEM((n, LANES), BF16)],
        compiler_params=_cparams(('arbitrary', 'arbitrary')),
    )(q, kv, kr)


TN_DIMS = (((0,), (0,)), ((), ()))


def _mla_bwd_call(q, kv, kr, o, lse, do, c_len):
    n = q.shape[0]
    h = q.shape[1] // LANES
    tq = c_len
    nq = n // tq

    def body(q_ref, kv_ref, kr_ref, o_ref, lse_ref, do_ref, dq_ref, dv_ref, dk_ref, kb):
        i = pl.program_id(1)

        @pl.when(i == 0)
        def _():
            kb[...] = _mla_keys(kv_ref[...], kr_ref[...])
            dv_ref[...] = jnp.zeros_like(dv_ref)
            dk_ref[...] = jnp.zeros_like(dk_ref)

        dov = do_ref[...]
        delta = jnp.sum(dov * o_ref[...], axis=-1, keepdims=True)
        qv, dob, lse2 = q_ref[...], dov.astype(BF16), lse_ref[0] * LOG2E

        def grad(chunks):
            acc = None
            for k0, kc in chunks:
                keys = kb[k0:k0 + kc, :]
                s = lax.dot_general(qv, keys, NT_DIMS, preferred_element_type=F32)
                p = jnp.exp2(s * MLA_SCALE_LOG2E - lse2)
                dp = lax.dot_general(dob, kv_ref[k0:k0 + kc, :], NT_DIMS, preferred_element_type=F32)
                ds = (p * (dp - delta)).astype(BF16)
                part = jnp.dot(ds, keys, preferred_element_type=F32)
                acc = part if acc is None else acc + part
                dk_ref[0, k0:k0 + kc, :] += lax.dot_general(ds, qv, TN_DIMS, preferred_element_type=F32)
                dv_ref[k0:k0 + kc, :] += lax.dot_general(p.astype(BF16), dob, TN_DIMS, preferred_element_type=F32)
            dq_ref[...] = acc * MLA_SCALE

        pl.when(i == 0)(lambda: grad([(0, c_len)]))
        pl.when(i > 0)(lambda: grad(_chunks(0, n, MLA_CHUNK)))

        @pl.when(i == nq - 1)
        def _():
            dk = dk_ref[0] * MLA_SCALE
            dk_ref[0] = dk
            dv_ref[...] = jnp.where(lax.broadcasted_iota(jnp.int32, dk.shape, 1) < MLA_NOPE, dk, dv_ref[...])

    qspec = pl.BlockSpec((tq, LANES), lambda a, i: (i, a))
    full = pl.BlockSpec((n, LANES), lambda a, i: (0, a))
    return pl.pallas_call(
        body, name='mla_attn_bwd', grid=(h, nq),
        in_specs=[qspec, full, pl.BlockSpec((n, LANES), lambda a, i: (0, 0)), qspec,
                  pl.BlockSpec((1, tq, 1), lambda a, i: (a, i, 0)), qspec],
        out_specs=[qspec, full, pl.BlockSpec((1, n, LANES), lambda a, i: (a, 0, 0))],
        out_shape=[jax.ShapeDtypeStruct((n, h * LANES), F32), jax.ShapeDtypeStruct((n, h * LANES), F32),
                   jax.ShapeDtypeStruct((h, n, LANES), F32)],
        scratch_shapes=[pltpu.VMEM((n, LANES), BF16)],
        compiler_params=_cparams(('arbitrary', 'arbitrary')),
    )(q, kv, kr, o, lse, do)


def _sum_leading(g):
    nl, r, _ = g.shape
    rb = _tile(r, 512, SUBLANES)

    def body(g_ref, o_ref):
        acc = g_ref[0]
        for j in range(1, nl):
            acc = acc + g_ref[j]
        o_ref[...] = acc

    return pl.pallas_call(
        body, name='sum_leading', grid=(r // rb,), in_specs=[pl.BlockSpec((nl, rb, LANES), lambda i: (0, i, 0))],
        out_specs=pl.BlockSpec((rb, LANES), lambda i: (i, 0)), out_shape=jax.ShapeDtypeStruct((r, LANES), F32),
        compiler_params=_cparams(('parallel',)),
    )(g)


def _make_mla(c_len):
    @jax.custom_vjp
    def f(q, kv, kr):
        return _mla_fwd_call(q, kv.astype(BF16), kr.astype(BF16), c_len)[0]

    def fwd(q, kv, kr):
        kvb, krb = kv.astype(BF16), kr.astype(BF16)
        o, lse = _mla_fwd_call(q, kvb, krb, c_len)
        return o, (q.astype(BF16), kvb, krb, o, lse)

    def bwd(res, do):
        q, kv, kr, o, lse = res
        dq, dkv, dk_full = _mla_bwd_call(q, kv, kr, o, lse, do, c_len)
        return dq, dkv, _sum_leading(dk_full)

    f.defvjp(fwd, bwd)
    return f


def _gqa_specs(band, c_len, nb, rows):
    cb = c_len // BLOCK
    q_spec = pl.BlockSpec((1, GQA_G, rows, GQA_D), lambda a, b: (a, 0, b, 0))
    ctx_spec = pl.BlockSpec((1, c_len, GQA_D), lambda a, b: (a, 0, 0))
    kv_specs = [ctx_spec]
    if band:
        kv_specs += [pl.BlockSpec((1, BLOCK, GQA_D), lambda a, b: (a, jnp.maximum(b - 1, 0) + cb, 0)),
                     pl.BlockSpec((1, BLOCK, GQA_D), lambda a, b: (a, b + cb, 0)),
                     pl.BlockSpec((1, BLOCK, GQA_D), lambda a, b: (a, jnp.minimum(b + 1, nb - 1) + cb, 0))]
    sink_spec = pl.BlockSpec((1, GQA_G * rows, 1), lambda a, b: (a, 0, 0))
    return q_spec, kv_specs, sink_spec


def _gqa_scores(q, kcat, sink, band, c_len, t_len, rows):
    scale = GQA_D ** -0.5
    s = lax.dot_general(q, kcat, (((1,), (1,)), ((), ())), preferred_element_type=F32) * scale
    if band:
        b = pl.program_id(1)
        shape = s.shape
        col = lax.broadcasted_iota(jnp.int32, shape, 1)
        qpos = b * BLOCK + (lax.broadcasted_iota(jnp.int32, shape, 0) & (BLOCK - 1))
        kpos = (b - 1) * BLOCK + (col - c_len)
        valid = (col < c_len) | ((jnp.abs(qpos - kpos) <= WINDOW) & (kpos >= 0) & (kpos < t_len))
        s = jnp.where(valid, s, NEG_INF)
    m = jnp.maximum(jnp.max(s, axis=-1, keepdims=True), sink)
    e = jnp.exp(s - m)
    es = jnp.exp(sink - m)
    den = es + jnp.sum(e, axis=-1, keepdims=True)
    return e / den, es / den


def _gqa_fwd_call(q4, k2, v2, sink_rows, band, c_len):
    kv, g, tq_all, d = q4.shape
    rows = BLOCK if band else tq_all
    nb = tq_all // rows
    t_len = k2.shape[1] - c_len
    nkv = 4 if band else 1
    q_spec, kv_specs, sink_spec = _gqa_specs(band, c_len, nb, rows)

    def body(*refs):
        q_ref, sink_ref, o_ref = refs[0], refs[1 + 2 * nkv], refs[-1]
        kcat = jnp.concatenate([r[0] for r in refs[1:1 + nkv]], axis=0).astype(BF16)
        vcat = jnp.concatenate([r[0] for r in refs[1 + nkv:1 + 2 * nkv]], axis=0).astype(BF16)
        q = q_ref[0].reshape(g * rows, d).astype(BF16)
        p, _ = _gqa_scores(q, kcat, sink_ref[0], band, c_len, t_len, rows)
        o_ref[0] = jnp.dot(p.astype(BF16), vcat, preferred_element_type=F32).reshape(g, rows, d)

    return pl.pallas_call(
        body, name='gqa_fwd_band' if band else 'gqa_fwd_ctx', grid=(kv, nb),
        in_specs=[q_spec] + kv_specs + kv_specs + [sink_spec], out_specs=q_spec,
        out_shape=jax.ShapeDtypeStruct(q4.shape, F32), compiler_params=_cparams(('parallel', 'parallel')),
    )(q4, *([k2] * nkv), *([v2] * nkv), sink_rows)


def _gqa_bwd_call(q4, k2, v2, sink_rows, do4, band, c_len):
    kv, g, tq_all, d = q4.shape
    rows = BLOCK if band else tq_all
    nb = tq_all // rows
    t_len = k2.shape[1] - c_len
    nkv = 4 if band else 1
    scale = GQA_D ** -0.5
    q_spec, kv_specs, sink_spec = _gqa_specs(band, c_len, nb, rows)

    def body(*refs):
        q_ref, sink_ref, do_ref = refs[0], refs[1 + 2 * nkv], refs[2 + 2 * nkv]
        outs = refs[3 + 2 * nkv:]
        dq_ref, dkc_ref, dvc_ref = outs[0], outs[1], outs[2]
        dsink_ref = outs[-1]
        b = pl.program_id(1)
        kcat = jnp.concatenate([r[0] for r in refs[1:1 + nkv]], axis=0).astype(BF16)
        vcat = jnp.concatenate([r[0] for r in refs[1 + nkv:1 + 2 * nkv]], axis=0).astype(BF16)
        q = q_ref[0].reshape(g * rows, d).astype(BF16)
        do = do_ref[0].reshape(g * rows, d).astype(BF16)
        p, p_sink = _gqa_scores(q, kcat, sink_ref[0], band, c_len, t_len, rows)
        dp = lax.dot_general(do, vcat, (((1,), (1,)), ((), ())), preferred_element_type=F32)
        rd = jnp.sum(p * dp, axis=-1, keepdims=True)
        ds = (p * (dp - rd) * scale).astype(BF16)
        dq_ref[0] = jnp.dot(ds, kcat, preferred_element_type=F32).reshape(g, rows, d)
        dkcat = lax.dot_general(ds, q, (((0,), (0,)), ((), ())), preferred_element_type=F32)
        dvcat = lax.dot_general(p.astype(BF16), do, (((0,), (0,)), ((), ())), preferred_element_type=F32)

        @pl.when(b == 0)
        def _():
            dkc_ref[...] = jnp.zeros_like(dkc_ref)
            dvc_ref[...] = jnp.zeros_like(dvc_ref)
            dsink_ref[...] = jnp.zeros_like(dsink_ref)

        dkc_ref[0] += dkcat[:c_len]
        dvc_ref[0] += dvcat[:c_len]
        dsink_ref[0] += -p_sink * rd
        if band:
            outs[3][0, 0] = dkcat[c_len:]
            outs[4][0, 0] = dvcat[c_len:]

    ctx_out = pl.BlockSpec((1, c_len, d), lambda a, b: (a, 0, 0))
    band_out = pl.BlockSpec((1, 1, 3 * BLOCK, d), lambda a, b: (a, b, 0, 0))
    out_specs = [q_spec, ctx_out, ctx_out] + ([band_out, band_out] if band else []) + [sink_spec]
    ctx_shape = jax.ShapeDtypeStruct((kv, c_len, d), F32)
    band_shape = jax.ShapeDtypeStruct((kv, nb, 3 * BLOCK, d), F32)
    out_shape = ([jax.ShapeDtypeStruct(q4.shape, F32), ctx_shape, ctx_shape] + ([band_shape, band_shape] if band else [])
                 + [jax.ShapeDtypeStruct(sink_rows.shape, F32)])
    return pl.pallas_call(
        body, name='gqa_bwd_band' if band else 'gqa_bwd_ctx', grid=(kv, nb),
        in_specs=[q_spec] + kv_specs + kv_specs + [sink_spec, q_spec], out_specs=out_specs, out_shape=out_shape,
        compiler_params=_cparams(('arbitrary', 'arbitrary')),
    )(q4, *([k2] * nkv), *([v2] * nkv), sink_rows, do4)


def _make_gqa(band, c_len):
    @jax.custom_vjp
    def f(q4, k2, v2, sink_rows):
        return _gqa_fwd_call(q4, k2, v2, sink_rows, band, c_len)

    def fwd(q4, k2, v2, sink_rows):
        return _gqa_fwd_call(q4, k2, v2, sink_rows, band, c_len), (q4, k2, v2, sink_rows)

    def bwd(res, do4):
        q4, k2, v2, sink_rows = res
        outs = _gqa_bwd_call(q4, k2, v2, sink_rows, do4, band, c_len)
        kv, n, d = k2.shape
        if not band:
            dq4, dkc, dvc, dsink = outs
            return dq4, dkc, dvc, dsink
        dq4, dkc, dvc, dkb, dvb, dsink = outs

        def fold(ctx_part, bands):
            cur = bands[:, :, BLOCK:2 * BLOCK]
            prv = jnp.pad(bands[:, 1:, :BLOCK], ((0, 0), (0, 1), (0, 0), (0, 0)))
            nxt = jnp.pad(bands[:, :-1, 2 * BLOCK:], ((0, 0), (1, 0), (0, 0), (0, 0)))
            lat = (cur + prv + nxt).reshape(kv, n - c_len, d)
            return jnp.concatenate([ctx_part, lat], axis=1)

        return dq4, fold(dkc, dkb), fold(dvc, dvb), dsink

    f.defvjp(fwd, bwd)
    return f


def _cmul(ar, ai, br, bi):
    return ar * br - ai * bi, ar * bi + ai * br


def _scan_tables(ar, ai, desc):
    a1 = (ar, ai)
    a2 = _cmul(*a1, *a1)
    a4 = _cmul(*a2, *a2)
    pw = [a1]
    for _ in range(SUBLANES - 1):
        pw.append(_cmul(*pw[-1], *a1))
    row = jnp.arange(SUBLANES)[:, None]
    tabs = []
    for dist, (pr, pi) in ((1, a1), (2, a2), (4, a4)):
        keep = (row <= SUBLANES - 1 - dist) if desc else (row >= dist)
        tabs += [jnp.where(keep, pr[None, :], 0.0), jnp.where(keep, pi[None, :], 0.0)]
    order = pw[::-1] if desc else pw
    tabs += [jnp.stack([p[0] for p in order]), jnp.stack([p[1] for p in order])]
    return jnp.stack(tabs).astype(F32)


def _scan_call(b_re, b_im, tabs, order, chunk, prev=None):
    n, s_dim = b_re.shape
    nch = n // chunk
    ng = chunk // SUBLANES
    desc = order in ('Fb', 'R')
    with_da = prev is not None

    def chunk_of(i):
        if order == 'F':
            return i
        if order == 'Fb':
            return nch - 1 - i
        if order == 'R':
            return jnp.where(i == 0, 0, nch - i)
        return jnp.where(i == nch - 1, 0, i + 1)

    def body(*refs):
        br_ref, bi_ref, tab_ref = refs[0], refs[1], refs[2]
        if with_da:
            pr_ref, pi_ref, sr_ref, si_ref, dar_ref, dai_ref, cr_ref, ci_ref = refs[3:]
        else:
            sr_ref, si_ref, pr_ref, pi_ref, cr_ref, ci_ref = refs[3:]

        @pl.when(pl.program_id(0) == 0)
        def _():
            cr_ref[...] = jnp.zeros_like(cr_ref)
            ci_ref[...] = jnp.zeros_like(ci_ref)
            if with_da:
                dar_ref[...] = jnp.zeros_like(dar_ref)
                dai_ref[...] = jnp.zeros_like(dai_ref)

        sub = lax.broadcasted_iota(jnp.int32, (SUBLANES, s_dim), 0)
        edge = SUBLANES - 1 if desc else 0
        last = 0 if desc else SUBLANES - 1

        def step(t, carry):
            gi = (ng - 1 - t) if desc else t
            rows = pl.ds(pl.multiple_of(gi * SUBLANES, SUBLANES), SUBLANES)
            xr, xi = br_ref[rows, :], bi_ref[rows, :]
            for j, dist in enumerate((1, 2, 4)):
                shift = SUBLANES - dist if desc else dist
                rr, ri = pltpu.roll(xr, shift, 0), pltpu.roll(xi, shift, 0)
                mr, mi = tab_ref[2 * j], tab_ref[2 * j + 1]
                xr, xi = xr + mr * rr - mi * ri, xi + mr * ri + mi * rr
            cr, ci = cr_ref[...], ci_ref[...]
            pwr, pwi = tab_ref[6], tab_ref[7]
            sr = xr + pwr * cr - pwi * ci
            si = xi + pwr * ci + pwi * cr
            sr_ref[rows, :] = sr
            si_ref[rows, :] = si
            if with_da:
                pr, pi = pr_ref[rows, :], pi_ref[rows, :]
                dar_ref[...] += sr * pr + si * pi
                dai_ref[...] += si * pr - sr * pi
            else:
                shift1 = SUBLANES - 1 if desc else 1
                pr_ref[rows, :] = jnp.where(sub == edge, cr, pltpu.roll(sr, shift1, 0))
                pi_ref[rows, :] = jnp.where(sub == edge, ci, pltpu.roll(si, shift1, 0))
            cr_ref[...] = jnp.broadcast_to(sr[last:last + 1, :], (SUBLANES, s_dim))
            ci_ref[...] = jnp.broadcast_to(si[last:last + 1, :], (SUBLANES, s_dim))
            return carry

        lax.fori_loop(0, ng, step, 0)

    blk = pl.BlockSpec((chunk, s_dim), lambda i: (chunk_of(i), 0))
    tab_spec = pl.BlockSpec((8, SUBLANES, s_dim), lambda i: (0, 0, 0))
    acc = pl.BlockSpec((SUBLANES, s_dim), lambda i: (0, 0))
    big = jax.ShapeDtypeStruct((n, s_dim), F32)
    small = jax.ShapeDtypeStruct((SUBLANES, s_dim), F32)
    if with_da:
        in_specs, ins = [blk, blk, tab_spec, blk, blk], [b_re, b_im, tabs, prev[0], prev[1]]
        out_specs, out_shape = [blk, blk, acc, acc], [big, big, small, small]
    else:
        in_specs, ins = [blk, blk, tab_spec], [b_re, b_im, tabs]
        out_specs, out_shape = [blk, blk, blk, blk], [big, big, big, big]
    return pl.pallas_call(
        body, name='s5_scan_' + order, grid=(nch,), in_specs=in_specs, out_specs=out_specs, out_shape=out_shape,
        scratch_shapes=[pltpu.VMEM((SUBLANES, s_dim), F32), pltpu.VMEM((SUBLANES, s_dim), F32)],
        compiler_params=_cparams(('arbitrary',)),
    )(*ins)


def _make_scan(rev, chunk):
    def run(b_re, b_im, ar, ai):
        tabs = _scan_tables(ar, ai, desc=rev)
        return _scan_call(b_re, b_im, tabs, 'R' if rev else 'F', chunk)

    @jax.custom_vjp
    def f(b_re, b_im, ar, ai):
        return tuple(run(b_re, b_im, ar, ai)[:2])

    def fwd(b_re, b_im, ar, ai):
        s_re, s_im, p_re, p_im = run(b_re, b_im, ar, ai)
        return (s_re, s_im), (p_re, p_im, ar, ai)

    def bwd(res, g):
        p_re, p_im, ar, ai = res
        tabs = _scan_tables(ar, -ai, desc=not rev)
        db_re, db_im, dar, dai = _scan_call(g[0], g[1], tabs, 'Rb' if rev else 'Fb', chunk, prev=(p_re, p_im))
        return db_re, db_im, jnp.sum(dar, axis=0), jnp.sum(dai, axis=0)

    f.defvjp(fwd, bwd)
    return f


def _sqerr_call(y, t):
    n, d = y.shape
    rb = _tile(n, 512, SUBLANES)

    def body(y_ref, t_ref, o_ref):
        @pl.when(pl.program_id(0) == 0)
        def _():
            o_ref[...] = jnp.zeros_like(o_ref)

        e = y_ref[...] - t_ref[...]
        o_ref[...] += jnp.sum(e * e, axis=0, keepdims=True)

    row = pl.BlockSpec((rb, d), lambda i: (i, 0))
    return pl.pallas_call(
        body, name='sq_err', grid=(n // rb,), in_specs=[row, row], out_specs=pl.BlockSpec((1, d), lambda i: (0, 0)),
        out_shape=jax.ShapeDtypeStruct((1, d), F32), compiler_params=_cparams(('arbitrary',)),
    )(y, t)


@jax.custom_vjp
def loss_head(y, t):
    return 0.5 * jnp.sum(_sqerr_call(y, t)) / y.shape[1]


def _loss_head_fwd(y, t):
    return loss_head(y, t), (y, t)


def _loss_head_bwd(res, g):
    y, t = res
    return g * (y - t) / y.shape[1], None


loss_head.defvjp(_loss_head_fwd, _loss_head_bwd)


def _adamw_call(w, g, m, v):
    r, c = w.shape
    rb = _tile(r, max(SUBLANES, (256 * 1024) // max(c, LANES) // SUBLANES * SUBLANES), SUBLANES)

    def body(w_ref, g_ref, m_ref, v_ref, d_ref, nm_ref, nv_ref):
        gv = g_ref[...]
        nm = ADAM_B1 * m_ref[...] + (1.0 - ADAM_B1) * gv
        nv = ADAM_B2 * v_ref[...] + (1.0 - ADAM_B2) * (gv * gv)
        m_hat = nm / (1.0 - ADAM_B1 ** ADAM_STEP)
        v_hat = nv / (1.0 - ADAM_B2 ** ADAM_STEP)
        d_ref[...] = -ADAM_LR * (m_hat / (jnp.sqrt(v_hat) + ADAM_EPS) + ADAM_WD * w_ref[...])
        nm_ref[...] = nm
        nv_ref[...] = nv

    blk = pl.BlockSpec((rb, c), lambda i: (i, 0))
    shape = jax.ShapeDtypeStruct((r, c), F32)
    return pl.pallas_call(
        body, name='adamw', grid=(r // rb,), in_specs=[blk] * 4, out_specs=[blk] * 3, out_shape=[shape] * 3,
        compiler_params=_cparams(('parallel',)),
    )(w, g, m, v)


MESH = pl.DeviceIdType.MESH
HBM_SPEC = pl.BlockSpec(memory_space=pltpu.HBM)


def _all_gather(x):
    def body(x_ref, out_ref, send_sems, recv_sems, local_sem):
        x, y, c = lax.axis_index('x'), lax.axis_index('y'), lax.axis_index('c')
        me, sibling = (x, y, c), (x, y, 1 - c)
        chips = [(1 - x, y), (x, 1 - y), (1 - x, 1 - y)]

        def slot(px, py, pc):
            return out_ref.at[4 * px + 2 * py + pc]

        def copy(k, block, to, src=None):
            return pltpu.make_async_remote_copy(
                src_ref=slot(*block) if src is None else src, dst_ref=slot(*block),
                send_sem=send_sems.at[k], recv_sem=recv_sems.at[k], device_id=to, device_id_type=MESH)

        mine = pltpu.make_async_copy(x_ref, slot(*me), local_sem)
        mine.start()
        first = [copy(0, me, sibling, src=x_ref)]
        first += [copy(1 + j, me, (*chip, c), src=x_ref) for j, chip in enumerate(chips)]
        for cp in first:
            cp.start()
        passed = [copy(4 + j, (*chip, c), sibling) for j, chip in enumerate(chips)]
        for j, chip in enumerate(chips):
            copy(1 + j, (*chip, c), me).wait_recv()
            passed[j].start()
        copy(0, sibling, me).wait_recv()
        for j, chip in enumerate(chips):
            copy(4 + j, (*chip, 1 - c), me).wait_recv()
        for cp in first + passed:
            cp.wait_send()
        mine.wait()

    return pl.pallas_call(
        body, name='all_gather', out_shape=jax.ShapeDtypeStruct((N_DEV,) + x.shape, x.dtype),
        in_specs=[HBM_SPEC], out_specs=HBM_SPEC,
        scratch_shapes=[pltpu.SemaphoreType.DMA((7,)), pltpu.SemaphoreType.DMA((7,)), pltpu.SemaphoreType.DMA],
    )(x)


def _exchange_sibling(g_all):
    def body(g_ref, out_ref, send_sem, recv_sem):
        x, y, c = lax.axis_index('x'), lax.axis_index('y'), lax.axis_index('c')
        cp = pltpu.make_async_remote_copy(src_ref=g_ref.at[1 - c], dst_ref=out_ref, send_sem=send_sem, recv_sem=recv_sem,
                                          device_id=(x, y, 1 - c), device_id_type=MESH)
        cp.start()
        cp.wait()

    return pl.pallas_call(
        body, name='rs_sibling', out_shape=jax.ShapeDtypeStruct(g_all.shape[1:], g_all.dtype),
        in_specs=[HBM_SPEC], out_specs=HBM_SPEC,
        scratch_shapes=[pltpu.SemaphoreType.DMA, pltpu.SemaphoreType.DMA],
    )(g_all)


def _exchange_chips(p):
    def body(p_ref, out_ref, send_sems, recv_sems):
        x, y, c = lax.axis_index('x'), lax.axis_index('y'), lax.axis_index('c')
        chips = [(1 - x, y), (x, 1 - y), (1 - x, 1 - y)]
        copies = [pltpu.make_async_remote_copy(src_ref=p_ref.at[2 * px + py], dst_ref=out_ref.at[j],
                                               send_sem=send_sems.at[j], recv_sem=recv_sems.at[j],
                                               device_id=(px, py, c), device_id_type=MESH)
                  for j, (px, py) in enumerate(chips)]
        for cp in copies:
            cp.start()
        for cp in copies:
            cp.wait_recv()
        for cp in copies:
            cp.wait_send()

    return pl.pallas_call(
        body, name='rs_chips', out_shape=jax.ShapeDtypeStruct((3,) + p.shape[1:], p.dtype),
        in_specs=[HBM_SPEC], out_specs=HBM_SPEC,
        scratch_shapes=[pltpu.SemaphoreType.DMA((3,)), pltpu.SemaphoreType.DMA((3,))],
    )(p)


def _add_sibling(g_all, recv, c_idx):
    _, nchip, r, _ = g_all.shape
    rb = _tile(r, PACK_ROWS, SUBLANES)

    def body(c_ref, g_ref, r_ref, o_ref, ob_ref):
        s = g_ref[0] + r_ref[...]
        o_ref[...] = s
        ob_ref[...] = s.astype(BF16)

    blk = pl.BlockSpec((1, rb, LANES), lambda k, i, c: (k, i, 0))
    return pl.pallas_call(
        body, name='rs_add_sibling',
        grid_spec=pltpu.PrefetchScalarGridSpec(
            num_scalar_prefetch=1, grid=(nchip, r // rb),
            in_specs=[pl.BlockSpec((1, 1, rb, LANES), lambda k, i, c: (c[0], k, i, 0)), blk],
            out_specs=[blk, blk]),
        out_shape=[jax.ShapeDtypeStruct(recv.shape, F32), jax.ShapeDtypeStruct(recv.shape, BF16)],
        compiler_params=_cparams(('parallel', 'parallel')),
    )(c_idx, g_all, recv)


def _add_chips(p, recv, chip_idx):
    _, r, _ = p.shape
    rb = _tile(r, PACK_ROWS, SUBLANES)

    def body(k_ref, p_ref, r0, r1, r2, o_ref):
        o_ref[...] = ((p_ref[0] + r0[0].astype(F32)) + r1[0].astype(F32)) + r2[0].astype(F32)

    rspec = lambda j: pl.BlockSpec((1, rb, LANES), lambda i, k: (j, i, 0))
    return pl.pallas_call(
        body, name='rs_add_chips',
        grid_spec=pltpu.PrefetchScalarGridSpec(
            num_scalar_prefetch=1, grid=(r // rb,),
            in_specs=[pl.BlockSpec((1, rb, LANES), lambda i, k: (k[0], i, 0)), rspec(0), rspec(1), rspec(2)],
            out_specs=pl.BlockSpec((rb, LANES), lambda i, k: (i, 0))),
        out_shape=jax.ShapeDtypeStruct((r, LANES), F32), compiler_params=_cparams(('parallel',)),
    )(chip_idx, p, recv, recv, recv)


def _reduce_scatter(g_all, c_idx, chip_idx):
    part, part_bf16 = _add_sibling(g_all, _exchange_sibling(g_all), c_idx)
    return _add_chips(part, _exchange_chips(part_bf16), chip_idx)


def _pad_to(n, mult):
    return (n + mult - 1) // mult * mult


def _pack(pieces, lead, dtype):
    flat = []
    total = 0
    for p in pieces:
        f = p.reshape(lead + (-1,)).astype(dtype)
        n = _pad_to(f.shape[-1], 16 * LANES)
        flat.append(jnp.pad(f, [(0, 0)] * len(lead) + [(0, n - f.shape[-1])]))
        total += n
    full = _pad_to(total, PACK_ROWS * LANES)
    if full > total:
        flat.append(jnp.zeros(lead + (full - total,), dtype))
    return jnp.concatenate(flat, axis=-1).reshape(lead + (full // LANES, LANES))


def _unpack(buf, lead, shapes):
    flat = buf.reshape(lead + (-1,))
    out, off = [], 0
    for s in shapes:
        n = math.prod(s)
        out.append(flat[..., off:off + n].reshape(lead + tuple(s)))
        off += _pad_to(n, 16 * LANES)
    return out


def _make_split(sizes, width):
    starts = [sum(sizes[:i]) for i in range(len(sizes))]
    tail = width - sum(sizes)

    @jax.custom_vjp
    def f(z):
        return tuple(z[:, o:o + s] for o, s in zip(starts, sizes))

    def fwd(z):
        return f(z), None

    def bwd(_, gs):
        pieces = list(gs) + ([jnp.zeros((gs[0].shape[0], tail), gs[0].dtype)] if tail else [])
        return (jnp.concatenate(pieces, axis=1),)

    f.defvjp(fwd, bwd)
    return f


def _rope_tables(c_len, t_len, n):
    quarter = n // 4
    inv = ROPE_BASE ** (-jnp.arange(0, 2 * quarter, 2, dtype=F32) / (2 * quarter))
    t = jnp.arange(t_len, dtype=jnp.int32)
    pos = jnp.stack([(t // GRID_W).astype(F32), (t % GRID_W).astype(F32)], axis=1)
    ang = pos[:, :, None] * inv[None, None, :]
    ang = jnp.concatenate([jnp.zeros((c_len, 2, quarter), F32), ang], axis=0)
    return jnp.cos(ang), jnp.sin(ang)


def _axial_rope(x, cos, sin):
    n_rows, h, n = x.shape
    xs = x.reshape(n_rows, h, 2, 2, n // 4)
    x1, x2 = xs[:, :, :, 0], xs[:, :, :, 1]
    c, s = cos[:, None], sin[:, None]
    return jnp.stack([x1 * c - x2 * s, x1 * s + x2 * c], axis=3).reshape(n_rows, h, n)


def _ssm_discretize(lam_re, lam_im, log_dt, b_re, b_im):
    dt = jnp.exp(log_dt)[:, None]
    mag = jnp.exp(lam_re * dt)
    a_re, a_im = mag * jnp.cos(lam_im * dt), mag * jnp.sin(lam_im * dt)
    den = lam_re * lam_re + lam_im * lam_im
    w_re = ((a_re - 1) * lam_re + a_im * lam_im) / den
    w_im = (a_im * lam_re - (a_re - 1) * lam_im) / den
    bb_re, bb_im = _cmul(w_re[..., None], w_im[..., None], b_re, b_im)
    return a_re, a_im, bb_re, bb_im


def _block_diag_in(b):
    g = b.shape[0]
    return jnp.einsum('gpm,gh->gmhp', b, jnp.eye(g, dtype=F32)).reshape(g * b.shape[2], g * b.shape[1])


def _block_diag_out(c):
    g = c.shape[0]
    return jnp.einsum('gmp,gh->gphm', c, jnp.eye(g, dtype=F32)).reshape(g * c.shape[2], g * c.shape[1])


def _w_in_layout(d_model):
    sizes = (MLA_Q_RANK, MLA_KV_RANK, MLA_ROPE, SSM_WIDTH, GQA_HEADS * GQA_D, GQA_KV * GQA_D, GQA_KV * GQA_D, 3 * d_model)
    starts = [0]
    for s in sizes[:-1]:
        starts.append(starts[-1] + s)
    names = ('cq', 'ckv', 'kr', 'u', 'gq', 'gk', 'gv', 'gates')
    orig = dict(zip(names, zip(starts, sizes)))
    order = ('cq', 'ckv', 'u', 'gq', 'gk', 'gv', 'gates', 'kr')
    return orig, order


def _permute_w_in(blocks, d_model):
    orig, order = _w_in_layout(d_model)
    r_pad = blocks.shape[1]
    r = sum(s for _, s in orig.values()) // N_DEV
    src = [(f // r) * r_pad + f % r for k in order for f in range(orig[k][0], orig[k][0] + orig[k][1])]
    src += [-1] * (_pad_to(len(src), LANES) - len(src))
    take = (jnp.asarray(src, jnp.int32)[:, None] == jnp.arange(N_DEV * r_pad, dtype=jnp.int32)[None, :]).astype(BF16)
    return _mm(take, blocks.reshape(N_DEV * r_pad, blocks.shape[2]), 'nn').astype(BF16)


def _unpermute_w_in(gp, d_model, r):
    orig, order = _w_in_layout(d_model)
    pos, off = {}, 0
    for k in order:
        pos[k] = off
        off += orig[k][1]
    names = sorted(orig, key=lambda k: orig[k][0])
    w = jnp.concatenate([gp[pos[k]:pos[k] + orig[k][1]] for k in names], axis=0)
    return w.reshape(N_DEV, r, gp.shape[1])


def _forward_loss(x_all, sinks, rp, wl, cc_in, target, c_len):
    n, d = x_all.shape
    t_len = n - c_len
    depth = len(wl)
    mla_attn = _make_mla(c_len)
    norm_mod = _make_norm(c_len, True)
    norm_tok = _make_norm(_tile(n, 512, SUBLANES), False)
    norm_out = _make_norm(_tile(t_len, 512, SUBLANES), False)
    half_res = _make_gated_res(c_len, 0.5)
    full_res = _make_gated_res(c_len, 1.0)
    gqa_band = _make_gqa(True, c_len)
    gqa_ctx = _make_gqa(False, c_len)
    scans = (_make_scan(False, c_len), _make_scan(True, c_len))
    cos_m, sin_m = _rope_tables(c_len, t_len, MLA_ROPE)
    cos_g, sin_g = _rope_tables(c_len, t_len, GQA_D)
    orig, order = _w_in_layout(d)
    split_in = _make_split([orig[k][1] for k in order], _pad_to(sum(orig[k][1] for k in order), LANES))
    halves = lambda a: _make_split([a.shape[1] // 2] * 2, a.shape[1])(a)

    cc = jnp.zeros((SUBLANES, d), F32).at[0].set(jax.nn.silu(rp['c_ctx'])).at[1].set(jax.nn.silu(cc_in))

    for l in range(depth):
        w, sk = wl[l], sinks[l]
        ctx_out = l < depth - 1

        def mm(h, name):
            return (matmul if name in ROW_SHARDED else matmul_t)(h, w[name], sk[name])

        def swiglu(h, name13, name2):
            return swiglu_ffn(h, w[name13], sk[name13], w[name2], sk[name2])

        mod = mm(cc, 'ada_w') + rp['ada_b'][l][None, :]
        md = [mod[0:2, i * d:(i + 1) * d] for i in range(N_MOD)]
        x_all = half_res(x_all, swiglu(norm_mod(x_all, rp['norm_ffn1'][l], md[0], md[1]), 'ffn1_w13', 'ffn1_w2'), md[2])

        z = mm(norm_mod(x_all, rp['norm_mix'][l], md[3], md[4]), 'w_in')
        part = dict(zip(order, split_in(z)))

        q3 = mm(norm_tok(part['cq'], rp['mla_q_norm'][l]), 'mla_w_uq').reshape(n, MLA_HEADS, LANES)
        q = jnp.concatenate([q3[..., :MLA_NOPE], _axial_rope(q3[..., MLA_NOPE:MLA_NOPE + MLA_ROPE], cos_m, sin_m),
                             q3[..., MLA_NOPE + MLA_ROPE:]], axis=-1).reshape(n, MLA_HEADS * LANES)
        kvp = mm(norm_tok(part['ckv'], rp['mla_kv_norm'][l]), 'mla_w_ukv')
        kr = _axial_rope(part['kr'].reshape(n, 1, MLA_ROPE), cos_m, sin_m).reshape(n, MLA_ROPE)
        kr = jnp.pad(kr, ((0, 0), (MLA_NOPE, LANES - MLA_NOPE - MLA_ROPE)))
        mla = mm(mla_attn(q, kvp, kr), 'mla_w_o')

        u = part['u']
        y = u * rp['ssm_d'][l][None, :]
        for direction in range(2):
            a_re, a_im, bb_re, bb_im = _ssm_discretize(
                rp['ssm_lambda_re'][l, direction], rp['ssm_lambda_im'][l, direction], rp['ssm_log_dt'][l, direction],
                rp['ssm_b_re'][l, direction], rp['ssm_b_im'][l, direction])
            s_re, s_im = scans[direction](matmul_d(u, _block_diag_in(bb_re)), matmul_d(u, _block_diag_in(bb_im)),
                                          a_re.reshape(-1), a_im.reshape(-1))
            y = y + (matmul_d(s_re, _block_diag_out(rp['ssm_c_re'][l, direction]))
                     - matmul_d(s_im, _block_diag_out(rp['ssm_c_im'][l, direction])))
        yg = mm(jax.nn.gelu(y), 'ssm_w_glu')
        ya, ygate = halves(yg)
        ssm = ya * jax.nn.sigmoid(ygate)

        gq = _axial_rope(part['gq'].reshape(n, GQA_HEADS, GQA_D), cos_g, sin_g)
        gk = _axial_rope(part['gk'].reshape(n, GQA_KV, GQA_D), cos_g, sin_g)
        q4 = jnp.transpose(gq.reshape(n, GQA_KV, GQA_G, GQA_D), (1, 2, 0, 3))
        k2 = jnp.transpose(gk, (1, 0, 2))
        v2 = jnp.transpose(part['gv'].reshape(n, GQA_KV, GQA_D), (1, 0, 2))
        sink = rp['gqa_sink'][l].reshape(GQA_KV, GQA_G, 1, 1)
        sink_rows = lambda rows: jnp.broadcast_to(sink, (GQA_KV, GQA_G, rows, 1)).reshape(GQA_KV, GQA_G * rows, 1)
        g_lat = gqa_band(q4[:, :, c_len:], k2, v2, sink_rows(BLOCK))
        if ctx_out:
            g_ctx = gqa_ctx(q4[:, :, :c_len], k2[:, :c_len], v2[:, :c_len], sink_rows(c_len))
        else:
            g_ctx = jnp.zeros((GQA_KV, GQA_G, c_len, GQA_D), F32)
        go = jnp.transpose(jnp.concatenate([g_ctx, g_lat], axis=2), (2, 0, 1, 3)).reshape(n, GQA_HEADS * GQA_D)
        gqa = mm(go, 'gqa_w_o')

        g0, g1, g2 = _make_split([d] * 3, 3 * d)(jax.nn.sigmoid(part['gates']))
        mixed = g0 * mla + g1 * ssm + g2 * gqa
        x_all = full_res(x_all, mm(mixed, 'w_out'), md[5])
        x_all = half_res(x_all, swiglu(norm_mod(x_all, rp['norm_ffn2'][l], md[6], md[7]), 'ffn2_w13', 'ffn2_w2'), md[8])

    return loss_head(norm_out(x_all[c_len:], rp['final_norm']), target)


def kernel(x, c, ctx, c_ctx, ada_w, ada_b, norm_ffn1, norm_mix, norm_ffn2, ffn1_w13, ffn1_w2, ffn2_w13, ffn2_w2, w_in, mla_q_norm, mla_kv_norm, mla_w_uq, mla_w_ukv, mla_w_o, ssm_lambda_re, ssm_lambda_im, ssm_log_dt, ssm_b_re, ssm_b_im, ssm_c_re, ssm_c_im, ssm_d, ssm_w_glu, gqa_sink, gqa_w_o, w_out, final_norm, loss_target, m_c_ctx, m_ada_w, m_ada_b, m_norm_ffn1, m_norm_mix, m_norm_ffn2, m_ffn1_w13, m_ffn1_w2, m_ffn2_w13, m_ffn2_w2, m_w_in, m_mla_q_norm, m_mla_kv_norm, m_mla_w_uq, m_mla_w_ukv, m_mla_w_o, m_ssm_lambda_re, m_ssm_lambda_im, m_ssm_log_dt, m_ssm_b_re, m_ssm_b_im, m_ssm_c_re, m_ssm_c_im, m_ssm_d, m_ssm_w_glu, m_gqa_sink, m_gqa_w_o, m_w_out, m_final_norm, v_c_ctx, v_ada_w, v_ada_b, v_norm_ffn1, v_norm_mix, v_norm_ffn2, v_ffn1_w13, v_ffn1_w2, v_ffn2_w13, v_ffn2_w2, v_w_in, v_mla_q_norm, v_mla_kv_norm, v_mla_w_uq, v_mla_w_ukv, v_mla_w_o, v_ssm_lambda_re, v_ssm_lambda_im, v_ssm_log_dt, v_ssm_b_re, v_ssm_b_im, v_ssm_c_re, v_ssm_c_im, v_ssm_d, v_ssm_w_glu, v_gqa_sink, v_gqa_w_o, v_w_out, v_final_norm):
    args = dict(locals())
    weights = {k: args[k] for k in WEIGHTS}
    moments_m = {k: args['m_' + k] for k in WEIGHTS}
    moments_v = {k: args['v_' + k] for k in WEIGHTS}
    depth = ada_w.shape[0]
    d = x.shape[-1]
    c_len = ctx.shape[1]
    my_c = lax.axis_index('c')
    my_chip = 2 * lax.axis_index('x') + lax.axis_index('y')
    qk_w = MLA_NOPE + MLA_ROPE

    r_in = weights['w_in'].shape[2]
    pad_in = _pad_to(r_in, 2 * SUBLANES) - r_in

    def as_rows(k, a):
        if k in ROW_SHARDED:
            return a
        return jnp.pad(a.T, ((0, pad_in), (0, 0))) if k == 'w_in' else a.T

    def from_rows(k, a):
        if k in ROW_SHARDED:
            return a
        return a[:r_in].T if k == 'w_in' else a.T

    shard_shapes = [as_rows(k, weights[k][0]).shape for k in SHARDED]
    layers = []
    for l in range(depth):
        gathered = _all_gather(_pack([as_rows(k, weights[k][l]) for k in SHARDED], (), BF16))
        full = {}
        for k, blocks in zip(SHARDED, _unpack(gathered, (N_DEV,), shard_shapes)):
            full[k] = _permute_w_in(blocks, d) if k == 'w_in' else blocks.reshape(N_DEV * blocks.shape[1], blocks.shape[2])
        full['mla_w_uq'] = jnp.pad(full['mla_w_uq'].reshape(MLA_HEADS, qk_w, MLA_Q_RANK),
                                   ((0, 0), (0, LANES - qk_w), (0, 0))).reshape(MLA_HEADS * LANES, MLA_Q_RANK)
        full['mla_w_o'] = jnp.pad(full['mla_w_o'].reshape(d, MLA_HEADS, MLA_V),
                                  ((0, 0), (0, 0), (LANES - MLA_V, 0))).reshape(d, MLA_HEADS * LANES)
        layers.append(full)
    sinks = [{k: jnp.zeros(v.shape, F32) for k, v in full.items()} for full in layers]

    rp = {k: weights[k] for k in REPLICATED}
    x_all = jnp.concatenate([ctx[0], x[0]], axis=0)
    loss_fn = functools.partial(_forward_loss, wl=layers, cc_in=c[0], target=loss_target[0], c_len=c_len)
    loss, vjp = jax.vjp(loss_fn, x_all, sinks, rp)
    g_x, g_layers, g_rp = vjp(jnp.ones((), F32))

    grads = {k: [] for k in SHARDED}
    for l in range(depth):
        gl = dict(g_layers[l])
        gl['mla_w_uq'] = gl['mla_w_uq'].reshape(MLA_HEADS, LANES, MLA_Q_RANK)[:, :qk_w].reshape(MLA_HEADS * qk_w, MLA_Q_RANK)
        gl['mla_w_o'] = gl['mla_w_o'].reshape(d, MLA_HEADS, LANES)[:, :, LANES - MLA_V:].reshape(d, MLA_HEADS * MLA_V)
        pieces = [jnp.pad(_unpermute_w_in(gl[k], d, r_in), ((0, 0), (0, pad_in), (0, 0))) if k == 'w_in'
                  else gl[k].reshape((N_DEV,) + shape)
                  for k, shape in zip(SHARDED, shard_shapes)]
        packed = _pack(pieces, (N_DEV,), F32)
        g_all = jnp.swapaxes(packed.reshape((4, 2) + packed.shape[1:]), 0, 1)
        mine = _reduce_scatter(g_all, my_c.reshape(1).astype(jnp.int32), my_chip.reshape(1).astype(jnp.int32))
        for k, g in zip(SHARDED, _unpack(mine, (), shard_shapes)):
            grads[k].append(from_rows(k, g))
    grads = {k: jnp.stack(v) for k, v in grads.items()}

    rep_shapes = [weights[k].shape for k in REPLICATED] + [(1,)]
    small = _pack([g_rp[k] for k in REPLICATED] + [loss.reshape(1)], (), F32)
    summed = _unpack(_sum_leading(_all_gather(small)), (), rep_shapes)
    for k, g in zip(REPLICATED, summed[:-1]):
        grads[k] = g
    loss_total = summed[-1].reshape(())

    delta, new_m, new_v = {}, {}, {}
    for k in SHARDED:
        shape = weights[k].shape
        as2d = lambda a: a.reshape(-1, shape[-1])
        outs = _adamw_call(as2d(weights[k]), as2d(grads[k]), as2d(moments_m[k]), as2d(moments_v[k]))
        delta[k], new_m[k], new_v[k] = (o.reshape(shape) for o in outs)
    rep_all = [weights[k].shape for k in REPLICATED]
    packs = [_pack([src[k] for k in REPLICATED], (), F32) for src in (weights, grads, moments_m, moments_v)]
    outs = [_unpack(o, (), rep_all) for o in _adamw_call(*packs)]
    for i, k in enumerate(REPLICATED):
        delta[k], new_m[k], new_v[k] = outs[0][i], outs[1][i], outs[2][i]

    return (loss_total, g_x[c_len:][None], *[grads[k] for k in WEIGHTS], *[delta[k] for k in WEIGHTS],
            *[new_m[k] for k in WEIGHTS], *[new_v[k] for k in WEIGHTS])
```

```python
import functools
import math

import jax
import jax.numpy as jnp
from jax import lax
from jax.experimental import pallas as pl
from jax.experimental.pallas import tpu as pltpu

F32 = jnp.float32
BF16 = jnp.bfloat16

MLA_HEADS, MLA_NOPE, MLA_ROPE, MLA_V = 8, 64, 32, 64
MLA_Q_RANK, MLA_KV_RANK = 384, 256
SSM_WIDTH, SSM_GROUP, SSM_STATE = 512, 16, 64
SSM_GROUPS = SSM_WIDTH // SSM_GROUP
GQA_HEADS, GQA_KV, GQA_D = 8, 2, 64
GQA_G = GQA_HEADS // GQA_KV
WINDOW, BLOCK, GRID_W = 128, 128, 64
N_MOD = 9
ROPE_BASE = 10000.0
EPS = 1e-6
NEG_INF = -1e30
ADAM_LR, ADAM_B1, ADAM_B2, ADAM_EPS, ADAM_WD, ADAM_STEP = 0.001, 0.9, 0.999, 1e-08, 0.01, 10

N_DEV = 8
LANES = 128
SUBLANES = 8
VMEM_LIMIT = 56 * 1024 * 1024
PACK_ROWS = 1024

SHARDED = ('ada_w', 'ffn1_w13', 'ffn1_w2', 'ffn2_w13', 'ffn2_w2', 'w_in', 'mla_w_uq', 'mla_w_ukv',
           'mla_w_o', 'ssm_w_glu', 'gqa_w_o', 'w_out')
ROW_SHARDED = ('ffn1_w2', 'ffn2_w2', 'w_out')
REPLICATED = ('c_ctx', 'ada_b', 'norm_ffn1', 'norm_mix', 'norm_ffn2', 'mla_q_norm', 'mla_kv_norm',
              'ssm_lambda_re', 'ssm_lambda_im', 'ssm_log_dt', 'ssm_b_re', 'ssm_b_im', 'ssm_c_re', 'ssm_c_im',
              'ssm_d', 'gqa_sink', 'final_norm')
WEIGHTS = ('c_ctx', 'ada_w', 'ada_b', 'norm_ffn1', 'norm_mix', 'norm_ffn2', 'ffn1_w13', 'ffn1_w2', 'ffn2_w13',
           'ffn2_w2', 'w_in', 'mla_q_norm', 'mla_kv_norm', 'mla_w_uq', 'mla_w_ukv', 'mla_w_o', 'ssm_lambda_re',
           'ssm_lambda_im', 'ssm_log_dt', 'ssm_b_re', 'ssm_b_im', 'ssm_c_re', 'ssm_c_im', 'ssm_d', 'ssm_w_glu',
           'gqa_sink', 'gqa_w_o', 'w_out', 'final_norm')


def _tile(n, target, mult):
    t = (min(target, n) // mult) * mult
    while t >= mult:
        if n % t == 0:
            return t
        t -= mult
    return n


def _cparams(sem):
    return pltpu.CompilerParams(dimension_semantics=sem, vmem_limit_bytes=VMEM_LIMIT)


def _mm(a, b, mode):
    if mode == 'nn':
        (M, K), N = a.shape, b.shape[1]
        tm, tn, tk = _tile(M, 1408, SUBLANES), _tile(N, 1024, LANES), _tile(K, 1408, LANES)
    elif mode == 'nt':
        (M, K), N = a.shape, b.shape[0]
        tm, tn, tk = _tile(M, 1408, SUBLANES), _tile(N, 1024, LANES), _tile(K, 1408, LANES)
    else:
        (K, M), N = a.shape, b.shape[1]
        tm, tn, tk = _tile(M, 1408, LANES), _tile(N, 1408, LANES), _tile(K, 768, 2 * SUBLANES)
    nk = K // tk
    dims = {'nn': (((1,), (0,)), ((), ())), 'nt': (((1,), (1,)), ((), ())), 'tn': (((0,), (0,)), ((), ()))}[mode]
    keep_a = nk == 1 and mode != 'tn' and N // tn > 1 and a.dtype != BF16

    def body(a_ref, b_ref, o_ref, *scratch):
        if keep_a:
            @pl.when(pl.program_id(1) == 0)
            def _():
                scratch[0][...] = a_ref[...].astype(BF16)

            av = scratch[0][...]
        else:
            av = a_ref[...].astype(BF16)
        part = lax.dot_general(av, b_ref[...].astype(BF16), dims, preferred_element_type=F32)
        if nk == 1:
            o_ref[...] = part
        else:
            @pl.when(pl.program_id(2) == 0)
            def _():
                o_ref[...] = part

            @pl.when(pl.program_id(2) > 0)
            def _():
                o_ref[...] += part

    a_spec = pl.BlockSpec((tk, tm), lambda i, j, k: (k, i)) if mode == 'tn' else pl.BlockSpec((tm, tk), lambda i, j, k: (i, k))
    b_spec = pl.BlockSpec((tn, tk), lambda i, j, k: (j, k)) if mode == 'nt' else pl.BlockSpec((tk, tn), lambda i, j, k: (k, j))
    return pl.pallas_call(
        body, name='mm_' + mode, grid=(M // tm, N // tn, nk),
        in_specs=[a_spec, b_spec], out_specs=pl.BlockSpec((tm, tn), lambda i, j, k: (i, j)),
        out_shape=jax.ShapeDtypeStruct((M, N), F32),
        scratch_shapes=[pltpu.VMEM((tm, tk), BF16)] if keep_a else [],
        compiler_params=_cparams(('parallel', 'arbitrary', 'arbitrary')),
    )(a, b)


@jax.custom_vjp
def matmul_d(x, w):
    return _mm(x, w, 'nn')


def _matmul_d_fwd(x, w):
    return _mm(x, w, 'nn'), (x, w)


def _matmul_d_bwd(res, g):
    x, w = res
    return _mm(g, w, 'nt'), _mm(x, g, 'tn')


matmul_d.defvjp(_matmul_d_fwd, _matmul_d_bwd)


@jax.custom_vjp
def matmul(x, w, sink):
    return _mm(x, w, 'nn')


def _matmul_fwd(x, w, sink):
    return _mm(x, w, 'nn'), (x, w)


def _matmul_bwd(res, g):
    x, w = res
    return _mm(g, w, 'nt'), jnp.zeros_like(w), _mm(x, g, 'tn')


matmul.defvjp(_matmul_fwd, _matmul_bwd)


@jax.custom_vjp
def matmul_t(x, wt, sink):
    return _mm(x, wt, 'nt')


def _matmul_t_fwd(x, wt, sink):
    return _mm(x, wt, 'nt'), (x, wt)


def _matmul_t_bwd(res, g):
    x, wt = res
    return _mm(g, wt, 'nn'), jnp.zeros_like(wt), _mm(g, x, 'tn')


matmul_t.defvjp(_matmul_t_fwd, _matmul_t_bwd)


def _ffn_up_call(x, w13t):
    m, k = x.shape
    f = w13t.shape[0] // 2
    tm, tn = _tile(m, 1408, SUBLANES), _tile(f, 256, LANES)
    nf = f // tn

    def body(x_ref, wa_ref, wb_ref, a_ref, b_ref, act_ref, xb):
        @pl.when(pl.program_id(1) == 0)
        def _():
            xb[...] = x_ref[...].astype(BF16)

        a = lax.dot_general(xb[...], wa_ref[...], NT_DIMS, preferred_element_type=F32)
        b = lax.dot_general(xb[...], wb_ref[...], NT_DIMS, preferred_element_type=F32)
        a_ref[...] = a
        b_ref[...] = b
        act_ref[...] = (a * jax.nn.sigmoid(a) * b).astype(BF16)

    out = pl.BlockSpec((tm, tn), lambda i, j: (i, j))
    return pl.pallas_call(
        body, name='ffn_up', grid=(m // tm, nf),
        in_specs=[pl.BlockSpec((tm, k), lambda i, j: (i, 0)), pl.BlockSpec((tn, k), lambda i, j: (j, 0)),
                  pl.BlockSpec((tn, k), lambda i, j: (j + nf, 0))],
        out_specs=[out, out, out],
        out_shape=[jax.ShapeDtypeStruct((m, f), F32), jax.ShapeDtypeStruct((m, f), F32), jax.ShapeDtypeStruct((m, f), BF16)],
        scratch_shapes=[pltpu.VMEM((tm, k), BF16)],
        compiler_params=_cparams(('parallel', 'arbitrary')),
    )(x, w13t, w13t)


@jax.custom_vjp
def swiglu_ffn(x, w13t, sink13, w2, sink2):
    return _mm(_ffn_up_call(x, w13t)[2], w2, 'nn')


def _swiglu_ffn_fwd(x, w13t, sink13, w2, sink2):
    a, b, act = _ffn_up_call(x, w13t)
    return _mm(act, w2, 'nn'), (x, a, b, act, w13t, w2)


def _ffn_down_bwd_call(g, w2, a, b):
    m, d = g.shape
    f = w2.shape[0]
    tm, tn = _tile(m, 1408, SUBLANES), _tile(f, 256, LANES)
    nf = f // tn

    def body(g_ref, w_ref, a_ref, b_ref, o_ref, gb):
        j = pl.program_id(1)

        @pl.when(j == 0)
        def _():
            gb[...] = g_ref[...].astype(BF16)

        dact = lax.dot_general(gb[...], w_ref[...], NT_DIMS, preferred_element_type=F32)
        av = a_ref[...]
        sa = jax.nn.sigmoid(av)
        o_ref[...] = jnp.where(j < nf, dact * b_ref[...] * (sa * (1.0 + av * (1.0 - sa))), dact * (av * sa))

    half = pl.BlockSpec((tm, tn), lambda i, j: (i, lax.rem(j, nf)))
    return pl.pallas_call(
        body, name='ffn_down_bwd', grid=(m // tm, 2 * nf),
        in_specs=[pl.BlockSpec((tm, d), lambda i, j: (i, 0)), pl.BlockSpec((tn, d), lambda i, j: (lax.rem(j, nf), 0)), half, half],
        out_specs=pl.BlockSpec((tm, tn), lambda i, j: (i, j)),
        out_shape=jax.ShapeDtypeStruct((m, 2 * f), F32),
        scratch_shapes=[pltpu.VMEM((tm, d), BF16)],
        compiler_params=_cparams(('parallel', 'arbitrary')),
    )(g, w2, a, b)


def _swiglu_ffn_bwd(res, g):
    x, a, b, act, w13t, w2 = res
    d13 = _ffn_down_bwd_call(g, w2, a, b)
    return _mm(d13, w13t, 'nn'), jnp.zeros_like(w13t), _mm(d13, x, 'tn'), jnp.zeros_like(w2), _mm(act, g, 'tn')


swiglu_ffn.defvjp(_swiglu_ffn_fwd, _swiglu_ffn_bwd)


def _norm_fwd_call(x, g, sh, sc, rb):
    n, d = x.shape
    has_mod = sh is not None

    def body(*refs):
        x_ref, g_ref = refs[0], refs[1]
        o_ref = refs[-1]
        xv = x_ref[...]
        r = lax.rsqrt(jnp.mean(xv * xv, axis=-1, keepdims=True) + EPS)
        y = xv * r * g_ref[...]
        if has_mod:
            lat = pl.program_id(0) > 0
            shv = jnp.where(lat, refs[2][1:2, :], refs[2][0:1, :])
            scv = jnp.where(lat, refs[3][1:2, :], refs[3][0:1, :])
            y = y * (1.0 + scv) + shv
        o_ref[...] = y

    row = pl.BlockSpec((rb, d), lambda i: (i, 0))
    vec = pl.BlockSpec((1, d), lambda i: (0, 0))
    two = pl.BlockSpec((2, d), lambda i: (0, 0))
    ins = [x, g.reshape(1, d)] + ([sh, sc] if has_mod else [])
    return pl.pallas_call(
        body, name='norm_fwd', grid=(n // rb,), in_specs=[row, vec] + ([two, two] if has_mod else []),
        out_specs=row, out_shape=jax.ShapeDtypeStruct((n, d), F32), compiler_params=_cparams(('parallel',)),
    )(*ins)


def _norm_bwd_call(x, g, sh, sc, dy, rb):
    n, d = x.shape
    has_mod = sh is not None

    def body(*refs):
        x_ref, g_ref, dy_ref = refs[0], refs[1], refs[-3]
        dx_ref, acc_ref = refs[-2], refs[-1]
        i = pl.program_id(0)
        xv, dyv, gv = x_ref[...], dy_ref[...], g_ref[...]
        r = lax.rsqrt(jnp.mean(xv * xv, axis=-1, keepdims=True) + EPS)
        xh = xv * r
        if has_mod:
            lat = i > 0
            scv = jnp.where(lat, refs[3][1:2, :], refs[3][0:1, :])
            dyg = dyv * (1.0 + scv)
        else:
            dyg = dyv
        dxh = dyg * gv
        dx_ref[...] = r * (dxh - xh * jnp.mean(dxh * xh, axis=-1, keepdims=True))

        @pl.when(i == 0)
        def _():
            acc_ref[...] = jnp.zeros_like(acc_ref)

        acc_ref[0:1, :] += jnp.sum(dyg * xh, axis=0, keepdims=True)
        if has_mod:
            dsh = jnp.sum(dyv, axis=0, keepdims=True)
            dsc = jnp.sum(dyv * xh * gv, axis=0, keepdims=True)

            @pl.when(i == 0)
            def _():
                acc_ref[1:2, :] += dsh
                acc_ref[3:4, :] += dsc

            @pl.when(i > 0)
            def _():
                acc_ref[2:3, :] += dsh
                acc_ref[4:5, :] += dsc

    row = pl.BlockSpec((rb, d), lambda i: (i, 0))
    vec = pl.BlockSpec((1, d), lambda i: (0, 0))
    two = pl.BlockSpec((2, d), lambda i: (0, 0))
    ins = [x, g.reshape(1, d)] + ([sh, sc] if has_mod else []) + [dy]
    return pl.pallas_call(
        body, name='norm_bwd', grid=(n // rb,), in_specs=[row, vec] + ([two, two] if has_mod else []) + [row],
        out_specs=[row, pl.BlockSpec((SUBLANES, d), lambda i: (0, 0))],
        out_shape=[jax.ShapeDtypeStruct((n, d), F32), jax.ShapeDtypeStruct((SUBLANES, d), F32)],
        compiler_params=_cparams(('arbitrary',)),
    )(*ins)


def _make_norm(rb, has_mod):
    if has_mod:
        @jax.custom_vjp
        def f(x, g, sh, sc):
            return _norm_fwd_call(x, g, sh, sc, rb)

        def fwd(x, g, sh, sc):
            return _norm_fwd_call(x, g, sh, sc, rb), (x, g, sh, sc)

        def bwd(res, dy):
            x, g, sh, sc = res
            dx, acc = _norm_bwd_call(x, g, sh, sc, dy, rb)
            return dx, acc[0], acc[1:3], acc[3:5]
    else:
        @jax.custom_vjp
        def f(x, g):
            return _norm_fwd_call(x, g, None, None, rb)

        def fwd(x, g):
            return _norm_fwd_call(x, g, None, None, rb), (x, g)

        def bwd(res, dy):
            x, g = res
            dx, acc = _norm_bwd_call(x, g, None, None, dy, rb)
            return dx, acc[0]
    f.defvjp(fwd, bwd)
    return f


def _make_gated_res(rb, coef):
    def fwd_call(x, f, gate):
        n, d = x.shape

        def body(x_ref, f_ref, g_ref, o_ref):
            gv = jnp.where(pl.program_id(0) > 0, g_ref[1:2, :], g_ref[0:1, :])
            o_ref[...] = x_ref[...] + coef * gv * f_ref[...]

        row = pl.BlockSpec((rb, d), lambda i: (i, 0))
        return pl.pallas_call(
            body, name='gated_res_fwd', grid=(n // rb,), in_specs=[row, row, pl.BlockSpec((2, d), lambda i: (0, 0))],
            out_specs=row, out_shape=jax.ShapeDtypeStruct((n, d), F32), compiler_params=_cparams(('parallel',)),
        )(x, f, gate)

    def bwd_call(dy, f, gate):
        n, d = dy.shape

        def body(dy_ref, f_ref, g_ref, df_ref, acc_ref):
            i = pl.program_id(0)
            gv = jnp.where(i > 0, g_ref[1:2, :], g_ref[0:1, :])
            dyv = dy_ref[...]
            df_ref[...] = coef * gv * dyv
            part = coef * jnp.sum(dyv * f_ref[...], axis=0, keepdims=True)

            @pl.when(i == 0)
            def _():
                acc_ref[...] = jnp.zeros_like(acc_ref)
                acc_ref[0:1, :] += part

            @pl.when(i > 0)
            def _():
                acc_ref[1:2, :] += part

        row = pl.BlockSpec((rb, d), lambda i: (i, 0))
        return pl.pallas_call(
            body, name='gated_res_bwd', grid=(n // rb,), in_specs=[row, row, pl.BlockSpec((2, d), lambda i: (0, 0))],
            out_specs=[row, pl.BlockSpec((SUBLANES, d), lambda i: (0, 0))],
            out_shape=[jax.ShapeDtypeStruct((n, d), F32), jax.ShapeDtypeStruct((SUBLANES, d), F32)],
            compiler_params=_cparams(('arbitrary',)),
        )(dy, f, gate)

    @jax.custom_vjp
    def f(x, fv, gate):
        return fwd_call(x, fv, gate)

    def fwd(x, fv, gate):
        return fwd_call(x, fv, gate), (fv, gate)

    def bwd(res, dy):
        fv, gate = res
        df, acc = bwd_call(dy, fv, gate)
        return dy, df, acc[0:2]

    f.defvjp(fwd, bwd)
    return f


MLA_SCALE = (MLA_NOPE + MLA_ROPE) ** -0.5
LOG2E = math.log2(math.e)
MLA_SCALE_LOG2E = MLA_SCALE * LOG2E
NT_DIMS = (((1,), (1,)), ((), ()))


def _mla_keys(kv, kr):
    lane = lax.broadcasted_iota(jnp.int32, kv.shape, 1)
    return jnp.where(lane < MLA_NOPE, kv, kr)


def _chunks(start, stop, target):
    size = _tile(stop - start, target, LANES)
    return [(start + t * size, size) for t in range((stop - start) // size)]


MLA_CHUNK = 2816


def _mla_fwd_call(q, kv, kr, c_len):
    n = q.shape[0]
    h = q.shape[1] // LANES
    tq = c_len

    def body(q_ref, kv_ref, kr_ref, o_ref, lse_ref, kb, vb):
        i = pl.program_id(1)

        @pl.when(i == 0)
        def _():
            kvv = kv_ref[...]
            kb[...] = _mla_keys(kvv, kr_ref[...])
            vb[...] = jnp.where(lax.broadcasted_iota(jnp.int32, kvv.shape, 1) < MLA_NOPE, jnp.ones_like(kvv), kvv)

        qv = q_ref[...].astype(BF16)

        def attend(nk):
            s = lax.dot_general(qv, kb[:nk, :], NT_DIMS, preferred_element_type=F32)
            m = jnp.max(s, axis=-1, keepdims=True)
            p = jnp.exp2((s - m) * MLA_SCALE_LOG2E)
            acc = jnp.dot(p.astype(BF16), vb[:nk, :], preferred_element_type=F32)
            l = acc[:, 0:1]
            o_ref[...] = acc / l
            lse_ref[0] = m * MLA_SCALE + jnp.log(l)

        pl.when(i == 0)(lambda: attend(c_len))
        pl.when(i > 0)(lambda: attend(n))

    qspec = pl.BlockSpec((tq, LANES), lambda a, i: (i, a))
    return pl.pallas_call(
        body, name='mla_attn_fwd', grid=(h, n // tq),
        in_specs=[qspec, pl.BlockSpec((n, LANES), lambda a, i: (0, a)), pl.BlockSpec((n, LANES), lambda a, i: (0, 0))],
        out_specs=[qspec, pl.BlockSpec((1, tq, 1), lambda a, i: (a, i, 0))],
        out_shape=[jax.ShapeDtypeStruct((n, h * LANES), F32), jax.ShapeDtypeStruct((h, n, 1), F32)],
        scratch_shapes=[pltpu.VMEM((n, LANES), BF16), pltpu.VMEM((n, LANES), BF16)],
        compiler_params=_cparams(('arbitrary', 'arbitrary')),
    )(q, kv, kr)


TN_DIMS = (((0,), (0,)), ((), ()))


def _mla_bwd_call(q, kv, kr, o, lse, do, c_len):
    n = q.shape[0]
    h = q.shape[1] // LANES
    tq = c_len
    nq = n // tq

    def body(q_ref, kv_ref, kr_ref, o_ref, lse_ref, do_ref, dq_ref, dv_ref, dk_ref, kb):
        i = pl.program_id(1)

        @pl.when(i == 0)
        def _():
            kb[...] = _mla_keys(kv_ref[...], kr_ref[...])
            dv_ref[...] = jnp.zeros_like(dv_ref)
            dk_ref[...] = jnp.zeros_like(dk_ref)

        dov = do_ref[...]
        delta = jnp.sum(dov * o_ref[...], axis=-1, keepdims=True)
        qv, dob, lse2 = q_ref[...], dov.astype(BF16), lse_ref[0] * LOG2E

        def grad(chunks):
            acc = None
            for k0, kc in chunks:
                keys = kb[k0:k0 + kc, :]
                s = lax.dot_general(qv, keys, NT_DIMS, preferred_element_type=F32)
                p = jnp.exp2(s * MLA_SCALE_LOG2E - lse2)
                dp = lax.dot_general(dob, kv_ref[k0:k0 + kc, :], NT_DIMS, preferred_element_type=F32)
                ds = (p * (dp - delta)).astype(BF16)
                part = jnp.dot(ds, keys, preferred_element_type=F32)
                acc = part if acc is None else acc + part
                dk_ref[0, k0:k0 + kc, :] += lax.dot_general(ds, qv, TN_DIMS, preferred_element_type=F32)
                dv_ref[k0:k0 + kc, :] += lax.dot_general(p.astype(BF16), dob, TN_DIMS, preferred_element_type=F32)
            dq_ref[...] = acc * MLA_SCALE

        pl.when(i == 0)(lambda: grad([(0, c_len)]))
        pl.when(i > 0)(lambda: grad(_chunks(0, n, MLA_CHUNK)))

        @pl.when(i == nq - 1)
        def _():
            dk = dk_ref[0] * MLA_SCALE
            dk_ref[0] = dk
            dv_ref[...] = jnp.where(lax.broadcasted_iota(jnp.int32, dk.shape, 1) < MLA_NOPE, dk, dv_ref[...])

    qspec = pl.BlockSpec((tq, LANES), lambda a, i: (i, a))
    full = pl.BlockSpec((n, LANES), lambda a, i: (0, a))
    return pl.pallas_call(
        body, name='mla_attn_bwd', grid=(h, nq),
        in_specs=[qspec, full, pl.BlockSpec((n, LANES), lambda a, i: (0, 0)), qspec,
                  pl.BlockSpec((1, tq, 1), lambda a, i: (a, i, 0)), qspec],
        out_specs=[qspec, full, pl.BlockSpec((1, n, LANES), lambda a, i: (a, 0, 0))],
        out_shape=[jax.ShapeDtypeStruct((n, h * LANES), F32), jax.ShapeDtypeStruct((n, h * LANES), F32),
                   jax.ShapeDtypeStruct((h, n, LANES), F32)],
        scratch_shapes=[pltpu.VMEM((n, LANES), BF16)],
        compiler_params=_cparams(('arbitrary', 'arbitrary')),
    )(q, kv, kr, o, lse, do)


def _sum_leading(g):
    nl, r, _ = g.shape
    rb = _tile(r, 512, SUBLANES)

    def body(g_ref, o_ref):
        acc = g_ref[0]
        for j in range(1, nl):
            acc = acc + g_ref[j]
        o_ref[...] = acc

    return pl.pallas_call(
        body, name='sum_leading', grid=(r // rb,), in_specs=[pl.BlockSpec((nl, rb, LANES), lambda i: (0, i, 0))],
        out_specs=pl.BlockSpec((rb, LANES), lambda i: (i, 0)), out_shape=jax.ShapeDtypeStruct((r, LANES), F32),
        compiler_params=_cparams(('parallel',)),
    )(g)


def _make_mla(c_len):
    @jax.custom_vjp
    def f(q, kv, kr):
        return _mla_fwd_call(q, kv.astype(BF16), kr.astype(BF16), c_len)[0]

    def fwd(q, kv, kr):
        kvb, krb = kv.astype(BF16), kr.astype(BF16)
        o, lse = _mla_fwd_call(q, kvb, krb, c_len)
        return o, (q.astype(BF16), kvb, krb, o, lse)

    def bwd(res, do):
        q, kv, kr, o, lse = res
        dq, dkv, dk_full = _mla_bwd_call(q, kv, kr, o, lse, do, c_len)
        return dq, dkv, _sum_leading(dk_full)

    f.defvjp(fwd, bwd)
    return f


def _gqa_specs(band, c_len, nb, rows):
    cb = c_len // BLOCK
    q_spec = pl.BlockSpec((1, GQA_G, rows, GQA_D), lambda a, b: (a, 0, b, 0))
    ctx_spec = pl.BlockSpec((1, c_len, GQA_D), lambda a, b: (a, 0, 0))
    kv_specs = [ctx_spec]
    if band:
        kv_specs += [pl.BlockSpec((1, BLOCK, GQA_D), lambda a, b: (a, jnp.maximum(b - 1, 0) + cb, 0)),
                     pl.BlockSpec((1, BLOCK, GQA_D), lambda a, b: (a, b + cb, 0)),
                     pl.BlockSpec((1, BLOCK, GQA_D), lambda a, b: (a, jnp.minimum(b + 1, nb - 1) + cb, 0))]
    sink_spec = pl.BlockSpec((1, GQA_G * rows, 1), lambda a, b: (a, 0, 0))
    return q_spec, kv_specs, sink_spec


def _gqa_scores(q, kcat, sink, band, c_len, t_len, rows):
    scale = GQA_D ** -0.5
    s = lax.dot_general(q, kcat, (((1,), (1,)), ((), ())), preferred_element_type=F32) * scale
    if band:
        b = pl.program_id(1)
        shape = s.shape
        col = lax.broadcasted_iota(jnp.int32, shape, 1)
        qpos = b * BLOCK + (lax.broadcasted_iota(jnp.int32, shape, 0) & (BLOCK - 1))
        kpos = (b - 1) * BLOCK + (col - c_len)
        valid = (col < c_len) | ((jnp.abs(qpos - kpos) <= WINDOW) & (kpos >= 0) & (kpos < t_len))
        s = jnp.where(valid, s, NEG_INF)
    m = jnp.maximum(jnp.max(s, axis=-1, keepdims=True), sink)
    e = jnp.exp(s - m)
    es = jnp.exp(sink - m)
    den = es + jnp.sum(e, axis=-1, keepdims=True)
    return e / den, es / den


def _gqa_fwd_call(q4, k2, v2, sink_rows, band, c_len):
    kv, g, tq_all, d = q4.shape
    rows = BLOCK if band else tq_all
    nb = tq_all // rows
    t_len = k2.shape[1] - c_len
    nkv = 4 if band else 1
    q_spec, kv_specs, sink_spec = _gqa_specs(band, c_len, nb, rows)

    def body(*refs):
        q_ref, sink_ref, o_ref = refs[0], refs[1 + 2 * nkv], refs[-1]
        kcat = jnp.concatenate([r[0] for r in refs[1:1 + nkv]], axis=0).astype(BF16)
        vcat = jnp.concatenate([r[0] for r in refs[1 + nkv:1 + 2 * nkv]], axis=0).astype(BF16)
        q = q_ref[0].reshape(g * rows, d).astype(BF16)
        p, _ = _gqa_scores(q, kcat, sink_ref[0], band, c_len, t_len, rows)
        o_ref[0] = jnp.dot(p.astype(BF16), vcat, preferred_element_type=F32).reshape(g, rows, d)

    return pl.pallas_call(
        body, name='gqa_fwd_band' if band else 'gqa_fwd_ctx', grid=(kv, nb),
        in_specs=[q_spec] + kv_specs + kv_specs + [sink_spec], out_specs=q_spec,
        out_shape=jax.ShapeDtypeStruct(q4.shape, F32), compiler_params=_cparams(('parallel', 'parallel')),
    )(q4, *([k2] * nkv), *([v2] * nkv), sink_rows)


def _gqa_bwd_call(q4, k2, v2, sink_rows, do4, band, c_len):
    kv, g, tq_all, d = q4.shape
    rows = BLOCK if band else tq_all
    nb = tq_all // rows
    t_len = k2.shape[1] - c_len
    nkv = 4 if band else 1
    scale = GQA_D ** -0.5
    q_spec, kv_specs, sink_spec = _gqa_specs(band, c_len, nb, rows)

    def body(*refs):
        q_ref, sink_ref, do_ref = refs[0], refs[1 + 2 * nkv], refs[2 + 2 * nkv]
        outs = refs[3 + 2 * nkv:]
        dq_ref, dkc_ref, dvc_ref = outs[0], outs[1], outs[2]
        dsink_ref = outs[-1]
        b = pl.program_id(1)
        kcat = jnp.concatenate([r[0] for r in refs[1:1 + nkv]], axis=0).astype(BF16)
        vcat = jnp.concatenate([r[0] for r in refs[1 + nkv:1 + 2 * nkv]], axis=0).astype(BF16)
        q = q_ref[0].reshape(g * rows, d).astype(BF16)
        do = do_ref[0].reshape(g * rows, d).astype(BF16)
        p, p_sink = _gqa_scores(q, kcat, sink_ref[0], band, c_len, t_len, rows)
        dp = lax.dot_general(do, vcat, (((1,), (1,)), ((), ())), preferred_element_type=F32)
        rd = jnp.sum(p * dp, axis=-1, keepdims=True)
        ds = (p * (dp - rd) * scale).astype(BF16)
        dq_ref[0] = jnp.dot(ds, kcat, preferred_element_type=F32).reshape(g, rows, d)
        dkcat = lax.dot_general(ds, q, (((0,), (0,)), ((), ())), preferred_element_type=F32)
        dvcat = lax.dot_general(p.astype(BF16), do, (((0,), (0,)), ((), ())), preferred_element_type=F32)

        @pl.when(b == 0)
        def _():
            dkc_ref[...] = jnp.zeros_like(dkc_ref)
            dvc_ref[...] = jnp.zeros_like(dvc_ref)
            dsink_ref[...] = jnp.zeros_like(dsink_ref)

        dkc_ref[0] += dkcat[:c_len]
        dvc_ref[0] += dvcat[:c_len]
        dsink_ref[0] += -p_sink * rd
        if band:
            outs[3][0, 0] = dkcat[c_len:]
            outs[4][0, 0] = dvcat[c_len:]

    ctx_out = pl.BlockSpec((1, c_len, d), lambda a, b: (a, 0, 0))
    band_out = pl.BlockSpec((1, 1, 3 * BLOCK, d), lambda a, b: (a, b, 0, 0))
    out_specs = [q_spec, ctx_out, ctx_out] + ([band_out, band_out] if band else []) + [sink_spec]
    ctx_shape = jax.ShapeDtypeStruct((kv, c_len, d), F32)
    band_shape = jax.ShapeDtypeStruct((kv, nb, 3 * BLOCK, d), F32)
    out_shape = ([jax.ShapeDtypeStruct(q4.shape, F32), ctx_shape, ctx_shape] + ([band_shape, band_shape] if band else [])
                 + [jax.ShapeDtypeStruct(sink_rows.shape, F32)])
    return pl.pallas_call(
        body, name='gqa_bwd_band' if band else 'gqa_bwd_ctx', grid=(kv, nb),
        in_specs=[q_spec] + kv_specs + kv_specs + [sink_spec, q_spec], out_specs=out_specs, out_shape=out_shape,
        compiler_params=_cparams(('arbitrary', 'arbitrary')),
    )(q4, *([k2] * nkv), *([v2] * nkv), sink_rows, do4)


def _make_gqa(band, c_len):
    @jax.custom_vjp
    def f(q4, k2, v2, sink_rows):
        return _gqa_fwd_call(q4, k2, v2, sink_rows, band, c_len)

    def fwd(q4, k2, v2, sink_rows):
        return _gqa_fwd_call(q4, k2, v2, sink_rows, band, c_len), (q4, k2, v2, sink_rows)

    def bwd(res, do4):
        q4, k2, v2, sink_rows = res
        outs = _gqa_bwd_call(q4, k2, v2, sink_rows, do4, band, c_len)
        kv, n, d = k2.shape
        if not band:
            dq4, dkc, dvc, dsink = outs
            return dq4, dkc, dvc, dsink
        dq4, dkc, dvc, dkb, dvb, dsink = outs

        def fold(ctx_part, bands):
            cur = bands[:, :, BLOCK:2 * BLOCK]
            prv = jnp.pad(bands[:, 1:, :BLOCK], ((0, 0), (0, 1), (0, 0), (0, 0)))
            nxt = jnp.pad(bands[:, :-1, 2 * BLOCK:], ((0, 0), (1, 0), (0, 0), (0, 0)))
            lat = (cur + prv + nxt).reshape(kv, n - c_len, d)
            return jnp.concatenate([ctx_part, lat], axis=1)

        return dq4, fold(dkc, dkb), fold(dvc, dvb), dsink

    f.defvjp(fwd, bwd)
    return f


def _cmul(ar, ai, br, bi):
    return ar * br - ai * bi, ar * bi + ai * br


def _scan_tables(ar, ai, desc):
    a1 = (ar, ai)
    a2 = _cmul(*a1, *a1)
    a4 = _cmul(*a2, *a2)
    pw = [a1]
    for _ in range(SUBLANES - 1):
        pw.append(_cmul(*pw[-1], *a1))
    row = jnp.arange(SUBLANES)[:, None]
    tabs = []
    for dist, (pr, pi) in ((1, a1), (2, a2), (4, a4)):
        keep = (row <= SUBLANES - 1 - dist) if desc else (row >= dist)
        tabs += [jnp.where(keep, pr[None, :], 0.0), jnp.where(keep, pi[None, :], 0.0)]
    order = pw[::-1] if desc else pw
    tabs += [jnp.stack([p[0] for p in order]), jnp.stack([p[1] for p in order])]
    return jnp.stack(tabs).astype(F32)


def _scan_call(b_re, b_im, tabs, order, chunk, prev=None):
    n, s_dim = b_re.shape
    nch = n // chunk
    ng = chunk // SUBLANES
    desc = order in ('Fb', 'R')
    with_da = prev is not None

    def chunk_of(i):
        if order == 'F':
            return i
        if order == 'Fb':
            return nch - 1 - i
        if order == 'R':
            return jnp.where(i == 0, 0, nch - i)
        return jnp.where(i == nch - 1, 0, i + 1)

    def body(*refs):
        br_ref, bi_ref, tab_ref = refs[0], refs[1], refs[2]
        if with_da:
            pr_ref, pi_ref, sr_ref, si_ref, dar_ref, dai_ref, cr_ref, ci_ref = refs[3:]
        else:
            sr_ref, si_ref, pr_ref, pi_ref, cr_ref, ci_ref = refs[3:]

        @pl.when(pl.program_id(0) == 0)
        def _():
            cr_ref[...] = jnp.zeros_like(cr_ref)
            ci_ref[...] = jnp.zeros_like(ci_ref)
            if with_da:
                dar_ref[...] = jnp.zeros_like(dar_ref)
                dai_ref[...] = jnp.zeros_like(dai_ref)

        sub = lax.broadcasted_iota(jnp.int32, (SUBLANES, s_dim), 0)
        edge = SUBLANES - 1 if desc else 0
        last = 0 if desc else SUBLANES - 1

        def step(t, carry):
            gi = (ng - 1 - t) if desc else t
            rows = pl.ds(pl.multiple_of(gi * SUBLANES, SUBLANES), SUBLANES)
            xr, xi = br_ref[rows, :], bi_ref[rows, :]
            for j, dist in enumerate((1, 2, 4)):
                shift = SUBLANES - dist if desc else dist
                rr, ri = pltpu.roll(xr, shift, 0), pltpu.roll(xi, shift, 0)
                mr, mi = tab_ref[2 * j], tab_ref[2 * j + 1]
                xr, xi = xr + mr * rr - mi * ri, xi + mr * ri + mi * rr
            cr, ci = cr_ref[...], ci_ref[...]
            pwr, pwi = tab_ref[6], tab_ref[7]
            sr = xr + pwr * cr - pwi * ci
            si = xi + pwr * ci + pwi * cr
            sr_ref[rows, :] = sr
            si_ref[rows, :] = si
            if with_da:
                pr, pi = pr_ref[rows, :], pi_ref[rows, :]
                dar_ref[...] += sr * pr + si * pi
                dai_ref[...] += si * pr - sr * pi
            else:
                shift1 = SUBLANES - 1 if desc else 1
                pr_ref[rows, :] = jnp.where(sub == edge, cr, pltpu.roll(sr, shift1, 0))
                pi_ref[rows, :] = jnp.where(sub == edge, ci, pltpu.roll(si, shift1, 0))
            cr_ref[...] = jnp.broadcast_to(sr[last:last + 1, :], (SUBLANES, s_dim))
            ci_ref[...] = jnp.broadcast_to(si[last:last + 1, :], (SUBLANES, s_dim))
            return carry

        lax.fori_loop(0, ng, step, 0)

    blk = pl.BlockSpec((chunk, s_dim), lambda i: (chunk_of(i), 0))
    tab_spec = pl.BlockSpec((8, SUBLANES, s_dim), lambda i: (0, 0, 0))
    acc = pl.BlockSpec((SUBLANES, s_dim), lambda i: (0, 0))
    big = jax.ShapeDtypeStruct((n, s_dim), F32)
    small = jax.ShapeDtypeStruct((SUBLANES, s_dim), F32)
    if with_da:
        in_specs, ins = [blk, blk, tab_spec, blk, blk], [b_re, b_im, tabs, prev[0], prev[1]]
        out_specs, out_shape = [blk, blk, acc, acc], [big, big, small, small]
    else:
        in_specs, ins = [blk, blk, tab_spec], [b_re, b_im, tabs]
        out_specs, out_shape = [blk, blk, blk, blk], [big, big, big, big]
    return pl.pallas_call(
        body, name='s5_scan_' + order, grid=(nch,), in_specs=in_specs, out_specs=out_specs, out_shape=out_shape,
        scratch_shapes=[pltpu.VMEM((SUBLANES, s_dim), F32), pltpu.VMEM((SUBLANES, s_dim), F32)],
        compiler_params=_cparams(('arbitrary',)),
    )(*ins)


def _make_scan(rev, chunk):
    def run(b_re, b_im, ar, ai):
        tabs = _scan_tables(ar, ai, desc=rev)
        return _scan_call(b_re, b_im, tabs, 'R' if rev else 'F', chunk)

    @jax.custom_vjp
    def f(b_re, b_im, ar, ai):
        return tuple(run(b_re, b_im, ar, ai)[:2])

    def fwd(b_re, b_im, ar, ai):
        s_re, s_im, p_re, p_im = run(b_re, b_im, ar, ai)
        return (s_re, s_im), (p_re, p_im, ar, ai)

    def bwd(res, g):
        p_re, p_im, ar, ai = res
        tabs = _scan_tables(ar, -ai, desc=not rev)
        db_re, db_im, dar, dai = _scan_call(g[0], g[1], tabs, 'Rb' if rev else 'Fb', chunk, prev=(p_re, p_im))
        return db_re, db_im, jnp.sum(dar, axis=0), jnp.sum(dai, axis=0)

    f.defvjp(fwd, bwd)
    return f


def _sqerr_call(y, t):
    n, d = y.shape
    rb = _tile(n, 512, SUBLANES)

    def body(y_ref, t_ref, o_ref):
        @pl.when(pl.program_id(0) == 0)
        def _():
            o_ref[...] = jnp.zeros_like(o_ref)

        e = y_ref[...] - t_ref[...]
        o_ref[...] += jnp.sum(e * e, axis=0, keepdims=True)

    row = pl.BlockSpec((rb, d), lambda i: (i, 0))
    return pl.pallas_call(
        body, name='sq_err', grid=(n // rb,), in_specs=[row, row], out_specs=pl.BlockSpec((1, d), lambda i: (0, 0)),
        out_shape=jax.ShapeDtypeStruct((1, d), F32), compiler_params=_cparams(('arbitrary',)),
    )(y, t)


@jax.custom_vjp
def loss_head(y, t):
    return 0.5 * jnp.sum(_sqerr_call(y, t)) / y.shape[1]


def _loss_head_fwd(y, t):
    return loss_head(y, t), (y, t)


def _loss_head_bwd(res, g):
    y, t = res
    return g * (y - t) / y.shape[1], None


loss_head.defvjp(_loss_head_fwd, _loss_head_bwd)


def _adamw_call(w, g, m, v):
    r, c = w.shape
    rb = _tile(r, max(SUBLANES, (256 * 1024) // max(c, LANES) // SUBLANES * SUBLANES), SUBLANES)

    def body(w_ref, g_ref, m_ref, v_ref, d_ref, nm_ref, nv_ref):
        gv = g_ref[...]
        nm = ADAM_B1 * m_ref[...] + (1.0 - ADAM_B1) * gv
        nv = ADAM_B2 * v_ref[...] + (1.0 - ADAM_B2) * (gv * gv)
        m_hat = nm / (1.0 - ADAM_B1 ** ADAM_STEP)
        v_hat = nv / (1.0 - ADAM_B2 ** ADAM_STEP)
        d_ref[...] = -ADAM_LR * (m_hat / (jnp.sqrt(v_hat) + ADAM_EPS) + ADAM_WD * w_ref[...])
        nm_ref[...] = nm
        nv_ref[...] = nv

    blk = pl.BlockSpec((rb, c), lambda i: (i, 0))
    shape = jax.ShapeDtypeStruct((r, c), F32)
    return pl.pallas_call(
        body, name='adamw', grid=(r // rb,), in_specs=[blk] * 4, out_specs=[blk] * 3, out_shape=[shape] * 3,
        compiler_params=_cparams(('parallel',)),
    )(w, g, m, v)


MESH = pl.DeviceIdType.MESH
HBM_SPEC = pl.BlockSpec(memory_space=pltpu.HBM)


def _all_gather(x):
    def body(x_ref, out_ref, send_sems, recv_sems, local_sem):
        x, y, c = lax.axis_index('x'), lax.axis_index('y'), lax.axis_index('c')
        me, sibling = (x, y, c), (x, y, 1 - c)
        chips = [(1 - x, y), (x, 1 - y), (1 - x, 1 - y)]

        def slot(px, py, pc):
            return out_ref.at[4 * px + 2 * py + pc]

        def copy(k, block, to, src=None):
            return pltpu.make_async_remote_copy(
                src_ref=slot(*block) if src is None else src, dst_ref=slot(*block),
                send_sem=send_sems.at[k], recv_sem=recv_sems.at[k], device_id=to, device_id_type=MESH)

        mine = pltpu.make_async_copy(x_ref, slot(*me), local_sem)
        mine.start()
        first = [copy(0, me, sibling, src=x_ref)]
        first += [copy(1 + j, me, (*chip, c), src=x_ref) for j, chip in enumerate(chips)]
        for cp in first:
            cp.start()
        passed = [copy(4 + j, (*chip, c), sibling) for j, chip in enumerate(chips)]
        for j, chip in enumerate(chips):
            copy(1 + j, (*chip, c), me).wait_recv()
            passed[j].start()
        copy(0, sibling, me).wait_recv()
        for j, chip in enumerate(chips):
            copy(4 + j, (*chip, 1 - c), me).wait_recv()
        for cp in first + passed:
            cp.wait_send()
        mine.wait()

    return pl.pallas_call(
        body, name='all_gather', out_shape=jax.ShapeDtypeStruct((N_DEV,) + x.shape, x.dtype),
        in_specs=[HBM_SPEC], out_specs=HBM_SPEC,
        scratch_shapes=[pltpu.SemaphoreType.DMA((7,)), pltpu.SemaphoreType.DMA((7,)), pltpu.SemaphoreType.DMA],
    )(x)


def _exchange_sibling(g_all):
    def body(g_ref, out_ref, send_sem, recv_sem):
        x, y, c = lax.axis_index('x'), lax.axis_index('y'), lax.axis_index('c')
        cp = pltpu.make_async_remote_copy(src_ref=g_ref.at[1 - c], dst_ref=out_ref, send_sem=send_sem, recv_sem=recv_sem,
                                          device_id=(x, y, 1 - c), device_id_type=MESH)
        cp.start()
        cp.wait()

    return pl.pallas_call(
        body, name='rs_sibling', out_shape=jax.ShapeDtypeStruct(g_all.shape[1:], g_all.dtype),
        in_specs=[HBM_SPEC], out_specs=HBM_SPEC,
        scratch_shapes=[pltpu.SemaphoreType.DMA, pltpu.SemaphoreType.DMA],
    )(g_all)


def _exchange_chips(p):
    def body(p_ref, out_ref, send_sems, recv_sems):
        x, y, c = lax.axis_index('x'), lax.axis_index('y'), lax.axis_index('c')
        chips = [(1 - x, y), (x, 1 - y), (1 - x, 1 - y)]
        copies = [pltpu.make_async_remote_copy(src_ref=p_ref.at[2 * px + py], dst_ref=out_ref.at[j],
                                               send_sem=send_sems.at[j], recv_sem=recv_sems.at[j],
                                               device_id=(px, py, c), device_id_type=MESH)
                  for j, (px, py) in enumerate(chips)]
        for cp in copies:
            cp.start()
        for cp in copies:
            cp.wait_recv()
        for cp in copies:
            cp.wait_send()

    return pl.pallas_call(
        body, name='rs_chips', out_shape=jax.ShapeDtypeStruct((3,) + p.shape[1:], p.dtype),
        in_specs=[HBM_SPEC], out_specs=HBM_SPEC,
        scratch_shapes=[pltpu.SemaphoreType.DMA((3,)), pltpu.SemaphoreType.DMA((3,))],
    )(p)


def _add_sibling(g_all, recv, c_idx):
    _, nchip, r, _ = g_all.shape
    rb = _tile(r, PACK_ROWS, SUBLANES)

    def body(c_ref, g_ref, r_ref, o_ref, ob_ref):
        s = g_ref[0] + r_ref[...]
        o_ref[...] = s
        ob_ref[...] = s.astype(BF16)

    blk = pl.BlockSpec((1, rb, LANES), lambda k, i, c: (k, i, 0))
    return pl.pallas_call(
        body, name='rs_add_sibling',
        grid_spec=pltpu.PrefetchScalarGridSpec(
            num_scalar_prefetch=1, grid=(nchip, r // rb),
            in_specs=[pl.BlockSpec((1, 1, rb, LANES), lambda k, i, c: (c[0], k, i, 0)), blk],
            out_specs=[blk, blk]),
        out_shape=[jax.ShapeDtypeStruct(recv.shape, F32), jax.ShapeDtypeStruct(recv.shape, BF16)],
        compiler_params=_cparams(('parallel', 'parallel')),
    )(c_idx, g_all, recv)


def _add_chips(p, recv, chip_idx):
    _, r, _ = p.shape
    rb = _tile(r, PACK_ROWS, SUBLANES)

    def body(k_ref, p_ref, r0, r1, r2, o_ref):
        o_ref[...] = ((p_ref[0] + r0[0].astype(F32)) + r1[0].astype(F32)) + r2[0].astype(F32)

    rspec = lambda j: pl.BlockSpec((1, rb, LANES), lambda i, k: (j, i, 0))
    return pl.pallas_call(
        body, name='rs_add_chips',
        grid_spec=pltpu.PrefetchScalarGridSpec(
            num_scalar_prefetch=1, grid=(r // rb,),
            in_specs=[pl.BlockSpec((1, rb, LANES), lambda i, k: (k[0], i, 0)), rspec(0), rspec(1), rspec(2)],
            out_specs=pl.BlockSpec((rb, LANES), lambda i, k: (i, 0))),
        out_shape=jax.ShapeDtypeStruct((r, LANES), F32), compiler_params=_cparams(('parallel',)),
    )(chip_idx, p, recv, recv, recv)


def _reduce_scatter(g_all, c_idx, chip_idx):
    part, part_bf16 = _add_sibling(g_all, _exchange_sibling(g_all), c_idx)
    return _add_chips(part, _exchange_chips(part_bf16), chip_idx)


def _pad_to(n, mult):
    return (n + mult - 1) // mult * mult


def _pack(pieces, lead, dtype):
    flat = []
    total = 0
    for p in pieces:
        f = p.reshape(lead + (-1,)).astype(dtype)
        n = _pad_to(f.shape[-1], 16 * LANES)
        flat.append(jnp.pad(f, [(0, 0)] * len(lead) + [(0, n - f.shape[-1])]))
        total += n
    full = _pad_to(total, PACK_ROWS * LANES)
    if full > total:
        flat.append(jnp.zeros(lead + (full - total,), dtype))
    return jnp.concatenate(flat, axis=-1).reshape(lead + (full // LANES, LANES))


def _unpack(buf, lead, shapes):
    flat = buf.reshape(lead + (-1,))
    out, off = [], 0
    for s in shapes:
        n = math.prod(s)
        out.append(flat[..., off:off + n].reshape(lead + tuple(s)))
        off += _pad_to(n, 16 * LANES)
    return out


def _make_split(sizes, width):
    starts = [sum(sizes[:i]) for i in range(len(sizes))]
    tail = width - sum(sizes)

    @jax.custom_vjp
    def f(z):
        return tuple(z[:, o:o + s] for o, s in zip(starts, sizes))

    def fwd(z):
        return f(z), None

    def bwd(_, gs):
        pieces = list(gs) + ([jnp.zeros((gs[0].shape[0], tail), gs[0].dtype)] if tail else [])
        return (jnp.concatenate(pieces, axis=1),)

    f.defvjp(fwd, bwd)
    return f


def _rope_tables(c_len, t_len, n):
    quarter = n // 4
    inv = ROPE_BASE ** (-jnp.arange(0, 2 * quarter, 2, dtype=F32) / (2 * quarter))
    t = jnp.arange(t_len, dtype=jnp.int32)
    pos = jnp.stack([(t // GRID_W).astype(F32), (t % GRID_W).astype(F32)], axis=1)
    ang = pos[:, :, None] * inv[None, None, :]
    ang = jnp.concatenate([jnp.zeros((c_len, 2, quarter), F32), ang], axis=0)
    return jnp.cos(ang), jnp.sin(ang)


def _axial_rope(x, cos, sin):
    n_rows, h, n = x.shape
    xs = x.reshape(n_rows, h, 2, 2, n // 4)
    x1, x2 = xs[:, :, :, 0], xs[:, :, :, 1]
    c, s = cos[:, None], sin[:, None]
    return jnp.stack([x1 * c - x2 * s, x1 * s + x2 * c], axis=3).reshape(n_rows, h, n)


def _ssm_discretize(lam_re, lam_im, log_dt, b_re, b_im):
    dt = jnp.exp(log_dt)[:, None]
    mag = jnp.exp(lam_re * dt)
    a_re, a_im = mag * jnp.cos(lam_im * dt), mag * jnp.sin(lam_im * dt)
    den = lam_re * lam_re + lam_im * lam_im
    w_re = ((a_re - 1) * lam_re + a_im * lam_im) / den
    w_im = (a_im * lam_re - (a_re - 1) * lam_im) / den
    bb_re, bb_im = _cmul(w_re[..., None], w_im[..., None], b_re, b_im)
    return a_re, a_im, bb_re, bb_im


def _block_diag_in(b):
    g = b.shape[0]
    return jnp.einsum('gpm,gh->gmhp', b, jnp.eye(g, dtype=F32)).reshape(g * b.shape[2], g * b.shape[1])


def _block_diag_out(c):
    g = c.shape[0]
    return jnp.einsum('gmp,gh->gphm', c, jnp.eye(g, dtype=F32)).reshape(g * c.shape[2], g * c.shape[1])


def _w_in_layout(d_model):
    sizes = (MLA_Q_RANK, MLA_KV_RANK, MLA_ROPE, SSM_WIDTH, GQA_HEADS * GQA_D, GQA_KV * GQA_D, GQA_KV * GQA_D, 3 * d_model)
    starts = [0]
    for s in sizes[:-1]:
        starts.append(starts[-1] + s)
    names = ('cq', 'ckv', 'kr', 'u', 'gq', 'gk', 'gv', 'gates')
    orig = dict(zip(names, zip(starts, sizes)))
    order = ('cq', 'ckv', 'u', 'gq', 'gk', 'gv', 'gates', 'kr')
    return orig, order


def _permute_w_in(blocks, d_model):
    orig, order = _w_in_layout(d_model)
    w = blocks.reshape(-1, blocks.shape[2])
    rows = [w[orig[k][0]:orig[k][0] + orig[k][1]] for k in order]
    width = sum(orig[k][1] for k in order)
    return jnp.pad(jnp.concatenate(rows, axis=0), ((0, _pad_to(width, LANES) - width), (0, 0)))


def _unpermute_w_in(gp, d_model, r):
    orig, order = _w_in_layout(d_model)
    pos, off = {}, 0
    for k in order:
        pos[k] = off
        off += orig[k][1]
    names = sorted(orig, key=lambda k: orig[k][0])
    w = jnp.concatenate([gp[pos[k]:pos[k] + orig[k][1]] for k in names], axis=0)
    return w.reshape(N_DEV, r, gp.shape[1])


def _forward_loss(x_all, sinks, rp, wl, cc_in, target, c_len):
    n, d = x_all.shape
    t_len = n - c_len
    depth = len(wl)
    mla_attn = _make_mla(c_len)
    norm_mod = _make_norm(c_len, True)
    norm_tok = _make_norm(_tile(n, 512, SUBLANES), False)
    norm_out = _make_norm(_tile(t_len, 512, SUBLANES), False)
    half_res = _make_gated_res(c_len, 0.5)
    full_res = _make_gated_res(c_len, 1.0)
    gqa_band = _make_gqa(True, c_len)
    gqa_ctx = _make_gqa(False, c_len)
    scans = (_make_scan(False, c_len), _make_scan(True, c_len))
    cos_m, sin_m = _rope_tables(c_len, t_len, MLA_ROPE)
    cos_g, sin_g = _rope_tables(c_len, t_len, GQA_D)
    orig, order = _w_in_layout(d)
    split_in = _make_split([orig[k][1] for k in order], _pad_to(sum(orig[k][1] for k in order), LANES))
    halves = lambda a: _make_split([a.shape[1] // 2] * 2, a.shape[1])(a)

    cc = jnp.zeros((SUBLANES, d), F32).at[0].set(jax.nn.silu(rp['c_ctx'])).at[1].set(jax.nn.silu(cc_in))

    for l in range(depth):
        w, sk = wl[l], sinks[l]
        ctx_out = l < depth - 1

        def mm(h, name):
            return (matmul if name in ROW_SHARDED else matmul_t)(h, w[name], sk[name])

        def swiglu(h, name13, name2):
            return swiglu_ffn(h, w[name13], sk[name13], w[name2], sk[name2])

        mod = mm(cc, 'ada_w') + rp['ada_b'][l][None, :]
        md = [mod[0:2, i * d:(i + 1) * d] for i in range(N_MOD)]
        x_all = half_res(x_all, swiglu(norm_mod(x_all, rp['norm_ffn1'][l], md[0], md[1]), 'ffn1_w13', 'ffn1_w2'), md[2])

        z = mm(norm_mod(x_all, rp['norm_mix'][l], md[3], md[4]), 'w_in')
        part = dict(zip(order, split_in(z)))

        q3 = mm(norm_tok(part['cq'], rp['mla_q_norm'][l]), 'mla_w_uq').reshape(n, MLA_HEADS, LANES)
        q = jnp.concatenate([q3[..., :MLA_NOPE], _axial_rope(q3[..., MLA_NOPE:MLA_NOPE + MLA_ROPE], cos_m, sin_m),
                             q3[..., MLA_NOPE + MLA_ROPE:]], axis=-1).reshape(n, MLA_HEADS * LANES)
        kvp = mm(norm_tok(part['ckv'], rp['mla_kv_norm'][l]), 'mla_w_ukv')
        kr = _axial_rope(part['kr'].reshape(n, 1, MLA_ROPE), cos_m, sin_m).reshape(n, MLA_ROPE)
        kr = jnp.pad(kr, ((0, 0), (MLA_NOPE, LANES - MLA_NOPE - MLA_ROPE)))
        mla = mm(mla_attn(q, kvp, kr), 'mla_w_o')

        u = part['u']
        y = u * rp['ssm_d'][l][None, :]
        for direction in range(2):
            a_re, a_im, bb_re, bb_im = _ssm_discretize(
                rp['ssm_lambda_re'][l, direction], rp['ssm_lambda_im'][l, direction], rp['ssm_log_dt'][l, direction],
                rp['ssm_b_re'][l, direction], rp['ssm_b_im'][l, direction])
            s_re, s_im = scans[direction](matmul_d(u, _block_diag_in(bb_re)), matmul_d(u, _block_diag_in(bb_im)),
                                          a_re.reshape(-1), a_im.reshape(-1))
            y = y + (matmul_d(s_re, _block_diag_out(rp['ssm_c_re'][l, direction]))
                     - matmul_d(s_im, _block_diag_out(rp['ssm_c_im'][l, direction])))
        yg = mm(jax.nn.gelu(y), 'ssm_w_glu')
        ya, ygate = halves(yg)
        ssm = ya * jax.nn.sigmoid(ygate)

        gq = _axial_rope(part['gq'].reshape(n, GQA_HEADS, GQA_D), cos_g, sin_g)
        gk = _axial_rope(part['gk'].reshape(n, GQA_KV, GQA_D), cos_g, sin_g)
        q4 = jnp.transpose(gq.reshape(n, GQA_KV, GQA_G, GQA_D), (1, 2, 0, 3))
        k2 = jnp.transpose(gk, (1, 0, 2))
        v2 = jnp.transpose(part['gv'].reshape(n, GQA_KV, GQA_D), (1, 0, 2))
        sink = rp['gqa_sink'][l].reshape(GQA_KV, GQA_G, 1, 1)
        sink_rows = lambda rows: jnp.broadcast_to(sink, (GQA_KV, GQA_G, rows, 1)).reshape(GQA_KV, GQA_G * rows, 1)
        g_lat = gqa_band(q4[:, :, c_len:], k2, v2, sink_rows(BLOCK))
        if ctx_out:
            g_ctx = gqa_ctx(q4[:, :, :c_len], k2[:, :c_len], v2[:, :c_len], sink_rows(c_len))
        else:
            g_ctx = jnp.zeros((GQA_KV, GQA_G, c_len, GQA_D), F32)
        go = jnp.transpose(jnp.concatenate([g_ctx, g_lat], axis=2), (2, 0, 1, 3)).reshape(n, GQA_HEADS * GQA_D)
        gqa = mm(go, 'gqa_w_o')

        g0, g1, g2 = _make_split([d] * 3, 3 * d)(jax.nn.sigmoid(part['gates']))
        mixed = g0 * mla + g1 * ssm + g2 * gqa
        x_all = full_res(x_all, mm(mixed, 'w_out'), md[5])
        x_all = half_res(x_all, swiglu(norm_mod(x_all, rp['norm_ffn2'][l], md[6], md[7]), 'ffn2_w13', 'ffn2_w2'), md[8])

    return loss_head(norm_out(x_all[c_len:], rp['final_norm']), target)


def kernel(x, c, ctx, c_ctx, ada_w, ada_b, norm_ffn1, norm_mix, norm_ffn2, ffn1_w13, ffn1_w2, ffn2_w13, ffn2_w2, w_in, mla_q_norm, mla_kv_norm, mla_w_uq, mla_w_ukv, mla_w_o, ssm_lambda_re, ssm_lambda_im, ssm_log_dt, ssm_b_re, ssm_b_im, ssm_c_re, ssm_c_im, ssm_d, ssm_w_glu, gqa_sink, gqa_w_o, w_out, final_norm, loss_target, m_c_ctx, m_ada_w, m_ada_b, m_norm_ffn1, m_norm_mix, m_norm_ffn2, m_ffn1_w13, m_ffn1_w2, m_ffn2_w13, m_ffn2_w2, m_w_in, m_mla_q_norm, m_mla_kv_norm, m_mla_w_uq, m_mla_w_ukv, m_mla_w_o, m_ssm_lambda_re, m_ssm_lambda_im, m_ssm_log_dt, m_ssm_b_re, m_ssm_b_im, m_ssm_c_re, m_ssm_c_im, m_ssm_d, m_ssm_w_glu, m_gqa_sink, m_gqa_w_o, m_w_out, m_final_norm, v_c_ctx, v_ada_w, v_ada_b, v_norm_ffn1, v_norm_mix, v_norm_ffn2, v_ffn1_w13, v_ffn1_w2, v_ffn2_w13, v_ffn2_w2, v_w_in, v_mla_q_norm, v_mla_kv_norm, v_mla_w_uq, v_mla_w_ukv, v_mla_w_o, v_ssm_lambda_re, v_ssm_lambda_im, v_ssm_log_dt, v_ssm_b_re, v_ssm_b_im, v_ssm_c_re, v_ssm_c_im, v_ssm_d, v_ssm_w_glu, v_gqa_sink, v_gqa_w_o, v_w_out, v_final_norm):
    args = dict(locals())
    weights = {k: args[k] for k in WEIGHTS}
    moments_m = {k: args['m_' + k] for k in WEIGHTS}
    moments_v = {k: args['v_' + k] for k in WEIGHTS}
    depth = ada_w.shape[0]
    d = x.shape[-1]
    c_len = ctx.shape[1]
    my_c = lax.axis_index('c')
    my_chip = 2 * lax.axis_index('x') + lax.axis_index('y')
    qk_w = MLA_NOPE + MLA_ROPE

    def as_rows(k, a):
        return a if k in ROW_SHARDED else a.T

    shard_shapes = [as_rows(k, weights[k][0]).shape for k in SHARDED]
    r_in = weights['w_in'].shape[2]
    layers = []
    gathered = _all_gather(_pack([as_rows(k, weights[k][l]) for l in range(depth) for k in SHARDED], (), BF16))
    all_blocks = _unpack(gathered, (N_DEV,), shard_shapes * depth)
    for l in range(depth):
        full = {}
        for k, blocks in zip(SHARDED, all_blocks[l * len(SHARDED):(l + 1) * len(SHARDED)]):
            full[k] = _permute_w_in(blocks, d) if k == 'w_in' else blocks.reshape(N_DEV * blocks.shape[1], blocks.shape[2])
        full['mla_w_uq'] = jnp.pad(full['mla_w_uq'].reshape(MLA_HEADS, qk_w, MLA_Q_RANK),
                                   ((0, 0), (0, LANES - qk_w), (0, 0))).reshape(MLA_HEADS * LANES, MLA_Q_RANK)
        full['mla_w_o'] = jnp.pad(full['mla_w_o'].reshape(d, MLA_HEADS, MLA_V),
                                  ((0, 0), (0, 0), (LANES - MLA_V, 0))).reshape(d, MLA_HEADS * LANES)
        layers.append(full)
    sinks = [{k: jnp.zeros(v.shape, F32) for k, v in full.items()} for full in layers]

    rp = {k: weights[k] for k in REPLICATED}
    x_all = jnp.concatenate([ctx[0], x[0]], axis=0)
    loss_fn = functools.partial(_forward_loss, wl=layers, cc_in=c[0], target=loss_target[0], c_len=c_len)
    loss, vjp = jax.vjp(loss_fn, x_all, sinks, rp)
    g_x, g_layers, g_rp = vjp(jnp.ones((), F32))

    grads = {k: [] for k in SHARDED}
    for l in range(depth):
        gl = dict(g_layers[l])
        gl['mla_w_uq'] = gl['mla_w_uq'].reshape(MLA_HEADS, LANES, MLA_Q_RANK)[:, :qk_w].reshape(MLA_HEADS * qk_w, MLA_Q_RANK)
        gl['mla_w_o'] = gl['mla_w_o'].reshape(d, MLA_HEADS, LANES)[:, :, LANES - MLA_V:].reshape(d, MLA_HEADS * MLA_V)
        pieces = [_unpermute_w_in(gl[k], d, r_in) if k == 'w_in' else gl[k].reshape((N_DEV,) + shape)
                  for k, shape in zip(SHARDED, shard_shapes)]
        packed = _pack(pieces, (N_DEV,), F32)
        g_all = jnp.swapaxes(packed.reshape((4, 2) + packed.shape[1:]), 0, 1)
        mine = _reduce_scatter(g_all, my_c.reshape(1).astype(jnp.int32), my_chip.reshape(1).astype(jnp.int32))
        for k, g in zip(SHARDED, _unpack(mine, (), shard_shapes)):
            grads[k].append(as_rows(k, g))
    grads = {k: jnp.stack(v) for k, v in grads.items()}

    rep_shapes = [weights[k].shape for k in REPLICATED] + [(1,)]
    small = _pack([g_rp[k] for k in REPLICATED] + [loss.reshape(1)], (), F32)
    summed = _unpack(_sum_leading(_all_gather(small)), (), rep_shapes)
    for k, g in zip(REPLICATED, summed[:-1]):
        grads[k] = g
    loss_total = summed[-1].reshape(())

    delta, new_m, new_v = {}, {}, {}
    for k in SHARDED:
        shape = weights[k].shape
        as2d = lambda a: a.reshape(-1, shape[-1])
        outs = _adamw_call(as2d(weights[k]), as2d(grads[k]), as2d(moments_m[k]), as2d(moments_v[k]))
        delta[k], new_m[k], new_v[k] = (o.reshape(shape) for o in outs)
    rep_all = [weights[k].shape for k in REPLICATED]
    packs = [_pack([src[k] for k in REPLICATED], (), F32) for src in (weights, grads, moments_m, moments_v)]
    outs = [_unpack(o, (), rep_all) for o in _adamw_call(*packs)]
    for i, k in enumerate(REPLICATED):
        delta[k], new_m[k], new_v[k] = outs[0][i], outs[1][i], outs[2][i]

    return (loss_total, g_x[c_len:][None], *[grads[k] for k in WEIGHTS], *[delta[k] for k in WEIGHTS],
            *[new_m[k] for k in WEIGHTS], *[new_v[k] for k in WEIGHTS])
```
